```python
import math
import jax
import jax.numpy as jnp
from jax import lax
import numpy as np

D_MODEL = 1024
BATCH = 8
SEQ = 4096
DEPTH = 1
DEC_BATCH = 32
DEC_SEQ = 8
PAST_LEN = 16384
PAGE_SIZE = 128

N_HEADS = 8
HEAD_DIM = 64
KV_HEADS = 2
HPG = N_HEADS // KV_HEADS
ATT_W = N_HEADS * HEAD_DIM
L_CMP = 32
S_CMP = 16
L_SLC = 64
N_SEL = 16
WINDOW = 512
CMP_HID = 64
Q_BLOCK = 64
ROPE_THETA = 10000.0
N_KV_SLOTS = 4
SSM_HEADS = 8
SSM_HEAD_DIM = 64
D_INNER = SSM_HEADS * SSM_HEAD_DIM
SSM_GROUPS = 2
D_STATE = 64
CONV_W = 4
CONV_DIM = D_INNER + 2 * SSM_GROUPS * D_STATE
SSM_CHUNK = 128
N_EXPERT_GROUPS = 4
EXPERTS_PER_GROUP = 8
N_EXPERTS = N_EXPERT_GROUPS * EXPERTS_PER_GROUP
TOP_K = 2
D_EXPERT = 512
MOE_BLOCK = 128
PLE_DIM = 256
EPS = 1e-6
SPLIT_SIZES = (ATT_W, 6 * KV_HEADS * HEAD_DIM, 3 * N_HEADS, D_INNER, CONV_DIM, SSM_HEADS)
D_IN_PROJ = ATT_W + 6 * KV_HEADS * HEAD_DIM + 3 * N_HEADS + D_INNER + CONV_DIM + SSM_HEADS

kernel_name = 'hymba_nsa_ssd_hmoe_decode_step'


def rmsnorm(x, g):
    xf = x.astype(jnp.float32)
    y = xf * lax.rsqrt(jnp.mean(xf * xf, axis=-1, keepdims=True) + EPS)
    return (y * g.astype(jnp.float32)).astype(x.dtype)


def rope(x, pos):
    half = HEAD_DIM // 2
    inv = ROPE_THETA ** (-jnp.arange(half, dtype=jnp.float32) / half)
    ang = pos.astype(jnp.float32)[:, None] * inv[None, :]
    cos, sin = jnp.cos(ang)[:, None, :], jnp.sin(ang)[:, None, :]
    xf = x.astype(jnp.float32)
    x1, x2 = xf[..., :half], xf[..., half:]
    return jnp.concatenate([x1 * cos - x2 * sin, x2 * cos + x1 * sin], axis=-1).astype(x.dtype)


def masked_softmax(s, mask):
    s = jnp.where(mask, s, -1e30)
    e = jnp.exp(s - jnp.max(s, axis=-1, keepdims=True)) * mask
    return e / jnp.maximum(jnp.sum(e, axis=-1, keepdims=True), 1e-30)


def compress(rows, pe, w1, w2):
    bt, t_kv = rows.shape[:2]
    n_seg = t_kv // S_CMP
    n_cmp = (t_kv - L_CMP) // S_CMP + 1
    seg = rows[:, :n_seg * S_CMP].reshape(bt, n_seg, S_CMP, KV_HEADS, HEAD_DIM)
    w1r = w1.reshape(L_CMP, HEAD_DIM, CMP_HID)
    hid = jnp.einsum('ld,ldf->f', pe, w1r, preferred_element_type=jnp.float32)
    for r in range(L_CMP // S_CMP):
        part = jnp.einsum('bnsgd,sdf->bngf', seg, w1r[r * S_CMP:(r + 1) * S_CMP], preferred_element_type=jnp.float32)
        hid = hid + part[:, r:r + n_cmp]
    return jnp.einsum('bngf,fd->bngd', jax.nn.silu(hid).astype(rows.dtype), w2)


def cover_matrix(n_cmp, n_slc):
    c0 = jnp.arange(n_cmp)[:, None] * S_CMP
    s0 = jnp.arange(n_slc)[None, :] * L_SLC
    shared = jnp.minimum(c0 + L_CMP, s0 + L_SLC) - jnp.maximum(c0, s0)
    return jnp.maximum(shared, 0).astype(jnp.float32) / L_CMP


def nsa_attention(q, kv_all, win_all, q_off, cmp_pe, cmp_w1, cmp_w2):
    bt, lq = q.shape[:2]
    t_kv = kv_all.shape[1]
    scale = HEAD_DIM ** -0.5
    q_rot = rope(q, q_off + jnp.arange(lq))
    kc = compress(kv_all[:, :, 0], cmp_pe[0], cmp_w1[0], cmp_w2[0])
    vc = compress(kv_all[:, :, 1], cmp_pe[1], cmp_w1[1], cmp_w2[1])
    n_cmp = kc.shape[1]
    n_slc = -(-t_kv // L_SLC)

    def to_blocks(r):
        r = jnp.pad(r, ((0, 0), (0, n_slc * L_SLC - t_kv), (0, 0), (0, 0)))
        return r.reshape(bt, n_slc, L_SLC, KV_HEADS, HEAD_DIM).transpose(0, 3, 1, 2, 4)

    ks, vs = to_blocks(kv_all[:, :, 2]), to_blocks(kv_all[:, :, 3])
    kw, vw = win_all[:, :, 0], win_all[:, :, 1]
    cover = cover_matrix(n_cmp, n_slc)
    cmp_end = jnp.arange(n_cmp) * S_CMP + (L_CMP - 1)
    blk = jnp.arange(n_slc)
    k_sel = min(N_SEL, n_slc)
    qb = math.gcd(lq, Q_BLOCK)
    n_qb = lq // qb
    qa = jnp.arange(qb)
    band = jnp.arange(qb + WINDOW)
    bi = jnp.arange(bt)[:, None, None, None]
    gi = jnp.arange(KV_HEADS)[None, :, None, None]

    def one_block(args):
        i, qp, qr = args
        pos = q_off + i * qb + qa
        s1 = jnp.einsum('bqghd,bngd->bghqn', qp, kc, preferred_element_type=jnp.float32) * scale
        p1 = masked_softmax(s1, cmp_end[None, :] <= pos[:, None])
        o_c = jnp.einsum('bghqn,bngd->bqghd', p1.astype(vc.dtype), vc)
        imp = jnp.einsum('bghqn,nj->bgqj', p1, cover)
        cur = pos // L_SLC
        vis = blk[None, :] <= cur[:, None]
        forced = vis & ((blk[None, :] == 0) | (blk[None, :] >= cur[:, None] - 1))
        score = jnp.where(forced, 1e9, jnp.where(vis, imp, -1.0))
        top_s, idx = lax.top_k(score, k_sel)
        kg = ks[bi, gi, idx].reshape(bt, KV_HEADS, qb, k_sel * L_SLC, HEAD_DIM)
        vg = vs[bi, gi, idx].reshape(bt, KV_HEADS, qb, k_sel * L_SLC, HEAD_DIM)
        kpos = idx[..., None] * L_SLC + jnp.arange(L_SLC)
        ok = ((top_s > -0.5)[..., None] & (kpos <= pos[None, None, :, None, None])).reshape(bt, KV_HEADS, qb, k_sel * L_SLC)
        s2 = jnp.einsum('bqghd,bgqmd->bghqm', qr, kg, preferred_element_type=jnp.float32) * scale
        p2 = masked_softmax(s2, ok[:, :, None])
        o_s = jnp.einsum('bghqm,bgqmd->bqghd', p2.astype(vg.dtype), vg)
        kwb = lax.dynamic_slice_in_dim(kw, i * qb, qb + WINDOW, axis=1)
        vwb = lax.dynamic_slice_in_dim(vw, i * qb, qb + WINDOW, axis=1)
        kpos_w = q_off - WINDOW + i * qb + band
        m3 = (band[None, :] > qa[:, None]) & (band[None, :] <= qa[:, None] + WINDOW) & (kpos_w >= 0)[None, :]
        s3 = jnp.einsum('bqghd,bkgd->bghqk', qr, kwb, preferred_element_type=jnp.float32) * scale
        p3 = masked_softmax(s3, m3)
        o_w = jnp.einsum('bghqk,bkgd->bqghd', p3.astype(vwb.dtype), vwb)
        return o_c, o_s, o_w

    def split_q(t):
        return jnp.moveaxis(t.reshape(bt, n_qb, qb, KV_HEADS, HPG, HEAD_DIM), 1, 0)

    def merge(o):
        return jnp.moveaxis(o, 0, 1).reshape(bt, lq, N_HEADS, HEAD_DIM)

    oc, os_, ow = lax.map(one_block, (jnp.arange(n_qb), split_q(q), split_q(q_rot)))
    return merge(oc), merge(os_), merge(ow)


def ssd_scan(x, a, b, c, s0, cl):
    bt, L, H, P = x.shape
    nc = L // cl
    x = x.reshape(bt, nc, cl, H, P)
    b = b.reshape(bt, nc, cl, H, -1)
    c = c.reshape(bt, nc, cl, H, -1)
    acum = jnp.cumsum(a.reshape(bt, nc, cl, H).transpose(0, 3, 1, 2), axis=-1)
    causal = jnp.tril(jnp.ones((cl, cl), dtype=bool))
    decay = jnp.exp(jnp.where(causal, acum[..., :, None] - acum[..., None, :], -jnp.inf))
    cb = jnp.einsum('bclhn,bcshn->bhcls', c, b) * decay
    y_diag = jnp.einsum('bhcls,bcshp->bclhp', cb, x)
    w_end = jnp.exp(acum[..., -1:] - acum)
    chunk_states = jnp.einsum('bclhn,bhcl,bclhp->bchpn', b, w_end, x)
    chunk_decay = jnp.exp(acum[..., -1])

    def step(s, inp):
        st, dec = inp
        return s * dec[..., None, None] + st, s

    s_final, s_in = lax.scan(step, s0, (jnp.moveaxis(chunk_states, 1, 0), jnp.moveaxis(chunk_decay, 2, 0)))
    y_off = jnp.einsum('bclhn,cbhpn,bhcl->bclhp', c, s_in, jnp.exp(acum))
    return (y_diag + y_off).reshape(bt, L, H, P), s_final


def token_mixers(a, q_off, kv_past, win_buf, ssm0, conv0, win_keep, w):
    bt, L, _ = a.shape
    cuts = [int(v) for v in np.cumsum(SPLIT_SIZES)[:-1]]
    q, kvs, gl, z, xbc, dt = jnp.split(a @ w['w_in'], cuts, axis=-1)
    pos = q_off + jnp.arange(L)
    q = rmsnorm(q.reshape(bt, L, N_HEADS, HEAD_DIM), w['q_norm'])
    kvs = kvs.reshape(bt, L, 6, KV_HEADS, HEAD_DIM)
    k_c = rmsnorm(kvs[:, :, 0], w['k_norm'][0])
    k_s = rope(rmsnorm(kvs[:, :, 2], w['k_norm'][1]), pos)
    k_w = rope(rmsnorm(kvs[:, :, 4], w['k_norm'][2]), pos)
    kv_new = jnp.stack([k_c, kvs[:, :, 1], k_s, kvs[:, :, 3]], axis=2)
    win_new = jnp.stack([k_w, kvs[:, :, 5]], axis=2)
    kv_all = jnp.concatenate([kv_past, kv_new], axis=1)
    win_cat = jnp.concatenate([win_buf, win_new], axis=1)
    win_all = jnp.pad(win_cat, ((0, 0), (WINDOW - win_buf.shape[1], 0), (0, 0), (0, 0), (0, 0)))
    o_c, o_s, o_w = nsa_attention(q, kv_all, win_all, q_off, w['cmp_pe'], w['cmp_w1'], w['cmp_w2'])
    g = jax.nn.sigmoid(gl.astype(jnp.float32)).reshape(bt, L, N_HEADS, 3, 1)
    attn = (g[:, :, :, 0] * o_c + g[:, :, :, 1] * o_s + g[:, :, :, 2] * o_w).astype(a.dtype).reshape(bt, L, ATT_W)
    xbc_all = jnp.concatenate([conv0, xbc], axis=1)
    conv = w['conv_b'] + xbc_all[:, 0:L] * w['conv_w'][0]
    for j in range(1, CONV_W):
        conv = conv + xbc_all[:, j:j + L] * w['conv_w'][j]
    conv_new = xbc_all[:, xbc_all.shape[1] - (CONV_W - 1):]
    xs, b_in, c_in = jnp.split(jax.nn.silu(conv), [D_INNER, D_INNER + SSM_GROUPS * D_STATE], axis=-1)
    rep = SSM_HEADS // SSM_GROUPS
    xs = xs.reshape(bt, L, SSM_HEADS, SSM_HEAD_DIM).astype(jnp.float32)
    b_h = jnp.repeat(b_in.reshape(bt, L, SSM_GROUPS, D_STATE), rep, axis=2).astype(jnp.float32)
    c_h = jnp.repeat(c_in.reshape(bt, L, SSM_GROUPS, D_STATE), rep, axis=2).astype(jnp.float32)
    dt = jax.nn.softplus(dt.astype(jnp.float32) + w['dt_bias'].astype(jnp.float32))
    a_h = -jnp.exp(w['a_log'].astype(jnp.float32))
    y, ssm_new = ssd_scan(xs * dt[..., None], a_h * dt, b_h, c_h, ssm0.astype(jnp.float32), math.gcd(L, SSM_CHUNK))
    y = y + w['ssm_d'].astype(jnp.float32)[:, None] * xs
    gated = y.reshape(bt, L, D_INNER) * jax.nn.silu(z.astype(jnp.float32))
    ssd_out = rmsnorm(gated, w['ssm_norm']).astype(a.dtype)
    mix = jnp.concatenate([attn, ssd_out], axis=-1) @ w['w_out']
    return mix, kv_new, win_cat[:, win_cat.shape[1] - win_keep:], ssm_new.astype(ssm0.dtype), conv_new


def hier_moe(f, w_rg, b_rg, w_re, b_re, w_gate, w_up, w_down):
    bt, L, d = f.shape
    xf = f.reshape(bt * L, d)
    n_tok = xf.shape[0]
    pg = jax.nn.softmax(jnp.dot(xf, w_rg, preferred_element_type=jnp.float32) + b_rg.astype(jnp.float32), axis=-1)
    g_sel = jnp.argmax(pg, axis=-1)
    g_w = jnp.max(pg, axis=-1)
    el = (jnp.dot(xf, w_re, preferred_element_type=jnp.float32) + b_re.astype(jnp.float32)).reshape(n_tok, N_EXPERT_GROUPS, EXPERTS_PER_GROUP)
    el = el[jnp.arange(n_tok), g_sel]
    top_p, top_i = lax.top_k(jax.nn.softmax(el, axis=-1), TOP_K)
    wts = (g_w[:, None] * top_p / jnp.sum(top_p, axis=-1, keepdims=True)).reshape(-1)
    eid = (g_sel[:, None] * EXPERTS_PER_GROUP + top_i).reshape(-1)
    n_asg = eid.shape[0]
    order = jnp.argsort(eid)
    e_sorted = eid[order]
    tok = order // TOP_K
    sizes = jnp.bincount(eid, length=N_EXPERTS)
    start = jnp.cumsum(sizes) - sizes
    padded = (sizes + MOE_BLOCK - 1) // MOE_BLOCK * MOE_BLOCK
    pend = jnp.cumsum(padded)
    dest = (pend - padded)[e_sorted] + jnp.arange(n_asg) - start[e_sorted]
    n_blk = -(-n_asg // MOE_BLOCK) + N_EXPERTS
    xpad = jnp.zeros((n_blk * MOE_BLOCK, d), xf.dtype).at[dest].set(xf[tok])
    blk_e = jnp.minimum(jnp.searchsorted(pend, jnp.arange(n_blk) * MOE_BLOCK, side='right'), N_EXPERTS - 1)

    def expert_block(args):
        xb, e = args
        hb = jax.nn.silu(xb @ w_gate[e]) * (xb @ w_up[e])
        return hb @ w_down[e]

    ypad = lax.map(expert_block, (xpad.reshape(n_blk, MOE_BLOCK, d), blk_e)).reshape(n_blk * MOE_BLOCK, d)
    y = ypad[dest] * wts[order][:, None].astype(xf.dtype)
    return jnp.zeros_like(xf).at[tok].add(y).reshape(bt, L, d)


def decoder_layer(x, p, q_off, kv_past, win_buf, ssm0, conv0, win_keep, w):
    mix, kv_new, win_new, ssm_new, conv_new = token_mixers(rmsnorm(x, w['attn_norm']), q_off, kv_past, win_buf, ssm0, conv0, win_keep, w)
    h = x + mix
    h = h + hier_moe(rmsnorm(h, w['ffn_norm']), w['w_rg'], w['b_rg'], w['w_re'], w['b_re'], w['w_gate'], w['w_up'], w['w_down'])
    h = h + jax.nn.sigmoid(h @ w['w_ple_gate']) * rmsnorm(p @ w['w_ple_proj'], w['ple_norm'])
    return h, kv_new, win_new, ssm_new, conv_new


def setup_inputs(seed: int = 0) -> dict:
    key = jax.random.key(seed)
    keys = iter(list(jax.random.split(key, 48)))

    def nrm(shape, scale=1.0):
        return scale * jax.random.normal(next(keys), shape, jnp.float32)

    def gain(shape):
        return 1.0 + 0.1 * jax.random.normal(next(keys), shape, jnp.float32)

    n_pages = PAST_LEN // PAGE_SIZE
    n_used = DEC_BATCH * n_pages
    n_pool = n_used + n_used // 4
    wbuf = min(WINDOW, PAST_LEN)
    inp = {}
    inp['x_prompt'] = nrm((BATCH, SEQ, D_MODEL))
    inp['x_sample'] = nrm((DEC_BATCH, DEC_SEQ, D_MODEL))
    inp['cache_kv'] = nrm((DEPTH, n_pool, PAGE_SIZE, N_KV_SLOTS, KV_HEADS, HEAD_DIM))
    inp['cache_win'] = nrm((DEPTH, DEC_BATCH, wbuf, 2, KV_HEADS, HEAD_DIM))
    inp['state_ssm'] = nrm((DEPTH, DEC_BATCH, SSM_HEADS, SSM_HEAD_DIM, D_STATE), 0.5)
    inp['state_conv'] = nrm((DEPTH, DEC_BATCH, CONV_W - 1, CONV_DIM))
    inp['page_table'] = jax.random.permutation(next(keys), n_pool)[:n_used].reshape(DEC_BATCH, n_pages).astype(jnp.int32)
    inp['p_prompt'] = nrm((DEPTH, BATCH, SEQ, PLE_DIM))
    inp['p_sample'] = nrm((DEPTH, DEC_BATCH, DEC_SEQ, PLE_DIM))
    inp['w_in'] = nrm((DEPTH, D_MODEL, D_IN_PROJ), D_MODEL ** -0.5)
    inp['w_out'] = nrm((DEPTH, ATT_W + D_INNER, D_MODEL), (ATT_W + D_INNER) ** -0.5)
    inp['q_norm'] = gain((DEPTH, HEAD_DIM))
    inp['k_norm'] = gain((DEPTH, 3, HEAD_DIM))
    inp['cmp_pe'] = nrm((DEPTH, 2, L_CMP, HEAD_DIM), 0.1)
    inp['cmp_w1'] = nrm((DEPTH, 2, L_CMP * HEAD_DIM, CMP_HID), (L_CMP * HEAD_DIM) ** -0.5)
    inp['cmp_w2'] = nrm((DEPTH, 2, CMP_HID, HEAD_DIM), CMP_HID ** -0.5)
    inp['conv_w'] = nrm((DEPTH, CONV_W, CONV_DIM), CONV_W ** -0.5)
    inp['conv_b'] = nrm((DEPTH, CONV_DIM), 0.1)
    dt0 = jnp.exp(jax.random.uniform(next(keys), (DEPTH, SSM_HEADS), jnp.float32, math.log(1e-3), math.log(1e-1)))
    inp['dt_bias'] = dt0 + jnp.log(-jnp.expm1(-dt0))
    inp['a_log'] = jnp.log(jax.random.uniform(next(keys), (DEPTH, SSM_HEADS), jnp.float32, 1.0, 16.0))
    inp['ssm_d'] = gain((DEPTH, SSM_HEADS))
    inp['ssm_norm'] = gain((DEPTH, D_INNER))
    inp['attn_norm'] = gain((DEPTH, D_MODEL))
    inp['ffn_norm'] = gain((DEPTH, D_MODEL))
    inp['w_rg'] = nrm((DEPTH, D_MODEL, N_EXPERT_GROUPS), D_MODEL ** -0.5)
    inp['b_rg'] = nrm((DEPTH, N_EXPERT_GROUPS), 0.01)
    inp['w_re'] = nrm((DEPTH, D_MODEL, N_EXPERTS), D_MODEL ** -0.5)
    inp['b_re'] = nrm((DEPTH, N_EXPERTS), 0.01)
    inp['w_gate'] = nrm((DEPTH, N_EXPERTS, D_MODEL, D_EXPERT), D_MODEL ** -0.5)
    inp['w_up'] = nrm((DEPTH, N_EXPERTS, D_MODEL, D_EXPERT), D_MODEL ** -0.5)
    inp['w_down'] = nrm((DEPTH, N_EXPERTS, D_EXPERT, D_MODEL), D_EXPERT ** -0.5)
    inp['w_ple_proj'] = nrm((DEPTH, PLE_DIM, D_MODEL), PLE_DIM ** -0.5)
    inp['ple_norm'] = gain((DEPTH, D_MODEL))
    inp['w_ple_gate'] = nrm((DEPTH, D_MODEL, D_MODEL), D_MODEL ** -0.5)
    return inp


def reference(x_prompt, x_sample, cache_kv, cache_win, state_ssm, state_conv, page_table, p_prompt, p_sample,
              w_in, w_out, q_norm, k_norm, cmp_pe, cmp_w1, cmp_w2, conv_w, conv_b, dt_bias, a_log, ssm_d, ssm_norm,
              attn_norm, ffn_norm, w_rg, b_rg, w_re, b_re, w_gate, w_up, w_down, w_ple_proj, ple_norm, w_ple_gate):
    hp, hs = x_prompt, x_sample
    bp = x_prompt.shape[0]
    kv_p, kv_s, win_p, win_s, ssm_p, ssm_s, conv_p, conv_s = [], [], [], [], [], [], [], []
    for l in range(DEPTH):
        w = dict(w_in=w_in[l], w_out=w_out[l], q_norm=q_norm[l], k_norm=k_norm[l], cmp_pe=cmp_pe[l],
                 cmp_w1=cmp_w1[l], cmp_w2=cmp_w2[l], conv_w=conv_w[l], conv_b=conv_b[l], dt_bias=dt_bias[l],
                 a_log=a_log[l], ssm_d=ssm_d[l], ssm_norm=ssm_norm[l], attn_norm=attn_norm[l],
                 ffn_norm=ffn_norm[l], w_rg=w_rg[l], b_rg=b_rg[l], w_re=w_re[l], b_re=b_re[l],
                 w_gate=w_gate[l], w_up=w_up[l], w_down=w_down[l], w_ple_proj=w_ple_proj[l],
                 ple_norm=ple_norm[l], w_ple_gate=w_ple_gate[l])
        hp, kvn, winn, ssmn, convn = decoder_layer(
            hp, p_prompt[l], 0,
            jnp.zeros((bp, 0, N_KV_SLOTS, KV_HEADS, HEAD_DIM), hp.dtype),
            jnp.zeros((bp, 0, 2, KV_HEADS, HEAD_DIM), hp.dtype),
            jnp.zeros((bp, SSM_HEADS, SSM_HEAD_DIM, D_STATE), hp.dtype),
            jnp.zeros((bp, CONV_W - 1, CONV_DIM), hp.dtype),
            min(WINDOW, hp.shape[1]), w)
        kv_p.append(kvn); win_p.append(winn); ssm_p.append(ssmn); conv_p.append(convn)
        pool = cache_kv[l]
        kv_past = pool[page_table].reshape(page_table.shape[0], page_table.shape[1] * PAGE_SIZE, N_KV_SLOTS, KV_HEADS, HEAD_DIM)
        hs, kvn, winn, ssmn, convn = decoder_layer(
            hs, p_sample[l], PAST_LEN, kv_past, cache_win[l], state_ssm[l], state_conv[l], cache_win.shape[2], w)
        kv_s.append(kvn); win_s.append(winn); ssm_s.append(ssmn); conv_s.append(convn)
    return (hp, hs, jnp.stack(kv_p), jnp.stack(kv_s), jnp.stack(win_p), jnp.stack(win_s), jnp.stack(ssm_p), jnp.stack(ssm_s), jnp.stack(conv_p), jnp.stack(conv_s))
```

```python
import functools
import math

import numpy as np
import jax
import jax.numpy as jnp
from jax import lax
from jax.experimental import pallas as pl
from jax.experimental.pallas import tpu as pltpu

F32 = jnp.float32
BF16 = jnp.bfloat16

D_MODEL = 1024
PAGE_SIZE = 128
N_HEADS = 8
HEAD_DIM = 64
KV_HEADS = 2
HPG = N_HEADS // KV_HEADS
ATT_W = N_HEADS * HEAD_DIM
L_CMP = 32
S_CMP = 16
L_SLC = 64
N_SEL = 16
WINDOW = 512
CMP_HID = 64
ROPE_THETA = 10000.0
SSM_HEADS = 8
SSM_HEAD_DIM = 64
D_INNER = SSM_HEADS * SSM_HEAD_DIM
SSM_GROUPS = 2
D_STATE = 64
CONV_W = 4
CONV_DIM = D_INNER + 2 * SSM_GROUPS * D_STATE
SSM_CHUNK = 128
N_EXPERT_GROUPS = 4
EXPERTS_PER_GROUP = 8
N_EXPERTS = N_EXPERT_GROUPS * EXPERTS_PER_GROUP
TOP_K = 2
D_EXPERT = 512
PLE_DIM = 256
EPS = 1e-6
N_GATES = 3 * N_HEADS
C_Q, C_KV, C_Z, C_XBC, C_SM = 512, 768, 512, 768, 128
D_IN_PAD = C_Q + C_KV + C_Z + C_XBC + C_SM

NEG = -1e30
VMEM_LIMIT = 48 * 1024 * 1024
MOE_ROWS = 256


def _cparams(sem):
    return pltpu.CompilerParams(dimension_semantics=sem, vmem_limit_bytes=VMEM_LIMIT)


def _dot(a, b):
    return jnp.dot(a, b, preferred_element_type=F32)


def _dot_nt(a, b):
    return lax.dot_general(a, b, (((1,), (1,)), ((), ())), preferred_element_type=F32)


def _dot_hi(a, b):
    return jnp.dot(a, b, preferred_element_type=F32, precision=lax.Precision.HIGHEST)


def _rms_inproj_kernel(x_ref, g_ref, w_ref, q_ref, kv_ref, z_ref, xbc_ref, sm_ref):
    x = x_ref[...]
    y = x * lax.rsqrt(jnp.mean(x * x, axis=-1, keepdims=True) + EPS) * g_ref[...]
    yb = y.astype(BF16)
    c0 = 0
    for ref, width in ((q_ref, C_Q), (kv_ref, C_KV), (z_ref, C_Z), (xbc_ref, C_XBC), (sm_ref, C_SM)):
        ref[...] = _dot(yb, w_ref[:, c0:c0 + width])
        c0 += width


def rms_inproj(x, gain, w_r, tm):
    m = x.shape[0]
    widths = (C_Q, C_KV, C_Z, C_XBC, C_SM)
    return pl.pallas_call(
        _rms_inproj_kernel,
        grid=(m // tm,),
        in_specs=[pl.BlockSpec((tm, D_MODEL), lambda i: (i, 0)),
                  pl.BlockSpec((1, D_MODEL), lambda i: (0, 0)),
                  pl.BlockSpec((D_MODEL, D_IN_PAD), lambda i: (0, 0))],
        out_specs=[pl.BlockSpec((tm, w), lambda i: (i, 0)) for w in widths],
        out_shape=[jax.ShapeDtypeStruct((m, w), F32) for w in widths],
        compiler_params=_cparams(("parallel",)),
        name="rms_inproj",
    )(x, gain.reshape(1, D_MODEL), w_r)


def _outproj_kernel(x_ref, a_ref, s_ref, w_ref, h_ref):
    acc = _dot(a_ref[...], w_ref[0:ATT_W, :]) + _dot(s_ref[...], w_ref[ATT_W:ATT_W + D_INNER, :])
    h_ref[...] = x_ref[...] + acc


def outproj(x, attn, ssd, w_out_b, tm):
    m = x.shape[0]
    return pl.pallas_call(
        _outproj_kernel,
        grid=(m // tm,),
        in_specs=[pl.BlockSpec((tm, D_MODEL), lambda i: (i, 0)),
                  pl.BlockSpec((tm, ATT_W), lambda i: (i, 0)),
                  pl.BlockSpec((tm, D_INNER), lambda i: (i, 0)),
                  pl.BlockSpec((ATT_W + D_INNER, D_MODEL), lambda i: (0, 0))],
        out_specs=pl.BlockSpec((tm, D_MODEL), lambda i: (i, 0)),
        out_shape=jax.ShapeDtypeStruct((m, D_MODEL), F32),
        compiler_params=_cparams(("parallel",)),
        name="outproj",
    )(x, attn, ssd, w_out_b)


def _ple_kernel(h_ref, p_ref, wg_ref, wp_ref, g_ref, o_ref):
    h = h_ref[...]
    gate = jax.nn.sigmoid(_dot(h.astype(BF16), wg_ref[...]))
    e = _dot(p_ref[...].astype(BF16), wp_ref[...])
    e = e * lax.rsqrt(jnp.mean(e * e, axis=-1, keepdims=True) + EPS) * g_ref[...]
    o_ref[...] = h + gate * e


def ple(h, p, wg_b, wp_b, gain, tm):
    m = h.shape[0]
    return pl.pallas_call(
        _ple_kernel,
        grid=(m // tm,),
        in_specs=[pl.BlockSpec((tm, D_MODEL), lambda i: (i, 0)),
                  pl.BlockSpec((tm, PLE_DIM), lambda i: (i, 0)),
                  pl.BlockSpec((D_MODEL, D_MODEL), lambda i: (0, 0)),
                  pl.BlockSpec((PLE_DIM, D_MODEL), lambda i: (0, 0)),
                  pl.BlockSpec((1, D_MODEL), lambda i: (0, 0))],
        out_specs=pl.BlockSpec((tm, D_MODEL), lambda i: (i, 0)),
        out_shape=jax.ShapeDtypeStruct((m, D_MODEL), F32),
        compiler_params=_cparams(("parallel",)),
        name="ple",
    )(h, p, wg_b, wp_b, gain.reshape(1, D_MODEL))


def _moe_kernel(be_ref, nb_ref, x_ref, wg_ref, wu_ref, wd_ref, y_ref):
    @pl.when(pl.program_id(0) < nb_ref[0])
    def _():
        x = x_ref[...]
        a = _dot(x, wg_ref[0])
        hb = (a * jax.nn.sigmoid(a)) * _dot(x, wu_ref[0])
        y_ref[...] = _dot(hb.astype(BF16), wd_ref[0])


def moe_experts(xpad, blk_e, n_used, wg_b, wu_b, wd_b):
    n_blk = xpad.shape[0] // MOE_ROWS
    grid_spec = pltpu.PrefetchScalarGridSpec(
        num_scalar_prefetch=2,
        grid=(n_blk,),
        in_specs=[pl.BlockSpec((MOE_ROWS, D_MODEL), lambda i, be, nb: (i, 0)),
                  pl.BlockSpec((1, D_MODEL, D_EXPERT), lambda i, be, nb: (be[i], 0, 0)),
                  pl.BlockSpec((1, D_MODEL, D_EXPERT), lambda i, be, nb: (be[i], 0, 0)),
                  pl.BlockSpec((1, D_EXPERT, D_MODEL), lambda i, be, nb: (be[i], 0, 0))],
        out_specs=pl.BlockSpec((MOE_ROWS, D_MODEL), lambda i, be, nb: (i, 0)),
    )
    return pl.pallas_call(
        _moe_kernel,
        grid_spec=grid_spec,
        out_shape=jax.ShapeDtypeStruct((n_blk * MOE_ROWS, D_MODEL), F32),
        compiler_params=_cparams(("arbitrary",)),
        name="moe_experts",
    )(blk_e, n_used, xpad, wg_b, wu_b, wd_b)


def _compress_kernel(seg_ref, w1a_ref, w1b_ref, pe_ref, w1_ref, w2_ref, o_ref):
    seg = seg_ref[0, 0]
    n_seg = seg.shape[0]
    hid0 = _dot(pe_ref[0], w1_ref[0])[0:1]
    p0 = _dot(seg, w1a_ref[0])
    p1 = pltpu.roll(_dot(seg, w1b_ref[0]), n_seg - 1, axis=0)
    hid = hid0 + p0 + p1
    act = hid * jax.nn.sigmoid(hid)
    o_ref[0, 0] = _dot(act.astype(BF16), w2_ref[0]).astype(BF16)


def compress(segs, cmp_pe, cmp_w1, cmp_w2):
    _, r, n_seg, k = segs.shape
    w1 = cmp_w1.astype(BF16)
    pe = jnp.broadcast_to(cmp_pe.reshape(2, 1, L_CMP * HEAD_DIM), (2, 8, L_CMP * HEAD_DIM)).astype(BF16)
    return pl.pallas_call(
        _compress_kernel,
        grid=(2, r),
        in_specs=[pl.BlockSpec((1, 1, n_seg, k), lambda s, i: (s, i, 0, 0)),
                  pl.BlockSpec((1, k, CMP_HID), lambda s, i: (s, 0, 0)),
                  pl.BlockSpec((1, k, CMP_HID), lambda s, i: (s, 1, 0)),
                  pl.BlockSpec((1, 8, 2 * k), lambda s, i: (s, 0, 0)),
                  pl.BlockSpec((1, 2 * k, CMP_HID), lambda s, i: (s, 0, 0)),
                  pl.BlockSpec((1, CMP_HID, HEAD_DIM), lambda s, i: (s, 0, 0))],
        out_specs=pl.BlockSpec((1, 1, n_seg, HEAD_DIM), lambda s, i: (s, i, 0, 0)),
        out_shape=jax.ShapeDtypeStruct((2, r, n_seg, HEAD_DIM), BF16),
        compiler_params=_cparams(("parallel", "parallel")),
        name="compress",
    )(segs, w1, w1, pe, w1, cmp_w2.astype(BF16))


def _online_update(s, v, m_sc, l_sc, acc_sc):
    hq, tq, tk = s.shape
    m_prev = m_sc[...]
    m_new = jnp.maximum(m_prev, jnp.max(s, axis=-1, keepdims=True))
    alpha = jnp.exp(m_prev - m_new)
    p = jnp.exp(s - m_new)
    l_sc[...] = alpha * l_sc[...] + jnp.sum(p, axis=-1, keepdims=True)
    pv = _dot(p.astype(BF16).reshape(hq * tq, tk), v).reshape(hq, tq, HEAD_DIM)
    acc_sc[...] = alpha * acc_sc[...] + pv
    m_sc[...] = m_new


def _nsa_prompt_kernel(qn_ref, qr_ref, kc_ref, vc_ref, ks_ref, vs_ref, kw_ref, vw_ref, cover_ref, e_ref,
                       gate_ref, o_ref, m_sc, l_sc, acc_sc, *, tq, tk, tkw):
    g = pl.program_id(1)
    qi = pl.program_id(2)
    pos0 = qi * tq
    rows = HPG * tq
    qn = qn_ref[0, 0].reshape(rows, HEAD_DIM)
    qr = qr_ref[0, 0].reshape(rows, HEAD_DIM)

    kc = kc_ref[0, 0]
    n_c = kc.shape[0]
    s1 = _dot_nt(qn, kc).reshape(HPG, tq, n_c)
    pos_c = pos0 + lax.broadcasted_iota(jnp.int32, (tq, n_c), 0)
    cend = lax.broadcasted_iota(jnp.int32, (tq, n_c), 1) * S_CMP + (L_CMP - 1)
    valid = cend <= pos_c
    s1 = jnp.where(valid[None], s1, NEG)
    e1 = jnp.exp(s1 - jnp.max(s1, axis=-1, keepdims=True)) * valid.astype(F32)[None]
    p1 = e1 / jnp.maximum(jnp.sum(e1, axis=-1, keepdims=True), 1e-30)
    p1b = p1.astype(BF16)
    o_c = _dot(p1b.reshape(rows, n_c), vc_ref[0, 0]).reshape(HPG, tq, HEAD_DIM)

    cover = cover_ref[...]
    n_s = cover.shape[1]
    imp = _dot(p1b[0], cover)
    for hh in range(1, HPG):
        imp = imp + _dot(p1b[hh], cover)
    blk = lax.broadcasted_iota(jnp.int32, (tq, n_s), 1)
    cur = (pos0 + lax.broadcasted_iota(jnp.int32, (tq, n_s), 0)) // L_SLC
    vis = blk <= cur
    forced = vis & ((blk == 0) | (blk >= cur - 1))
    score = jnp.where(forced, 1e9, jnp.where(vis, imp, -1.0))
    rank = jnp.zeros((tq, n_s), F32)
    for i in range(n_s):
        c = score[:, i:i + 1]
        beats = (c > score) | ((c == score) & (blk > i))
        rank = rank + beats.astype(F32)
    selb = (vis & (rank < N_SEL)).astype(BF16)

    m_sc[...] = jnp.full(m_sc.shape, NEG, F32)
    l_sc[...] = jnp.zeros(l_sc.shape, F32)
    acc_sc[...] = jnp.zeros(acc_sc.shape, F32)

    def sel_body(kt, carry):
        k0 = pl.multiple_of(kt * tk, tk)
        s = _dot_nt(qr, ks_ref[0, 0, pl.ds(k0, tk), :]).reshape(HPG, tq, tk)
        picked = _dot(selb, e_ref[kt])
        kpos = k0 + lax.broadcasted_iota(jnp.int32, (tq, tk), 1)
        qpos = pos0 + lax.broadcasted_iota(jnp.int32, (tq, tk), 0)
        ok = (picked > 0.5) & (kpos <= qpos)
        s = s + jnp.where(ok, 0.0, NEG)[None]
        _online_update(s, vs_ref[0, 0, pl.ds(k0, tk), :], m_sc, l_sc, acc_sc)
        return carry

    lax.fori_loop(0, (pos0 + tq - 1) // tk + 1, sel_body, 0)
    o_s = acc_sc[...] / l_sc[...]

    m_sc[...] = jnp.full(m_sc.shape, NEG, F32)
    l_sc[...] = jnp.zeros(l_sc.shape, F32)
    acc_sc[...] = jnp.zeros(acc_sc.shape, F32)
    kt_hi = (pos0 + tq - 1) // tkw
    kt_lo = jnp.maximum(pos0 - (WINDOW - 1), 0) // tkw

    def win_body(d, carry):
        k0 = pl.multiple_of((kt_hi - d) * tkw, tkw)
        s = _dot_nt(qr, kw_ref[0, 0, pl.ds(k0, tkw), :]).reshape(HPG, tq, tkw)
        kpos = k0 + lax.broadcasted_iota(jnp.int32, (tq, tkw), 1)
        qpos = pos0 + lax.broadcasted_iota(jnp.int32, (tq, tkw), 0)
        ok = (kpos <= qpos) & (kpos > qpos - WINDOW)
        s = s + jnp.where(ok, 0.0, NEG)[None]
        _online_update(s, vw_ref[0, 0, pl.ds(k0, tkw), :], m_sc, l_sc, acc_sc)
        return carry

    lax.fori_loop(0, kt_hi - kt_lo + 1, win_body, 0)
    o_w = acc_sc[...] / l_sc[...]

    gate = jax.nn.sigmoid(gate_ref[...])
    for hh in range(HPG):
        def gcol(br):
            lo = 3 * hh + br
            hi = 3 * (HPG + hh) + br
            return jnp.where(g == 0, gate[:, lo:lo + 1], gate[:, hi:hi + 1])
        o = gcol(0) * o_c[hh] + gcol(1) * o_s[hh] + gcol(2) * o_w[hh]
        o_ref[0, 0, hh] = o.astype(o_ref.dtype)


def nsa_prompt(qn, qr, kc, vc, ks, vs, kw, vw, gates, bsz, t, tq=128, tk=256, tkw=128):
    n_c = kc.shape[2]
    n_s = t // L_SLC
    c0 = np.arange(n_c)[:, None] * S_CMP
    s0 = np.arange(n_s)[None, :] * L_SLC
    cover = (np.maximum(np.minimum(c0 + L_CMP, s0 + L_SLC) - np.maximum(c0, s0), 0) / L_CMP).astype(np.float32)
    expand = (np.arange(t)[None, :] // L_SLC == np.arange(n_s)[:, None]).astype(np.float32)
    expand = expand.reshape(n_s, t // tk, tk).transpose(1, 0, 2)
    qspec = pl.BlockSpec((1, 1, HPG, tq, HEAD_DIM), lambda b, g, i: (b, g, 0, i, 0))
    kspec = pl.BlockSpec((1, 1, t, HEAD_DIM), lambda b, g, i: (b, g, 0, 0))
    nq = t // tq
    kern = functools.partial(_nsa_prompt_kernel, tq=tq, tk=tk, tkw=tkw)
    return pl.pallas_call(
        kern,
        grid=(bsz, KV_HEADS, nq),
        in_specs=[qspec, qspec,
                  pl.BlockSpec((1, 1, n_c, HEAD_DIM), lambda b, g, i: (0, b * KV_HEADS + g, 0, 0)),
                  pl.BlockSpec((1, 1, n_c, HEAD_DIM), lambda b, g, i: (1, b * KV_HEADS + g, 0, 0)),
                  kspec, kspec, kspec, kspec,
                  pl.BlockSpec((n_c, n_s), lambda b, g, i: (0, 0)),
                  pl.BlockSpec((t // tk, n_s, tk), lambda b, g, i: (0, 0, 0)),
                  pl.BlockSpec((tq, 128), lambda b, g, i: (b * nq + i, 0))],
        out_specs=qspec,
        out_shape=jax.ShapeDtypeStruct((bsz, KV_HEADS, HPG, t, HEAD_DIM), BF16),
        scratch_shapes=[pltpu.VMEM((HPG, tq, 1), F32), pltpu.VMEM((HPG, tq, 1), F32),
                        pltpu.VMEM((HPG, tq, HEAD_DIM), F32)],
        compiler_params=_cparams(("parallel", "parallel", "arbitrary")),
        name="nsa_prompt",
    )(qn, qr, kc, kc, ks, vs, kw, vw, jnp.asarray(cover, BF16), jnp.asarray(expand, BF16), gates)


def _ssd_kernel(d_ref, x_ref, bt_ref, c_ref, acol_ref, arow_ref, dt_ref, s0_ref, y_ref, sout_ref, s_sc, *, cl):
    ci = pl.program_id(1)

    @pl.when(ci == 0)
    def _():
        s_sc[...] = s0_ref[0]

    li = lax.broadcasted_iota(jnp.int32, (cl, cl), 0)
    si = lax.broadcasted_iota(jnp.int32, (cl, cl), 1)
    causal = li >= si
    acum_col = _dot_hi(causal.astype(F32), acol_ref[0])
    acum_row = _dot_hi(arow_ref[0], (li <= si).astype(F32))
    dt = dt_ref[0]
    for g in range(SSM_GROUPS):
        cg = c_ref[0, g].astype(BF16)
        btg = bt_ref[0, g]
        cb = _dot(cg, btg.astype(BF16))
        for hh in range(SSM_HEADS // SSM_GROUPS):
            h = g * (SSM_HEADS // SSM_GROUPS) + hh
            ac = acum_col[:, h:h + 1]
            ar = acum_row[h:h + 1, :]
            decay = jnp.exp(jnp.where(causal, ac - ar, NEG))
            xs = x_ref[0, h]
            xd = (xs * dt[:, h:h + 1]).astype(BF16)
            st = s_sc[h]
            y = _dot((cb * decay).astype(BF16), xd) + _dot(cg, st.astype(BF16)) * jnp.exp(ac)
            y_ref[0, h] = y + d_ref[h] * xs
            a_last = ar[:, cl - 1:cl]
            snew = _dot((btg * jnp.exp(a_last - ar)).astype(BF16), xd)
            s_sc[h] = st * jnp.exp(a_last) + snew

    @pl.when(ci == pl.num_programs(1) - 1)
    def _():
        sout_ref[0] = s_sc[...]


def ssd_scan(ssm_d, x_hm, b_t, c, a_col, a_row, dt_col, s0_t, cl):
    bsz, nh, t, p = x_hm.shape
    n = D_STATE
    nc = t // cl
    kern = functools.partial(_ssd_kernel, cl=cl)
    return pl.pallas_call(
        kern,
        grid=(bsz, nc),
        in_specs=[pl.BlockSpec(memory_space=pltpu.SMEM),
                  pl.BlockSpec((1, nh, cl, p), lambda b, c_: (b, 0, c_, 0)),
                  pl.BlockSpec((1, SSM_GROUPS, n, cl), lambda b, c_: (b, 0, 0, c_)),
                  pl.BlockSpec((1, SSM_GROUPS, cl, n), lambda b, c_: (b, 0, c_, 0)),
                  pl.BlockSpec((1, cl, 128), lambda b, c_: (b, c_, 0)),
                  pl.BlockSpec((1, 8, cl), lambda b, c_: (b, 0, c_)),
                  pl.BlockSpec((1, cl, 128), lambda b, c_: (b, c_, 0)),
                  pl.BlockSpec((1, nh, n, p), lambda b, c_: (b, 0, 0, 0))],
        out_specs=[pl.BlockSpec((1, nh, cl, p), lambda b, c_: (b, 0, c_, 0)),
                   pl.BlockSpec((1, nh, n, p), lambda b, c_: (b, 0, 0, 0))],
        out_shape=[jax.ShapeDtypeStruct((bsz, nh, t, p), F32), jax.ShapeDtypeStruct((bsz, nh, n, p), F32)],
        scratch_shapes=[pltpu.VMEM((nh, n, p), F32)],
        compiler_params=_cparams(("parallel", "arbitrary")),
        name="ssd_scan",
    )(ssm_d, x_hm, b_t, c, a_col, a_row, dt_col, s0_t)


def _rmsnorm(x, g):
    return x * lax.rsqrt(jnp.mean(x * x, axis=-1, keepdims=True) + EPS) * g


def _rope(x, pos):
    half = HEAD_DIM // 2
    inv = ROPE_THETA ** (-jnp.arange(half, dtype=F32) / half)
    ang = pos.astype(F32)[:, None] * inv[None, :]
    cos, sin = jnp.cos(ang)[:, None, :], jnp.sin(ang)[:, None, :]
    x1, x2 = x[..., :half], x[..., half:]
    return jnp.concatenate([x1 * cos - x2 * sin, x2 * cos + x1 * sin], axis=-1)


def _masked_softmax(s, mask):
    s = jnp.where(mask, s, NEG)
    e = jnp.exp(s - jnp.max(s, axis=-1, keepdims=True)) * mask
    return e / jnp.maximum(jnp.sum(e, axis=-1, keepdims=True), 1e-30)


def _cover_matrix(n_cmp, n_slc):
    c0 = jnp.arange(n_cmp)[:, None] * S_CMP
    s0 = jnp.arange(n_slc)[None, :] * L_SLC
    shared = jnp.minimum(c0 + L_CMP, s0 + L_SLC) - jnp.maximum(c0, s0)
    return jnp.maximum(shared, 0).astype(F32) / L_CMP


def _ssd_inputs(xbc_all, dt_raw, w, t_pad):
    bsz = xbc_all.shape[0]
    length = xbc_all.shape[1] - (CONV_W - 1)
    conv = w['conv_b'] + xbc_all[:, 0:length] * w['conv_w'][0]
    for j in range(1, CONV_W):
        conv = conv + xbc_all[:, j:j + length] * w['conv_w'][j]
    act = jax.nn.silu(conv)
    dt = jax.nn.softplus(dt_raw + w['dt_bias'])
    a = -jnp.exp(w['a_log']) * dt
    pad = t_pad - length
    if pad:
        act = jnp.pad(act, ((0, 0), (0, pad), (0, 0)))
        dt = jnp.pad(dt, ((0, 0), (0, pad), (0, 0)))
        a = jnp.pad(a, ((0, 0), (0, pad), (0, 0)))
    xs = act[..., :D_INNER].reshape(bsz, t_pad, SSM_HEADS, SSM_HEAD_DIM).transpose(0, 2, 1, 3)
    b_in = act[..., D_INNER:D_INNER + SSM_GROUPS * D_STATE].reshape(bsz, t_pad, SSM_GROUPS, D_STATE)
    c_in = act[..., D_INNER + SSM_GROUPS * D_STATE:].reshape(bsz, t_pad, SSM_GROUPS, D_STATE)
    lane_pad = ((0, 0), (0, 0), (0, 128 - SSM_HEADS))
    return (xs, b_in.transpose(0, 2, 3, 1), c_in.transpose(0, 2, 1, 3), jnp.pad(a, lane_pad),
            a.transpose(0, 2, 1), jnp.pad(dt, lane_pad))


def _moe(h, w):
    n_tok = h.shape[0]
    f = _rmsnorm(h, w['ffn_norm'])
    hi = lax.Precision.HIGHEST
    pg = jax.nn.softmax(jnp.dot(f, w['w_rg'], precision=hi) + w['b_rg'], axis=-1)
    g_sel = jnp.argmax(pg, axis=-1)
    g_w = jnp.max(pg, axis=-1)
    el = (jnp.dot(f, w['w_re'], precision=hi) + w['b_re']).reshape(n_tok, N_EXPERT_GROUPS, EXPERTS_PER_GROUP)
    el = jnp.take_along_axis(el, g_sel[:, None, None], axis=1)[:, 0]
    top_p, top_i = lax.top_k(jax.nn.softmax(el, axis=-1), TOP_K)
    wts = g_w[:, None] * top_p / jnp.sum(top_p, axis=-1, keepdims=True)
    eid = (g_sel[:, None] * EXPERTS_PER_GROUP + top_i).reshape(-1).astype(jnp.int32)
    n_asg = eid.shape[0]
    onehot = (eid[:, None] == jnp.arange(N_EXPERTS, dtype=jnp.int32)[None, :]).astype(jnp.int32)
    within = jnp.take_along_axis(jnp.cumsum(onehot, axis=0), eid[:, None], axis=1)[:, 0] - 1
    sizes = jnp.sum(onehot, axis=0)
    padded = (sizes + MOE_ROWS - 1) // MOE_ROWS * MOE_ROWS
    pend = jnp.cumsum(padded)
    dest = (pend - padded)[eid] + within
    n_blk = -(-n_asg // MOE_ROWS) + N_EXPERTS
    src = jnp.zeros((n_blk * MOE_ROWS,), jnp.int32).at[dest].set(jnp.arange(n_asg, dtype=jnp.int32) // TOP_K)
    xpad = f.astype(BF16)[src]
    blk_e = jnp.minimum(jnp.searchsorted(pend, jnp.arange(n_blk, dtype=jnp.int32) * MOE_ROWS, side='right'),
                        N_EXPERTS - 1).astype(jnp.int32)
    n_used = (pend[-1] // MOE_ROWS).astype(jnp.int32).reshape(1)
    ypad = moe_experts(xpad, blk_e, n_used, w['wg_b'], w['wu_b'], w['wd_b'])
    y = ypad[dest].reshape(n_tok, TOP_K, D_MODEL) * wts[:, :, None]
    return h + jnp.sum(y, axis=1)


def _token_tail(x2, mix_in, p2, w, tm):
    h = outproj(x2, mix_in[0], mix_in[1], w['w_out_b'], tm)
    h = _moe(h, w)
    return ple(h, p2, w['wpg_b'], w['wpp_b'], w['ple_norm'], tm)


def _ssd_finish(y_hm, z, w, length):
    bsz = y_hm.shape[0]
    y = y_hm[:, :, :length].transpose(0, 2, 1, 3).reshape(bsz, length, D_INNER)
    gated = y * jax.nn.silu(z)
    return _rmsnorm(gated, w['ssm_norm']).astype(BF16)


def _prompt_group(x, p, w):
    bsz, t, _ = x.shape
    m = bsz * t
    q, kvs, z, xbc, sm = rms_inproj(x.reshape(m, D_MODEL), w['attn_norm'], w['w_in_r'], 512)
    pos = jnp.arange(t)
    scale = HEAD_DIM ** -0.5
    qn = _rmsnorm(q.reshape(bsz, t, N_HEADS, HEAD_DIM), w['q_norm'])
    qr = _rope(qn, pos)

    def heads_major(a):
        return (a * scale).astype(BF16).reshape(bsz, t, KV_HEADS, HPG, HEAD_DIM).transpose(0, 2, 3, 1, 4)

    kvs = kvs.reshape(bsz, t, 6, KV_HEADS, HEAD_DIM)
    k_c = _rmsnorm(kvs[:, :, 0], w['k_norm'][0])
    k_s = _rope(_rmsnorm(kvs[:, :, 2], w['k_norm'][1]), pos)
    k_w = _rope(_rmsnorm(kvs[:, :, 4], w['k_norm'][2]), pos)
    kv_new = jnp.stack([k_c, kvs[:, :, 1], k_s, kvs[:, :, 3]], axis=2)
    win_new = jnp.stack([k_w, kvs[:, :, 5]], axis=2)

    def group_major(a):
        return a.astype(BF16).transpose(0, 2, 1, 3)

    n_seg = t // S_CMP
    segs = jnp.stack([group_major(k_c), group_major(kvs[:, :, 1])]).reshape(2, bsz * KV_HEADS, n_seg, S_CMP * HEAD_DIM)
    kvc = compress(segs, w['cmp_pe'], w['cmp_w1'], w['cmp_w2'])
    attn = nsa_prompt(heads_major(qn), heads_major(qr), kvc, kvc, group_major(k_s), group_major(kvs[:, :, 3]),
                      group_major(k_w), group_major(kvs[:, :, 5]), sm, bsz, t)
    attn = attn.transpose(0, 3, 1, 2, 4).reshape(m, ATT_W)

    xbc3 = xbc.reshape(bsz, t, CONV_DIM)
    xbc_all = jnp.concatenate([jnp.zeros((bsz, CONV_W - 1, CONV_DIM), F32), xbc3], axis=1)
    dt_raw = sm[:, N_GATES:N_GATES + SSM_HEADS].reshape(bsz, t, SSM_HEADS)
    xs, b_t, c_in, a_col, a_row, dt_col = _ssd_inputs(xbc_all, dt_raw, w, t)
    s0 = jnp.zeros((bsz, SSM_HEADS, D_STATE, SSM_HEAD_DIM), F32)
    y_hm, s_t = ssd_scan(w['ssm_d'], xs, b_t, c_in, a_col, a_row, dt_col, s0, SSM_CHUNK)
    ssd = _ssd_finish(y_hm, z.reshape(bsz, t, D_INNER), w, t).reshape(m, D_INNER)

    y = _token_tail(x.reshape(m, D_MODEL), (attn, ssd), p.reshape(m, PLE_DIM), w, 512)
    keep = min(WINDOW, t)
    return (y.reshape(bsz, t, D_MODEL), kv_new, win_new[:, t - keep:], s_t.transpose(0, 1, 3, 2),
            xbc_all[:, xbc_all.shape[1] - (CONV_W - 1):])


def _sample_attention(qn, qr, kvc, kv_new, win_all, pool, page_table, q_off, gl):
    bt, lq = qn.shape[:2]
    scale = HEAD_DIM ** -0.5
    t_kv = q_off + lq
    n_cmp = (t_kv - L_CMP) // S_CMP + 1
    n_slc = -(-t_kv // L_SLC)
    kc = kvc[0].reshape(bt, KV_HEADS, -1, HEAD_DIM)[:, :, :n_cmp].astype(F32)
    vc = kvc[1].reshape(bt, KV_HEADS, -1, HEAD_DIM)[:, :, :n_cmp].astype(F32)
    qp = qn.reshape(bt, lq, KV_HEADS, HPG, HEAD_DIM)
    qrr = qr.reshape(bt, lq, KV_HEADS, HPG, HEAD_DIM)
    pos = q_off + jnp.arange(lq)
    cmp_end = jnp.arange(n_cmp) * S_CMP + (L_CMP - 1)
    s1 = jnp.einsum('bqghd,bgnd->bghqn', qp, kc, preferred_element_type=F32) * scale
    p1 = _masked_softmax(s1, cmp_end[None, :] <= pos[:, None])
    o_c = jnp.einsum('bghqn,bgnd->bqghd', p1, vc, preferred_element_type=F32)
    imp = jnp.einsum('bghqn,nj->bgqj', p1, _cover_matrix(n_cmp, n_slc), preferred_element_type=F32)
    blk = jnp.arange(n_slc)
    cur = pos // L_SLC
    vis = blk[None, :] <= cur[:, None]
    forced = vis & ((blk[None, :] == 0) | (blk[None, :] >= cur[:, None] - 1))
    score = jnp.where(forced, 1e9, jnp.where(vis, imp, -1.0))
    top_s, idx = lax.top_k(score, min(N_SEL, n_slc))
    n_past_blk = q_off // L_SLC
    per_page = PAGE_SIZE // L_SLC
    pool6 = pool.reshape(pool.shape[0], per_page, L_SLC, 4, KV_HEADS, HEAD_DIM)
    idx_c = jnp.minimum(idx, n_past_blk - 1)
    bi = jnp.arange(bt)[:, None, None, None]
    gi = jnp.arange(KV_HEADS)[None, :, None, None]
    pages = page_table[bi, idx_c // per_page]
    kg = pool6[pages, idx_c % per_page, :, 2, gi]
    vg = pool6[pages, idx_c % per_page, :, 3, gi]
    new_k = jnp.pad(kv_new[:, :, 2], ((0, 0), (0, L_SLC - lq), (0, 0), (0, 0))).transpose(0, 2, 1, 3)
    new_v = jnp.pad(kv_new[:, :, 3], ((0, 0), (0, L_SLC - lq), (0, 0), (0, 0))).transpose(0, 2, 1, 3)
    is_new = (idx >= n_past_blk)[..., None, None]
    kg = jnp.where(is_new, new_k[:, :, None, None], kg)
    vg = jnp.where(is_new, new_v[:, :, None, None], vg)
    k_sel = idx.shape[-1]
    kg = kg.reshape(bt, KV_HEADS, lq, k_sel * L_SLC, HEAD_DIM)
    vg = vg.reshape(bt, KV_HEADS, lq, k_sel * L_SLC, HEAD_DIM)
    kpos = idx[..., None] * L_SLC + jnp.arange(L_SLC)
    ok = ((top_s > -0.5)[..., None] & (kpos <= pos[None, None, :, None, None])).reshape(bt, KV_HEADS, lq, k_sel * L_SLC)
    s2 = jnp.einsum('bqghd,bgqmd->bghqm', qrr, kg, preferred_element_type=F32) * scale
    p2 = _masked_softmax(s2, ok[:, :, None])
    o_s = jnp.einsum('bghqm,bgqmd->bqghd', p2, vg, preferred_element_type=F32)
    kw, vw = win_all[:, :, 0], win_all[:, :, 1]
    n_w = win_all.shape[1]
    kpos_w = q_off + lq - n_w + jnp.arange(n_w)
    m3 = (kpos_w[None, :] <= pos[:, None]) & (kpos_w[None, :] > pos[:, None] - WINDOW) & (kpos_w >= 0)[None, :]
    s3 = jnp.einsum('bqghd,bkgd->bghqk', qrr, kw, preferred_element_type=F32) * scale
    p3 = _masked_softmax(s3, m3)
    o_w = jnp.einsum('bghqk,bkgd->bqghd', p3, vw, preferred_element_type=F32)
    g = jax.nn.sigmoid(gl).reshape(bt, lq, N_HEADS, 3, 1)
    merge = lambda o: o.reshape(bt, lq, N_HEADS, HEAD_DIM)
    attn = g[:, :, :, 0] * merge(o_c) + g[:, :, :, 1] * merge(o_s) + g[:, :, :, 2] * merge(o_w)
    return attn.astype(BF16).reshape(bt * lq, ATT_W)


def _sample_group(x, p, pool, page_table, cache_win, state_ssm, state_conv, w):
    bsz, lq, _ = x.shape
    m = bsz * lq
    q_off = page_table.shape[1] * PAGE_SIZE
    q, kvs, z, xbc, sm = rms_inproj(x.reshape(m, D_MODEL), w['attn_norm'], w['w_in_r'], m)
    pos = q_off + jnp.arange(lq)
    qn = _rmsnorm(q.reshape(bsz, lq, N_HEADS, HEAD_DIM), w['q_norm'])
    qr = _rope(qn, pos)
    kvs = kvs.reshape(bsz, lq, 6, KV_HEADS, HEAD_DIM)
    k_c = _rmsnorm(kvs[:, :, 0], w['k_norm'][0])
    k_s = _rope(_rmsnorm(kvs[:, :, 2], w['k_norm'][1]), pos)
    k_w = _rope(_rmsnorm(kvs[:, :, 4], w['k_norm'][2]), pos)
    kv_new = jnp.stack([k_c, kvs[:, :, 1], k_s, kvs[:, :, 3]], axis=2)
    win_new = jnp.stack([k_w, kvs[:, :, 5]], axis=2)
    win_cat = jnp.concatenate([cache_win, win_new], axis=1)

    n_seg = q_off // S_CMP
    past01 = pool[:, :, 0:2][page_table]
    segs = past01.astype(BF16).transpose(3, 0, 4, 1, 2, 5).reshape(2, bsz * KV_HEADS, n_seg, S_CMP * HEAD_DIM)
    kvc = compress(segs, w['cmp_pe'], w['cmp_w1'], w['cmp_w2'])
    attn = _sample_attention(qn, qr, kvc, kv_new, win_cat, pool, page_table, q_off, sm[:, :N_GATES])

    xbc_all = jnp.concatenate([state_conv, xbc.reshape(bsz, lq, CONV_DIM)], axis=1)
    dt_raw = sm[:, N_GATES:N_GATES + SSM_HEADS].reshape(bsz, lq, SSM_HEADS)
    xs, b_t, c_in, a_col, a_row, dt_col = _ssd_inputs(xbc_all, dt_raw, w, SSM_CHUNK)
    y_hm, s_t = ssd_scan(w['ssm_d'], xs, b_t, c_in, a_col, a_row, dt_col, state_ssm.transpose(0, 1, 3, 2), SSM_CHUNK)
    ssd = _ssd_finish(y_hm, z.reshape(bsz, lq, D_INNER), w, lq).reshape(m, D_INNER)

    y = _token_tail(x.reshape(m, D_MODEL), (attn, ssd), p.reshape(m, PLE_DIM), w, m)
    keep = cache_win.shape[1]
    return (y.reshape(bsz, lq, D_MODEL), kv_new, win_cat[:, win_cat.shape[1] - keep:], s_t.transpose(0, 1, 3, 2),
            xbc_all[:, xbc_all.shape[1] - (CONV_W - 1):])


def kernel(x_prompt, x_sample, cache_kv, cache_win, state_ssm, state_conv, page_table, p_prompt, p_sample,
           w_in, w_out, q_norm, k_norm, cmp_pe, cmp_w1, cmp_w2, conv_w, conv_b, dt_bias, a_log, ssm_d, ssm_norm,
           attn_norm, ffn_norm, w_rg, b_rg, w_re, b_re, w_gate, w_up, w_down, w_ple_proj, ple_norm, w_ple_gate):
    depth = w_in.shape[0]
    hp, hs = x_prompt, x_sample
    outs = [[] for _ in range(8)]
    cuts = np.cumsum((ATT_W, 6 * KV_HEADS * HEAD_DIM, N_GATES, D_INNER, CONV_DIM, SSM_HEADS))
    for l in range(depth):
        wi = w_in[l]
        w_in_r = jnp.concatenate(
            [wi[:, :cuts[1]], wi[:, cuts[2]:cuts[3]], wi[:, cuts[3]:cuts[4]], wi[:, cuts[1]:cuts[2]],
             wi[:, cuts[4]:cuts[5]], jnp.zeros((D_MODEL, C_SM - N_GATES - SSM_HEADS), F32)], axis=1).astype(BF16)
        w = dict(w_in_r=w_in_r, w_out_b=w_out[l].astype(BF16), q_norm=q_norm[l], k_norm=k_norm[l],
                 cmp_pe=cmp_pe[l], cmp_w1=cmp_w1[l], cmp_w2=cmp_w2[l], conv_w=conv_w[l], conv_b=conv_b[l],
                 dt_bias=dt_bias[l], a_log=a_log[l], ssm_d=ssm_d[l], ssm_norm=ssm_norm[l], attn_norm=attn_norm[l],
                 ffn_norm=ffn_norm[l], w_rg=w_rg[l], b_rg=b_rg[l], w_re=w_re[l], b_re=b_re[l],
                 wg_b=w_gate[l].astype(BF16), wu_b=w_up[l].astype(BF16), wd_b=w_down[l].astype(BF16),
                 wpp_b=w_ple_proj[l].astype(BF16), ple_norm=ple_norm[l], wpg_b=w_ple_gate[l].astype(BF16))
        hp, *rest_p = _prompt_group(hp, p_prompt[l], w)
        hs, *rest_s = _sample_group(hs, p_sample[l], cache_kv[l], page_table, cache_win[l], state_ssm[l],
                                    state_conv[l], w)
        for j in range(4):
            outs[2 * j].append(rest_p[j])
            outs[2 * j + 1].append(rest_s[j])
    return (hp, hs) + tuple(jnp.stack(o) for o in outs)
```

```python
import functools
import math

import numpy as np
import jax
import jax.numpy as jnp
from jax import lax
from jax.experimental import pallas as pl
from jax.experimental.pallas import tpu as pltpu

F32 = jnp.float32
BF16 = jnp.bfloat16

D_MODEL = 1024
PAGE_SIZE = 128
N_HEADS = 8
HEAD_DIM = 64
KV_HEADS = 2
HPG = N_HEADS // KV_HEADS
ATT_W = N_HEADS * HEAD_DIM
L_CMP = 32
S_CMP = 16
L_SLC = 64
N_SEL = 16
WINDOW = 512
CMP_HID = 64
ROPE_THETA = 10000.0
SSM_HEADS = 8
SSM_HEAD_DIM = 64
D_INNER = SSM_HEADS * SSM_HEAD_DIM
SSM_GROUPS = 2
D_STATE = 64
CONV_W = 4
CONV_DIM = D_INNER + 2 * SSM_GROUPS * D_STATE
SSM_CHUNK = 128
N_EXPERT_GROUPS = 4
EXPERTS_PER_GROUP = 8
N_EXPERTS = N_EXPERT_GROUPS * EXPERTS_PER_GROUP
TOP_K = 2
D_EXPERT = 512
PLE_DIM = 256
EPS = 1e-6
N_GATES = 3 * N_HEADS
C_Q, C_KV, C_Z, C_XBC, C_SM = 512, 768, 512, 768, 128
D_IN_PAD = C_Q + C_KV + C_Z + C_XBC + C_SM

NEG = -1e30
VMEM_LIMIT = 48 * 1024 * 1024
MOE_ROWS = 256


def _cparams(sem):
    return pltpu.CompilerParams(dimension_semantics=sem, vmem_limit_bytes=VMEM_LIMIT)


def _dot(a, b):
    return jnp.dot(a, b, preferred_element_type=F32)


def _dot_nt(a, b):
    return lax.dot_general(a, b, (((1,), (1,)), ((), ())), preferred_element_type=F32)


def _dot_hi(a, b):
    return jnp.dot(a, b, preferred_element_type=F32, precision=lax.Precision.HIGHEST)


def _rms_inproj_kernel(x_ref, g_ref, w_ref, q_ref, kv_ref, z_ref, xbc_ref, sm_ref):
    x = x_ref[...]
    y = x * lax.rsqrt(jnp.mean(x * x, axis=-1, keepdims=True) + EPS) * g_ref[...]
    yb = y.astype(BF16)
    c0 = 0
    for ref, width in ((q_ref, C_Q), (kv_ref, C_KV), (z_ref, C_Z), (xbc_ref, C_XBC), (sm_ref, C_SM)):
        ref[...] = _dot(yb, w_ref[:, c0:c0 + width])
        c0 += width


def rms_inproj(x, gain, w_r, tm):
    m = x.shape[0]
    widths = (C_Q, C_KV, C_Z, C_XBC, C_SM)
    return pl.pallas_call(
        _rms_inproj_kernel,
        grid=(m // tm,),
        in_specs=[pl.BlockSpec((tm, D_MODEL), lambda i: (i, 0)),
                  pl.BlockSpec((1, D_MODEL), lambda i: (0, 0)),
                  pl.BlockSpec((D_MODEL, D_IN_PAD), lambda i: (0, 0))],
        out_specs=[pl.BlockSpec((tm, w), lambda i: (i, 0)) for w in widths],
        out_shape=[jax.ShapeDtypeStruct((m, w), F32) for w in widths],
        compiler_params=_cparams(("parallel",)),
        name="rms_inproj",
    )(x, gain.reshape(1, D_MODEL), w_r)


def _outproj_kernel(x_ref, a_ref, s_ref, w_ref, h_ref):
    acc = _dot(a_ref[...], w_ref[0:ATT_W, :]) + _dot(s_ref[...], w_ref[ATT_W:ATT_W + D_INNER, :])
    h_ref[...] = x_ref[...] + acc


def outproj(x, attn, ssd, w_out_b, tm):
    m = x.shape[0]
    return pl.pallas_call(
        _outproj_kernel,
        grid=(m // tm,),
        in_specs=[pl.BlockSpec((tm, D_MODEL), lambda i: (i, 0)),
                  pl.BlockSpec((tm, ATT_W), lambda i: (i, 0)),
                  pl.BlockSpec((tm, D_INNER), lambda i: (i, 0)),
                  pl.BlockSpec((ATT_W + D_INNER, D_MODEL), lambda i: (0, 0))],
        out_specs=pl.BlockSpec((tm, D_MODEL), lambda i: (i, 0)),
        out_shape=jax.ShapeDtypeStruct((m, D_MODEL), F32),
        compiler_params=_cparams(("parallel",)),
        name="outproj",
    )(x, attn, ssd, w_out_b)


def _ple_kernel(h_ref, p_ref, wg_ref, wp_ref, g_ref, o_ref):
    h = h_ref[...]
    gate = jax.nn.sigmoid(_dot(h.astype(BF16), wg_ref[...]))
    e = _dot(p_ref[...].astype(BF16), wp_ref[...])
    e = e * lax.rsqrt(jnp.mean(e * e, axis=-1, keepdims=True) + EPS) * g_ref[...]
    o_ref[...] = h + gate * e


def ple(h, p, wg_b, wp_b, gain, tm):
    m = h.shape[0]
    return pl.pallas_call(
        _ple_kernel,
        grid=(m // tm,),
        in_specs=[pl.BlockSpec((tm, D_MODEL), lambda i: (i, 0)),
                  pl.BlockSpec((tm, PLE_DIM), lambda i: (i, 0)),
                  pl.BlockSpec((D_MODEL, D_MODEL), lambda i: (0, 0)),
                  pl.BlockSpec((PLE_DIM, D_MODEL), lambda i: (0, 0)),
                  pl.BlockSpec((1, D_MODEL), lambda i: (0, 0))],
        out_specs=pl.BlockSpec((tm, D_MODEL), lambda i: (i, 0)),
        out_shape=jax.ShapeDtypeStruct((m, D_MODEL), F32),
        compiler_params=_cparams(("parallel",)),
        name="ple",
    )(h, p, wg_b, wp_b, gain.reshape(1, D_MODEL))


def _moe_kernel(be_ref, nb_ref, x_ref, wg_ref, wu_ref, wd_ref, y_ref):
    @pl.when(pl.program_id(0) < nb_ref[0])
    def _():
        x = x_ref[...]
        a = _dot(x, wg_ref[0])
        hb = (a * jax.nn.sigmoid(a)) * _dot(x, wu_ref[0])
        y_ref[...] = _dot(hb.astype(BF16), wd_ref[0])


def moe_experts(xpad, blk_e, n_used, wg_b, wu_b, wd_b):
    n_blk = xpad.shape[0] // MOE_ROWS
    grid_spec = pltpu.PrefetchScalarGridSpec(
        num_scalar_prefetch=2,
        grid=(n_blk,),
        in_specs=[pl.BlockSpec((MOE_ROWS, D_MODEL), lambda i, be, nb: (i, 0)),
                  pl.BlockSpec((1, D_MODEL, D_EXPERT), lambda i, be, nb: (be[i], 0, 0)),
                  pl.BlockSpec((1, D_MODEL, D_EXPERT), lambda i, be, nb: (be[i], 0, 0)),
                  pl.BlockSpec((1, D_EXPERT, D_MODEL), lambda i, be, nb: (be[i], 0, 0))],
        out_specs=pl.BlockSpec((MOE_ROWS, D_MODEL), lambda i, be, nb: (i, 0)),
    )
    return pl.pallas_call(
        _moe_kernel,
        grid_spec=grid_spec,
        out_shape=jax.ShapeDtypeStruct((n_blk * MOE_ROWS, D_MODEL), F32),
        compiler_params=_cparams(("arbitrary",)),
        name="moe_experts",
    )(blk_e, n_used, xpad, wg_b, wu_b, wd_b)


def _compress_kernel(seg_ref, w1a_ref, w1b_ref, pe_ref, w1_ref, w2_ref, o_ref):
    seg = seg_ref[0, 0]
    n_seg = seg.shape[0]
    hid0 = _dot(pe_ref[0], w1_ref[0])[0:1]
    p0 = _dot(seg, w1a_ref[0])
    p1 = pltpu.roll(_dot(seg, w1b_ref[0]), n_seg - 1, axis=0)
    hid = hid0 + p0 + p1
    act = hid * jax.nn.sigmoid(hid)
    o_ref[0, 0] = _dot(act.astype(BF16), w2_ref[0]).astype(BF16)


def compress(segs, cmp_pe, cmp_w1, cmp_w2):
    _, r, n_seg, k = segs.shape
    w1 = cmp_w1.astype(BF16)
    pe = jnp.broadcast_to(cmp_pe.reshape(2, 1, L_CMP * HEAD_DIM), (2, 8, L_CMP * HEAD_DIM)).astype(BF16)
    return pl.pallas_call(
        _compress_kernel,
        grid=(2, r),
        in_specs=[pl.BlockSpec((1, 1, n_seg, k), lambda s, i: (s, i, 0, 0)),
                  pl.BlockSpec((1, k, CMP_HID), lambda s, i: (s, 0, 0)),
                  pl.BlockSpec((1, k, CMP_HID), lambda s, i: (s, 1, 0)),
                  pl.BlockSpec((1, 8, 2 * k), lambda s, i: (s, 0, 0)),
                  pl.BlockSpec((1, 2 * k, CMP_HID), lambda s, i: (s, 0, 0)),
                  pl.BlockSpec((1, CMP_HID, HEAD_DIM), lambda s, i: (s, 0, 0))],
        out_specs=pl.BlockSpec((1, 1, n_seg, HEAD_DIM), lambda s, i: (s, i, 0, 0)),
        out_shape=jax.ShapeDtypeStruct((2, r, n_seg, HEAD_DIM), BF16),
        compiler_params=_cparams(("parallel", "parallel")),
        name="compress",
    )(segs, w1, w1, pe, w1, cmp_w2.astype(BF16))


def _flash_step(s_t, v_t, m_sc, l_sc, acc_sc):
    m_prev = m_sc[...]
    m_new = jnp.maximum(m_prev, jnp.max(s_t, axis=0, keepdims=True))
    alpha = jnp.exp(m_prev - m_new)
    p = jnp.exp(s_t - m_new)
    l_sc[...] = alpha * l_sc[...] + jnp.sum(p, axis=0, keepdims=True)
    acc_sc[...] = alpha * acc_sc[...] + _dot(v_t, p.astype(BF16))
    m_sc[...] = m_new


def _flash_reset(m_sc, l_sc, acc_sc):
    m_sc[...] = jnp.full(m_sc.shape, NEG, F32)
    l_sc[...] = jnp.zeros(l_sc.shape, F32)
    acc_sc[...] = jnp.zeros(acc_sc.shape, F32)


def _nsa_prompt_kernel(qn_ref, qr_ref, kc_ref, vct_ref, ks_ref, vst_ref, kw_ref, vwt_ref, covert_ref, gate_ref,
                       o_ref, qa_sc, m_sc, l_sc, acc_sc, out_sc, *, tq, tk, tkw):
    qi = pl.program_id(2)
    pos0 = qi * tq
    w = HPG * tq

    def positions(rows, base):
        kpos = base + lax.broadcasted_iota(jnp.int32, (rows, w), 0)
        qpos = pos0 + (lax.broadcasted_iota(jnp.int32, (rows, w), 1) & (tq - 1))
        return kpos, qpos

    gate = jax.nn.sigmoid(gate_ref[0, 0, 0])

    kc = kc_ref[0, 0]
    n_c = kc.shape[0]
    cidx, qpos_c = positions(n_c, 0)
    valid = cidx * S_CMP + (L_CMP - 1) <= qpos_c
    s1 = jnp.where(valid, _dot(kc, qn_ref[0, 0, 0]), NEG)
    e1 = jnp.exp(s1 - jnp.max(s1, axis=0, keepdims=True)) * valid.astype(F32)
    p1b = (e1 * (1.0 / jnp.maximum(jnp.sum(e1, axis=0, keepdims=True), 1e-30))).astype(BF16)
    out_sc[...] = gate[0:1] * _dot(vct_ref[0, 0], p1b)

    p1_stack = jnp.concatenate([p1b[:, hh * tq:(hh + 1) * tq] for hh in range(HPG)], axis=0)
    imp = _dot(covert_ref[...], p1_stack)
    n_s = imp.shape[0]
    blk = lax.broadcasted_iota(jnp.int32, (n_s, tq), 0)
    cur = (pos0 + lax.broadcasted_iota(jnp.int32, (n_s, tq), 1)) // L_SLC
    vis = blk <= cur
    forced = vis & ((blk == 0) | (blk >= cur - 1))
    score = jnp.where(forced, 1e9, jnp.where(vis, imp, -1.0))
    rank = jnp.zeros((n_s, tq), F32)
    for i in range(n_s):
        c = score[i:i + 1, :]
        beats = (c > score) | ((c == score) & (blk > i))
        rank = rank + beats.astype(F32)
    sel_bias = jnp.where(vis & (rank < N_SEL), 0.0, NEG).astype(BF16)
    qa_sc[0:HEAD_DIM, :] = qr_ref[0, 0, 0]
    qa_sc[HEAD_DIM:HEAD_DIM + n_s, :] = jnp.concatenate([sel_bias] * HPG, axis=1)

    _flash_reset(m_sc, l_sc, acc_sc)
    n_full = pos0 // tk

    def sel_body(kt, carry):
        k0 = pl.multiple_of(kt * tk, tk)
        _flash_step(_dot(ks_ref[0, 0, pl.ds(k0, tk), :], qa_sc[...]), vst_ref[0, 0, kt], m_sc, l_sc, acc_sc)
        return carry

    lax.fori_loop(0, n_full, sel_body, 0)
    k0 = pl.multiple_of(n_full * tk, tk)
    kpos, qpos = positions(tk, k0)
    s = _dot(ks_ref[0, 0, pl.ds(k0, tk), :], qa_sc[...])
    _flash_step(jnp.where(kpos <= qpos, s, NEG), vst_ref[0, 0, n_full], m_sc, l_sc, acc_sc)
    out_sc[...] = out_sc[...] + (gate[1:2] * (1.0 / l_sc[...])) * acc_sc[...]

    _flash_reset(m_sc, l_sc, acc_sc)
    qr = qr_ref[0, 0, 0]
    kpos, qpos = positions(tkw, pos0)
    s = _dot(kw_ref[0, 0, pl.ds(pl.multiple_of(pos0, tkw), tkw), :], qr)
    _flash_step(jnp.where(kpos <= qpos, s, NEG), vwt_ref[0, 0, qi], m_sc, l_sc, acc_sc)
    n_inner = WINDOW // tkw - 1

    def win_body(d, carry):
        kt = qi - 1 - d
        k0 = pl.multiple_of(kt * tkw, tkw)
        _flash_step(_dot(kw_ref[0, 0, pl.ds(k0, tkw), :], qr), vwt_ref[0, 0, kt], m_sc, l_sc, acc_sc)
        return carry

    lax.fori_loop(0, jnp.minimum(qi, n_inner), win_body, 0)

    @pl.when(qi > n_inner)
    def _():
        kt = qi - 1 - n_inner
        k0 = pl.multiple_of(kt * tkw, tkw)
        kpos, qpos = positions(tkw, k0)
        s = _dot(kw_ref[0, 0, pl.ds(k0, tkw), :], qr)
        _flash_step(jnp.where(kpos > qpos - WINDOW, s, NEG), vwt_ref[0, 0, kt], m_sc, l_sc, acc_sc)

    o = out_sc[...] + (gate[2:3] * (1.0 / l_sc[...])) * acc_sc[...]
    o_ref[0, 0, 0] = o.astype(o_ref.dtype)


NSA_TQ, NSA_TK, NSA_TKW = 128, 256, 128


def nsa_prompt(qn_t, qr_t, kc, vc_t, ks_aug, vs_t, kw, vw_t, gates_t, bsz, t):
    tq, tk, tkw = NSA_TQ, NSA_TK, NSA_TKW
    assert tq == tkw and tk % tq == 0 and WINDOW % tkw == 0
    n_c = kc.shape[1]
    n_s = t // L_SLC
    w = HPG * tq
    c0 = np.arange(n_c)[None, :] * S_CMP
    s0 = np.arange(n_s)[:, None] * L_SLC
    cover_t = (np.maximum(np.minimum(c0 + L_CMP, s0 + L_SLC) - np.maximum(c0, s0), 0) / L_CMP).astype(np.float32)
    cover_t = np.tile(cover_t, (1, HPG))
    qspec = pl.BlockSpec((1, 1, 1, HEAD_DIM, w), lambda b, g, i: (b, g, i, 0, 0))
    kern = functools.partial(_nsa_prompt_kernel, tq=tq, tk=tk, tkw=tkw)
    return pl.pallas_call(
        kern,
        grid=(bsz, KV_HEADS, t // tq),
        in_specs=[qspec, qspec,
                  pl.BlockSpec((1, 1, n_c, HEAD_DIM), lambda b, g, i: (b, g, 0, 0)),
                  pl.BlockSpec((1, 1, HEAD_DIM, n_c), lambda b, g, i: (b, g, 0, 0)),
                  pl.BlockSpec((1, 1, t, HEAD_DIM + n_s), lambda b, g, i: (b, g, 0, 0)),
                  pl.BlockSpec((1, 1, t // tk, HEAD_DIM, tk), lambda b, g, i: (b, g, 0, 0, 0)),
                  pl.BlockSpec((1, 1, t, HEAD_DIM), lambda b, g, i: (b, g, 0, 0)),
                  pl.BlockSpec((1, 1, t // tkw, HEAD_DIM, tkw), lambda b, g, i: (b, g, 0, 0, 0)),
                  pl.BlockSpec((n_s, HPG * n_c), lambda b, g, i: (0, 0)),
                  pl.BlockSpec((1, 1, 1, 8, w), lambda b, g, i: (b, g, i, 0, 0))],
        out_specs=qspec,
        out_shape=jax.ShapeDtypeStruct((bsz, KV_HEADS, t // tq, HEAD_DIM, w), BF16),
        scratch_shapes=[pltpu.VMEM((HEAD_DIM + n_s, w), BF16), pltpu.VMEM((1, w), F32), pltpu.VMEM((1, w), F32),
                        pltpu.VMEM((HEAD_DIM, w), F32), pltpu.VMEM((HEAD_DIM, w), F32)],
        compiler_params=_cparams(("parallel", "parallel", "arbitrary")),
        name="nsa_prompt",
    )(qn_t, qr_t, kc.reshape(bsz, KV_HEADS, n_c, HEAD_DIM), vc_t.reshape(bsz, KV_HEADS, HEAD_DIM, n_c),
      ks_aug, vs_t, kw, vw_t, jnp.asarray(cover_t, BF16), gates_t)


def _ssd_kernel(d_ref, x_ref, bt_ref, c_ref, acol_ref, arow_ref, dt_ref, s0_ref, y_ref, sout_ref, s_sc, *, cl):
    ci = pl.program_id(1)

    @pl.when(ci == 0)
    def _():
        s_sc[...] = s0_ref[0]

    li = lax.broadcasted_iota(jnp.int32, (cl, cl), 0)
    si = lax.broadcasted_iota(jnp.int32, (cl, cl), 1)
    causal = li >= si
    acum_col = _dot_hi(causal.astype(F32), acol_ref[0])
    acum_row = _dot_hi(arow_ref[0], (li <= si).astype(F32))
    dt = dt_ref[0]
    for g in range(SSM_GROUPS):
        cg = c_ref[0, g].astype(BF16)
        btg = bt_ref[0, g]
        cb = _dot(cg, btg.astype(BF16))
        for hh in range(SSM_HEADS // SSM_GROUPS):
            h = g * (SSM_HEADS // SSM_GROUPS) + hh
            ac = acum_col[:, h:h + 1]
            ar = acum_row[h:h + 1, :]
            decay = jnp.exp(jnp.where(causal, ac - ar, NEG))
            xs = x_ref[0, h]
            xd = (xs * dt[:, h:h + 1]).astype(BF16)
            st = s_sc[h]
            y = _dot((cb * decay).astype(BF16), xd) + _dot(cg, st.astype(BF16)) * jnp.exp(ac)
            y_ref[0, h] = y + d_ref[h] * xs
            a_last = ar[:, cl - 1:cl]
            snew = _dot((btg * jnp.exp(a_last - ar)).astype(BF16), xd)
            s_sc[h] = st * jnp.exp(a_last) + snew

    @pl.when(ci == pl.num_programs(1) - 1)
    def _():
        sout_ref[0] = s_sc[...]


def ssd_scan(ssm_d, x_hm, b_t, c, a_col, a_row, dt_col, s0_t, cl):
    bsz, nh, t, p = x_hm.shape
    n = D_STATE
    nc = t // cl
    kern = functools.partial(_ssd_kernel, cl=cl)
    return pl.pallas_call(
        kern,
        grid=(bsz, nc),
        in_specs=[pl.BlockSpec(memory_space=pltpu.SMEM),
                  pl.BlockSpec((1, nh, cl, p), lambda b, c_: (b, 0, c_, 0)),
                  pl.BlockSpec((1, SSM_GROUPS, n, cl), lambda b, c_: (b, 0, 0, c_)),
                  pl.BlockSpec((1, SSM_GROUPS, cl, n), lambda b, c_: (b, 0, c_, 0)),
                  pl.BlockSpec((1, cl, 128), lambda b, c_: (b, c_, 0)),
                  pl.BlockSpec((1, 8, cl), lambda b, c_: (b, 0, c_)),
                  pl.BlockSpec((1, cl, 128), lambda b, c_: (b, c_, 0)),
                  pl.BlockSpec((1, nh, n, p), lambda b, c_: (b, 0, 0, 0))],
        out_specs=[pl.BlockSpec((1, nh, cl, p), lambda b, c_: (b, 0, c_, 0)),
                   pl.BlockSpec((1, nh, n, p), lambda b, c_: (b, 0, 0, 0))],
        out_shape=[jax.ShapeDtypeStruct((bsz, nh, t, p), F32), jax.ShapeDtypeStruct((bsz, nh, n, p), F32)],
        scratch_shapes=[pltpu.VMEM((nh, n, p), F32)],
        compiler_params=_cparams(("parallel", "arbitrary")),
        name="ssd_scan",
    )(ssm_d, x_hm, b_t, c, a_col, a_row, dt_col, s0_t)


def _rmsnorm(x, g):
    return x * lax.rsqrt(jnp.mean(x * x, axis=-1, keepdims=True) + EPS) * g


def _rope(x, pos):
    half = HEAD_DIM // 2
    inv = ROPE_THETA ** (-jnp.arange(half, dtype=F32) / half)
    ang = pos.astype(F32)[:, None] * inv[None, :]
    cos, sin = jnp.cos(ang)[:, None, :], jnp.sin(ang)[:, None, :]
    x1, x2 = x[..., :half], x[..., half:]
    return jnp.concatenate([x1 * cos - x2 * sin, x2 * cos + x1 * sin], axis=-1)


def _masked_softmax(s, mask):
    s = jnp.where(mask, s, NEG)
    e = jnp.exp(s - jnp.max(s, axis=-1, keepdims=True)) * mask
    return e / jnp.maximum(jnp.sum(e, axis=-1, keepdims=True), 1e-30)


def _cover_matrix(n_cmp, n_slc):
    c0 = jnp.arange(n_cmp)[:, None] * S_CMP
    s0 = jnp.arange(n_slc)[None, :] * L_SLC
    shared = jnp.minimum(c0 + L_CMP, s0 + L_SLC) - jnp.maximum(c0, s0)
    return jnp.maximum(shared, 0).astype(F32) / L_CMP


def _ssd_inputs(xbc_all, dt_raw, w, t_pad):
    bsz = xbc_all.shape[0]
    length = xbc_all.shape[1] - (CONV_W - 1)
    conv = w['conv_b'] + xbc_all[:, 0:length] * w['conv_w'][0]
    for j in range(1, CONV_W):
        conv = conv + xbc_all[:, j:j + length] * w['conv_w'][j]
    act = jax.nn.silu(conv)
    dt = jax.nn.softplus(dt_raw + w['dt_bias'])
    a = -jnp.exp(w['a_log']) * dt
    pad = t_pad - length
    if pad:
        act = jnp.pad(act, ((0, 0), (0, pad), (0, 0)))
        dt = jnp.pad(dt, ((0, 0), (0, pad), (0, 0)))
        a = jnp.pad(a, ((0, 0), (0, pad), (0, 0)))
    xs = act[..., :D_INNER].reshape(bsz, t_pad, SSM_HEADS, SSM_HEAD_DIM).transpose(0, 2, 1, 3)
    b_in = act[..., D_INNER:D_INNER + SSM_GROUPS * D_STATE].reshape(bsz, t_pad, SSM_GROUPS, D_STATE)
    c_in = act[..., D_INNER + SSM_GROUPS * D_STATE:].reshape(bsz, t_pad, SSM_GROUPS, D_STATE)
    lane_pad = ((0, 0), (0, 0), (0, 128 - SSM_HEADS))
    return (xs, b_in.transpose(0, 2, 3, 1), c_in.transpose(0, 2, 1, 3), jnp.pad(a, lane_pad),
            a.transpose(0, 2, 1), jnp.pad(dt, lane_pad))


def _moe(h, w):
    n_tok = h.shape[0]
    f = _rmsnorm(h, w['ffn_norm'])
    hi = lax.Precision.HIGHEST
    pg = jax.nn.softmax(jnp.dot(f, w['w_rg'], precision=hi) + w['b_rg'], axis=-1)
    g_sel = jnp.argmax(pg, axis=-1)
    g_w = jnp.max(pg, axis=-1)
    el = (jnp.dot(f, w['w_re'], precision=hi) + w['b_re']).reshape(n_tok, N_EXPERT_GROUPS, EXPERTS_PER_GROUP)
    el = jnp.take_along_axis(el, g_sel[:, None, None], axis=1)[:, 0]
    top_p, top_i = lax.top_k(jax.nn.softmax(el, axis=-1), TOP_K)
    wts = g_w[:, None] * top_p / jnp.sum(top_p, axis=-1, keepdims=True)
    eid = (g_sel[:, None] * EXPERTS_PER_GROUP + top_i).reshape(-1).astype(jnp.int32)
    n_asg = eid.shape[0]
    onehot = (eid[:, None] == jnp.arange(N_EXPERTS, dtype=jnp.int32)[None, :]).astype(jnp.int32)
    within = jnp.take_along_axis(jnp.cumsum(onehot, axis=0), eid[:, None], axis=1)[:, 0] - 1
    sizes = jnp.sum(onehot, axis=0)
    padded = (sizes + MOE_ROWS - 1) // MOE_ROWS * MOE_ROWS
    pend = jnp.cumsum(padded)
    dest = (pend - padded)[eid] + within
    n_blk = -(-n_asg // MOE_ROWS) + N_EXPERTS
    src = jnp.zeros((n_blk * MOE_ROWS,), jnp.int32).at[dest].set(jnp.arange(n_asg, dtype=jnp.int32) // TOP_K)
    xpad = f.astype(BF16)[src]
    blk_e = jnp.minimum(jnp.searchsorted(pend, jnp.arange(n_blk, dtype=jnp.int32) * MOE_ROWS, side='right'),
                        N_EXPERTS - 1).astype(jnp.int32)
    n_used = (pend[-1] // MOE_ROWS).astype(jnp.int32).reshape(1)
    ypad = moe_experts(xpad, blk_e, n_used, w['wg_b'], w['wu_b'], w['wd_b'])
    y = ypad[dest].reshape(n_tok, TOP_K, D_MODEL) * wts[:, :, None]
    return h + jnp.sum(y, axis=1)


def _token_tail(x2, mix_in, p2, w, tm):
    h = outproj(x2, mix_in[0], mix_in[1], w['w_out_b'], tm)
    h = _moe(h, w)
    return ple(h, p2, w['wpg_b'], w['wpp_b'], w['ple_norm'], tm)


def _ssd_finish(y_hm, z, w, length):
    bsz = y_hm.shape[0]
    y = y_hm[:, :, :length].transpose(0, 2, 1, 3).reshape(bsz, length, D_INNER)
    gated = y * jax.nn.silu(z)
    return _rmsnorm(gated, w['ssm_norm']).astype(BF16)


def _prompt_group(x, p, w):
    bsz, t, _ = x.shape
    m = bsz * t
    q, kvs, z, xbc, sm = rms_inproj(x.reshape(m, D_MODEL), w['attn_norm'], w['w_in_r'], 512)
    pos = jnp.arange(t)
    scale = HEAD_DIM ** -0.5
    qn = _rmsnorm(q.reshape(bsz, t, N_HEADS, HEAD_DIM), w['q_norm'])
    qr = _rope(qn, pos)

    nq = t // NSA_TQ

    def heads_t(a):
        a = a.reshape(bsz, nq, NSA_TQ, KV_HEADS, HPG, -1).transpose(0, 3, 1, 5, 4, 2)
        return a.reshape(bsz, KV_HEADS, nq, -1, HPG * NSA_TQ)

    def tiles_t(a, tile):
        return a.astype(BF16).reshape(bsz, t // tile, tile, KV_HEADS, HEAD_DIM).transpose(0, 3, 1, 4, 2)

    kvs = kvs.reshape(bsz, t, 6, KV_HEADS, HEAD_DIM)
    k_c = _rmsnorm(kvs[:, :, 0], w['k_norm'][0])
    k_s = _rope(_rmsnorm(kvs[:, :, 2], w['k_norm'][1]), pos)
    k_w = _rope(_rmsnorm(kvs[:, :, 4], w['k_norm'][2]), pos)
    kv_new = jnp.stack([k_c, kvs[:, :, 1], k_s, kvs[:, :, 3]], axis=2)
    win_new = jnp.stack([k_w, kvs[:, :, 5]], axis=2)

    def group_major(a):
        return a.astype(BF16).transpose(0, 2, 1, 3)

    n_seg = t // S_CMP
    segs = jnp.stack([group_major(k_c), group_major(kvs[:, :, 1])]).reshape(2, bsz * KV_HEADS, n_seg, S_CMP * HEAD_DIM)
    kvc = compress(segs, w['cmp_pe'], w['cmp_w1'], w['cmp_w2'])
    gates_t = jnp.pad(heads_t(sm[:, :N_GATES]), ((0, 0), (0, 0), (0, 0), (0, 8 - 3), (0, 0)))
    n_slc = t // L_SLC
    blk_onehot = (jnp.arange(t)[:, None] // L_SLC == jnp.arange(n_slc)[None, :]).astype(BF16)
    ks_aug = jnp.concatenate(
        [group_major(k_s), jnp.broadcast_to(blk_onehot, (bsz, KV_HEADS, t, n_slc))], axis=-1)
    attn = nsa_prompt(heads_t((qn * scale).astype(BF16)), heads_t((qr * scale).astype(BF16)), kvc[0],
                      kvc[1].transpose(0, 2, 1), ks_aug, tiles_t(kvs[:, :, 3], NSA_TK), group_major(k_w),
                      tiles_t(kvs[:, :, 5], NSA_TKW), gates_t, bsz, t)
    attn = attn.reshape(bsz, KV_HEADS, nq, HEAD_DIM, HPG, NSA_TQ).transpose(0, 2, 5, 1, 4, 3).reshape(m, ATT_W)

    xbc3 = xbc.reshape(bsz, t, CONV_DIM)
    xbc_all = jnp.concatenate([jnp.zeros((bsz, CONV_W - 1, CONV_DIM), F32), xbc3], axis=1)
    dt_raw = sm[:, N_GATES:N_GATES + SSM_HEADS].reshape(bsz, t, SSM_HEADS)
    xs, b_t, c_in, a_col, a_row, dt_col = _ssd_inputs(xbc_all, dt_raw, w, t)
    s0 = jnp.zeros((bsz, SSM_HEADS, D_STATE, SSM_HEAD_DIM), F32)
    y_hm, s_t = ssd_scan(w['ssm_d'], xs, b_t, c_in, a_col, a_row, dt_col, s0, SSM_CHUNK)
    ssd = _ssd_finish(y_hm, z.reshape(bsz, t, D_INNER), w, t).reshape(m, D_INNER)

    y = _token_tail(x.reshape(m, D_MODEL), (attn, ssd), p.reshape(m, PLE_DIM), w, 512)
    keep = min(WINDOW, t)
    return (y.reshape(bsz, t, D_MODEL), kv_new, win_new[:, t - keep:], s_t.transpose(0, 1, 3, 2),
            xbc_all[:, xbc_all.shape[1] - (CONV_W - 1):])


def _sample_attention(qn, qr, kvc, kv_new, win_all, pool, page_table, q_off, gl):
    bt, lq = qn.shape[:2]
    scale = HEAD_DIM ** -0.5
    t_kv = q_off + lq
    n_cmp = (t_kv - L_CMP) // S_CMP + 1
    n_slc = -(-t_kv // L_SLC)
    kc = kvc[0].reshape(bt, KV_HEADS, -1, HEAD_DIM)[:, :, :n_cmp].astype(F32)
    vc = kvc[1].reshape(bt, KV_HEADS, -1, HEAD_DIM)[:, :, :n_cmp].astype(F32)
    qp = qn.reshape(bt, lq, KV_HEADS, HPG, HEAD_DIM)
    qrr = qr.reshape(bt, lq, KV_HEADS, HPG, HEAD_DIM)
    pos = q_off + jnp.arange(lq)
    cmp_end = jnp.arange(n_cmp) * S_CMP + (L_CMP - 1)
    s1 = jnp.einsum('bqghd,bgnd->bghqn', qp, kc, preferred_element_type=F32) * scale
    p1 = _masked_softmax(s1, cmp_end[None, :] <= pos[:, None])
    o_c = jnp.einsum('bghqn,bgnd->bqghd', p1, vc, preferred_element_type=F32)
    imp = jnp.einsum('bghqn,nj->bgqj', p1, _cover_matrix(n_cmp, n_slc), preferred_element_type=F32)
    blk = jnp.arange(n_slc)
    cur = pos // L_SLC
    vis = blk[None, :] <= cur[:, None]
    forced = vis & ((blk[None, :] == 0) | (blk[None, :] >= cur[:, None] - 1))
    score = jnp.where(forced, 1e9, jnp.where(vis, imp, -1.0))
    top_s, idx = lax.top_k(score, min(N_SEL, n_slc))
    n_past_blk = q_off // L_SLC
    per_page = PAGE_SIZE // L_SLC
    pool6 = pool.reshape(pool.shape[0], per_page, L_SLC, 4, KV_HEADS, HEAD_DIM)
    idx_c = jnp.minimum(idx, n_past_blk - 1)
    bi = jnp.arange(bt)[:, None, None, None]
    gi = jnp.arange(KV_HEADS)[None, :, None, None]
    pages = page_table[bi, idx_c // per_page]
    kg = pool6[pages, idx_c % per_page, :, 2, gi]
    vg = pool6[pages, idx_c % per_page, :, 3, gi]
    new_k = jnp.pad(kv_new[:, :, 2], ((0, 0), (0, L_SLC - lq), (0, 0), (0, 0))).transpose(0, 2, 1, 3)
    new_v = jnp.pad(kv_new[:, :, 3], ((0, 0), (0, L_SLC - lq), (0, 0), (0, 0))).transpose(0, 2, 1, 3)
    is_new = (idx >= n_past_blk)[..., None, None]
    kg = jnp.where(is_new, new_k[:, :, None, None], kg)
    vg = jnp.where(is_new, new_v[:, :, None, None], vg)
    k_sel = idx.shape[-1]
    kg = kg.reshape(bt, KV_HEADS, lq, k_sel * L_SLC, HEAD_DIM)
    vg = vg.reshape(bt, KV_HEADS, lq, k_sel * L_SLC, HEAD_DIM)
    kpos = idx[..., None] * L_SLC + jnp.arange(L_SLC)
    ok = ((top_s > -0.5)[..., None] & (kpos <= pos[None, None, :, None, None])).reshape(bt, KV_HEADS, lq, k_sel * L_SLC)
    s2 = jnp.einsum('bqghd,bgqmd->bghqm', qrr, kg, preferred_element_type=F32) * scale
    p2 = _masked_softmax(s2, ok[:, :, None])
    o_s = jnp.einsum('bghqm,bgqmd->bqghd', p2, vg, preferred_element_type=F32)
    kw, vw = win_all[:, :, 0], win_all[:, :, 1]
    n_w = win_all.shape[1]
    kpos_w = q_off + lq - n_w + jnp.arange(n_w)
    m3 = (kpos_w[None, :] <= pos[:, None]) & (kpos_w[None, :] > pos[:, None] - WINDOW) & (kpos_w >= 0)[None, :]
    s3 = jnp.einsum('bqghd,bkgd->bghqk', qrr, kw, preferred_element_type=F32) * scale
    p3 = _masked_softmax(s3, m3)
    o_w = jnp.einsum('bghqk,bkgd->bqghd', p3, vw, preferred_element_type=F32)
    g = jax.nn.sigmoid(gl).reshape(bt, lq, N_HEADS, 3, 1)
    merge = lambda o: o.reshape(bt, lq, N_HEADS, HEAD_DIM)
    attn = g[:, :, :, 0] * merge(o_c) + g[:, :, :, 1] * merge(o_s) + g[:, :, :, 2] * merge(o_w)
    return attn.astype(BF16).reshape(bt * lq, ATT_W)


def _sample_group(x, p, pool, page_table, cache_win, state_ssm, state_conv, w):
    bsz, lq, _ = x.shape
    m = bsz * lq
    q_off = page_table.shape[1] * PAGE_SIZE
    q, kvs, z, xbc, sm = rms_inproj(x.reshape(m, D_MODEL), w['attn_norm'], w['w_in_r'], m)
    pos = q_off + jnp.arange(lq)
    qn = _rmsnorm(q.reshape(bsz, lq, N_HEADS, HEAD_DIM), w['q_norm'])
    qr = _rope(qn, pos)
    kvs = kvs.reshape(bsz, lq, 6, KV_HEADS, HEAD_DIM)
    k_c = _rmsnorm(kvs[:, :, 0], w['k_norm'][0])
    k_s = _rope(_rmsnorm(kvs[:, :, 2], w['k_norm'][1]), pos)
    k_w = _rope(_rmsnorm(kvs[:, :, 4], w['k_norm'][2]), pos)
    kv_new = jnp.stack([k_c, kvs[:, :, 1], k_s, kvs[:, :, 3]], axis=2)
    win_new = jnp.stack([k_w, kvs[:, :, 5]], axis=2)
    win_cat = jnp.concatenate([cache_win, win_new], axis=1)

    n_seg = q_off // S_CMP
    past01 = pool[:, :, 0:2][page_table]
    segs = past01.astype(BF16).transpose(3, 0, 4, 1, 2, 5).reshape(2, bsz * KV_HEADS, n_seg, S_CMP * HEAD_DIM)
    kvc = compress(segs, w['cmp_pe'], w['cmp_w1'], w['cmp_w2'])
    attn = _sample_attention(qn, qr, kvc, kv_new, win_cat, pool, page_table, q_off, sm[:, :N_GATES])

    xbc_all = jnp.concatenate([state_conv, xbc.reshape(bsz, lq, CONV_DIM)], axis=1)
    dt_raw = sm[:, N_GATES:N_GATES + SSM_HEADS].reshape(bsz, lq, SSM_HEADS)
    xs, b_t, c_in, a_col, a_row, dt_col = _ssd_inputs(xbc_all, dt_raw, w, SSM_CHUNK)
    y_hm, s_t = ssd_scan(w['ssm_d'], xs, b_t, c_in, a_col, a_row, dt_col, state_ssm.transpose(0, 1, 3, 2), SSM_CHUNK)
    ssd = _ssd_finish(y_hm, z.reshape(bsz, lq, D_INNER), w, lq).reshape(m, D_INNER)

    y = _token_tail(x.reshape(m, D_MODEL), (attn, ssd), p.reshape(m, PLE_DIM), w, m)
    keep = cache_win.shape[1]
    return (y.reshape(bsz, lq, D_MODEL), kv_new, win_cat[:, win_cat.shape[1] - keep:], s_t.transpose(0, 1, 3, 2),
            xbc_all[:, xbc_all.shape[1] - (CONV_W - 1):])


def kernel(x_prompt, x_sample, cache_kv, cache_win, state_ssm, state_conv, page_table, p_prompt, p_sample,
           w_in, w_out, q_norm, k_norm, cmp_pe, cmp_w1, cmp_w2, conv_w, conv_b, dt_bias, a_log, ssm_d, ssm_norm,
           attn_norm, ffn_norm, w_rg, b_rg, w_re, b_re, w_gate, w_up, w_down, w_ple_proj, ple_norm, w_ple_gate):
    depth = w_in.shape[0]
    hp, hs = x_prompt, x_sample
    outs = [[] for _ in range(8)]
    cuts = np.cumsum((ATT_W, 6 * KV_HEADS * HEAD_DIM, N_GATES, D_INNER, CONV_DIM, SSM_HEADS))
    for l in range(depth):
        wi = w_in[l]
        w_in_r = jnp.concatenate(
            [wi[:, :cuts[1]], wi[:, cuts[2]:cuts[3]], wi[:, cuts[3]:cuts[4]], wi[:, cuts[1]:cuts[2]],
             wi[:, cuts[4]:cuts[5]], jnp.zeros((D_MODEL, C_SM - N_GATES - SSM_HEADS), F32)], axis=1).astype(BF16)
        w = dict(w_in_r=w_in_r, w_out_b=w_out[l].astype(BF16), q_norm=q_norm[l], k_norm=k_norm[l],
                 cmp_pe=cmp_pe[l], cmp_w1=cmp_w1[l], cmp_w2=cmp_w2[l], conv_w=conv_w[l], conv_b=conv_b[l],
                 dt_bias=dt_bias[l], a_log=a_log[l], ssm_d=ssm_d[l], ssm_norm=ssm_norm[l], attn_norm=attn_norm[l],
                 ffn_norm=ffn_norm[l], w_rg=w_rg[l], b_rg=b_rg[l], w_re=w_re[l], b_re=b_re[l],
                 wg_b=w_gate[l].astype(BF16), wu_b=w_up[l].astype(BF16), wd_b=w_down[l].astype(BF16),
                 wpp_b=w_ple_proj[l].astype(BF16), ple_norm=ple_norm[l], wpg_b=w_ple_gate[l].astype(BF16))
        hp, *rest_p = _prompt_group(hp, p_prompt[l], w)
        hs, *rest_s = _sample_group(hs, p_sample[l], cache_kv[l], page_table, cache_win[l], state_ssm[l],
                                    state_conv[l], w)
        for j in range(4):
            outs[2 * j].append(rest_p[j])
            outs[2 * j + 1].append(rest_s[j])
    return (hp, hs) + tuple(jnp.stack(o) for o in outs)
```

```python
import functools
import math

import numpy as np
import jax
import jax.numpy as jnp
from jax import lax
from jax.experimental import pallas as pl
from jax.experimental.pallas import tpu as pltpu

F32 = jnp.float32
BF16 = jnp.bfloat16

D_MODEL = 1024
PAGE_SIZE = 128
N_HEADS = 8
HEAD_DIM = 64
KV_HEADS = 2
HPG = N_HEADS // KV_HEADS
ATT_W = N_HEADS * HEAD_DIM
L_CMP = 32
S_CMP = 16
L_SLC = 64
N_SEL = 16
WINDOW = 512
CMP_HID = 64
ROPE_THETA = 10000.0
SSM_HEADS = 8
SSM_HEAD_DIM = 64
D_INNER = SSM_HEADS * SSM_HEAD_DIM
SSM_GROUPS = 2
D_STATE = 64
CONV_W = 4
CONV_DIM = D_INNER + 2 * SSM_GROUPS * D_STATE
SSM_CHUNK = 128
N_EXPERT_GROUPS = 4
EXPERTS_PER_GROUP = 8
N_EXPERTS = N_EXPERT_GROUPS * EXPERTS_PER_GROUP
TOP_K = 2
D_EXPERT = 512
PLE_DIM = 256
EPS = 1e-6
N_GATES = 3 * N_HEADS
C_Q, C_KV, C_Z, C_XBC, C_SM = 512, 768, 512, 768, 128
D_IN_PAD = C_Q + C_KV + C_Z + C_XBC + C_SM

NEG = -1e30
VMEM_LIMIT = 48 * 1024 * 1024
MOE_ROWS = 256


def _cparams(sem):
    return pltpu.CompilerParams(dimension_semantics=sem, vmem_limit_bytes=VMEM_LIMIT)


def _dot(a, b):
    return jnp.dot(a, b, preferred_element_type=F32)


def _dot_nt(a, b):
    return lax.dot_general(a, b, (((1,), (1,)), ((), ())), preferred_element_type=F32)


def _dot_hi(a, b):
    return jnp.dot(a, b, preferred_element_type=F32, precision=lax.Precision.HIGHEST)


def _rms_inproj_kernel(x_ref, g_ref, w_ref, q_ref, kv_ref, z_ref, xbc_ref, sm_ref):
    x = x_ref[...]
    y = x * lax.rsqrt(jnp.mean(x * x, axis=-1, keepdims=True) + EPS) * g_ref[...]
    yb = y.astype(BF16)
    c0 = 0
    for ref, width in ((q_ref, C_Q), (kv_ref, C_KV), (z_ref, C_Z), (xbc_ref, C_XBC), (sm_ref, C_SM)):
        ref[...] = _dot(yb, w_ref[:, c0:c0 + width])
        c0 += width


def rms_inproj(x, gain, w_r, tm):
    m = x.shape[0]
    widths = (C_Q, C_KV, C_Z, C_XBC, C_SM)
    return pl.pallas_call(
        _rms_inproj_kernel,
        grid=(m // tm,),
        in_specs=[pl.BlockSpec((tm, D_MODEL), lambda i: (i, 0)),
                  pl.BlockSpec((1, D_MODEL), lambda i: (0, 0)),
                  pl.BlockSpec((D_MODEL, D_IN_PAD), lambda i: (0, 0))],
        out_specs=[pl.BlockSpec((tm, w), lambda i: (i, 0)) for w in widths],
        out_shape=[jax.ShapeDtypeStruct((m, w), F32) for w in widths],
        compiler_params=_cparams(("parallel",)),
        name="rms_inproj",
    )(x, gain.reshape(1, D_MODEL), w_r)


def _outproj_kernel(x_ref, a_ref, s_ref, w_ref, h_ref):
    acc = _dot(a_ref[...], w_ref[0:ATT_W, :]) + _dot(s_ref[...], w_ref[ATT_W:ATT_W + D_INNER, :])
    h_ref[...] = x_ref[...] + acc


def outproj(x, attn, ssd, w_out_b, tm):
    m = x.shape[0]
    return pl.pallas_call(
        _outproj_kernel,
        grid=(m // tm,),
        in_specs=[pl.BlockSpec((tm, D_MODEL), lambda i: (i, 0)),
                  pl.BlockSpec((tm, ATT_W), lambda i: (i, 0)),
                  pl.BlockSpec((tm, D_INNER), lambda i: (i, 0)),
                  pl.BlockSpec((ATT_W + D_INNER, D_MODEL), lambda i: (0, 0))],
        out_specs=pl.BlockSpec((tm, D_MODEL), lambda i: (i, 0)),
        out_shape=jax.ShapeDtypeStruct((m, D_MODEL), F32),
        compiler_params=_cparams(("parallel",)),
        name="outproj",
    )(x, attn, ssd, w_out_b)


def _ple_kernel(h_ref, y0_ref, y1_ref, wt_ref, p_ref, wg_ref, wp_ref, g_ref, o_ref):
    wt = wt_ref[...]
    h = h_ref[...] + (y0_ref[...] * wt[:, 0:1] + y1_ref[...] * wt[:, 1:2])
    gate = jax.nn.sigmoid(_dot(h.astype(BF16), wg_ref[...]))
    e = _dot(p_ref[...].astype(BF16), wp_ref[...])
    e = e * lax.rsqrt(jnp.mean(e * e, axis=-1, keepdims=True) + EPS) * g_ref[...]
    o_ref[...] = h + gate * e


def moe_combine_ple(h, y01, wts, p, wg_b, wp_b, gain, tm):
    m = h.shape[0]
    nt = m // tm
    return pl.pallas_call(
        _ple_kernel,
        grid=(nt,),
        in_specs=[pl.BlockSpec((tm, D_MODEL), lambda i: (i, 0)),
                  pl.BlockSpec((tm, D_MODEL), lambda i: (i, 0)),
                  pl.BlockSpec((tm, D_MODEL), lambda i: (i + nt, 0)),
                  pl.BlockSpec((tm, 128), lambda i: (i, 0)),
                  pl.BlockSpec((tm, PLE_DIM), lambda i: (i, 0)),
                  pl.BlockSpec((D_MODEL, D_MODEL), lambda i: (0, 0)),
                  pl.BlockSpec((PLE_DIM, D_MODEL), lambda i: (0, 0)),
                  pl.BlockSpec((1, D_MODEL), lambda i: (0, 0))],
        out_specs=pl.BlockSpec((tm, D_MODEL), lambda i: (i, 0)),
        out_shape=jax.ShapeDtypeStruct((m, D_MODEL), F32),
        compiler_params=_cparams(("parallel",)),
        name="moe_combine_ple",
    )(h, y01, y01, wts, p, wg_b, wp_b, gain.reshape(1, D_MODEL))


def _moe_kernel(be_ref, nb_ref, x_ref, wg_ref, wu_ref, wd_ref, y_ref):
    @pl.when(pl.program_id(0) < nb_ref[0])
    def _():
        x = x_ref[...]
        a = _dot(x, wg_ref[0])
        hb = (a * jax.nn.sigmoid(a)) * _dot(x, wu_ref[0])
        y_ref[...] = _dot(hb.astype(BF16), wd_ref[0])

    @pl.when(pl.program_id(0) >= nb_ref[0])
    def _():
        y_ref[...] = jnp.zeros(y_ref.shape, F32)


def moe_experts(xpad, blk_e, n_used, wg_b, wu_b, wd_b):
    n_blk = xpad.shape[0] // MOE_ROWS
    grid_spec = pltpu.PrefetchScalarGridSpec(
        num_scalar_prefetch=2,
        grid=(n_blk,),
        in_specs=[pl.BlockSpec((MOE_ROWS, D_MODEL), lambda i, be, nb: (i, 0)),
                  pl.BlockSpec((1, D_MODEL, D_EXPERT), lambda i, be, nb: (be[i], 0, 0)),
                  pl.BlockSpec((1, D_MODEL, D_EXPERT), lambda i, be, nb: (be[i], 0, 0)),
                  pl.BlockSpec((1, D_EXPERT, D_MODEL), lambda i, be, nb: (be[i], 0, 0))],
        out_specs=pl.BlockSpec((MOE_ROWS, D_MODEL), lambda i, be, nb: (i, 0)),
    )
    return pl.pallas_call(
        _moe_kernel,
        grid_spec=grid_spec,
        out_shape=jax.ShapeDtypeStruct((n_blk * MOE_ROWS, D_MODEL), F32),
        compiler_params=_cparams(("arbitrary",)),
        name="moe_experts",
    )(blk_e, n_used, xpad, wg_b, wu_b, wd_b)


def _compress_kernel(seg_ref, w1a_ref, w1b_ref, pe_ref, w1_ref, w2_ref, o_ref):
    seg = seg_ref[0, 0]
    n_seg = seg.shape[0]
    hid0 = _dot(pe_ref[0], w1_ref[0])[0:1]
    p0 = _dot(seg, w1a_ref[0])
    p1 = pltpu.roll(_dot(seg, w1b_ref[0]), n_seg - 1, axis=0)
    hid = hid0 + p0 + p1
    act = hid * jax.nn.sigmoid(hid)
    o_ref[0, 0] = _dot(act.astype(BF16), w2_ref[0]).astype(BF16)


def compress(segs, cmp_pe, cmp_w1, cmp_w2):
    _, r, n_seg, k = segs.shape
    w1 = cmp_w1.astype(BF16)
    pe = jnp.broadcast_to(cmp_pe.reshape(2, 1, L_CMP * HEAD_DIM), (2, 8, L_CMP * HEAD_DIM)).astype(BF16)
    return pl.pallas_call(
        _compress_kernel,
        grid=(2, r),
        in_specs=[pl.BlockSpec((1, 1, n_seg, k), lambda s, i: (s, i, 0, 0)),
                  pl.BlockSpec((1, k, CMP_HID), lambda s, i: (s, 0, 0)),
                  pl.BlockSpec((1, k, CMP_HID), lambda s, i: (s, 1, 0)),
                  pl.BlockSpec((1, 8, 2 * k), lambda s, i: (s, 0, 0)),
                  pl.BlockSpec((1, 2 * k, CMP_HID), lambda s, i: (s, 0, 0)),
                  pl.BlockSpec((1, CMP_HID, HEAD_DIM), lambda s, i: (s, 0, 0))],
        out_specs=pl.BlockSpec((1, 1, n_seg, HEAD_DIM), lambda s, i: (s, i, 0, 0)),
        out_shape=jax.ShapeDtypeStruct((2, r, n_seg, HEAD_DIM), BF16),
        compiler_params=_cparams(("parallel", "parallel")),
        name="compress",
    )(segs, w1, w1, pe, w1, cmp_w2.astype(BF16))


PAGES_PER_STEP = 16


def _page_copies(pt_ref, pool_ref, buf, sem, b, j, slot):
    return [pltpu.make_async_copy(pool_ref.at[pt_ref[b, j * PAGES_PER_STEP + p], pl.ds(0, 2)],
                                  buf.at[slot, p], sem.at[slot]) for p in range(PAGES_PER_STEP)]


def _paged_partials_kernel(pt_ref, pool_ref, wa_ref, wb_ref, p0_ref, p1_ref, buf, sem, x_sc):
    b = pl.program_id(0)
    j = pl.program_id(1)
    n_grp = pl.num_programs(1)
    step = b * n_grp + j
    slot = step % 2

    @pl.when(step == 0)
    def _():
        for c in _page_copies(pt_ref, pool_ref, buf, sem, b, j, slot):
            c.start()

    @pl.when(step + 1 < pl.num_programs(0) * n_grp)
    def _():
        wrap = j + 1 == n_grp
        for c in _page_copies(pt_ref, pool_ref, buf, sem, jnp.where(wrap, b + 1, b), jnp.where(wrap, 0, j + 1),
                              1 - slot):
            c.start()

    for c in _page_copies(pt_ref, pool_ref, buf, sem, b, j, slot):
        c.wait()

    n_seg = PAGES_PER_STEP * PAGE_SIZE // S_CMP
    for kv in range(2):
        for g in range(KV_HEADS):
            for p in range(PAGES_PER_STEP):
                x_sc[p * PAGE_SIZE:(p + 1) * PAGE_SIZE, :] = buf[slot, p, kv, g].T
            acc0 = jnp.zeros((n_seg, CMP_HID), F32)
            acc1 = jnp.zeros((n_seg, CMP_HID), F32)
            for s in range(S_CMP):
                xs = x_sc[pl.ds(s, n_seg, stride=S_CMP), :].astype(BF16)
                acc0 = acc0 + _dot(xs, wa_ref[kv, s])
                acc1 = acc1 + _dot(xs, wb_ref[kv, s])
            p0_ref[kv, 0, g] = acc0
            p1_ref[kv, 0, g] = acc1


def _compress_finish_kernel(p0_ref, p1_ref, pe_ref, w1_ref, w2_ref, o_ref):
    n_seg = p0_ref.shape[3]
    hid0 = _dot(pe_ref[0], w1_ref[0])[0:1]
    hid = hid0 + p0_ref[0, 0, 0] + pltpu.roll(p1_ref[0, 0, 0], n_seg - 1, axis=0)
    act = hid * jax.nn.sigmoid(hid)
    o_ref[0, 0] = _dot(act.astype(BF16), w2_ref[0]).astype(BF16)


def compress_paged(pool_t, page_table, cmp_pe, cmp_w1, cmp_w2):
    bsz, n_pages = page_table.shape
    assert n_pages % PAGES_PER_STEP == 0
    n_grp = n_pages // PAGES_PER_STEP
    seg_step = PAGES_PER_STEP * PAGE_SIZE // S_CMP
    n_seg = n_grp * seg_step
    w1 = cmp_w1.astype(BF16).reshape(2, L_CMP // S_CMP, S_CMP, HEAD_DIM, CMP_HID)
    part_shape = jax.ShapeDtypeStruct((2, bsz, KV_HEADS, n_seg, CMP_HID), F32)
    part_spec = pl.BlockSpec((2, 1, KV_HEADS, seg_step, CMP_HID), lambda b, j, pt: (0, b, 0, j, 0))
    wspec = pl.BlockSpec((2, S_CMP, HEAD_DIM, CMP_HID), lambda b, j, pt: (0, 0, 0, 0))
    p0, p1 = pl.pallas_call(
        _paged_partials_kernel,
        grid_spec=pltpu.PrefetchScalarGridSpec(
            num_scalar_prefetch=1,
            grid=(bsz, n_grp),
            in_specs=[pl.BlockSpec(memory_space=pl.ANY), wspec, wspec],
            out_specs=[part_spec, part_spec],
            scratch_shapes=[pltpu.VMEM((2, PAGES_PER_STEP, 2, KV_HEADS, HEAD_DIM, PAGE_SIZE), F32),
                            pltpu.SemaphoreType.DMA((2,)),
                            pltpu.VMEM((PAGES_PER_STEP * PAGE_SIZE, HEAD_DIM), F32)]),
        out_shape=[part_shape, part_shape],
        compiler_params=_cparams(("arbitrary", "arbitrary")),
        name="paged_partials",
    )(page_table, pool_t, w1[:, 0], w1[:, 1])
    pe = jnp.broadcast_to(cmp_pe.reshape(2, 1, L_CMP * HEAD_DIM), (2, 8, L_CMP * HEAD_DIM)).astype(BF16)
    pspec = pl.BlockSpec((1, 1, 1, n_seg, CMP_HID), lambda s, i: (s, i // KV_HEADS, i % KV_HEADS, 0, 0))
    return pl.pallas_call(
        _compress_finish_kernel,
        grid=(2, bsz * KV_HEADS),
        in_specs=[pspec, pspec,
                  pl.BlockSpec((1, 8, L_CMP * HEAD_DIM), lambda s, i: (s, 0, 0)),
                  pl.BlockSpec((1, L_CMP * HEAD_DIM, CMP_HID), lambda s, i: (s, 0, 0)),
                  pl.BlockSpec((1, CMP_HID, HEAD_DIM), lambda s, i: (s, 0, 0))],
        out_specs=pl.BlockSpec((1, 1, n_seg, HEAD_DIM), lambda s, i: (s, i, 0, 0)),
        out_shape=jax.ShapeDtypeStruct((2, bsz * KV_HEADS, n_seg, HEAD_DIM), BF16),
        compiler_params=_cparams(("parallel", "parallel")),
        name="compress_finish",
    )(p0, p1, pe, cmp_w1.astype(BF16), cmp_w2.astype(BF16))


def _flash_step(s_t, v_t, m_sc, l_sc, acc_sc):
    m_prev = m_sc[...]
    m_new = jnp.maximum(m_prev, jnp.max(s_t, axis=0, keepdims=True))
    alpha = jnp.exp(m_prev - m_new)
    p = jnp.exp(s_t - m_new)
    l_sc[...] = alpha * l_sc[...] + jnp.sum(p, axis=0, keepdims=True)
    acc_sc[...] = alpha * acc_sc[...] + _dot(v_t, p.astype(BF16))
    m_sc[...] = m_new


def _flash_reset(m_sc, l_sc, acc_sc):
    m_sc[...] = jnp.full(m_sc.shape, NEG, F32)
    l_sc[...] = jnp.zeros(l_sc.shape, F32)
    acc_sc[...] = jnp.zeros(acc_sc.shape, F32)


def _nsa_prompt_kernel(qn_ref, qr_ref, kc_ref, vct_ref, ks_ref, vst_ref, kw_ref, vwt_ref, covert_ref, gate_ref,
                       o_ref, qa_sc, m_sc, l_sc, acc_sc, out_sc, *, tq, tk, tkw):
    qi = pl.program_id(2)
    pos0 = qi * tq
    w = HPG * tq

    def positions(rows, base):
        kpos = base + lax.broadcasted_iota(jnp.int32, (rows, w), 0)
        qpos = pos0 + (lax.broadcasted_iota(jnp.int32, (rows, w), 1) & (tq - 1))
        return kpos, qpos

    gate = jax.nn.sigmoid(gate_ref[0, 0, 0])

    kc = kc_ref[0, 0]
    n_c = kc.shape[0]
    cidx, qpos_c = positions(n_c, 0)
    valid = cidx * S_CMP + (L_CMP - 1) <= qpos_c
    s1 = jnp.where(valid, _dot(kc, qn_ref[0, 0, 0]), NEG)
    e1 = jnp.exp(s1 - jnp.max(s1, axis=0, keepdims=True)) * valid.astype(F32)
    p1b = (e1 * (1.0 / jnp.maximum(jnp.sum(e1, axis=0, keepdims=True), 1e-30))).astype(BF16)
    out_sc[...] = gate[0:1] * _dot(vct_ref[0, 0], p1b)

    p1_stack = jnp.concatenate([p1b[:, hh * tq:(hh + 1) * tq] for hh in range(HPG)], axis=0)
    imp = _dot(covert_ref[...], p1_stack)
    n_s = imp.shape[0]
    blk = lax.broadcasted_iota(jnp.int32, (n_s, tq), 0)
    cur = (pos0 + lax.broadcasted_iota(jnp.int32, (n_s, tq), 1)) // L_SLC
    vis = blk <= cur
    forced = vis & ((blk == 0) | (blk >= cur - 1))
    score = jnp.where(forced, 1e9, jnp.where(vis, imp, -1.0))
    rank = jnp.zeros((n_s, tq), F32)
    for i in range(n_s):
        c = score[i:i + 1, :]
        beats = (c > score) | ((c == score) & (blk > i))
        rank = rank + beats.astype(F32)
    sel_bias = jnp.where(vis & (rank < N_SEL), 0.0, NEG).astype(BF16)
    qa_sc[0:HEAD_DIM, :] = qr_ref[0, 0, 0]
    qa_sc[HEAD_DIM:HEAD_DIM + n_s, :] = jnp.concatenate([sel_bias] * HPG, axis=1)

    _flash_reset(m_sc, l_sc, acc_sc)
    n_full = pos0 // tk

    def sel_body(kt, carry):
        k0 = pl.multiple_of(kt * tk, tk)
        _flash_step(_dot(ks_ref[0, 0, pl.ds(k0, tk), :], qa_sc[...]), vst_ref[0, 0, kt], m_sc, l_sc, acc_sc)
        return carry

    lax.fori_loop(0, n_full, sel_body, 0)
    k0 = pl.multiple_of(n_full * tk, tk)
    kpos, qpos = positions(tk, k0)
    s = _dot(ks_ref[0, 0, pl.ds(k0, tk), :], qa_sc[...])
    _flash_step(jnp.where(kpos <= qpos, s, NEG), vst_ref[0, 0, n_full], m_sc, l_sc, acc_sc)
    out_sc[...] = out_sc[...] + (gate[1:2] * (1.0 / l_sc[...])) * acc_sc[...]

    _flash_reset(m_sc, l_sc, acc_sc)
    qr = qr_ref[0, 0, 0]
    kpos, qpos = positions(tkw, pos0)
    s = _dot(kw_ref[0, 0, pl.ds(pl.multiple_of(pos0, tkw), tkw), :], qr)
    _flash_step(jnp.where(kpos <= qpos, s, NEG), vwt_ref[0, 0, qi], m_sc, l_sc, acc_sc)
    n_inner = WINDOW // tkw - 1

    def win_body(d, carry):
        kt = qi - 1 - d
        k0 = pl.multiple_of(kt * tkw, tkw)
        _flash_step(_dot(kw_ref[0, 0, pl.ds(k0, tkw), :], qr), vwt_ref[0, 0, kt], m_sc, l_sc, acc_sc)
        return carry

    lax.fori_loop(0, jnp.minimum(qi, n_inner), win_body, 0)

    @pl.when(qi > n_inner)
    def _():
        kt = qi - 1 - n_inner
        k0 = pl.multiple_of(kt * tkw, tkw)
        kpos, qpos = positions(tkw, k0)
        s = _dot(kw_ref[0, 0, pl.ds(k0, tkw), :], qr)
        _flash_step(jnp.where(kpos > qpos - WINDOW, s, NEG), vwt_ref[0, 0, kt], m_sc, l_sc, acc_sc)

    o = out_sc[...] + (gate[2:3] * (1.0 / l_sc[...])) * acc_sc[...]
    o_ref[0, 0, 0] = o.astype(o_ref.dtype)


NSA_TQ, NSA_TK, NSA_TKW = 128, 256, 128


def nsa_prompt(qn_t, qr_t, kc, vc_t, ks_aug, vs_t, kw, vw_t, gates_t, bsz, t):
    tq, tk, tkw = NSA_TQ, NSA_TK, NSA_TKW
    assert tq == tkw and tk % tq == 0 and WINDOW % tkw == 0
    n_c = kc.shape[1]
    n_s = t // L_SLC
    w = HPG * tq
    c0 = np.arange(n_c)[None, :] * S_CMP
    s0 = np.arange(n_s)[:, None] * L_SLC
    cover_t = (np.maximum(np.minimum(c0 + L_CMP, s0 + L_SLC) - np.maximum(c0, s0), 0) / L_CMP).astype(np.float32)
    cover_t = np.tile(cover_t, (1, HPG))
    qspec = pl.BlockSpec((1, 1, 1, HEAD_DIM, w), lambda b, g, i: (b, g, i, 0, 0))
    kern = functools.partial(_nsa_prompt_kernel, tq=tq, tk=tk, tkw=tkw)
    return pl.pallas_call(
        kern,
        grid=(bsz, KV_HEADS, t // tq),
        in_specs=[qspec, qspec,
                  pl.BlockSpec((1, 1, n_c, HEAD_DIM), lambda b, g, i: (b, g, 0, 0)),
                  pl.BlockSpec((1, 1, HEAD_DIM, n_c), lambda b, g, i: (b, g, 0, 0)),
                  pl.BlockSpec((1, 1, t, HEAD_DIM + n_s), lambda b, g, i: (b, g, 0, 0)),
                  pl.BlockSpec((1, 1, t // tk, HEAD_DIM, tk), lambda b, g, i: (b, g, 0, 0, 0)),
                  pl.BlockSpec((1, 1, t, HEAD_DIM), lambda b, g, i: (b, g, 0, 0)),
                  pl.BlockSpec((1, 1, t // tkw, HEAD_DIM, tkw), lambda b, g, i: (b, g, 0, 0, 0)),
                  pl.BlockSpec((n_s, HPG * n_c), lambda b, g, i: (0, 0)),
                  pl.BlockSpec((1, 1, 1, 8, w), lambda b, g, i: (b, g, i, 0, 0))],
        out_specs=qspec,
        out_shape=jax.ShapeDtypeStruct((bsz, KV_HEADS, t // tq, HEAD_DIM, w), BF16),
        scratch_shapes=[pltpu.VMEM((HEAD_DIM + n_s, w), BF16), pltpu.VMEM((1, w), F32), pltpu.VMEM((1, w), F32),
                        pltpu.VMEM((HEAD_DIM, w), F32), pltpu.VMEM((HEAD_DIM, w), F32)],
        compiler_params=_cparams(("parallel", "parallel", "arbitrary")),
        name="nsa_prompt",
    )(qn_t, qr_t, kc.reshape(bsz, KV_HEADS, n_c, HEAD_DIM), vc_t.reshape(bsz, KV_HEADS, HEAD_DIM, n_c),
      ks_aug, vs_t, kw, vw_t, jnp.asarray(cover_t, BF16), gates_t)


def _ssd_kernel(d_ref, x_ref, bt_ref, c_ref, acol_ref, arow_ref, dt_ref, s0_ref, y_ref, sout_ref, s_sc, *, cl):
    ci = pl.program_id(1)

    @pl.when(ci == 0)
    def _():
        s_sc[...] = s0_ref[0]

    li = lax.broadcasted_iota(jnp.int32, (cl, cl), 0)
    si = lax.broadcasted_iota(jnp.int32, (cl, cl), 1)
    causal = li >= si
    acum_col = _dot_hi(causal.astype(F32), acol_ref[0])
    acum_row = _dot_hi(arow_ref[0], (li <= si).astype(F32))
    dt = dt_ref[0]
    for g in range(SSM_GROUPS):
        cg = c_ref[0, g].astype(BF16)
        btg = bt_ref[0, g]
        cb = _dot(cg, btg.astype(BF16))
        for hh in range(SSM_HEADS // SSM_GROUPS):
            h = g * (SSM_HEADS // SSM_GROUPS) + hh
            ac = acum_col[:, h:h + 1]
            ar = acum_row[h:h + 1, :]
            decay = jnp.exp(jnp.where(causal, ac - ar, NEG))
            xs = x_ref[0, h]
            xd = (xs * dt[:, h:h + 1]).astype(BF16)
            st = s_sc[h]
            y = _dot((cb * decay).astype(BF16), xd) + _dot(cg, st.astype(BF16)) * jnp.exp(ac)
            y_ref[0, h] = y + d_ref[h] * xs
            a_last = ar[:, cl - 1:cl]
            snew = _dot((btg * jnp.exp(a_last - ar)).astype(BF16), xd)
            s_sc[h] = st * jnp.exp(a_last) + snew

    @pl.when(ci == pl.num_programs(1) - 1)
    def _():
        sout_ref[0] = s_sc[...]


def ssd_scan(ssm_d, x_hm, b_t, c, a_col, a_row, dt_col, s0_t, cl):
    bsz, nh, t, p = x_hm.shape
    n = D_STATE
    nc = t // cl
    kern = functools.partial(_ssd_kernel, cl=cl)
    return pl.pallas_call(
        kern,
        grid=(bsz, nc),
        in_specs=[pl.BlockSpec(memory_space=pltpu.SMEM),
                  pl.BlockSpec((1, nh, cl, p), lambda b, c_: (b, 0, c_, 0)),
                  pl.BlockSpec((1, SSM_GROUPS, n, cl), lambda b, c_: (b, 0, 0, c_)),
                  pl.BlockSpec((1, SSM_GROUPS, cl, n), lambda b, c_: (b, 0, c_, 0)),
                  pl.BlockSpec((1, cl, 128), lambda b, c_: (b, c_, 0)),
                  pl.BlockSpec((1, 8, cl), lambda b, c_: (b, 0, c_)),
                  pl.BlockSpec((1, cl, 128), lambda b, c_: (b, c_, 0)),
                  pl.BlockSpec((1, nh, n, p), lambda b, c_: (b, 0, 0, 0))],
        out_specs=[pl.BlockSpec((1, nh, cl, p), lambda b, c_: (b, 0, c_, 0)),
                   pl.BlockSpec((1, nh, n, p), lambda b, c_: (b, 0, 0, 0))],
        out_shape=[jax.ShapeDtypeStruct((bsz, nh, t, p), F32), jax.ShapeDtypeStruct((bsz, nh, n, p), F32)],
        scratch_shapes=[pltpu.VMEM((nh, n, p), F32)],
        compiler_params=_cparams(("parallel", "arbitrary")),
        name="ssd_scan",
    )(ssm_d, x_hm, b_t, c, a_col, a_row, dt_col, s0_t)


def _rmsnorm(x, g):
    return x * lax.rsqrt(jnp.mean(x * x, axis=-1, keepdims=True) + EPS) * g


def _rope(x, pos):
    half = HEAD_DIM // 2
    inv = ROPE_THETA ** (-jnp.arange(half, dtype=F32) / half)
    ang = pos.astype(F32)[:, None] * inv[None, :]
    cos, sin = jnp.cos(ang)[:, None, :], jnp.sin(ang)[:, None, :]
    x1, x2 = x[..., :half], x[..., half:]
    return jnp.concatenate([x1 * cos - x2 * sin, x2 * cos + x1 * sin], axis=-1)


def _masked_softmax(s, mask):
    s = jnp.where(mask, s, NEG)
    e = jnp.exp(s - jnp.max(s, axis=-1, keepdims=True)) * mask
    return e / jnp.maximum(jnp.sum(e, axis=-1, keepdims=True), 1e-30)


def _cover_matrix(n_cmp, n_slc):
    c0 = jnp.arange(n_cmp)[:, None] * S_CMP
    s0 = jnp.arange(n_slc)[None, :] * L_SLC
    shared = jnp.minimum(c0 + L_CMP, s0 + L_SLC) - jnp.maximum(c0, s0)
    return jnp.maximum(shared, 0).astype(F32) / L_CMP


def _ssd_inputs(xbc_all, dt_raw, w, t_pad):
    bsz = xbc_all.shape[0]
    length = xbc_all.shape[1] - (CONV_W - 1)
    conv = w['conv_b'] + xbc_all[:, 0:length] * w['conv_w'][0]
    for j in range(1, CONV_W):
        conv = conv + xbc_all[:, j:j + length] * w['conv_w'][j]
    act = jax.nn.silu(conv)
    dt = jax.nn.softplus(dt_raw + w['dt_bias'])
    a = -jnp.exp(w['a_log']) * dt
    pad = t_pad - length
    if pad:
        act = jnp.pad(act, ((0, 0), (0, pad), (0, 0)))
        dt = jnp.pad(dt, ((0, 0), (0, pad), (0, 0)))
        a = jnp.pad(a, ((0, 0), (0, pad), (0, 0)))
    xs = act[..., :D_INNER].reshape(bsz, t_pad, SSM_HEADS, SSM_HEAD_DIM).transpose(0, 2, 1, 3)
    b_in = act[..., D_INNER:D_INNER + SSM_GROUPS * D_STATE].reshape(bsz, t_pad, SSM_GROUPS, D_STATE)
    c_in = act[..., D_INNER + SSM_GROUPS * D_STATE:].reshape(bsz, t_pad, SSM_GROUPS, D_STATE)
    lane_pad = ((0, 0), (0, 0), (0, 128 - SSM_HEADS))
    return (xs, b_in.transpose(0, 2, 3, 1), c_in.transpose(0, 2, 1, 3), jnp.pad(a, lane_pad),
            a.transpose(0, 2, 1), jnp.pad(dt, lane_pad))


def _moe(h, w):
    n_tok = h.shape[0]
    f = _rmsnorm(h, w['ffn_norm'])
    hi = lax.Precision.HIGHEST
    pg = jax.nn.softmax(jnp.dot(f, w['w_rg'], precision=hi) + w['b_rg'], axis=-1)
    g_sel = jnp.argmax(pg, axis=-1)
    g_w = jnp.max(pg, axis=-1)
    el = (jnp.dot(f, w['w_re'], precision=hi) + w['b_re']).reshape(n_tok, N_EXPERT_GROUPS, EXPERTS_PER_GROUP)
    el = jnp.take_along_axis(el, g_sel[:, None, None], axis=1)[:, 0]
    top_p, top_i = lax.top_k(jax.nn.softmax(el, axis=-1), TOP_K)
    wts = g_w[:, None] * top_p / jnp.sum(top_p, axis=-1, keepdims=True)
    eid = (g_sel[:, None] * EXPERTS_PER_GROUP + top_i).reshape(-1).astype(jnp.int32)
    n_asg = eid.shape[0]
    onehot = (eid[:, None] == jnp.arange(N_EXPERTS, dtype=jnp.int32)[None, :]).astype(jnp.int32)
    within = jnp.take_along_axis(jnp.cumsum(onehot, axis=0), eid[:, None], axis=1)[:, 0] - 1
    sizes = jnp.sum(onehot, axis=0)
    padded = (sizes + MOE_ROWS - 1) // MOE_ROWS * MOE_ROWS
    pend = jnp.cumsum(padded)
    dest = (pend - padded)[eid] + within
    n_blk = -(-n_asg // MOE_ROWS) + N_EXPERTS
    src = jnp.zeros((n_blk * MOE_ROWS,), jnp.int32).at[dest].set(jnp.arange(n_asg, dtype=jnp.int32) // TOP_K)
    xpad = f.astype(BF16)[src]
    blk_start = jnp.arange(n_blk, dtype=jnp.int32) * MOE_ROWS
    blk_e = jnp.minimum(jnp.sum((pend[None, :] <= blk_start[:, None]).astype(jnp.int32), axis=1), N_EXPERTS - 1)
    n_used = (pend[-1] // MOE_ROWS).astype(jnp.int32).reshape(1)
    ypad = moe_experts(xpad, blk_e, n_used, w['wg_b'], w['wu_b'], w['wd_b'])
    y01 = ypad[dest.reshape(n_tok, TOP_K).T.reshape(-1)]
    return y01, jnp.pad(wts, ((0, 0), (0, 128 - TOP_K)))


def _token_tail(x2, mix_in, p2, w, tm):
    h = outproj(x2, mix_in[0], mix_in[1], w['w_out_b'], tm)
    y01, wts = _moe(h, w)
    return moe_combine_ple(h, y01, wts, p2, w['wpg_b'], w['wpp_b'], w['ple_norm'], tm)


def _ssd_finish(y_hm, z, w, length):
    bsz = y_hm.shape[0]
    y = y_hm[:, :, :length].transpose(0, 2, 1, 3).reshape(bsz, length, D_INNER)
    gated = y * jax.nn.silu(z)
    return _rmsnorm(gated, w['ssm_norm']).astype(BF16)


def _prompt_group(x, p, w):
    bsz, t, _ = x.shape
    m = bsz * t
    q, kvs, z, xbc, sm = rms_inproj(x.reshape(m, D_MODEL), w['attn_norm'], w['w_in_r'], 512)
    pos = jnp.arange(t)
    scale = HEAD_DIM ** -0.5
    qn = _rmsnorm(q.reshape(bsz, t, N_HEADS, HEAD_DIM), w['q_norm'])
    qr = _rope(qn, pos)

    nq = t // NSA_TQ

    def heads_t(a):
        a = a.reshape(bsz, nq, NSA_TQ, KV_HEADS, HPG, -1).transpose(0, 3, 1, 5, 4, 2)
        return a.reshape(bsz, KV_HEADS, nq, -1, HPG * NSA_TQ)

    def tiles_t(a, tile):
        return a.astype(BF16).reshape(bsz, t // tile, tile, KV_HEADS, HEAD_DIM).transpose(0, 3, 1, 4, 2)

    kvs = kvs.reshape(bsz, t, 6, KV_HEADS, HEAD_DIM)
    k_c = _rmsnorm(kvs[:, :, 0], w['k_norm'][0])
    k_s = _rope(_rmsnorm(kvs[:, :, 2], w['k_norm'][1]), pos)
    k_w = _rope(_rmsnorm(kvs[:, :, 4], w['k_norm'][2]), pos)
    kv_new = jnp.stack([k_c, kvs[:, :, 1], k_s, kvs[:, :, 3]], axis=2)
    win_new = jnp.stack([k_w, kvs[:, :, 5]], axis=2)

    def group_major(a):
        return a.astype(BF16).transpose(0, 2, 1, 3)

    n_seg = t // S_CMP
    segs = jnp.stack([group_major(k_c), group_major(kvs[:, :, 1])]).reshape(2, bsz * KV_HEADS, n_seg, S_CMP * HEAD_DIM)
    kvc = compress(segs, w['cmp_pe'], w['cmp_w1'], w['cmp_w2'])
    gates_t = jnp.pad(heads_t(sm[:, :N_GATES]), ((0, 0), (0, 0), (0, 0), (0, 8 - 3), (0, 0)))
    n_slc = t // L_SLC
    blk_onehot = (jnp.arange(t)[:, None] // L_SLC == jnp.arange(n_slc)[None, :]).astype(BF16)
    ks_aug = jnp.concatenate(
        [group_major(k_s), jnp.broadcast_to(blk_onehot, (bsz, KV_HEADS, t, n_slc))], axis=-1)
    attn = nsa_prompt(heads_t((qn * scale).astype(BF16)), heads_t((qr * scale).astype(BF16)), kvc[0],
                      kvc[1].transpose(0, 2, 1), ks_aug, tiles_t(kvs[:, :, 3], NSA_TK), group_major(k_w),
                      tiles_t(kvs[:, :, 5], NSA_TKW), gates_t, bsz, t)
    attn = attn.reshape(bsz, KV_HEADS, nq, HEAD_DIM, HPG, NSA_TQ).transpose(0, 2, 5, 1, 4, 3).reshape(m, ATT_W)

    xbc3 = xbc.reshape(bsz, t, CONV_DIM)
    xbc_all = jnp.concatenate([jnp.zeros((bsz, CONV_W - 1, CONV_DIM), F32), xbc3], axis=1)
    dt_raw = sm[:, N_GATES:N_GATES + SSM_HEADS].reshape(bsz, t, SSM_HEADS)
    xs, b_t, c_in, a_col, a_row, dt_col = _ssd_inputs(xbc_all, dt_raw, w, t)
    s0 = jnp.zeros((bsz, SSM_HEADS, D_STATE, SSM_HEAD_DIM), F32)
    y_hm, s_t = ssd_scan(w['ssm_d'], xs, b_t, c_in, a_col, a_row, dt_col, s0, SSM_CHUNK)
    ssd = _ssd_finish(y_hm, z.reshape(bsz, t, D_INNER), w, t).reshape(m, D_INNER)

    y = _token_tail(x.reshape(m, D_MODEL), (attn, ssd), p.reshape(m, PLE_DIM), w, 512)
    keep = min(WINDOW, t)
    return (y.reshape(bsz, t, D_MODEL), kv_new, win_new[:, t - keep:], s_t.transpose(0, 1, 3, 2),
            xbc_all[:, xbc_all.shape[1] - (CONV_W - 1):])


def _sample_attention(qn, qr, kvc, kv_new, win_all, pool, page_table, q_off, gl):
    bt, lq = qn.shape[:2]
    scale = HEAD_DIM ** -0.5
    t_kv = q_off + lq
    n_cmp = (t_kv - L_CMP) // S_CMP + 1
    n_slc = -(-t_kv // L_SLC)
    kc = kvc[0].reshape(bt, KV_HEADS, -1, HEAD_DIM)[:, :, :n_cmp].astype(F32)
    vc = kvc[1].reshape(bt, KV_HEADS, -1, HEAD_DIM)[:, :, :n_cmp].astype(F32)
    qp = qn.reshape(bt, lq, KV_HEADS, HPG, HEAD_DIM)
    qrr = qr.reshape(bt, lq, KV_HEADS, HPG, HEAD_DIM)
    pos = q_off + jnp.arange(lq)
    cmp_end = jnp.arange(n_cmp) * S_CMP + (L_CMP - 1)
    s1 = jnp.einsum('bqghd,bgnd->bghqn', qp, kc, preferred_element_type=F32) * scale
    p1 = _masked_softmax(s1, cmp_end[None, :] <= pos[:, None])
    o_c = jnp.einsum('bghqn,bgnd->bqghd', p1, vc, preferred_element_type=F32)
    imp = jnp.einsum('bghqn,nj->bgqj', p1, _cover_matrix(n_cmp, n_slc), preferred_element_type=F32)
    blk = jnp.arange(n_slc)
    cur = pos // L_SLC
    vis = blk[None, :] <= cur[:, None]
    forced = vis & ((blk[None, :] == 0) | (blk[None, :] >= cur[:, None] - 1))
    score = jnp.where(forced, 1e9, jnp.where(vis, imp, -1.0))
    top_s, idx = lax.top_k(score, min(N_SEL, n_slc))
    n_past_blk = q_off // L_SLC
    per_page = PAGE_SIZE // L_SLC
    pool6 = pool.reshape(pool.shape[0], per_page, L_SLC, 4, KV_HEADS, HEAD_DIM)
    idx_c = jnp.minimum(idx, n_past_blk - 1)
    bi = jnp.arange(bt)[:, None, None, None]
    gi = jnp.arange(KV_HEADS)[None, :, None, None]
    pages = page_table[bi, idx_c // per_page]
    kg = pool6[pages, idx_c % per_page, :, 2, gi]
    vg = pool6[pages, idx_c % per_page, :, 3, gi]
    new_k = jnp.pad(kv_new[:, :, 2], ((0, 0), (0, L_SLC - lq), (0, 0), (0, 0))).transpose(0, 2, 1, 3)
    new_v = jnp.pad(kv_new[:, :, 3], ((0, 0), (0, L_SLC - lq), (0, 0), (0, 0))).transpose(0, 2, 1, 3)
    is_new = (idx >= n_past_blk)[..., None, None]
    kg = jnp.where(is_new, new_k[:, :, None, None], kg)
    vg = jnp.where(is_new, new_v[:, :, None, None], vg)
    k_sel = idx.shape[-1]
    kg = kg.reshape(bt, KV_HEADS, lq, k_sel * L_SLC, HEAD_DIM)
    vg = vg.reshape(bt, KV_HEADS, lq, k_sel * L_SLC, HEAD_DIM)
    kpos = idx[..., None] * L_SLC + jnp.arange(L_SLC)
    ok = ((top_s > -0.5)[..., None] & (kpos <= pos[None, None, :, None, None])).reshape(bt, KV_HEADS, lq, k_sel * L_SLC)
    s2 = jnp.einsum('bqghd,bgqmd->bghqm', qrr, kg, preferred_element_type=F32) * scale
    p2 = _masked_softmax(s2, ok[:, :, None])
    o_s = jnp.einsum('bghqm,bgqmd->bqghd', p2, vg, preferred_element_type=F32)
    kw, vw = win_all[:, :, 0], win_all[:, :, 1]
    n_w = win_all.shape[1]
    kpos_w = q_off + lq - n_w + jnp.arange(n_w)
    m3 = (kpos_w[None, :] <= pos[:, None]) & (kpos_w[None, :] > pos[:, None] - WINDOW) & (kpos_w >= 0)[None, :]
    s3 = jnp.einsum('bqghd,bkgd->bghqk', qrr, kw, preferred_element_type=F32) * scale
    p3 = _masked_softmax(s3, m3)
    o_w = jnp.einsum('bghqk,bkgd->bqghd', p3, vw, preferred_element_type=F32)
    g = jax.nn.sigmoid(gl).reshape(bt, lq, N_HEADS, 3, 1)
    merge = lambda o: o.reshape(bt, lq, N_HEADS, HEAD_DIM)
    attn = g[:, :, :, 0] * merge(o_c) + g[:, :, :, 1] * merge(o_s) + g[:, :, :, 2] * merge(o_w)
    return attn.astype(BF16).reshape(bt * lq, ATT_W)


def _sample_group(x, p, pool, page_table, cache_win, state_ssm, state_conv, w):
    bsz, lq, _ = x.shape
    m = bsz * lq
    q_off = page_table.shape[1] * PAGE_SIZE
    q, kvs, z, xbc, sm = rms_inproj(x.reshape(m, D_MODEL), w['attn_norm'], w['w_in_r'], m)
    pos = q_off + jnp.arange(lq)
    qn = _rmsnorm(q.reshape(bsz, lq, N_HEADS, HEAD_DIM), w['q_norm'])
    qr = _rope(qn, pos)
    kvs = kvs.reshape(bsz, lq, 6, KV_HEADS, HEAD_DIM)
    k_c = _rmsnorm(kvs[:, :, 0], w['k_norm'][0])
    k_s = _rope(_rmsnorm(kvs[:, :, 2], w['k_norm'][1]), pos)
    k_w = _rope(_rmsnorm(kvs[:, :, 4], w['k_norm'][2]), pos)
    kv_new = jnp.stack([k_c, kvs[:, :, 1], k_s, kvs[:, :, 3]], axis=2)
    win_new = jnp.stack([k_w, kvs[:, :, 5]], axis=2)
    win_cat = jnp.concatenate([cache_win, win_new], axis=1)

    kvc = compress_paged(jnp.transpose(pool, (0, 2, 3, 4, 1)), page_table, w['cmp_pe'], w['cmp_w1'], w['cmp_w2'])
    attn = _sample_attention(qn, qr, kvc, kv_new, win_cat, pool, page_table, q_off, sm[:, :N_GATES])

    xbc_all = jnp.concatenate([state_conv, xbc.reshape(bsz, lq, CONV_DIM)], axis=1)
    dt_raw = sm[:, N_GATES:N_GATES + SSM_HEADS].reshape(bsz, lq, SSM_HEADS)
    xs, b_t, c_in, a_col, a_row, dt_col = _ssd_inputs(xbc_all, dt_raw, w, SSM_CHUNK)
    y_hm, s_t = ssd_scan(w['ssm_d'], xs, b_t, c_in, a_col, a_row, dt_col, state_ssm.transpose(0, 1, 3, 2), SSM_CHUNK)
    ssd = _ssd_finish(y_hm, z.reshape(bsz, lq, D_INNER), w, lq).reshape(m, D_INNER)

    y = _token_tail(x.reshape(m, D_MODEL), (attn, ssd), p.reshape(m, PLE_DIM), w, m)
    keep = cache_win.shape[1]
    return (y.reshape(bsz, lq, D_MODEL), kv_new, win_cat[:, win_cat.shape[1] - keep:], s_t.transpose(0, 1, 3, 2),
            xbc_all[:, xbc_all.shape[1] - (CONV_W - 1):])


def kernel(x_prompt, x_sample, cache_kv, cache_win, state_ssm, state_conv, page_table, p_prompt, p_sample,
           w_in, w_out, q_norm, k_norm, cmp_pe, cmp_w1, cmp_w2, conv_w, conv_b, dt_bias, a_log, ssm_d, ssm_norm,
           attn_norm, ffn_norm, w_rg, b_rg, w_re, b_re, w_gate, w_up, w_down, w_ple_proj, ple_norm, w_ple_gate):
    depth = w_in.shape[0]
    hp, hs = x_prompt, x_sample
    outs = [[] for _ in range(8)]
    cuts = np.cumsum((ATT_W, 6 * KV_HEADS * HEAD_DIM, N_GATES, D_INNER, CONV_DIM, SSM_HEADS))
    for l in range(depth):
        wi = w_in[l]
        w_in_r = jnp.concatenate(
            [wi[:, :cuts[1]], wi[:, cuts[2]:cuts[3]], wi[:, cuts[3]:cuts[4]], wi[:, cuts[1]:cuts[2]],
             wi[:, cuts[4]:cuts[5]], jnp.zeros((D_MODEL, C_SM - N_GATES - SSM_HEADS), F32)], axis=1).astype(BF16)
        w = dict(w_in_r=w_in_r, w_out_b=w_out[l].astype(BF16), q_norm=q_norm[l], k_norm=k_norm[l],
                 cmp_pe=cmp_pe[l], cmp_w1=cmp_w1[l], cmp_w2=cmp_w2[l], conv_w=conv_w[l], conv_b=conv_b[l],
                 dt_bias=dt_bias[l], a_log=a_log[l], ssm_d=ssm_d[l], ssm_norm=ssm_norm[l], attn_norm=attn_norm[l],
                 ffn_norm=ffn_norm[l], w_rg=w_rg[l], b_rg=b_rg[l], w_re=w_re[l], b_re=b_re[l],
                 wg_b=w_gate[l].astype(BF16), wu_b=w_up[l].astype(BF16), wd_b=w_down[l].astype(BF16),
                 wpp_b=w_ple_proj[l].astype(BF16), ple_norm=ple_norm[l], wpg_b=w_ple_gate[l].astype(BF16))
        hp, *rest_p = _prompt_group(hp, p_prompt[l], w)
        hs, *rest_s = _sample_group(hs, p_sample[l], cache_kv[l], page_table, cache_win[l], state_ssm[l],
                                    state_conv[l], w)
        for j in range(4):
            outs[2 * j].append(rest_p[j])
            outs[2 * j + 1].append(rest_s[j])
    return (hp, hs) + tuple(jnp.stack(o) for o in outs)
```

```python
import functools
import math

import numpy as np
import jax
import jax.numpy as jnp
from jax import lax
from jax.experimental import pallas as pl
from jax.experimental.pallas import tpu as pltpu

F32 = jnp.float32
BF16 = jnp.bfloat16

D_MODEL = 1024
PAGE_SIZE = 128
N_HEADS = 8
HEAD_DIM = 64
KV_HEADS = 2
HPG = N_HEADS // KV_HEADS
ATT_W = N_HEADS * HEAD_DIM
L_CMP = 32
S_CMP = 16
L_SLC = 64
N_SEL = 16
WINDOW = 512
CMP_HID = 64
ROPE_THETA = 10000.0
SSM_HEADS = 8
SSM_HEAD_DIM = 64
D_INNER = SSM_HEADS * SSM_HEAD_DIM
SSM_GROUPS = 2
D_STATE = 64
CONV_W = 4
CONV_DIM = D_INNER + 2 * SSM_GROUPS * D_STATE
SSM_CHUNK = 128
N_EXPERT_GROUPS = 4
EXPERTS_PER_GROUP = 8
N_EXPERTS = N_EXPERT_GROUPS * EXPERTS_PER_GROUP
TOP_K = 2
D_EXPERT = 512
PLE_DIM = 256
EPS = 1e-6
N_GATES = 3 * N_HEADS
C_Q, C_KV, C_Z, C_XBC, C_SM = 512, 768, 512, 768, 128
D_IN_PAD = C_Q + C_KV + C_Z + C_XBC + C_SM

NEG = -1e30
VMEM_LIMIT = 48 * 1024 * 1024
MOE_ROWS = 256


def _cparams(sem):
    return pltpu.CompilerParams(dimension_semantics=sem, vmem_limit_bytes=VMEM_LIMIT)


def _dot(a, b):
    return jnp.dot(a, b, preferred_element_type=F32)


def _dot_nt(a, b):
    return lax.dot_general(a, b, (((1,), (1,)), ((), ())), preferred_element_type=F32)


def _dot_hi(a, b):
    return jnp.dot(a, b, preferred_element_type=F32, precision=lax.Precision.HIGHEST)


def _rms_inproj_kernel(x_ref, g_ref, w_ref, q_ref, kv_ref, z_ref, xbc_ref, sm_ref):
    x = x_ref[...]
    y = x * lax.rsqrt(jnp.mean(x * x, axis=-1, keepdims=True) + EPS) * g_ref[...]
    yb = y.astype(BF16)
    c0 = 0
    for ref, width in ((q_ref, C_Q), (kv_ref, C_KV), (z_ref, C_Z), (xbc_ref, C_XBC), (sm_ref, C_SM)):
        ref[...] = _dot(yb, w_ref[:, c0:c0 + width])
        c0 += width


def rms_inproj(x, gain, w_r, tm):
    m = x.shape[0]
    widths = (C_Q, C_KV, C_Z, C_XBC, C_SM)
    return pl.pallas_call(
        _rms_inproj_kernel,
        grid=(m // tm,),
        in_specs=[pl.BlockSpec((tm, D_MODEL), lambda i: (i, 0)),
                  pl.BlockSpec((1, D_MODEL), lambda i: (0, 0)),
                  pl.BlockSpec((D_MODEL, D_IN_PAD), lambda i: (0, 0))],
        out_specs=[pl.BlockSpec((tm, w), lambda i: (i, 0)) for w in widths],
        out_shape=[jax.ShapeDtypeStruct((m, w), F32) for w in widths],
        compiler_params=_cparams(("parallel",)),
        name="rms_inproj",
    )(x, gain.reshape(1, D_MODEL), w_r)


def _outproj_kernel(x_ref, a_ref, s_ref, w_ref, h_ref):
    acc = _dot(a_ref[...], w_ref[0:ATT_W, :]) + _dot(s_ref[...], w_ref[ATT_W:ATT_W + D_INNER, :])
    h_ref[...] = x_ref[...] + acc


def outproj(x, attn, ssd, w_out_b, tm):
    m = x.shape[0]
    return pl.pallas_call(
        _outproj_kernel,
        grid=(m // tm,),
        in_specs=[pl.BlockSpec((tm, D_MODEL), lambda i: (i, 0)),
                  pl.BlockSpec((tm, ATT_W), lambda i: (i, 0)),
                  pl.BlockSpec((tm, D_INNER), lambda i: (i, 0)),
                  pl.BlockSpec((ATT_W + D_INNER, D_MODEL), lambda i: (0, 0))],
        out_specs=pl.BlockSpec((tm, D_MODEL), lambda i: (i, 0)),
        out_shape=jax.ShapeDtypeStruct((m, D_MODEL), F32),
        compiler_params=_cparams(("parallel",)),
        name="outproj",
    )(x, attn, ssd, w_out_b)


def _ple_kernel(h_ref, y0_ref, y1_ref, wt_ref, p_ref, wg_ref, wp_ref, g_ref, o_ref):
    wt = wt_ref[...]
    h = h_ref[...] + (y0_ref[...] * wt[:, 0:1] + y1_ref[...] * wt[:, 1:2])
    gate = jax.nn.sigmoid(_dot(h.astype(BF16), wg_ref[...]))
    e = _dot(p_ref[...].astype(BF16), wp_ref[...])
    e = e * lax.rsqrt(jnp.mean(e * e, axis=-1, keepdims=True) + EPS) * g_ref[...]
    o_ref[...] = h + gate * e


def moe_combine_ple(h, y01, wts, p, wg_b, wp_b, gain, tm):
    m = h.shape[0]
    nt = m // tm
    return pl.pallas_call(
        _ple_kernel,
        grid=(nt,),
        in_specs=[pl.BlockSpec((tm, D_MODEL), lambda i: (i, 0)),
                  pl.BlockSpec((tm, D_MODEL), lambda i: (i, 0)),
                  pl.BlockSpec((tm, D_MODEL), lambda i: (i + nt, 0)),
                  pl.BlockSpec((tm, 128), lambda i: (i, 0)),
                  pl.BlockSpec((tm, PLE_DIM), lambda i: (i, 0)),
                  pl.BlockSpec((D_MODEL, D_MODEL), lambda i: (0, 0)),
                  pl.BlockSpec((PLE_DIM, D_MODEL), lambda i: (0, 0)),
                  pl.BlockSpec((1, D_MODEL), lambda i: (0, 0))],
        out_specs=pl.BlockSpec((tm, D_MODEL), lambda i: (i, 0)),
        out_shape=jax.ShapeDtypeStruct((m, D_MODEL), F32),
        compiler_params=_cparams(("parallel",)),
        name="moe_combine_ple",
    )(h, y01, y01, wts, p, wg_b, wp_b, gain.reshape(1, D_MODEL))


def _moe_kernel(be_ref, nb_ref, x_ref, wg_ref, wu_ref, wd_ref, y_ref, wg_sc, wu_sc, wd_sc):
    i = pl.program_id(0)

    @pl.when((i == 0) | (be_ref[i] != be_ref[jnp.maximum(i - 1, 0)]))
    def _():
        wg_sc[...] = wg_ref[0].astype(BF16)
        wu_sc[...] = wu_ref[0].astype(BF16)
        wd_sc[...] = wd_ref[0].astype(BF16)

    @pl.when(i < nb_ref[0])
    def _():
        x = x_ref[...]
        a = _dot(x, wg_sc[...])
        hb = (a * jax.nn.sigmoid(a)) * _dot(x, wu_sc[...])
        y_ref[...] = _dot(hb.astype(BF16), wd_sc[...])

    @pl.when(i >= nb_ref[0])
    def _():
        y_ref[...] = jnp.zeros(y_ref.shape, F32)


def moe_experts(xpad, blk_e, n_used, w_gate, w_up, w_down):
    n_blk = xpad.shape[0] // MOE_ROWS
    grid_spec = pltpu.PrefetchScalarGridSpec(
        num_scalar_prefetch=2,
        grid=(n_blk,),
        in_specs=[pl.BlockSpec((MOE_ROWS, D_MODEL), lambda i, be, nb: (i, 0)),
                  pl.BlockSpec((1, D_MODEL, D_EXPERT), lambda i, be, nb: (be[i], 0, 0)),
                  pl.BlockSpec((1, D_MODEL, D_EXPERT), lambda i, be, nb: (be[i], 0, 0)),
                  pl.BlockSpec((1, D_EXPERT, D_MODEL), lambda i, be, nb: (be[i], 0, 0))],
        out_specs=pl.BlockSpec((MOE_ROWS, D_MODEL), lambda i, be, nb: (i, 0)),
        scratch_shapes=[pltpu.VMEM((D_MODEL, D_EXPERT), BF16), pltpu.VMEM((D_MODEL, D_EXPERT), BF16),
                        pltpu.VMEM((D_EXPERT, D_MODEL), BF16)],
    )
    return pl.pallas_call(
        _moe_kernel,
        grid_spec=grid_spec,
        out_shape=jax.ShapeDtypeStruct((n_blk * MOE_ROWS, D_MODEL), F32),
        compiler_params=_cparams(("arbitrary",)),
        name="moe_experts",
    )(blk_e, n_used, xpad, w_gate, w_up, w_down)


def _compress_kernel(seg_ref, w1a_ref, w1b_ref, pe_ref, w1_ref, w2_ref, o_ref):
    seg = seg_ref[0, 0]
    n_seg = seg.shape[0]
    hid0 = _dot(pe_ref[0], w1_ref[0])[0:1]
    p0 = _dot(seg, w1a_ref[0])
    p1 = pltpu.roll(_dot(seg, w1b_ref[0]), n_seg - 1, axis=0)
    hid = hid0 + p0 + p1
    act = hid * jax.nn.sigmoid(hid)
    o_ref[0, 0] = _dot(act.astype(BF16), w2_ref[0]).astype(BF16)


def compress(segs, cmp_pe, cmp_w1, cmp_w2):
    _, r, n_seg, k = segs.shape
    w1 = cmp_w1.astype(BF16)
    pe = jnp.broadcast_to(cmp_pe.reshape(2, 1, L_CMP * HEAD_DIM), (2, 8, L_CMP * HEAD_DIM)).astype(BF16)
    return pl.pallas_call(
        _compress_kernel,
        grid=(2, r),
        in_specs=[pl.BlockSpec((1, 1, n_seg, k), lambda s, i: (s, i, 0, 0)),
                  pl.BlockSpec((1, k, CMP_HID), lambda s, i: (s, 0, 0)),
                  pl.BlockSpec((1, k, CMP_HID), lambda s, i: (s, 1, 0)),
                  pl.BlockSpec((1, 8, 2 * k), lambda s, i: (s, 0, 0)),
                  pl.BlockSpec((1, 2 * k, CMP_HID), lambda s, i: (s, 0, 0)),
                  pl.BlockSpec((1, CMP_HID, HEAD_DIM), lambda s, i: (s, 0, 0))],
        out_specs=pl.BlockSpec((1, 1, n_seg, HEAD_DIM), lambda s, i: (s, i, 0, 0)),
        out_shape=jax.ShapeDtypeStruct((2, r, n_seg, HEAD_DIM), BF16),
        compiler_params=_cparams(("parallel", "parallel")),
        name="compress",
    )(segs, w1, w1, pe, w1, cmp_w2.astype(BF16))


PAGES_PER_STEP = 16


def _page_copies(pt_ref, pool_ref, buf, sem, b, j, slot):
    return [pltpu.make_async_copy(pool_ref.at[pt_ref[b, j * PAGES_PER_STEP + p], pl.ds(0, 2)],
                                  buf.at[slot, p], sem.at[slot]) for p in range(PAGES_PER_STEP)]


def _paged_partials_kernel(pt_ref, pool_ref, wa_ref, wb_ref, p0_ref, p1_ref, buf, sem, x_sc):
    b = pl.program_id(0)
    j = pl.program_id(1)
    n_grp = pl.num_programs(1)
    step = b * n_grp + j
    slot = step % 2

    @pl.when(step == 0)
    def _():
        for c in _page_copies(pt_ref, pool_ref, buf, sem, b, j, slot):
            c.start()

    @pl.when(step + 1 < pl.num_programs(0) * n_grp)
    def _():
        wrap = j + 1 == n_grp
        for c in _page_copies(pt_ref, pool_ref, buf, sem, jnp.where(wrap, b + 1, b), jnp.where(wrap, 0, j + 1),
                              1 - slot):
            c.start()

    for c in _page_copies(pt_ref, pool_ref, buf, sem, b, j, slot):
        c.wait()

    n_seg = PAGES_PER_STEP * PAGE_SIZE // S_CMP
    for kv in range(2):
        for g in range(KV_HEADS):
            for p in range(PAGES_PER_STEP):
                x_sc[p * PAGE_SIZE:(p + 1) * PAGE_SIZE, :] = buf[slot, p, kv, g].T
            acc0 = jnp.zeros((n_seg, CMP_HID), F32)
            acc1 = jnp.zeros((n_seg, CMP_HID), F32)
            for s in range(S_CMP):
                xs = x_sc[pl.ds(s, n_seg, stride=S_CMP), :].astype(BF16)
                acc0 = acc0 + _dot(xs, wa_ref[kv, s])
                acc1 = acc1 + _dot(xs, wb_ref[kv, s])
            p0_ref[kv, 0, g] = acc0
            p1_ref[kv, 0, g] = acc1


def _compress_finish_kernel(p0_ref, p1_ref, pe_ref, w1_ref, w2_ref, o_ref):
    n_seg = p0_ref.shape[3]
    hid0 = _dot(pe_ref[0], w1_ref[0])[0:1]
    hid = hid0 + p0_ref[0, 0, 0] + pltpu.roll(p1_ref[0, 0, 0], n_seg - 1, axis=0)
    act = hid * jax.nn.sigmoid(hid)
    o_ref[0, 0] = _dot(act.astype(BF16), w2_ref[0]).astype(BF16)


def compress_paged(pool_t, page_table, cmp_pe, cmp_w1, cmp_w2):
    bsz, n_pages = page_table.shape
    assert n_pages % PAGES_PER_STEP == 0
    n_grp = n_pages // PAGES_PER_STEP
    seg_step = PAGES_PER_STEP * PAGE_SIZE // S_CMP
    n_seg = n_grp * seg_step
    w1 = cmp_w1.astype(BF16).reshape(2, L_CMP // S_CMP, S_CMP, HEAD_DIM, CMP_HID)
    part_shape = jax.ShapeDtypeStruct((2, bsz, KV_HEADS, n_seg, CMP_HID), F32)
    part_spec = pl.BlockSpec((2, 1, KV_HEADS, seg_step, CMP_HID), lambda b, j, pt: (0, b, 0, j, 0))
    wspec = pl.BlockSpec((2, S_CMP, HEAD_DIM, CMP_HID), lambda b, j, pt: (0, 0, 0, 0))
    p0, p1 = pl.pallas_call(
        _paged_partials_kernel,
        grid_spec=pltpu.PrefetchScalarGridSpec(
            num_scalar_prefetch=1,
            grid=(bsz, n_grp),
            in_specs=[pl.BlockSpec(memory_space=pl.ANY), wspec, wspec],
            out_specs=[part_spec, part_spec],
            scratch_shapes=[pltpu.VMEM((2, PAGES_PER_STEP, 2, KV_HEADS, HEAD_DIM, PAGE_SIZE), F32),
                            pltpu.SemaphoreType.DMA((2,)),
                            pltpu.VMEM((PAGES_PER_STEP * PAGE_SIZE, HEAD_DIM), F32)]),
        out_shape=[part_shape, part_shape],
        compiler_params=_cparams(("arbitrary", "arbitrary")),
        name="paged_partials",
    )(page_table, pool_t, w1[:, 0], w1[:, 1])
    pe = jnp.broadcast_to(cmp_pe.reshape(2, 1, L_CMP * HEAD_DIM), (2, 8, L_CMP * HEAD_DIM)).astype(BF16)
    pspec = pl.BlockSpec((1, 1, 1, n_seg, CMP_HID), lambda s, i: (s, i // KV_HEADS, i % KV_HEADS, 0, 0))
    return pl.pallas_call(
        _compress_finish_kernel,
        grid=(2, bsz * KV_HEADS),
        in_specs=[pspec, pspec,
                  pl.BlockSpec((1, 8, L_CMP * HEAD_DIM), lambda s, i: (s, 0, 0)),
                  pl.BlockSpec((1, L_CMP * HEAD_DIM, CMP_HID), lambda s, i: (s, 0, 0)),
                  pl.BlockSpec((1, CMP_HID, HEAD_DIM), lambda s, i: (s, 0, 0))],
        out_specs=pl.BlockSpec((1, 1, n_seg, HEAD_DIM), lambda s, i: (s, i, 0, 0)),
        out_shape=jax.ShapeDtypeStruct((2, bsz * KV_HEADS, n_seg, HEAD_DIM), BF16),
        compiler_params=_cparams(("parallel", "parallel")),
        name="compress_finish",
    )(p0, p1, pe, cmp_w1.astype(BF16), cmp_w2.astype(BF16))


FLASH_SPLIT = 4


def _flash_step(k, q_ref, v_t, m_sc, l_sc, acc_sc, keep=None):
    cw = q_ref.shape[1] // FLASH_SPLIT
    cols = [slice(i * cw, (i + 1) * cw) for i in range(FLASH_SPLIT)]
    scores = [_dot(k, q_ref[:, c]) for c in cols]
    for c, s_t in zip(cols, scores):
        if keep is not None:
            s_t = jnp.where(keep[:, c], s_t, NEG)
        m_prev = m_sc[:, c]
        m_new = jnp.maximum(m_prev, jnp.max(s_t, axis=0, keepdims=True))
        alpha = jnp.exp(m_prev - m_new)
        p = jnp.exp(s_t - m_new)
        l_sc[:, c] = alpha * l_sc[:, c] + jnp.sum(p, axis=0, keepdims=True)
        acc_sc[:, c] = alpha * acc_sc[:, c] + _dot(v_t, p.astype(BF16))
        m_sc[:, c] = m_new


def _flash_reset(m_sc, l_sc, acc_sc):
    m_sc[...] = jnp.full(m_sc.shape, NEG, F32)
    l_sc[...] = jnp.zeros(l_sc.shape, F32)
    acc_sc[...] = jnp.zeros(acc_sc.shape, F32)


def _nsa_prompt_kernel(qn_ref, qr_ref, kc_ref, vct_ref, ks_ref, vst_ref, kw_ref, vwt_ref, covert_ref, gate_ref,
                       o_ref, qa_sc, m_sc, l_sc, acc_sc, out_sc, score_sc, rank_sc, *, tq, tk, tkw):
    qi = pl.program_id(2)
    pos0 = qi * tq
    w = HPG * tq

    def positions(rows, base):
        kpos = base + lax.broadcasted_iota(jnp.int32, (rows, w), 0)
        qpos = pos0 + (lax.broadcasted_iota(jnp.int32, (rows, w), 1) & (tq - 1))
        return kpos, qpos

    gate = jax.nn.sigmoid(gate_ref[0, 0, 0])

    kc = kc_ref[0, 0]
    n_c = kc.shape[0]
    cidx, qpos_c = positions(n_c, 0)
    valid = cidx * S_CMP + (L_CMP - 1) <= qpos_c
    s1 = jnp.where(valid, _dot(kc, qn_ref[0, 0, 0]), NEG)
    e1 = jnp.exp(s1 - jnp.max(s1, axis=0, keepdims=True)) * valid.astype(F32)
    p1b = (e1 * (1.0 / jnp.maximum(jnp.sum(e1, axis=0, keepdims=True), 1e-30))).astype(BF16)
    out_sc[...] = gate[0:1] * _dot(vct_ref[0, 0], p1b)

    p1_stack = jnp.concatenate([p1b[:, hh * tq:(hh + 1) * tq] for hh in range(HPG)], axis=0)
    imp = _dot(covert_ref[...], p1_stack)
    n_s = imp.shape[0]
    blk = lax.broadcasted_iota(jnp.int32, (n_s, tq), 0)
    cur = (pos0 + lax.broadcasted_iota(jnp.int32, (n_s, tq), 1)) // L_SLC
    vis = blk <= cur
    forced = vis & ((blk == 0) | (blk >= cur - 1))
    score_sc[...] = jnp.where(forced, 1e9, jnp.where(vis, imp, -1.0))
    rank_sc[...] = jnp.zeros(rank_sc.shape, F32)
    last_blk = (pos0 + tq - 1) // L_SLC
    n_oct = n_s // 8
    for oi in range(n_oct):
        @pl.when(oi * 8 <= last_blk)
        def _():
            rows = [slice(8 * oj, 8 * oj + 8) for oj in range(n_oct)]
            parts = [rank_sc[r, :] for r in rows]
            for i in range(8 * oi, 8 * oi + 8):
                c = score_sc[i:i + 1, :]
                for oj, r in enumerate(rows):
                    sj = score_sc[r, :]
                    if oj > oi:
                        beats = c >= sj
                    elif oj < oi:
                        beats = c > sj
                    else:
                        above = lax.broadcasted_iota(jnp.int32, (8, tq), 0) > i - 8 * oi
                        beats = (c > sj) | ((c == sj) & above)
                    parts[oj] = parts[oj] + beats.astype(F32)
            for r, part in zip(rows, parts):
                rank_sc[r, :] = part
    sel_bias = jnp.where(vis & (rank_sc[...] < N_SEL), 0.0, NEG).astype(BF16)
    qa_sc[0:HEAD_DIM, :] = qr_ref[0, 0, 0]
    qa_sc[HEAD_DIM:HEAD_DIM + n_s, :] = jnp.concatenate([sel_bias] * HPG, axis=1)

    _flash_reset(m_sc, l_sc, acc_sc)
    n_full = pos0 // tk

    def sel_body(kt, carry):
        k0 = pl.multiple_of(kt * tk, tk)
        _flash_step(ks_ref[0, 0, pl.ds(k0, tk), :], qa_sc, vst_ref[0, 0, kt], m_sc, l_sc, acc_sc)
        return carry

    lax.fori_loop(0, n_full, sel_body, 0)
    k0 = pl.multiple_of(n_full * tk, tk)
    kpos, qpos = positions(tk, k0)
    _flash_step(ks_ref[0, 0, pl.ds(k0, tk), :], qa_sc, vst_ref[0, 0, n_full], m_sc, l_sc, acc_sc, kpos <= qpos)
    out_sc[...] = out_sc[...] + (gate[1:2] * (1.0 / l_sc[...])) * acc_sc[...]

    _flash_reset(m_sc, l_sc, acc_sc)
    qr = qr_ref.at[0, 0, 0]
    kpos, qpos = positions(tkw, pos0)
    _flash_step(kw_ref[0, 0, pl.ds(pl.multiple_of(pos0, tkw), tkw), :], qr, vwt_ref[0, 0, qi], m_sc, l_sc, acc_sc,
                kpos <= qpos)
    n_inner = WINDOW // tkw - 1

    def win_body(d, carry):
        kt = qi - 1 - d
        k0 = pl.multiple_of(kt * tkw, tkw)
        _flash_step(kw_ref[0, 0, pl.ds(k0, tkw), :], qr, vwt_ref[0, 0, kt], m_sc, l_sc, acc_sc)
        return carry

    lax.fori_loop(0, jnp.minimum(qi, n_inner), win_body, 0)

    @pl.when(qi > n_inner)
    def _():
        kt = qi - 1 - n_inner
        k0 = pl.multiple_of(kt * tkw, tkw)
        kpos, qpos = positions(tkw, k0)
        _flash_step(kw_ref[0, 0, pl.ds(k0, tkw), :], qr, vwt_ref[0, 0, kt], m_sc, l_sc, acc_sc,
                    kpos > qpos - WINDOW)

    o = out_sc[...] + (gate[2:3] * (1.0 / l_sc[...])) * acc_sc[...]
    o_ref[0, 0, 0] = o.astype(o_ref.dtype)


NSA_TQ, NSA_TK, NSA_TKW = 256, 256, 256


def nsa_prompt(qn_t, qr_t, kc, vc_t, ks_aug, vs_t, kw, vw_t, gates_t, bsz, t):
    tq, tk, tkw = NSA_TQ, NSA_TK, NSA_TKW
    assert tq == tkw and tk % tq == 0 and WINDOW % tkw == 0
    n_c = kc.shape[1]
    n_s = t // L_SLC
    w = HPG * tq
    c0 = np.arange(n_c)[None, :] * S_CMP
    s0 = np.arange(n_s)[:, None] * L_SLC
    cover_t = (np.maximum(np.minimum(c0 + L_CMP, s0 + L_SLC) - np.maximum(c0, s0), 0) / L_CMP).astype(np.float32)
    cover_t = np.tile(cover_t, (1, HPG))
    qspec = pl.BlockSpec((1, 1, 1, HEAD_DIM, w), lambda b, g, i: (b, g, i, 0, 0))
    kern = functools.partial(_nsa_prompt_kernel, tq=tq, tk=tk, tkw=tkw)
    return pl.pallas_call(
        kern,
        grid=(bsz, KV_HEADS, t // tq),
        in_specs=[qspec, qspec,
                  pl.BlockSpec((1, 1, n_c, HEAD_DIM), lambda b, g, i: (b, g, 0, 0)),
                  pl.BlockSpec((1, 1, HEAD_DIM, n_c), lambda b, g, i: (b, g, 0, 0)),
                  pl.BlockSpec((1, 1, t, HEAD_DIM + n_s), lambda b, g, i: (b, g, 0, 0)),
                  pl.BlockSpec((1, 1, t // tk, HEAD_DIM, tk), lambda b, g, i: (b, g, 0, 0, 0)),
                  pl.BlockSpec((1, 1, t, HEAD_DIM), lambda b, g, i: (b, g, 0, 0)),
                  pl.BlockSpec((1, 1, t // tkw, HEAD_DIM, tkw), lambda b, g, i: (b, g, 0, 0, 0)),
                  pl.BlockSpec((n_s, HPG * n_c), lambda b, g, i: (0, 0)),
                  pl.BlockSpec((1, 1, 1, 8, w), lambda b, g, i: (b, g, i, 0, 0))],
        out_specs=qspec,
        out_shape=jax.ShapeDtypeStruct((bsz, KV_HEADS, t // tq, HEAD_DIM, w), BF16),
        scratch_shapes=[pltpu.VMEM((HEAD_DIM + n_s, w), BF16), pltpu.VMEM((1, w), F32), pltpu.VMEM((1, w), F32),
                        pltpu.VMEM((HEAD_DIM, w), F32), pltpu.VMEM((HEAD_DIM, w), F32),
                        pltpu.VMEM((n_s, tq), F32), pltpu.VMEM((n_s, tq), F32)],
        compiler_params=_cparams(("parallel", "parallel", "arbitrary")),
        name="nsa_prompt",
    )(qn_t, qr_t, kc.reshape(bsz, KV_HEADS, n_c, HEAD_DIM), vc_t.reshape(bsz, KV_HEADS, HEAD_DIM, n_c),
      ks_aug, vs_t, kw, vw_t, jnp.asarray(cover_t, BF16), gates_t)


def _sample_select_kernel(qn_ref, kc_ref, vc_ref, cover_ref, oc_ref, sb_ref, *, q_off, lq, n_cmp, n_slc):
    rows = HPG * lq
    for g in range(KV_HEADS):
        kc = kc_ref[0, g]
        n_c = kc.shape[0]
        cidx = lax.broadcasted_iota(jnp.int32, (rows, n_c), 1)
        qpos = q_off + (lax.broadcasted_iota(jnp.int32, (rows, n_c), 0) & (lq - 1))
        valid = (cidx * S_CMP + (L_CMP - 1) <= qpos) & (cidx < n_cmp)
        s1 = jnp.where(valid, _dot_nt(qn_ref[0, g], kc), NEG)
        e1 = jnp.exp(s1 - jnp.max(s1, axis=-1, keepdims=True)) * valid.astype(F32)
        p1b = (e1 * (1.0 / jnp.maximum(jnp.sum(e1, axis=-1, keepdims=True), 1e-30))).astype(BF16)
        oc_ref[0, g] = _dot(p1b, vc_ref[0, g])
        imp4 = _dot(p1b, cover_ref[...])
        imp = imp4[0:lq]
        for hh in range(1, HPG):
            imp = imp + imp4[hh * lq:(hh + 1) * lq]
        n_sp = imp.shape[1]
        blk = lax.broadcasted_iota(jnp.int32, (lq, n_sp), 1)
        cur = (q_off + lax.broadcasted_iota(jnp.int32, (lq, n_sp), 0)) // L_SLC
        vis = (blk <= cur) & (blk < n_slc)
        forced = vis & ((blk == 0) | (blk >= cur - 1))
        score = jnp.where(forced, 1e9, jnp.where(vis, imp, -1.0))
        rank = jnp.zeros((lq, n_sp), F32)
        for i in range(n_slc):
            c = score[:, i:i + 1]
            beats = (c > score) | ((c == score) & (blk > i))
            rank = rank + beats.astype(F32)
        sb_ref[0, g] = jnp.where(vis & (rank < min(N_SEL, n_slc)), 0.0, NEG)


def _sample_attend_kernel(pt_ref, pool_ref, qr_ref, sb_ref, exp_ref, oc_ref, gate_ref, kn_ref, vn_ref, kw_ref, vw_ref,
                          o_ref, buf, sem, m_sc, l_sc, acc_sc, *, q_off, lq, n_win):
    b = pl.program_id(0)
    j = pl.program_id(1)
    n_grp = pl.num_programs(1)
    step = b * n_grp + j
    slot = step % 2
    rows = HPG * lq

    def copies(bb, jj, sl):
        return [pltpu.make_async_copy(pool_ref.at[pt_ref[bb, jj * PAGES_PER_STEP + p], pl.ds(2, 2)],
                                      buf.at[sl, p], sem.at[sl]) for p in range(PAGES_PER_STEP)]

    @pl.when(step == 0)
    def _():
        for c in copies(b, j, slot):
            c.start()

    @pl.when(step + 1 < pl.num_programs(0) * n_grp)
    def _():
        wrap = j + 1 == n_grp
        for c in copies(jnp.where(wrap, b + 1, b), jnp.where(wrap, 0, j + 1), 1 - slot):
            c.start()

    for c in copies(b, j, slot):
        c.wait()

    @pl.when(j == 0)
    def _():
        m_sc[...] = jnp.full(m_sc.shape, NEG, F32)
        l_sc[...] = jnp.zeros(l_sc.shape, F32)
        acc_sc[...] = jnp.zeros(acc_sc.shape, F32)

    def update(g, s, v, v_transposed):
        m_prev = m_sc[g]
        m_new = jnp.maximum(m_prev, jnp.max(s, axis=-1, keepdims=True))
        alpha = jnp.exp(m_prev - m_new)
        p = jnp.exp(s - m_new)
        l_sc[g] = alpha * l_sc[g] + jnp.sum(p, axis=-1, keepdims=True)
        pv = _dot_nt(p.astype(BF16), v) if v_transposed else _dot(p.astype(BF16), v)
        acc_sc[g] = alpha * acc_sc[g] + pv
        m_sc[g] = m_new

    for g in range(KV_HEADS):
        k_t = jnp.concatenate([buf[slot, p, 0, g] for p in range(PAGES_PER_STEP)], axis=1).astype(BF16)
        v_t = jnp.concatenate([buf[slot, p, 1, g] for p in range(PAGES_PER_STEP)], axis=1).astype(BF16)
        sb = sb_ref[0, g, 0]
        bias = _dot(jnp.concatenate([sb] * HPG, axis=0).astype(BF16), exp_ref[...])
        update(g, _dot(qr_ref[0, g], k_t) + bias, v_t, True)

    @pl.when(j == n_grp - 1)
    def _():
        for g in range(KV_HEADS):
            qr = qr_ref[0, g]
            kn = kn_ref[0, g]
            kidx = lax.broadcasted_iota(jnp.int32, (rows, kn.shape[0]), 1)
            qidx = lax.broadcasted_iota(jnp.int32, (rows, kn.shape[0]), 0) & (lq - 1)
            update(g, jnp.where((kidx <= qidx) & (kidx < lq), _dot_nt(qr, kn), NEG), vn_ref[0, g], False)
            o_s = acc_sc[g] * (1.0 / l_sc[g])
            kw = kw_ref[0, g]
            widx = lax.broadcasted_iota(jnp.int32, (rows, kw.shape[0]), 1)
            kpos = q_off + lq - n_win + widx
            qpos = q_off + (lax.broadcasted_iota(jnp.int32, (rows, kw.shape[0]), 0) & (lq - 1))
            ok = (widx < n_win) & (kpos <= qpos) & (kpos > qpos - WINDOW) & (kpos >= 0)
            s3 = jnp.where(ok, _dot_nt(qr, kw), NEG)
            e3 = jnp.exp(s3 - jnp.max(s3, axis=-1, keepdims=True)) * ok.astype(F32)
            p3 = e3 * (1.0 / jnp.maximum(jnp.sum(e3, axis=-1, keepdims=True), 1e-30))
            o_w = _dot(p3.astype(BF16), vw_ref[0, g])
            gate = jax.nn.sigmoid(gate_ref[0, g])
            o_ref[0, g] = gate[:, 0:1] * oc_ref[0, g] + gate[:, 1:2] * o_s + gate[:, 2:3] * o_w


def nsa_sample(q, q_rot, kvc, kv_new, win_cat, pool_t, page_table, gate_logits, q_off):
    bsz, lq = q.shape[:2]
    assert lq & (lq - 1) == 0 and lq <= L_SLC and q_off % (PAGES_PER_STEP * PAGE_SIZE) == 0
    rows = HPG * lq
    scale = HEAD_DIM ** -0.5
    n_seg = kvc.shape[2]
    t_kv = q_off + lq
    n_cmp = (t_kv - L_CMP) // S_CMP + 1
    n_slc = -(-t_kv // L_SLC)
    n_sp = -(-n_slc // 128) * 128
    n_grp = page_table.shape[1] // PAGES_PER_STEP
    blk_step = PAGES_PER_STEP * PAGE_SIZE // L_SLC
    key_step = PAGES_PER_STEP * PAGE_SIZE

    def rows_major(a):
        return a.reshape(bsz, lq, KV_HEADS, HPG, -1).transpose(0, 2, 3, 1, 4).reshape(bsz, KV_HEADS, rows, -1)

    def keys_major(a, n_pad):
        a = a.astype(BF16).transpose(0, 2, 1, 3)
        return jnp.pad(a, ((0, 0), (0, 0), (0, n_pad - a.shape[2]), (0, 0)))

    qn_r = rows_major((q * scale).astype(BF16))
    qr_r = rows_major((q_rot * scale).astype(BF16))
    c0 = np.arange(n_seg)[:, None] * S_CMP
    s0 = np.arange(n_sp)[None, :] * L_SLC
    cover = np.maximum(np.minimum(c0 + L_CMP, s0 + L_SLC) - np.maximum(c0, s0), 0) / L_CMP
    cover = cover * (np.arange(n_seg)[:, None] < n_cmp) * (np.arange(n_sp)[None, :] < n_slc)
    kvc4 = kvc.reshape(2, bsz, KV_HEADS, n_seg, HEAD_DIM)
    spec_q = pl.BlockSpec((1, KV_HEADS, rows, HEAD_DIM), lambda b: (b, 0, 0, 0))
    o_c, sel_bias = pl.pallas_call(
        functools.partial(_sample_select_kernel, q_off=q_off, lq=lq, n_cmp=n_cmp, n_slc=n_slc),
        grid=(bsz,),
        in_specs=[spec_q,
                  pl.BlockSpec((1, KV_HEADS, n_seg, HEAD_DIM), lambda b: (b, 0, 0, 0)),
                  pl.BlockSpec((1, KV_HEADS, n_seg, HEAD_DIM), lambda b: (b, 0, 0, 0)),
                  pl.BlockSpec((n_seg, n_sp), lambda b: (0, 0))],
        out_specs=[spec_q, pl.BlockSpec((1, KV_HEADS, lq, n_sp), lambda b: (b, 0, 0, 0))],
        out_shape=[jax.ShapeDtypeStruct((bsz, KV_HEADS, rows, HEAD_DIM), F32),
                   jax.ShapeDtypeStruct((bsz, KV_HEADS, lq, n_sp), F32)],
        compiler_params=_cparams(("parallel",)),
        name="sample_select",
    )(qn_r, kvc4[0], kvc4[1], jnp.asarray(cover, BF16))

    sb_steps = sel_bias[..., :n_grp * blk_step].reshape(bsz, KV_HEADS, lq, n_grp, blk_step).transpose(0, 1, 3, 2, 4)
    expand = (np.arange(key_step)[None, :] // L_SLC == np.arange(blk_step)[:, None]).astype(np.float32)
    gates = jnp.pad(rows_major(gate_logits.reshape(bsz, lq, -1)), ((0, 0), (0, 0), (0, 0), (0, 128 - 3)))
    n_win = win_cat.shape[1]
    n_wp = -(-n_win // 128) * 128
    spec2 = lambda shape: pl.BlockSpec((1,) + shape, lambda b, j, pt: (b,) + (0,) * len(shape))
    attn = pl.pallas_call(
        functools.partial(_sample_attend_kernel, q_off=q_off, lq=lq, n_win=n_win),
        grid_spec=pltpu.PrefetchScalarGridSpec(
            num_scalar_prefetch=1,
            grid=(bsz, n_grp),
            in_specs=[pl.BlockSpec(memory_space=pl.ANY),
                      spec2((KV_HEADS, rows, HEAD_DIM)),
                      pl.BlockSpec((1, KV_HEADS, 1, lq, blk_step), lambda b, j, pt: (b, 0, j, 0, 0)),
                      pl.BlockSpec((blk_step, key_step), lambda b, j, pt: (0, 0)),
                      spec2((KV_HEADS, rows, HEAD_DIM)),
                      spec2((KV_HEADS, rows, 128)),
                      spec2((KV_HEADS, 128, HEAD_DIM)), spec2((KV_HEADS, 128, HEAD_DIM)),
                      spec2((KV_HEADS, n_wp, HEAD_DIM)), spec2((KV_HEADS, n_wp, HEAD_DIM))],
            out_specs=spec2((KV_HEADS, rows, HEAD_DIM)),
            scratch_shapes=[pltpu.VMEM((2, PAGES_PER_STEP, 2, KV_HEADS, HEAD_DIM, PAGE_SIZE), F32),
                            pltpu.SemaphoreType.DMA((2,)),
                            pltpu.VMEM((KV_HEADS, rows, 1), F32), pltpu.VMEM((KV_HEADS, rows, 1), F32),
                            pltpu.VMEM((KV_HEADS, rows, HEAD_DIM), F32)]),
        out_shape=jax.ShapeDtypeStruct((bsz, KV_HEADS, rows, HEAD_DIM), F32),
        compiler_params=_cparams(("arbitrary", "arbitrary")),
        name="sample_attend",
    )(page_table, pool_t, qr_r, sb_steps, jnp.asarray(expand, BF16), o_c, gates,
      keys_major(kv_new[:, :, 2], 128), keys_major(kv_new[:, :, 3], 128),
      keys_major(win_cat[:, :, 0], n_wp), keys_major(win_cat[:, :, 1], n_wp))
    attn = attn.reshape(bsz, KV_HEADS, HPG, lq, HEAD_DIM).transpose(0, 3, 1, 2, 4)
    return attn.astype(BF16).reshape(bsz * lq, ATT_W)


def _ssd_kernel(d_ref, x_ref, bt_ref, c_ref, acol_ref, arow_ref, dt_ref, s0_ref, y_ref, sout_ref, s_sc, *, cl):
    ci = pl.program_id(1)

    @pl.when(ci == 0)
    def _():
        s_sc[...] = s0_ref[0]

    li = lax.broadcasted_iota(jnp.int32, (cl, cl), 0)
    si = lax.broadcasted_iota(jnp.int32, (cl, cl), 1)
    causal = li >= si
    acum_col = _dot_hi(causal.astype(F32), acol_ref[0])
    acum_row = _dot_hi(arow_ref[0], (li <= si).astype(F32))
    dt = dt_ref[0]
    for g in range(SSM_GROUPS):
        cg = c_ref[0, g].astype(BF16)
        btg = bt_ref[0, g]
        cb = _dot(cg, btg.astype(BF16))
        for hh in range(SSM_HEADS // SSM_GROUPS):
            h = g * (SSM_HEADS // SSM_GROUPS) + hh
            ac = acum_col[:, h:h + 1]
            ar = acum_row[h:h + 1, :]
            decay = jnp.exp(jnp.where(causal, ac - ar, NEG))
            xs = x_ref[0, h]
            xd = (xs * dt[:, h:h + 1]).astype(BF16)
            st = s_sc[h]
            y = _dot((cb * decay).astype(BF16), xd) + _dot(cg, st.astype(BF16)) * jnp.exp(ac)
            y_ref[0, h] = y + d_ref[h] * xs
            a_last = ar[:, cl - 1:cl]
            snew = _dot((btg * jnp.exp(a_last - ar)).astype(BF16), xd)
            s_sc[h] = st * jnp.exp(a_last) + snew

    @pl.when(ci == pl.num_programs(1) - 1)
    def _():
        sout_ref[0] = s_sc[...]


def ssd_scan(ssm_d, x_hm, b_t, c, a_col, a_row, dt_col, s0_t, cl):
    bsz, nh, t, p = x_hm.shape
    n = D_STATE
    nc = t // cl
    kern = functools.partial(_ssd_kernel, cl=cl)
    return pl.pallas_call(
        kern,
        grid=(bsz, nc),
        in_specs=[pl.BlockSpec(memory_space=pltpu.SMEM),
                  pl.BlockSpec((1, nh, cl, p), lambda b, c_: (b, 0, c_, 0)),
                  pl.BlockSpec((1, SSM_GROUPS, n, cl), lambda b, c_: (b, 0, 0, c_)),
                  pl.BlockSpec((1, SSM_GROUPS, cl, n), lambda b, c_: (b, 0, c_, 0)),
                  pl.BlockSpec((1, cl, 128), lambda b, c_: (b, c_, 0)),
                  pl.BlockSpec((1, 8, cl), lambda b, c_: (b, 0, c_)),
                  pl.BlockSpec((1, cl, 128), lambda b, c_: (b, c_, 0)),
                  pl.BlockSpec((1, nh, n, p), lambda b, c_: (b, 0, 0, 0))],
        out_specs=[pl.BlockSpec((1, nh, cl, p), lambda b, c_: (b, 0, c_, 0)),
                   pl.BlockSpec((1, nh, n, p), lambda b, c_: (b, 0, 0, 0))],
        out_shape=[jax.ShapeDtypeStruct((bsz, nh, t, p), F32), jax.ShapeDtypeStruct((bsz, nh, n, p), F32)],
        scratch_shapes=[pltpu.VMEM((nh, n, p), F32)],
        compiler_params=_cparams(("parallel", "arbitrary")),
        name="ssd_scan",
    )(ssm_d, x_hm, b_t, c, a_col, a_row, dt_col, s0_t)


def _rmsnorm(x, g):
    return x * lax.rsqrt(jnp.mean(x * x, axis=-1, keepdims=True) + EPS) * g


def _rope(x, pos):
    half = HEAD_DIM // 2
    inv = ROPE_THETA ** (-jnp.arange(half, dtype=F32) / half)
    ang = pos.astype(F32)[:, None] * inv[None, :]
    cos, sin = jnp.cos(ang)[:, None, :], jnp.sin(ang)[:, None, :]
    x1, x2 = x[..., :half], x[..., half:]
    return jnp.concatenate([x1 * cos - x2 * sin, x2 * cos + x1 * sin], axis=-1)


def _ssd_inputs(xbc_all, dt_raw, w, t_pad):
    bsz = xbc_all.shape[0]
    length = xbc_all.shape[1] - (CONV_W - 1)
    conv = w['conv_b'] + xbc_all[:, 0:length] * w['conv_w'][0]
    for j in range(1, CONV_W):
        conv = conv + xbc_all[:, j:j + length] * w['conv_w'][j]
    act = jax.nn.silu(conv)
    dt = jax.nn.softplus(dt_raw + w['dt_bias'])
    a = -jnp.exp(w['a_log']) * dt
    pad = t_pad - length
    if pad:
        act = jnp.pad(act, ((0, 0), (0, pad), (0, 0)))
        dt = jnp.pad(dt, ((0, 0), (0, pad), (0, 0)))
        a = jnp.pad(a, ((0, 0), (0, pad), (0, 0)))
    xs = act[..., :D_INNER].reshape(bsz, t_pad, SSM_HEADS, SSM_HEAD_DIM).transpose(0, 2, 1, 3)
    b_in = act[..., D_INNER:D_INNER + SSM_GROUPS * D_STATE].reshape(bsz, t_pad, SSM_GROUPS, D_STATE)
    c_in = act[..., D_INNER + SSM_GROUPS * D_STATE:].reshape(bsz, t_pad, SSM_GROUPS, D_STATE)
    lane_pad = ((0, 0), (0, 0), (0, 128 - SSM_HEADS))
    return (xs, b_in.transpose(0, 2, 3, 1), c_in.transpose(0, 2, 1, 3), jnp.pad(a, lane_pad),
            a.transpose(0, 2, 1), jnp.pad(dt, lane_pad))


def _moe(h, w):
    n_tok = h.shape[0]
    f = _rmsnorm(h, w['ffn_norm'])
    hi = lax.Precision.HIGHEST
    pg = jax.nn.softmax(jnp.dot(f, w['w_rg'], precision=hi) + w['b_rg'], axis=-1)
    g_sel = jnp.argmax(pg, axis=-1)
    g_w = jnp.max(pg, axis=-1)
    el = (jnp.dot(f, w['w_re'], precision=hi) + w['b_re']).reshape(n_tok, N_EXPERT_GROUPS, EXPERTS_PER_GROUP)
    el = jnp.take_along_axis(el, g_sel[:, None, None], axis=1)[:, 0]
    top_p, top_i = lax.top_k(jax.nn.softmax(el, axis=-1), TOP_K)
    wts = g_w[:, None] * top_p / jnp.sum(top_p, axis=-1, keepdims=True)
    eid = (g_sel[:, None] * EXPERTS_PER_GROUP + top_i).reshape(-1).astype(jnp.int32)
    n_asg = eid.shape[0]
    onehot = (eid[:, None] == jnp.arange(N_EXPERTS, dtype=jnp.int32)[None, :]).astype(jnp.int32)
    within = jnp.take_along_axis(jnp.cumsum(onehot, axis=0), eid[:, None], axis=1)[:, 0] - 1
    sizes = jnp.sum(onehot, axis=0)
    padded = (sizes + MOE_ROWS - 1) // MOE_ROWS * MOE_ROWS
    pend = jnp.cumsum(padded)
    dest = (pend - padded)[eid] + within
    n_blk = -(-n_asg // MOE_ROWS) + N_EXPERTS
    src = jnp.zeros((n_blk * MOE_ROWS,), jnp.int32).at[dest].set(jnp.arange(n_asg, dtype=jnp.int32) // TOP_K)
    xpad = f.astype(BF16)[src]
    blk_start = jnp.arange(n_blk, dtype=jnp.int32) * MOE_ROWS
    blk_e = jnp.minimum(jnp.sum((pend[None, :] <= blk_start[:, None]).astype(jnp.int32), axis=1), N_EXPERTS - 1)
    n_used = (pend[-1] // MOE_ROWS).astype(jnp.int32).reshape(1)
    ypad = moe_experts(xpad, blk_e, n_used, w['w_gate'], w['w_up'], w['w_down'])
    y01 = ypad[dest.reshape(n_tok, TOP_K).T.reshape(-1)]
    return y01, jnp.pad(wts, ((0, 0), (0, 128 - TOP_K)))


def _token_tail(x2, mix_in, p2, w, tm):
    h = outproj(x2, mix_in[0], mix_in[1], w['w_out_b'], tm)
    y01, wts = _moe(h, w)
    return moe_combine_ple(h, y01, wts, p2, w['wpg_b'], w['wpp_b'], w['ple_norm'], tm)


def _ssd_finish(y_hm, z, w, length):
    bsz = y_hm.shape[0]
    y = y_hm[:, :, :length].transpose(0, 2, 1, 3).reshape(bsz, length, D_INNER)
    gated = y * jax.nn.silu(z)
    return _rmsnorm(gated, w['ssm_norm']).astype(BF16)


def _prompt_group(x, p, w):
    bsz, t, _ = x.shape
    m = bsz * t
    q, kvs, z, xbc, sm = rms_inproj(x.reshape(m, D_MODEL), w['attn_norm'], w['w_in_r'], 512)
    pos = jnp.arange(t)
    scale = HEAD_DIM ** -0.5
    qn = _rmsnorm(q.reshape(bsz, t, N_HEADS, HEAD_DIM), w['q_norm'])
    qr = _rope(qn, pos)

    nq = t // NSA_TQ

    def heads_t(a):
        a = a.reshape(bsz, nq, NSA_TQ, KV_HEADS, HPG, -1).transpose(0, 3, 1, 5, 4, 2)
        return a.reshape(bsz, KV_HEADS, nq, -1, HPG * NSA_TQ)

    def tiles_t(a, tile):
        return a.astype(BF16).reshape(bsz, t // tile, tile, KV_HEADS, HEAD_DIM).transpose(0, 3, 1, 4, 2)

    kvs = kvs.reshape(bsz, t, 6, KV_HEADS, HEAD_DIM)
    k_c = _rmsnorm(kvs[:, :, 0], w['k_norm'][0])
    k_s = _rope(_rmsnorm(kvs[:, :, 2], w['k_norm'][1]), pos)
    k_w = _rope(_rmsnorm(kvs[:, :, 4], w['k_norm'][2]), pos)
    kv_new = jnp.stack([k_c, kvs[:, :, 1], k_s, kvs[:, :, 3]], axis=2)
    win_new = jnp.stack([k_w, kvs[:, :, 5]], axis=2)

    def group_major(a):
        return a.astype(BF16).transpose(0, 2, 1, 3)

    n_seg = t // S_CMP
    segs = jnp.stack([group_major(k_c), group_major(kvs[:, :, 1])]).reshape(2, bsz * KV_HEADS, n_seg, S_CMP * HEAD_DIM)
    kvc = compress(segs, w['cmp_pe'], w['cmp_w1'], w['cmp_w2'])
    gates_t = jnp.pad(heads_t(sm[:, :N_GATES]), ((0, 0), (0, 0), (0, 0), (0, 8 - 3), (0, 0)))
    n_slc = t // L_SLC
    blk_onehot = (jnp.arange(t)[:, None] // L_SLC == jnp.arange(n_slc)[None, :]).astype(BF16)
    ks_aug = jnp.concatenate(
        [group_major(k_s), jnp.broadcast_to(blk_onehot, (bsz, KV_HEADS, t, n_slc))], axis=-1)
    attn = nsa_prompt(heads_t((qn * scale).astype(BF16)), heads_t((qr * scale).astype(BF16)), kvc[0],
                      kvc[1].transpose(0, 2, 1), ks_aug, tiles_t(kvs[:, :, 3], NSA_TK), group_major(k_w),
                      tiles_t(kvs[:, :, 5], NSA_TKW), gates_t, bsz, t)
    attn = attn.reshape(bsz, KV_HEADS, nq, HEAD_DIM, HPG, NSA_TQ).transpose(0, 2, 5, 1, 4, 3).reshape(m, ATT_W)

    xbc3 = xbc.reshape(bsz, t, CONV_DIM)
    xbc_all = jnp.concatenate([jnp.zeros((bsz, CONV_W - 1, CONV_DIM), F32), xbc3], axis=1)
    dt_raw = sm[:, N_GATES:N_GATES + SSM_HEADS].reshape(bsz, t, SSM_HEADS)
    xs, b_t, c_in, a_col, a_row, dt_col = _ssd_inputs(xbc_all, dt_raw, w, t)
    s0 = jnp.zeros((bsz, SSM_HEADS, D_STATE, SSM_HEAD_DIM), F32)
    y_hm, s_t = ssd_scan(w['ssm_d'], xs, b_t, c_in, a_col, a_row, dt_col, s0, SSM_CHUNK)
    ssd = _ssd_finish(y_hm, z.reshape(bsz, t, D_INNER), w, t).reshape(m, D_INNER)

    y = _token_tail(x.reshape(m, D_MODEL), (attn, ssd), p.reshape(m, PLE_DIM), w, 512)
    keep = min(WINDOW, t)
    return (y.reshape(bsz, t, D_MODEL), kv_new, win_new[:, t - keep:], s_t.transpose(0, 1, 3, 2),
            xbc_all[:, xbc_all.shape[1] - (CONV_W - 1):])


def _sample_group(x, p, pool, page_table, cache_win, state_ssm, state_conv, w):
    bsz, lq, _ = x.shape
    m = bsz * lq
    q_off = page_table.shape[1] * PAGE_SIZE
    q, kvs, z, xbc, sm = rms_inproj(x.reshape(m, D_MODEL), w['attn_norm'], w['w_in_r'], m)
    pos = q_off + jnp.arange(lq)
    qn = _rmsnorm(q.reshape(bsz, lq, N_HEADS, HEAD_DIM), w['q_norm'])
    qr = _rope(qn, pos)
    kvs = kvs.reshape(bsz, lq, 6, KV_HEADS, HEAD_DIM)
    k_c = _rmsnorm(kvs[:, :, 0], w['k_norm'][0])
    k_s = _rope(_rmsnorm(kvs[:, :, 2], w['k_norm'][1]), pos)
    k_w = _rope(_rmsnorm(kvs[:, :, 4], w['k_norm'][2]), pos)
    kv_new = jnp.stack([k_c, kvs[:, :, 1], k_s, kvs[:, :, 3]], axis=2)
    win_new = jnp.stack([k_w, kvs[:, :, 5]], axis=2)
    win_cat = jnp.concatenate([cache_win, win_new], axis=1)

    pool_t = jnp.transpose(pool, (0, 2, 3, 4, 1))
    kvc = compress_paged(pool_t, page_table, w['cmp_pe'], w['cmp_w1'], w['cmp_w2'])
    attn = nsa_sample(qn, qr, kvc, kv_new, win_cat, pool_t, page_table, sm[:, :N_GATES], q_off)

    xbc_all = jnp.concatenate([state_conv, xbc.reshape(bsz, lq, CONV_DIM)], axis=1)
    dt_raw = sm[:, N_GATES:N_GATES + SSM_HEADS].reshape(bsz, lq, SSM_HEADS)
    xs, b_t, c_in, a_col, a_row, dt_col = _ssd_inputs(xbc_all, dt_raw, w, SSM_CHUNK)
    y_hm, s_t = ssd_scan(w['ssm_d'], xs, b_t, c_in, a_col, a_row, dt_col, state_ssm.transpose(0, 1, 3, 2), SSM_CHUNK)
    ssd = _ssd_finish(y_hm, z.reshape(bsz, lq, D_INNER), w, lq).reshape(m, D_INNER)

    y = _token_tail(x.reshape(m, D_MODEL), (attn, ssd), p.reshape(m, PLE_DIM), w, m)
    keep = cache_win.shape[1]
    return (y.reshape(bsz, lq, D_MODEL), kv_new, win_cat[:, win_cat.shape[1] - keep:], s_t.transpose(0, 1, 3, 2),
            xbc_all[:, xbc_all.shape[1] - (CONV_W - 1):])


def kernel(x_prompt, x_sample, cache_kv, cache_win, state_ssm, state_conv, page_table, p_prompt, p_sample,
           w_in, w_out, q_norm, k_norm, cmp_pe, cmp_w1, cmp_w2, conv_w, conv_b, dt_bias, a_log, ssm_d, ssm_norm,
           attn_norm, ffn_norm, w_rg, b_rg, w_re, b_re, w_gate, w_up, w_down, w_ple_proj, ple_norm, w_ple_gate):
    depth = w_in.shape[0]
    hp, hs = x_prompt, x_sample
    outs = [[] for _ in range(8)]
    cuts = np.cumsum((ATT_W, 6 * KV_HEADS * HEAD_DIM, N_GATES, D_INNER, CONV_DIM, SSM_HEADS))
    for l in range(depth):
        wi = w_in[l]
        w_in_r = jnp.concatenate(
            [wi[:, :cuts[1]], wi[:, cuts[2]:cuts[3]], wi[:, cuts[3]:cuts[4]], wi[:, cuts[1]:cuts[2]],
             wi[:, cuts[4]:cuts[5]], jnp.zeros((D_MODEL, C_SM - N_GATES - SSM_HEADS), F32)], axis=1).astype(BF16)
        w = dict(w_in_r=w_in_r, w_out_b=w_out[l].astype(BF16), q_norm=q_norm[l], k_norm=k_norm[l],
                 cmp_pe=cmp_pe[l], cmp_w1=cmp_w1[l], cmp_w2=cmp_w2[l], conv_w=conv_w[l], conv_b=conv_b[l],
                 dt_bias=dt_bias[l], a_log=a_log[l], ssm_d=ssm_d[l], ssm_norm=ssm_norm[l], attn_norm=attn_norm[l],
                 ffn_norm=ffn_norm[l], w_rg=w_rg[l], b_rg=b_rg[l], w_re=w_re[l], b_re=b_re[l],
                 w_gate=w_gate[l], w_up=w_up[l], w_down=w_down[l],
                 wpp_b=w_ple_proj[l].astype(BF16), ple_norm=ple_norm[l], wpg_b=w_ple_gate[l].astype(BF16))
        hp, *rest_p = _prompt_group(hp, p_prompt[l], w)
        hs, *rest_s = _sample_group(hs, p_sample[l], cache_kv[l], page_table, cache_win[l], state_ssm[l],
                                    state_conv[l], w)
        for j in range(4):
            outs[2 * j].append(rest_p[j])
            outs[2 * j + 1].append(rest_s[j])
    return (hp, hs) + tuple(jnp.stack(o) for o in outs)
```

```python
import functools
import math

import numpy as np
import jax
import jax.numpy as jnp
from jax import lax
from jax.experimental import pallas as pl
from jax.experimental.pallas import tpu as pltpu

F32 = jnp.float32
BF16 = jnp.bfloat16

D_MODEL = 1024
PAGE_SIZE = 128
N_HEADS = 8
HEAD_DIM = 64
KV_HEADS = 2
HPG = N_HEADS // KV_HEADS
ATT_W = N_HEADS * HEAD_DIM
L_CMP = 32
S_CMP = 16
L_SLC = 64
N_SEL = 16
WINDOW = 512
CMP_HID = 64
ROPE_THETA = 10000.0
SSM_HEADS = 8
SSM_HEAD_DIM = 64
D_INNER = SSM_HEADS * SSM_HEAD_DIM
SSM_GROUPS = 2
D_STATE = 64
CONV_W = 4
CONV_DIM = D_INNER + 2 * SSM_GROUPS * D_STATE
SSM_CHUNK = 128
N_EXPERT_GROUPS = 4
EXPERTS_PER_GROUP = 8
N_EXPERTS = N_EXPERT_GROUPS * EXPERTS_PER_GROUP
TOP_K = 2
D_EXPERT = 512
PLE_DIM = 256
EPS = 1e-6
N_GATES = 3 * N_HEADS
C_Q, C_KV, C_Z, C_XBC, C_SM = 512, 768, 512, 768, 128
D_IN_PAD = C_Q + C_KV + C_Z + C_XBC + C_SM

NEG = -1e30
VMEM_LIMIT = 48 * 1024 * 1024
MOE_ROWS = 256


def _cparams(sem):
    return pltpu.CompilerParams(dimension_semantics=sem, vmem_limit_bytes=VMEM_LIMIT)


def _dot(a, b):
    return jnp.dot(a, b, preferred_element_type=F32)


def _dot_nt(a, b):
    return lax.dot_general(a, b, (((1,), (1,)), ((), ())), preferred_element_type=F32)


def _dot_hi(a, b):
    return jnp.dot(a, b, preferred_element_type=F32, precision=lax.Precision.HIGHEST)


def _rms_inproj_kernel(x_ref, g_ref, w_ref, q_ref, kv_ref, z_ref, xbc_ref, sm_ref):
    x = x_ref[...]
    y = x * lax.rsqrt(jnp.mean(x * x, axis=-1, keepdims=True) + EPS) * g_ref[...]
    yb = y.astype(BF16)
    c0 = 0
    for ref, width in ((q_ref, C_Q), (kv_ref, C_KV), (z_ref, C_Z), (xbc_ref, C_XBC), (sm_ref, C_SM)):
        ref[...] = _dot(yb, w_ref[:, c0:c0 + width])
        c0 += width


def rms_inproj(x, gain, w_r, tm):
    m = x.shape[0]
    widths = (C_Q, C_KV, C_Z, C_XBC, C_SM)
    return pl.pallas_call(
        _rms_inproj_kernel,
        grid=(m // tm,),
        in_specs=[pl.BlockSpec((tm, D_MODEL), lambda i: (i, 0)),
                  pl.BlockSpec((1, D_MODEL), lambda i: (0, 0)),
                  pl.BlockSpec((D_MODEL, D_IN_PAD), lambda i: (0, 0))],
        out_specs=[pl.BlockSpec((tm, w), lambda i: (i, 0)) for w in widths],
        out_shape=[jax.ShapeDtypeStruct((m, w), F32) for w in widths],
        compiler_params=_cparams(("parallel",)),
        name="rms_inproj",
    )(x, gain.reshape(1, D_MODEL), w_r)


def _inproj_prompt_kernel(x_ref, g_ref, w_ref, cos_ref, sin_ref, gq_ref, gk_ref, z_ref, xbc_ref, smt_ref, kv_ref,
                          win_ref, qn_ref, qr_ref, ksa_ref, kw_ref, vst_ref, vwt_ref, *, tiles_per_seq):
    tm = x_ref.shape[0]
    x = x_ref[...]
    yb = (x * lax.rsqrt(jnp.mean(x * x, axis=-1, keepdims=True) + EPS) * g_ref[...]).astype(BF16)
    cos, sin = cos_ref[...], sin_ref[...]
    scale = HEAD_DIM ** -0.5
    half = HEAD_DIM // 2

    def proj(c0, width=128):
        return _dot(yb, w_ref[:, c0:c0 + width])

    def head_norm(x_t, gain):
        return x_t * lax.rsqrt(jnp.mean(x_t * x_t, axis=0, keepdims=True) + EPS) * gain

    def rotate(y_t):
        y1, y2 = y_t[0:half], y_t[half:HEAD_DIM]
        return jnp.concatenate([y1 * cos - y2 * sin, y2 * cos + y1 * sin], axis=0)

    def per_group(blk, fn):
        t = blk.T
        return jnp.concatenate([fn(t[g * HEAD_DIM:(g + 1) * HEAD_DIM]) for g in range(KV_HEADS)], axis=0).T

    for j in range(ATT_W // 128):
        q_t = proj(128 * j).T
        for h2 in range(128 // HEAD_DIM):
            head = (128 // HEAD_DIM) * j + h2
            g, hh = head // HPG, head % HPG
            qn = head_norm(q_t[h2 * HEAD_DIM:(h2 + 1) * HEAD_DIM], gq_ref[...])
            qn_ref[0, g, 0, :, hh * tm:(hh + 1) * tm] = (qn * scale).astype(BF16)
            qr_ref[0, g, 0, :, hh * tm:(hh + 1) * tm] = (rotate(qn) * scale).astype(BF16)

    c_kv = C_Q
    k_c = per_group(proj(c_kv), lambda t: head_norm(t, gk_ref[0]))
    v_c = proj(c_kv + 128)
    k_s = per_group(proj(c_kv + 256), lambda t: rotate(head_norm(t, gk_ref[1])))
    v_s = proj(c_kv + 384)
    k_w = per_group(proj(c_kv + 512), lambda t: rotate(head_norm(t, gk_ref[2])))
    v_w = proj(c_kv + 640)
    kv_ref[:, 0:128] = k_c
    kv_ref[:, 128:256] = v_c
    kv_ref[:, 256:384] = k_s
    kv_ref[:, 384:512] = v_s
    win_ref[:, 0:128] = k_w
    win_ref[:, 128:256] = v_w
    kw_ref[...] = k_w.astype(BF16)
    pos0 = (pl.program_id(0) % tiles_per_seq) * tm
    lane = lax.broadcasted_iota(jnp.int32, (tm, 128), 1)
    blk_of_row = (pos0 + lax.broadcasted_iota(jnp.int32, (tm, 128), 0)) // L_SLC
    onehot = (blk_of_row == (lane & (HEAD_DIM - 1))).astype(BF16)
    k_sb = k_s.astype(BF16)
    ksa_ref[0, 0] = jnp.where(lane < HEAD_DIM, k_sb, onehot)
    ksa_ref[0, 1] = jnp.where(lane >= HEAD_DIM, k_sb, onehot)
    vs_t = v_s.T.astype(BF16)
    vw_t = v_w.T.astype(BF16)
    for g in range(KV_HEADS):
        vst_ref[0, g, 0] = vs_t[g * HEAD_DIM:(g + 1) * HEAD_DIM]
        vwt_ref[0, g, 0] = vw_t[g * HEAD_DIM:(g + 1) * HEAD_DIM]
    z_ref[...] = proj(C_Q + C_KV, C_Z)
    xbc_ref[...] = proj(C_Q + C_KV + C_Z, C_XBC)
    smt_ref[0, 0] = proj(C_Q + C_KV + C_Z + C_XBC).T[0:32]


def rms_inproj_prompt(x, gain, w_r, q_gain, k_gain, bsz, t):
    tm = NSA_TQ
    assert NSA_TK == tm and NSA_TKW == tm and t // L_SLC <= HEAD_DIM
    m = bsz * t
    nt = t // tm
    inv = ROPE_THETA ** (-jnp.arange(HEAD_DIM // 2, dtype=F32) / (HEAD_DIM // 2))
    ang = inv[:, None] * jnp.arange(t, dtype=F32)[None, :]
    row = lambda width: pl.BlockSpec((tm, width), lambda i: (i, 0))
    qspec = pl.BlockSpec((1, KV_HEADS, 1, HEAD_DIM, HPG * tm), lambda i: (i // nt, 0, i % nt, 0, 0))
    vspec = pl.BlockSpec((1, KV_HEADS, 1, HEAD_DIM, tm), lambda i: (i // nt, 0, i % nt, 0, 0))
    tspec = pl.BlockSpec((HEAD_DIM // 2, tm), lambda i: (0, i % nt))
    qshape = jax.ShapeDtypeStruct((bsz, KV_HEADS, nt, HEAD_DIM, HPG * tm), BF16)
    vshape = jax.ShapeDtypeStruct((bsz, KV_HEADS, nt, HEAD_DIM, tm), BF16)
    return pl.pallas_call(
        functools.partial(_inproj_prompt_kernel, tiles_per_seq=nt),
        grid=(m // tm,),
        in_specs=[row(D_MODEL),
                  pl.BlockSpec((1, D_MODEL), lambda i: (0, 0)),
                  pl.BlockSpec((D_MODEL, D_IN_PAD), lambda i: (0, 0)),
                  tspec, tspec,
                  pl.BlockSpec((HEAD_DIM, 1), lambda i: (0, 0)),
                  pl.BlockSpec((3, HEAD_DIM, 1), lambda i: (0, 0, 0))],
        out_specs=[row(C_Z), row(C_XBC),
                   pl.BlockSpec((1, 1, 32, tm), lambda i: (i // nt, i % nt, 0, 0)),
                   row(512), row(256), qspec, qspec,
                   pl.BlockSpec((1, KV_HEADS, tm, 128), lambda i: (i // nt, 0, i % nt, 0)),
                   row(128), vspec, vspec],
        out_shape=[jax.ShapeDtypeStruct((m, C_Z), F32), jax.ShapeDtypeStruct((m, C_XBC), F32),
                   jax.ShapeDtypeStruct((bsz, nt, 32, tm), F32),
                   jax.ShapeDtypeStruct((m, 512), F32), jax.ShapeDtypeStruct((m, 256), F32), qshape, qshape,
                   jax.ShapeDtypeStruct((bsz, KV_HEADS, t, 128), BF16),
                   jax.ShapeDtypeStruct((m, 128), BF16), vshape, vshape],
        compiler_params=_cparams(("parallel",)),
        name="rms_inproj_prompt",
    )(x, gain.reshape(1, D_MODEL), w_r, jnp.cos(ang), jnp.sin(ang), q_gain.reshape(HEAD_DIM, 1),
      k_gain.reshape(3, HEAD_DIM, 1))


def _outproj_kernel(x_ref, a_ref, s_ref, w_ref, h_ref):
    acc = _dot(a_ref[...], w_ref[0:ATT_W, :]) + _dot(s_ref[...], w_ref[ATT_W:ATT_W + D_INNER, :])
    h_ref[...] = x_ref[...] + acc


def outproj(x, attn, ssd, w_out_b, tm):
    m = x.shape[0]
    return pl.pallas_call(
        _outproj_kernel,
        grid=(m // tm,),
        in_specs=[pl.BlockSpec((tm, D_MODEL), lambda i: (i, 0)),
                  pl.BlockSpec((tm, ATT_W), lambda i: (i, 0)),
                  pl.BlockSpec((tm, D_INNER), lambda i: (i, 0)),
                  pl.BlockSpec((ATT_W + D_INNER, D_MODEL), lambda i: (0, 0))],
        out_specs=pl.BlockSpec((tm, D_MODEL), lambda i: (i, 0)),
        out_shape=jax.ShapeDtypeStruct((m, D_MODEL), F32),
        compiler_params=_cparams(("parallel",)),
        name="outproj",
    )(x, attn, ssd, w_out_b)


def _ple_kernel(h_ref, y0_ref, y1_ref, wt_ref, p_ref, wg_ref, wp_ref, g_ref, o_ref):
    wt = wt_ref[...]
    h = h_ref[...] + (y0_ref[...] * wt[:, 0:1] + y1_ref[...] * wt[:, 1:2])
    gate = jax.nn.sigmoid(_dot(h.astype(BF16), wg_ref[...]))
    e = _dot(p_ref[...].astype(BF16), wp_ref[...])
    e = e * lax.rsqrt(jnp.mean(e * e, axis=-1, keepdims=True) + EPS) * g_ref[...]
    o_ref[...] = h + gate * e


def moe_combine_ple(h, y01, wts, p, wg_b, wp_b, gain, tm):
    m = h.shape[0]
    nt = m // tm
    return pl.pallas_call(
        _ple_kernel,
        grid=(nt,),
        in_specs=[pl.BlockSpec((tm, D_MODEL), lambda i: (i, 0)),
                  pl.BlockSpec((tm, D_MODEL), lambda i: (i, 0)),
                  pl.BlockSpec((tm, D_MODEL), lambda i: (i + nt, 0)),
                  pl.BlockSpec((tm, 128), lambda i: (i, 0)),
                  pl.BlockSpec((tm, PLE_DIM), lambda i: (i, 0)),
                  pl.BlockSpec((D_MODEL, D_MODEL), lambda i: (0, 0)),
                  pl.BlockSpec((PLE_DIM, D_MODEL), lambda i: (0, 0)),
                  pl.BlockSpec((1, D_MODEL), lambda i: (0, 0))],
        out_specs=pl.BlockSpec((tm, D_MODEL), lambda i: (i, 0)),
        out_shape=jax.ShapeDtypeStruct((m, D_MODEL), F32),
        compiler_params=_cparams(("parallel",)),
        name="moe_combine_ple",
    )(h, y01, y01, wts, p, wg_b, wp_b, gain.reshape(1, D_MODEL))


def _moe_kernel(be_ref, nb_ref, x_ref, wg_ref, wu_ref, wd_ref, y_ref, wg_sc, wu_sc, wd_sc):
    i = pl.program_id(0)

    @pl.when((i == 0) | (be_ref[i] != be_ref[jnp.maximum(i - 1, 0)]))
    def _():
        wg_sc[...] = wg_ref[0].astype(BF16)
        wu_sc[...] = wu_ref[0].astype(BF16)
        wd_sc[...] = wd_ref[0].astype(BF16)

    @pl.when(i < nb_ref[0])
    def _():
        x = x_ref[...]
        a = _dot(x, wg_sc[...])
        hb = (a * jax.nn.sigmoid(a)) * _dot(x, wu_sc[...])
        y_ref[...] = _dot(hb.astype(BF16), wd_sc[...])

    @pl.when(i >= nb_ref[0])
    def _():
        y_ref[...] = jnp.zeros(y_ref.shape, F32)


def moe_experts(xpad, blk_e, n_used, w_gate, w_up, w_down):
    n_blk = xpad.shape[0] // MOE_ROWS
    grid_spec = pltpu.PrefetchScalarGridSpec(
        num_scalar_prefetch=2,
        grid=(n_blk,),
        in_specs=[pl.BlockSpec((MOE_ROWS, D_MODEL), lambda i, be, nb: (i, 0)),
                  pl.BlockSpec((1, D_MODEL, D_EXPERT), lambda i, be, nb: (be[i], 0, 0)),
                  pl.BlockSpec((1, D_MODEL, D_EXPERT), lambda i, be, nb: (be[i], 0, 0)),
                  pl.BlockSpec((1, D_EXPERT, D_MODEL), lambda i, be, nb: (be[i], 0, 0))],
        out_specs=pl.BlockSpec((MOE_ROWS, D_MODEL), lambda i, be, nb: (i, 0)),
        scratch_shapes=[pltpu.VMEM((D_MODEL, D_EXPERT), BF16), pltpu.VMEM((D_MODEL, D_EXPERT), BF16),
                        pltpu.VMEM((D_EXPERT, D_MODEL), BF16)],
    )
    return pl.pallas_call(
        _moe_kernel,
        grid_spec=grid_spec,
        out_shape=jax.ShapeDtypeStruct((n_blk * MOE_ROWS, D_MODEL), F32),
        compiler_params=_cparams(("arbitrary",)),
        name="moe_experts",
    )(blk_e, n_used, xpad, w_gate, w_up, w_down)


def _compress_kernel(seg_ref, w1a_ref, w1b_ref, pe_ref, w1_ref, w2_ref, o_ref):
    seg = seg_ref[0, 0]
    n_seg = seg.shape[0]
    hid0 = _dot(pe_ref[0], w1_ref[0])[0:1]
    p0 = _dot(seg, w1a_ref[0])
    p1 = pltpu.roll(_dot(seg, w1b_ref[0]), n_seg - 1, axis=0)
    hid = hid0 + p0 + p1
    act = hid * jax.nn.sigmoid(hid)
    o_ref[0, 0] = _dot(act.astype(BF16), w2_ref[0]).astype(BF16)


def compress(segs, cmp_pe, cmp_w1, cmp_w2):
    _, r, n_seg, k = segs.shape
    w1 = cmp_w1.astype(BF16)
    pe = jnp.broadcast_to(cmp_pe.reshape(2, 1, L_CMP * HEAD_DIM), (2, 8, L_CMP * HEAD_DIM)).astype(BF16)
    return pl.pallas_call(
        _compress_kernel,
        grid=(2, r),
        in_specs=[pl.BlockSpec((1, 1, n_seg, k), lambda s, i: (s, i, 0, 0)),
                  pl.BlockSpec((1, k, CMP_HID), lambda s, i: (s, 0, 0)),
                  pl.BlockSpec((1, k, CMP_HID), lambda s, i: (s, 1, 0)),
                  pl.BlockSpec((1, 8, 2 * k), lambda s, i: (s, 0, 0)),
                  pl.BlockSpec((1, 2 * k, CMP_HID), lambda s, i: (s, 0, 0)),
                  pl.BlockSpec((1, CMP_HID, HEAD_DIM), lambda s, i: (s, 0, 0))],
        out_specs=pl.BlockSpec((1, 1, n_seg, HEAD_DIM), lambda s, i: (s, i, 0, 0)),
        out_shape=jax.ShapeDtypeStruct((2, r, n_seg, HEAD_DIM), BF16),
        compiler_params=_cparams(("parallel", "parallel")),
        name="compress",
    )(segs, w1, w1, pe, w1, cmp_w2.astype(BF16))


PAGES_PER_STEP = 16


def _page_copies(pt_ref, pool_ref, buf, sem, b, j, slot):
    return [pltpu.make_async_copy(pool_ref.at[pt_ref[b, j * PAGES_PER_STEP + p], pl.ds(0, 2)],
                                  buf.at[slot, p], sem.at[slot]) for p in range(PAGES_PER_STEP)]


def _paged_partials_kernel(pt_ref, pool_ref, wa_ref, wb_ref, p0_ref, p1_ref, buf, sem, x_sc):
    b = pl.program_id(0)
    j = pl.program_id(1)
    n_grp = pl.num_programs(1)
    step = b * n_grp + j
    slot = step % 2

    @pl.when(step == 0)
    def _():
        for c in _page_copies(pt_ref, pool_ref, buf, sem, b, j, slot):
            c.start()

    @pl.when(step + 1 < pl.num_programs(0) * n_grp)
    def _():
        wrap = j + 1 == n_grp
        for c in _page_copies(pt_ref, pool_ref, buf, sem, jnp.where(wrap, b + 1, b), jnp.where(wrap, 0, j + 1),
                              1 - slot):
            c.start()

    for c in _page_copies(pt_ref, pool_ref, buf, sem, b, j, slot):
        c.wait()

    n_seg = PAGES_PER_STEP * PAGE_SIZE // S_CMP
    for kv in range(2):
        for g in range(KV_HEADS):
            for p in range(PAGES_PER_STEP):
                x_sc[p * PAGE_SIZE:(p + 1) * PAGE_SIZE, :] = buf[slot, p, kv, g].T
            acc0 = jnp.zeros((n_seg, CMP_HID), F32)
            acc1 = jnp.zeros((n_seg, CMP_HID), F32)
            for s in range(S_CMP):
                xs = x_sc[pl.ds(s, n_seg, stride=S_CMP), :].astype(BF16)
                acc0 = acc0 + _dot(xs, wa_ref[kv, s])
                acc1 = acc1 + _dot(xs, wb_ref[kv, s])
            p0_ref[kv, 0, g] = acc0
            p1_ref[kv, 0, g] = acc1


def _compress_finish_kernel(p0_ref, p1_ref, pe_ref, w1_ref, w2_ref, o_ref):
    n_seg = p0_ref.shape[3]
    hid0 = _dot(pe_ref[0], w1_ref[0])[0:1]
    hid = hid0 + p0_ref[0, 0, 0] + pltpu.roll(p1_ref[0, 0, 0], n_seg - 1, axis=0)
    act = hid * jax.nn.sigmoid(hid)
    o_ref[0, 0] = _dot(act.astype(BF16), w2_ref[0]).astype(BF16)


def compress_paged(pool_t, page_table, cmp_pe, cmp_w1, cmp_w2):
    bsz, n_pages = page_table.shape
    assert n_pages % PAGES_PER_STEP == 0
    n_grp = n_pages // PAGES_PER_STEP
    seg_step = PAGES_PER_STEP * PAGE_SIZE // S_CMP
    n_seg = n_grp * seg_step
    w1 = cmp_w1.astype(BF16).reshape(2, L_CMP // S_CMP, S_CMP, HEAD_DIM, CMP_HID)
    part_shape = jax.ShapeDtypeStruct((2, bsz, KV_HEADS, n_seg, CMP_HID), F32)
    part_spec = pl.BlockSpec((2, 1, KV_HEADS, seg_step, CMP_HID), lambda b, j, pt: (0, b, 0, j, 0))
    wspec = pl.BlockSpec((2, S_CMP, HEAD_DIM, CMP_HID), lambda b, j, pt: (0, 0, 0, 0))
    p0, p1 = pl.pallas_call(
        _paged_partials_kernel,
        grid_spec=pltpu.PrefetchScalarGridSpec(
            num_scalar_prefetch=1,
            grid=(bsz, n_grp),
            in_specs=[pl.BlockSpec(memory_space=pl.ANY), wspec, wspec],
            out_specs=[part_spec, part_spec],
            scratch_shapes=[pltpu.VMEM((2, PAGES_PER_STEP, 2, KV_HEADS, HEAD_DIM, PAGE_SIZE), F32),
                            pltpu.SemaphoreType.DMA((2,)),
                            pltpu.VMEM((PAGES_PER_STEP * PAGE_SIZE, HEAD_DIM), F32)]),
        out_shape=[part_shape, part_shape],
        compiler_params=_cparams(("arbitrary", "arbitrary")),
        name="paged_partials",
    )(page_table, pool_t, w1[:, 0], w1[:, 1])
    pe = jnp.broadcast_to(cmp_pe.reshape(2, 1, L_CMP * HEAD_DIM), (2, 8, L_CMP * HEAD_DIM)).astype(BF16)
    pspec = pl.BlockSpec((1, 1, 1, n_seg, CMP_HID), lambda s, i: (s, i // KV_HEADS, i % KV_HEADS, 0, 0))
    return pl.pallas_call(
        _compress_finish_kernel,
        grid=(2, bsz * KV_HEADS),
        in_specs=[pspec, pspec,
                  pl.BlockSpec((1, 8, L_CMP * HEAD_DIM), lambda s, i: (s, 0, 0)),
                  pl.BlockSpec((1, L_CMP * HEAD_DIM, CMP_HID), lambda s, i: (s, 0, 0)),
                  pl.BlockSpec((1, CMP_HID, HEAD_DIM), lambda s, i: (s, 0, 0))],
        out_specs=pl.BlockSpec((1, 1, n_seg, HEAD_DIM), lambda s, i: (s, i, 0, 0)),
        out_shape=jax.ShapeDtypeStruct((2, bsz * KV_HEADS, n_seg, HEAD_DIM), BF16),
        compiler_params=_cparams(("parallel", "parallel")),
        name="compress_finish",
    )(p0, p1, pe, cmp_w1.astype(BF16), cmp_w2.astype(BF16))


FLASH_SPLIT = 4


def _flash_step(k, q_ref, v_t, m_sc, l_sc, acc_sc, keep=None):
    cw = q_ref.shape[1] // FLASH_SPLIT
    cols = [slice(i * cw, (i + 1) * cw) for i in range(FLASH_SPLIT)]
    scores = [_dot(k, q_ref[:, c]) for c in cols]
    for c, s_t in zip(cols, scores):
        if keep is not None:
            s_t = jnp.where(keep[:, c], s_t, NEG)
        m_prev = m_sc[:, c]
        m_new = jnp.maximum(m_prev, jnp.max(s_t, axis=0, keepdims=True))
        alpha = jnp.exp(m_prev - m_new)
        p = jnp.exp(s_t - m_new)
        l_sc[:, c] = alpha * l_sc[:, c] + jnp.sum(p, axis=0, keepdims=True)
        acc_sc[:, c] = alpha * acc_sc[:, c] + _dot(v_t, p.astype(BF16))
        m_sc[:, c] = m_new


def _flash_reset(m_sc, l_sc, acc_sc):
    m_sc[...] = jnp.full(m_sc.shape, NEG, F32)
    l_sc[...] = jnp.zeros(l_sc.shape, F32)
    acc_sc[...] = jnp.zeros(acc_sc.shape, F32)


def _nsa_prompt_kernel(qn_ref, qr_ref, kc_ref, vct_ref, ks_ref, vst_ref, kw_ref, vwt_ref, covert_ref, gate_ref,
                       o_ref, qa_sc, qw_sc, m_sc, l_sc, acc_sc, out_sc, score_sc, rank_sc, *, tq, tk, tkw):
    qi = pl.program_id(2)
    pos0 = qi * tq
    w = HPG * tq

    def positions(rows, base):
        kpos = base + lax.broadcasted_iota(jnp.int32, (rows, w), 0)
        qpos = pos0 + (lax.broadcasted_iota(jnp.int32, (rows, w), 1) & (tq - 1))
        return kpos, qpos

    g = pl.program_id(1)

    def gate_row(branch):
        rows = [gate_ref[0, 0, pl.ds(3 * (g * HPG + hh) + branch, 1), :] for hh in range(HPG)]
        return jax.nn.sigmoid(jnp.concatenate(rows, axis=1))

    gate = [gate_row(branch) for branch in range(3)]

    kc = kc_ref[0, 0]
    n_c = kc.shape[0]
    cidx, qpos_c = positions(n_c, 0)
    valid = cidx * S_CMP + (L_CMP - 1) <= qpos_c
    s1 = jnp.where(valid, _dot(kc, qn_ref[0, 0, 0]), NEG)
    e1 = jnp.exp(s1 - jnp.max(s1, axis=0, keepdims=True)) * valid.astype(F32)
    p1b = (e1 * (1.0 / jnp.maximum(jnp.sum(e1, axis=0, keepdims=True), 1e-30))).astype(BF16)
    out_sc[...] = gate[0] * _dot(vct_ref[0, 0], p1b)

    p1_stack = jnp.concatenate([p1b[:, hh * tq:(hh + 1) * tq] for hh in range(HPG)], axis=0)
    imp = _dot(covert_ref[...], p1_stack)
    n_s = imp.shape[0]
    blk = lax.broadcasted_iota(jnp.int32, (n_s, tq), 0)
    cur = (pos0 + lax.broadcasted_iota(jnp.int32, (n_s, tq), 1)) // L_SLC
    vis = blk <= cur
    forced = vis & ((blk == 0) | (blk >= cur - 1))
    score_sc[...] = jnp.where(forced, 1e9, jnp.where(vis, imp, -1.0))
    rank_sc[...] = jnp.zeros(rank_sc.shape, F32)
    last_blk = (pos0 + tq - 1) // L_SLC
    n_oct = n_s // 8
    for oi in range(n_oct):
        @pl.when(oi * 8 <= last_blk)
        def _():
            rows = [slice(8 * oj, 8 * oj + 8) for oj in range(n_oct)]
            parts = [rank_sc[r, :] for r in rows]
            for i in range(8 * oi, 8 * oi + 8):
                c = score_sc[i:i + 1, :]
                for oj, r in enumerate(rows):
                    sj = score_sc[r, :]
                    if oj > oi:
                        beats = c >= sj
                    elif oj < oi:
                        beats = c > sj
                    else:
                        above = lax.broadcasted_iota(jnp.int32, (8, tq), 0) > i - 8 * oi
                        beats = (c > sj) | ((c == sj) & above)
                    parts[oj] = parts[oj] + beats.astype(F32)
            for r, part in zip(rows, parts):
                rank_sc[r, :] = part
    sel_bias = jnp.where(vis & (rank_sc[...] < N_SEL), 0.0, NEG).astype(BF16)
    own = pl.ds(pl.multiple_of(g * HEAD_DIM, HEAD_DIM), HEAD_DIM)
    other = pl.ds(pl.multiple_of((1 - g) * HEAD_DIM, HEAD_DIM), HEAD_DIM)
    qa_sc[other, :] = jnp.zeros((HEAD_DIM, w), BF16)
    qw_sc[other, :] = jnp.zeros((HEAD_DIM, w), BF16)
    qa_sc[own, :] = qr_ref[0, 0, 0]
    qw_sc[own, :] = qr_ref[0, 0, 0]
    qa_sc[pl.ds(pl.multiple_of((1 - g) * HEAD_DIM, HEAD_DIM), n_s), :] = jnp.concatenate([sel_bias] * HPG, axis=1)

    _flash_reset(m_sc, l_sc, acc_sc)
    n_full = pos0 // tk

    def sel_body(kt, carry):
        k0 = pl.multiple_of(kt * tk, tk)
        _flash_step(ks_ref[0, 0, pl.ds(k0, tk), :], qa_sc, vst_ref[0, 0, kt], m_sc, l_sc, acc_sc)
        return carry

    lax.fori_loop(0, n_full, sel_body, 0)
    k0 = pl.multiple_of(n_full * tk, tk)
    kpos, qpos = positions(tk, k0)
    _flash_step(ks_ref[0, 0, pl.ds(k0, tk), :], qa_sc, vst_ref[0, 0, n_full], m_sc, l_sc, acc_sc, kpos <= qpos)
    out_sc[...] = out_sc[...] + (gate[1] * (1.0 / l_sc[...])) * acc_sc[...]

    _flash_reset(m_sc, l_sc, acc_sc)
    qr = qw_sc
    kpos, qpos = positions(tkw, pos0)
    _flash_step(kw_ref[pl.ds(pl.multiple_of(pos0, tkw), tkw), :], qr, vwt_ref[0, 0, qi], m_sc, l_sc, acc_sc,
                kpos <= qpos)
    n_inner = WINDOW // tkw - 1

    def win_body(d, carry):
        kt = qi - 1 - d
        k0 = pl.multiple_of(kt * tkw, tkw)
        _flash_step(kw_ref[pl.ds(k0, tkw), :], qr, vwt_ref[0, 0, kt], m_sc, l_sc, acc_sc)
        return carry

    lax.fori_loop(0, jnp.minimum(qi, n_inner), win_body, 0)

    @pl.when(qi > n_inner)
    def _():
        kt = qi - 1 - n_inner
        k0 = pl.multiple_of(kt * tkw, tkw)
        kpos, qpos = positions(tkw, k0)
        _flash_step(kw_ref[pl.ds(k0, tkw), :], qr, vwt_ref[0, 0, kt], m_sc, l_sc, acc_sc,
                    kpos > qpos - WINDOW)

    o = out_sc[...] + (gate[2] * (1.0 / l_sc[...])) * acc_sc[...]
    o_ref[0, 0, 0] = o.astype(o_ref.dtype)


NSA_TQ, NSA_TK, NSA_TKW = 256, 256, 256


def nsa_prompt(qn_t, qr_t, kc, vc_t, ks_aug, vs_t, kw, vw_t, sm_t, bsz, t):
    tq, tk, tkw = NSA_TQ, NSA_TK, NSA_TKW
    assert tq == tkw and tk % tq == 0 and WINDOW % tkw == 0
    n_c = kc.shape[1]
    n_s = t // L_SLC
    assert n_s <= HEAD_DIM and KV_HEADS == 2
    w = HPG * tq
    c0 = np.arange(n_c)[None, :] * S_CMP
    s0 = np.arange(n_s)[:, None] * L_SLC
    cover_t = (np.maximum(np.minimum(c0 + L_CMP, s0 + L_SLC) - np.maximum(c0, s0), 0) / L_CMP).astype(np.float32)
    cover_t = np.tile(cover_t, (1, HPG))
    qspec = pl.BlockSpec((1, 1, 1, HEAD_DIM, w), lambda b, g, i: (b, g, i, 0, 0))
    kern = functools.partial(_nsa_prompt_kernel, tq=tq, tk=tk, tkw=tkw)
    return pl.pallas_call(
        kern,
        grid=(bsz, KV_HEADS, t // tq),
        in_specs=[qspec, qspec,
                  pl.BlockSpec((1, 1, n_c, HEAD_DIM), lambda b, g, i: (b, g, 0, 0)),
                  pl.BlockSpec((1, 1, HEAD_DIM, n_c), lambda b, g, i: (b, g, 0, 0)),
                  pl.BlockSpec((1, 1, t, 2 * HEAD_DIM), lambda b, g, i: (b, g, 0, 0)),
                  pl.BlockSpec((1, 1, t // tk, HEAD_DIM, tk), lambda b, g, i: (b, g, 0, 0, 0)),
                  pl.BlockSpec((t, KV_HEADS * HEAD_DIM), lambda b, g, i: (b, 0)),
                  pl.BlockSpec((1, 1, t // tkw, HEAD_DIM, tkw), lambda b, g, i: (b, g, 0, 0, 0)),
                  pl.BlockSpec((n_s, HPG * n_c), lambda b, g, i: (0, 0)),
                  pl.BlockSpec((1, 1, 32, tq), lambda b, g, i: (b, i, 0, 0))],
        out_specs=qspec,
        out_shape=jax.ShapeDtypeStruct((bsz, KV_HEADS, t // tq, HEAD_DIM, w), BF16),
        scratch_shapes=[pltpu.VMEM((2 * HEAD_DIM, w), BF16), pltpu.VMEM((2 * HEAD_DIM, w), BF16),
                        pltpu.VMEM((1, w), F32), pltpu.VMEM((1, w), F32),
                        pltpu.VMEM((HEAD_DIM, w), F32), pltpu.VMEM((HEAD_DIM, w), F32),
                        pltpu.VMEM((n_s, tq), F32), pltpu.VMEM((n_s, tq), F32)],
        compiler_params=_cparams(("parallel", "parallel", "arbitrary")),
        name="nsa_prompt",
    )(qn_t, qr_t, kc.reshape(bsz, KV_HEADS, n_c, HEAD_DIM), vc_t.reshape(bsz, KV_HEADS, HEAD_DIM, n_c),
      ks_aug, vs_t, kw, vw_t, jnp.asarray(cover_t, BF16), sm_t)


def _sample_select_kernel(qn_ref, kc_ref, vc_ref, cover_ref, oc_ref, sb_ref, *, q_off, lq, n_cmp, n_slc):
    rows = HPG * lq
    for g in range(KV_HEADS):
        kc = kc_ref[0, g]
        n_c = kc.shape[0]
        cidx = lax.broadcasted_iota(jnp.int32, (rows, n_c), 1)
        qpos = q_off + (lax.broadcasted_iota(jnp.int32, (rows, n_c), 0) & (lq - 1))
        valid = (cidx * S_CMP + (L_CMP - 1) <= qpos) & (cidx < n_cmp)
        s1 = jnp.where(valid, _dot_nt(qn_ref[0, g], kc), NEG)
        e1 = jnp.exp(s1 - jnp.max(s1, axis=-1, keepdims=True)) * valid.astype(F32)
        p1b = (e1 * (1.0 / jnp.maximum(jnp.sum(e1, axis=-1, keepdims=True), 1e-30))).astype(BF16)
        oc_ref[0, g] = _dot(p1b, vc_ref[0, g])
        imp4 = _dot(p1b, cover_ref[...])
        imp = imp4[0:lq]
        for hh in range(1, HPG):
            imp = imp + imp4[hh * lq:(hh + 1) * lq]
        n_sp = imp.shape[1]
        blk = lax.broadcasted_iota(jnp.int32, (lq, n_sp), 1)
        cur = (q_off + lax.broadcasted_iota(jnp.int32, (lq, n_sp), 0)) // L_SLC
        vis = (blk <= cur) & (blk < n_slc)
        forced = vis & ((blk == 0) | (blk >= cur - 1))
        score = jnp.where(forced, 1e9, jnp.where(vis, imp, -1.0))
        rank = jnp.zeros((lq, n_sp), F32)
        for i in range(n_slc):
            c = score[:, i:i + 1]
            beats = (c > score) | ((c == score) & (blk > i))
            rank = rank + beats.astype(F32)
        sb_ref[0, g] = jnp.where(vis & (rank < min(N_SEL, n_slc)), 0.0, NEG)


def _sample_attend_kernel(pt_ref, pool_ref, qr_ref, sb_ref, exp_ref, oc_ref, gate_ref, kn_ref, vn_ref, kw_ref, vw_ref,
                          o_ref, buf, sem, m_sc, l_sc, acc_sc, *, q_off, lq, n_win):
    b = pl.program_id(0)
    j = pl.program_id(1)
    n_grp = pl.num_programs(1)
    step = b * n_grp + j
    slot = step % 2
    rows = HPG * lq

    def copies(bb, jj, sl):
        return [pltpu.make_async_copy(pool_ref.at[pt_ref[bb, jj * PAGES_PER_STEP + p], pl.ds(2, 2)],
                                      buf.at[sl, p], sem.at[sl]) for p in range(PAGES_PER_STEP)]

    @pl.when(step == 0)
    def _():
        for c in copies(b, j, slot):
            c.start()

    @pl.when(step + 1 < pl.num_programs(0) * n_grp)
    def _():
        wrap = j + 1 == n_grp
        for c in copies(jnp.where(wrap, b + 1, b), jnp.where(wrap, 0, j + 1), 1 - slot):
            c.start()

    for c in copies(b, j, slot):
        c.wait()

    @pl.when(j == 0)
    def _():
        m_sc[...] = jnp.full(m_sc.shape, NEG, F32)
        l_sc[...] = jnp.zeros(l_sc.shape, F32)
        acc_sc[...] = jnp.zeros(acc_sc.shape, F32)

    def update(g, s, v, v_transposed):
        m_prev = m_sc[g]
        m_new = jnp.maximum(m_prev, jnp.max(s, axis=-1, keepdims=True))
        alpha = jnp.exp(m_prev - m_new)
        p = jnp.exp(s - m_new)
        l_sc[g] = alpha * l_sc[g] + jnp.sum(p, axis=-1, keepdims=True)
        pv = _dot_nt(p.astype(BF16), v) if v_transposed else _dot(p.astype(BF16), v)
        acc_sc[g] = alpha * acc_sc[g] + pv
        m_sc[g] = m_new

    for g in range(KV_HEADS):
        k_t = jnp.concatenate([buf[slot, p, 0, g] for p in range(PAGES_PER_STEP)], axis=1).astype(BF16)
        v_t = jnp.concatenate([buf[slot, p, 1, g] for p in range(PAGES_PER_STEP)], axis=1).astype(BF16)
        sb = sb_ref[0, g, 0]
        bias = _dot(jnp.concatenate([sb] * HPG, axis=0).astype(BF16), exp_ref[...])
        update(g, _dot(qr_ref[0, g], k_t) + bias, v_t, True)

    @pl.when(j == n_grp - 1)
    def _():
        for g in range(KV_HEADS):
            qr = qr_ref[0, g]
            kn = kn_ref[0, g]
            kidx = lax.broadcasted_iota(jnp.int32, (rows, kn.shape[0]), 1)
            qidx = lax.broadcasted_iota(jnp.int32, (rows, kn.shape[0]), 0) & (lq - 1)
            update(g, jnp.where((kidx <= qidx) & (kidx < lq), _dot_nt(qr, kn), NEG), vn_ref[0, g], False)
            o_s = acc_sc[g] * (1.0 / l_sc[g])
            kw = kw_ref[0, g]
            widx = lax.broadcasted_iota(jnp.int32, (rows, kw.shape[0]), 1)
            kpos = q_off + lq - n_win + widx
            qpos = q_off + (lax.broadcasted_iota(jnp.int32, (rows, kw.shape[0]), 0) & (lq - 1))
            ok = (widx < n_win) & (kpos <= qpos) & (kpos > qpos - WINDOW) & (kpos >= 0)
            s3 = jnp.where(ok, _dot_nt(qr, kw), NEG)
            e3 = jnp.exp(s3 - jnp.max(s3, axis=-1, keepdims=True)) * ok.astype(F32)
            p3 = e3 * (1.0 / jnp.maximum(jnp.sum(e3, axis=-1, keepdims=True), 1e-30))
            o_w = _dot(p3.astype(BF16), vw_ref[0, g])
            gate = jax.nn.sigmoid(gate_ref[0, g])
            o_ref[0, g] = gate[:, 0:1] * oc_ref[0, g] + gate[:, 1:2] * o_s + gate[:, 2:3] * o_w


def nsa_sample(q, q_rot, kvc, kv_new, win_cat, pool_t, page_table, gate_logits, q_off):
    bsz, lq = q.shape[:2]
    assert lq & (lq - 1) == 0 and lq <= L_SLC and q_off % (PAGES_PER_STEP * PAGE_SIZE) == 0
    rows = HPG * lq
    scale = HEAD_DIM ** -0.5
    n_seg = kvc.shape[2]
    t_kv = q_off + lq
    n_cmp = (t_kv - L_CMP) // S_CMP + 1
    n_slc = -(-t_kv // L_SLC)
    n_sp = -(-n_slc // 128) * 128
    n_grp = page_table.shape[1] // PAGES_PER_STEP
    blk_step = PAGES_PER_STEP * PAGE_SIZE // L_SLC
    key_step = PAGES_PER_STEP * PAGE_SIZE

    def rows_major(a):
        return a.reshape(bsz, lq, KV_HEADS, HPG, -1).transpose(0, 2, 3, 1, 4).reshape(bsz, KV_HEADS, rows, -1)

    def keys_major(a, n_pad):
        a = a.astype(BF16).transpose(0, 2, 1, 3)
        return jnp.pad(a, ((0, 0), (0, 0), (0, n_pad - a.shape[2]), (0, 0)))

    qn_r = rows_major((q * scale).astype(BF16))
    qr_r = rows_major((q_rot * scale).astype(BF16))
    c0 = np.arange(n_seg)[:, None] * S_CMP
    s0 = np.arange(n_sp)[None, :] * L_SLC
    cover = np.maximum(np.minimum(c0 + L_CMP, s0 + L_SLC) - np.maximum(c0, s0), 0) / L_CMP
    cover = cover * (np.arange(n_seg)[:, None] < n_cmp) * (np.arange(n_sp)[None, :] < n_slc)
    kvc4 = kvc.reshape(2, bsz, KV_HEADS, n_seg, HEAD_DIM)
    spec_q = pl.BlockSpec((1, KV_HEADS, rows, HEAD_DIM), lambda b: (b, 0, 0, 0))
    o_c, sel_bias = pl.pallas_call(
        functools.partial(_sample_select_kernel, q_off=q_off, lq=lq, n_cmp=n_cmp, n_slc=n_slc),
        grid=(bsz,),
        in_specs=[spec_q,
                  pl.BlockSpec((1, KV_HEADS, n_seg, HEAD_DIM), lambda b: (b, 0, 0, 0)),
                  pl.BlockSpec((1, KV_HEADS, n_seg, HEAD_DIM), lambda b: (b, 0, 0, 0)),
                  pl.BlockSpec((n_seg, n_sp), lambda b: (0, 0))],
        out_specs=[spec_q, pl.BlockSpec((1, KV_HEADS, lq, n_sp), lambda b: (b, 0, 0, 0))],
        out_shape=[jax.ShapeDtypeStruct((bsz, KV_HEADS, rows, HEAD_DIM), F32),
                   jax.ShapeDtypeStruct((bsz, KV_HEADS, lq, n_sp), F32)],
        compiler_params=_cparams(("parallel",)),
        name="sample_select",
    )(qn_r, kvc4[0], kvc4[1], jnp.asarray(cover, BF16))

    sb_steps = sel_bias[..., :n_grp * blk_step].reshape(bsz, KV_HEADS, lq, n_grp, blk_step).transpose(0, 1, 3, 2, 4)
    expand = (np.arange(key_step)[None, :] // L_SLC == np.arange(blk_step)[:, None]).astype(np.float32)
    gates = jnp.pad(rows_major(gate_logits.reshape(bsz, lq, -1)), ((0, 0), (0, 0), (0, 0), (0, 128 - 3)))
    n_win = win_cat.shape[1]
    n_wp = -(-n_win // 128) * 128
    spec2 = lambda shape: pl.BlockSpec((1,) + shape, lambda b, j, pt: (b,) + (0,) * len(shape))
    attn = pl.pallas_call(
        functools.partial(_sample_attend_kernel, q_off=q_off, lq=lq, n_win=n_win),
        grid_spec=pltpu.PrefetchScalarGridSpec(
            num_scalar_prefetch=1,
            grid=(bsz, n_grp),
            in_specs=[pl.BlockSpec(memory_space=pl.ANY),
                      spec2((KV_HEADS, rows, HEAD_DIM)),
                      pl.BlockSpec((1, KV_HEADS, 1, lq, blk_step), lambda b, j, pt: (b, 0, j, 0, 0)),
                      pl.BlockSpec((blk_step, key_step), lambda b, j, pt: (0, 0)),
                      spec2((KV_HEADS, rows, HEAD_DIM)),
                      spec2((KV_HEADS, rows, 128)),
                      spec2((KV_HEADS, 128, HEAD_DIM)), spec2((KV_HEADS, 128, HEAD_DIM)),
                      spec2((KV_HEADS, n_wp, HEAD_DIM)), spec2((KV_HEADS, n_wp, HEAD_DIM))],
            out_specs=spec2((KV_HEADS, rows, HEAD_DIM)),
            scratch_shapes=[pltpu.VMEM((2, PAGES_PER_STEP, 2, KV_HEADS, HEAD_DIM, PAGE_SIZE), F32),
                            pltpu.SemaphoreType.DMA((2,)),
                            pltpu.VMEM((KV_HEADS, rows, 1), F32), pltpu.VMEM((KV_HEADS, rows, 1), F32),
                            pltpu.VMEM((KV_HEADS, rows, HEAD_DIM), F32)]),
        out_shape=jax.ShapeDtypeStruct((bsz, KV_HEADS, rows, HEAD_DIM), F32),
        compiler_params=_cparams(("arbitrary", "arbitrary")),
        name="sample_attend",
    )(page_table, pool_t, qr_r, sb_steps, jnp.asarray(expand, BF16), o_c, gates,
      keys_major(kv_new[:, :, 2], 128), keys_major(kv_new[:, :, 3], 128),
      keys_major(win_cat[:, :, 0], n_wp), keys_major(win_cat[:, :, 1], n_wp))
    attn = attn.reshape(bsz, KV_HEADS, HPG, lq, HEAD_DIM).transpose(0, 3, 1, 2, 4)
    return attn.astype(BF16).reshape(bsz * lq, ATT_W)


def _ssd_kernel(d_ref, x_ref, bt_ref, c_ref, acol_ref, arow_ref, dt_ref, s0_ref, y_ref, sout_ref, s_sc, *, cl):
    ci = pl.program_id(1)

    @pl.when(ci == 0)
    def _():
        s_sc[...] = s0_ref[0]

    li = lax.broadcasted_iota(jnp.int32, (cl, cl), 0)
    si = lax.broadcasted_iota(jnp.int32, (cl, cl), 1)
    causal = li >= si
    acum_col = _dot_hi(causal.astype(F32), acol_ref[0])
    acum_row = _dot_hi(arow_ref[0], (li <= si).astype(F32))
    dt = dt_ref[0]
    for g in range(SSM_GROUPS):
        cg = c_ref[0, g].astype(BF16)
        btg = bt_ref[0, g]
        cb = _dot(cg, btg.astype(BF16))
        for hh in range(SSM_HEADS // SSM_GROUPS):
            h = g * (SSM_HEADS // SSM_GROUPS) + hh
            ac = acum_col[:, h:h + 1]
            ar = acum_row[h:h + 1, :]
            decay = jnp.exp(jnp.where(causal, ac - ar, NEG))
            xs = x_ref[0, h]
            xd = (xs * dt[:, h:h + 1]).astype(BF16)
            st = s_sc[h]
            y = _dot((cb * decay).astype(BF16), xd) + _dot(cg, st.astype(BF16)) * jnp.exp(ac)
            y_ref[0, h] = y + d_ref[h] * xs
            a_last = ar[:, cl - 1:cl]
            snew = _dot((btg * jnp.exp(a_last - ar)).astype(BF16), xd)
            s_sc[h] = st * jnp.exp(a_last) + snew

    @pl.when(ci == pl.num_programs(1) - 1)
    def _():
        sout_ref[0] = s_sc[...]


def ssd_scan(ssm_d, x_hm, b_t, c, a_col, a_row, dt_col, s0_t, cl):
    bsz, nh, t, p = x_hm.shape
    n = D_STATE
    nc = t // cl
    kern = functools.partial(_ssd_kernel, cl=cl)
    return pl.pallas_call(
        kern,
        grid=(bsz, nc),
        in_specs=[pl.BlockSpec(memory_space=pltpu.SMEM),
                  pl.BlockSpec((1, nh, cl, p), lambda b, c_: (b, 0, c_, 0)),
                  pl.BlockSpec((1, SSM_GROUPS, n, cl), lambda b, c_: (b, 0, 0, c_)),
                  pl.BlockSpec((1, SSM_GROUPS, cl, n), lambda b, c_: (b, 0, c_, 0)),
                  pl.BlockSpec((1, cl, 128), lambda b, c_: (b, c_, 0)),
                  pl.BlockSpec((1, 8, cl), lambda b, c_: (b, 0, c_)),
                  pl.BlockSpec((1, cl, 128), lambda b, c_: (b, c_, 0)),
                  pl.BlockSpec((1, nh, n, p), lambda b, c_: (b, 0, 0, 0))],
        out_specs=[pl.BlockSpec((1, nh, cl, p), lambda b, c_: (b, 0, c_, 0)),
                   pl.BlockSpec((1, nh, n, p), lambda b, c_: (b, 0, 0, 0))],
        out_shape=[jax.ShapeDtypeStruct((bsz, nh, t, p), F32), jax.ShapeDtypeStruct((bsz, nh, n, p), F32)],
        scratch_shapes=[pltpu.VMEM((nh, n, p), F32)],
        compiler_params=_cparams(("parallel", "arbitrary")),
        name="ssd_scan",
    )(ssm_d, x_hm, b_t, c, a_col, a_row, dt_col, s0_t)


def _rmsnorm(x, g):
    return x * lax.rsqrt(jnp.mean(x * x, axis=-1, keepdims=True) + EPS) * g


def _rope(x, pos):
    half = HEAD_DIM // 2
    inv = ROPE_THETA ** (-jnp.arange(half, dtype=F32) / half)
    ang = pos.astype(F32)[:, None] * inv[None, :]
    cos, sin = jnp.cos(ang)[:, None, :], jnp.sin(ang)[:, None, :]
    x1, x2 = x[..., :half], x[..., half:]
    return jnp.concatenate([x1 * cos - x2 * sin, x2 * cos + x1 * sin], axis=-1)


def _ssd_inputs(xbc_all, dt_raw, w, t_pad):
    bsz = xbc_all.shape[0]
    length = xbc_all.shape[1] - (CONV_W - 1)
    conv = w['conv_b'] + xbc_all[:, 0:length] * w['conv_w'][0]
    for j in range(1, CONV_W):
        conv = conv + xbc_all[:, j:j + length] * w['conv_w'][j]
    act = jax.nn.silu(conv)
    dt = jax.nn.softplus(dt_raw + w['dt_bias'])
    a = -jnp.exp(w['a_log']) * dt
    pad = t_pad - length
    if pad:
        act = jnp.pad(act, ((0, 0), (0, pad), (0, 0)))
        dt = jnp.pad(dt, ((0, 0), (0, pad), (0, 0)))
        a = jnp.pad(a, ((0, 0), (0, pad), (0, 0)))
    xs = act[..., :D_INNER].reshape(bsz, t_pad, SSM_HEADS, SSM_HEAD_DIM).transpose(0, 2, 1, 3)
    b_in = act[..., D_INNER:D_INNER + SSM_GROUPS * D_STATE].reshape(bsz, t_pad, SSM_GROUPS, D_STATE)
    c_in = act[..., D_INNER + SSM_GROUPS * D_STATE:].reshape(bsz, t_pad, SSM_GROUPS, D_STATE)
    lane_pad = ((0, 0), (0, 0), (0, 128 - SSM_HEADS))
    return (xs, b_in.transpose(0, 2, 3, 1), c_in.transpose(0, 2, 1, 3), jnp.pad(a, lane_pad),
            a.transpose(0, 2, 1), jnp.pad(dt, lane_pad))


def _moe(h, w):
    n_tok = h.shape[0]
    f = _rmsnorm(h, w['ffn_norm'])
    hi = lax.Precision.HIGHEST
    pg = jax.nn.softmax(jnp.dot(f, w['w_rg'], precision=hi) + w['b_rg'], axis=-1)
    g_sel = jnp.argmax(pg, axis=-1)
    g_w = jnp.max(pg, axis=-1)
    el = (jnp.dot(f, w['w_re'], precision=hi) + w['b_re']).reshape(n_tok, N_EXPERT_GROUPS, EXPERTS_PER_GROUP)
    el = jnp.take_along_axis(el, g_sel[:, None, None], axis=1)[:, 0]
    top_p, top_i = lax.top_k(jax.nn.softmax(el, axis=-1), TOP_K)
    wts = g_w[:, None] * top_p / jnp.sum(top_p, axis=-1, keepdims=True)
    eid = (g_sel[:, None] * EXPERTS_PER_GROUP + top_i).reshape(-1).astype(jnp.int32)
    n_asg = eid.shape[0]
    onehot = (eid[:, None] == jnp.arange(N_EXPERTS, dtype=jnp.int32)[None, :]).astype(jnp.int32)
    within = jnp.take_along_axis(jnp.cumsum(onehot, axis=0), eid[:, None], axis=1)[:, 0] - 1
    sizes = jnp.sum(onehot, axis=0)
    padded = (sizes + MOE_ROWS - 1) // MOE_ROWS * MOE_ROWS
    pend = jnp.cumsum(padded)
    dest = (pend - padded)[eid] + within
    n_blk = -(-n_asg // MOE_ROWS) + N_EXPERTS
    src = jnp.zeros((n_blk * MOE_ROWS,), jnp.int32).at[dest].set(jnp.arange(n_asg, dtype=jnp.int32) // TOP_K)
    xpad = f.astype(BF16)[src]
    blk_start = jnp.arange(n_blk, dtype=jnp.int32) * MOE_ROWS
    blk_e = jnp.minimum(jnp.sum((pend[None, :] <= blk_start[:, None]).astype(jnp.int32), axis=1), N_EXPERTS - 1)
    n_used = (pend[-1] // MOE_ROWS).astype(jnp.int32).reshape(1)
    ypad = moe_experts(xpad, blk_e, n_used, w['w_gate'], w['w_up'], w['w_down'])
    y01 = ypad[dest.reshape(n_tok, TOP_K).T.reshape(-1)]
    return y01, jnp.pad(wts, ((0, 0), (0, 128 - TOP_K)))


def _token_tail(x2, mix_in, p2, w, tm):
    h = outproj(x2, mix_in[0], mix_in[1], w['w_out_b'], tm)
    y01, wts = _moe(h, w)
    return moe_combine_ple(h, y01, wts, p2, w['wpg_b'], w['wpp_b'], w['ple_norm'], tm)


def _ssd_finish(y_hm, z, w, length):
    bsz = y_hm.shape[0]
    y = y_hm[:, :, :length].transpose(0, 2, 1, 3).reshape(bsz, length, D_INNER)
    gated = y * jax.nn.silu(z)
    return _rmsnorm(gated, w['ssm_norm']).astype(BF16)


def _prompt_group(x, p, w):
    bsz, t, _ = x.shape
    m = bsz * t
    z, xbc, sm_t, kv, win, qn_t, qr_t, ks_aug, kw, vs_t, vw_t = rms_inproj_prompt(
        x.reshape(m, D_MODEL), w['attn_norm'], w['w_in_r'], w['q_norm'], w['k_norm'], bsz, t)
    nq = t // NSA_TQ
    kv_new = kv.reshape(bsz, t, 4, KV_HEADS, HEAD_DIM)
    win_new = win.reshape(bsz, t, 2, KV_HEADS, HEAD_DIM)

    n_seg = t // S_CMP
    segs = kv_new[:, :, 0:2].astype(BF16).transpose(2, 0, 3, 1, 4).reshape(2, bsz * KV_HEADS, n_seg, S_CMP * HEAD_DIM)
    kvc = compress(segs, w['cmp_pe'], w['cmp_w1'], w['cmp_w2'])
    attn = nsa_prompt(qn_t, qr_t, kvc[0], kvc[1].transpose(0, 2, 1), ks_aug, vs_t, kw, vw_t, sm_t, bsz, t)
    attn = attn.reshape(bsz, KV_HEADS, nq, HEAD_DIM, HPG, NSA_TQ).transpose(0, 2, 5, 1, 4, 3).reshape(m, ATT_W)

    xbc3 = xbc.reshape(bsz, t, CONV_DIM)
    xbc_all = jnp.concatenate([jnp.zeros((bsz, CONV_W - 1, CONV_DIM), F32), xbc3], axis=1)
    dt_raw = sm_t[:, :, N_GATES:N_GATES + SSM_HEADS].transpose(0, 1, 3, 2).reshape(bsz, t, SSM_HEADS)
    xs, b_t, c_in, a_col, a_row, dt_col = _ssd_inputs(xbc_all, dt_raw, w, t)
    s0 = jnp.zeros((bsz, SSM_HEADS, D_STATE, SSM_HEAD_DIM), F32)
    y_hm, s_t = ssd_scan(w['ssm_d'], xs, b_t, c_in, a_col, a_row, dt_col, s0, SSM_CHUNK)
    ssd = _ssd_finish(y_hm, z.reshape(bsz, t, D_INNER), w, t).reshape(m, D_INNER)

    y = _token_tail(x.reshape(m, D_MODEL), (attn, ssd), p.reshape(m, PLE_DIM), w, 512)
    keep = min(WINDOW, t)
    return (y.reshape(bsz, t, D_MODEL), kv_new, win_new[:, t - keep:], s_t.transpose(0, 1, 3, 2),
            xbc_all[:, xbc_all.shape[1] - (CONV_W - 1):])


def _sample_group(x, p, pool, page_table, cache_win, state_ssm, state_conv, w):
    bsz, lq, _ = x.shape
    m = bsz * lq
    q_off = page_table.shape[1] * PAGE_SIZE
    q, kvs, z, xbc, sm = rms_inproj(x.reshape(m, D_MODEL), w['attn_norm'], w['w_in_r'], m)
    pos = q_off + jnp.arange(lq)
    qn = _rmsnorm(q.reshape(bsz, lq, N_HEADS, HEAD_DIM), w['q_norm'])
    qr = _rope(qn, pos)
    kvs = kvs.reshape(bsz, lq, 6, KV_HEADS, HEAD_DIM)
    k_c = _rmsnorm(kvs[:, :, 0], w['k_norm'][0])
    k_s = _rope(_rmsnorm(kvs[:, :, 2], w['k_norm'][1]), pos)
    k_w = _rope(_rmsnorm(kvs[:, :, 4], w['k_norm'][2]), pos)
    kv_new = jnp.stack([k_c, kvs[:, :, 1], k_s, kvs[:, :, 3]], axis=2)
    win_new = jnp.stack([k_w, kvs[:, :, 5]], axis=2)
    win_cat = jnp.concatenate([cache_win, win_new], axis=1)

    pool_t = jnp.transpose(pool, (0, 2, 3, 4, 1))
    kvc = compress_paged(pool_t, page_table, w['cmp_pe'], w['cmp_w1'], w['cmp_w2'])
    attn = nsa_sample(qn, qr, kvc, kv_new, win_cat, pool_t, page_table, sm[:, :N_GATES], q_off)

    xbc_all = jnp.concatenate([state_conv, xbc.reshape(bsz, lq, CONV_DIM)], axis=1)
    dt_raw = sm[:, N_GATES:N_GATES + SSM_HEADS].reshape(bsz, lq, SSM_HEADS)
    xs, b_t, c_in, a_col, a_row, dt_col = _ssd_inputs(xbc_all, dt_raw, w, SSM_CHUNK)
    y_hm, s_t = ssd_scan(w['ssm_d'], xs, b_t, c_in, a_col, a_row, dt_col, state_ssm.transpose(0, 1, 3, 2), SSM_CHUNK)
    ssd = _ssd_finish(y_hm, z.reshape(bsz, lq, D_INNER), w, lq).reshape(m, D_INNER)

    y = _token_tail(x.reshape(m, D_MODEL), (attn, ssd), p.reshape(m, PLE_DIM), w, m)
    keep = cache_win.shape[1]
    return (y.reshape(bsz, lq, D_MODEL), kv_new, win_cat[:, win_cat.shape[1] - keep:], s_t.transpose(0, 1, 3, 2),
            xbc_all[:, xbc_all.shape[1] - (CONV_W - 1):])


def kernel(x_prompt, x_sample, cache_kv, cache_win, state_ssm, state_conv, page_table, p_prompt, p_sample,
           w_in, w_out, q_norm, k_norm, cmp_pe, cmp_w1, cmp_w2, conv_w, conv_b, dt_bias, a_log, ssm_d, ssm_norm,
           attn_norm, ffn_norm, w_rg, b_rg, w_re, b_re, w_gate, w_up, w_down, w_ple_proj, ple_norm, w_ple_gate):
    depth = w_in.shape[0]
    hp, hs = x_prompt, x_sample
    outs = [[] for _ in range(8)]
    cuts = np.cumsum((ATT_W, 6 * KV_HEADS * HEAD_DIM, N_GATES, D_INNER, CONV_DIM, SSM_HEADS))
    for l in range(depth):
        wi = w_in[l]
        w_in_r = jnp.concatenate(
            [wi[:, :cuts[1]], wi[:, cuts[2]:cuts[3]], wi[:, cuts[3]:cuts[4]], wi[:, cuts[1]:cuts[2]],
             wi[:, cuts[4]:cuts[5]], jnp.zeros((D_MODEL, C_SM - N_GATES - SSM_HEADS), F32)], axis=1).astype(BF16)
        w = dict(w_in_r=w_in_r, w_out_b=w_out[l].astype(BF16), q_norm=q_norm[l], k_norm=k_norm[l],
                 cmp_pe=cmp_pe[l], cmp_w1=cmp_w1[l], cmp_w2=cmp_w2[l], conv_w=conv_w[l], conv_b=conv_b[l],
                 dt_bias=dt_bias[l], a_log=a_log[l], ssm_d=ssm_d[l], ssm_norm=ssm_norm[l], attn_norm=attn_norm[l],
                 ffn_norm=ffn_norm[l], w_rg=w_rg[l], b_rg=b_rg[l], w_re=w_re[l], b_re=b_re[l],
                 w_gate=w_gate[l], w_up=w_up[l], w_down=w_down[l],
                 wpp_b=w_ple_proj[l].astype(BF16), ple_norm=ple_norm[l], wpg_b=w_ple_gate[l].astype(BF16))
        hp, *rest_p = _prompt_group(hp, p_prompt[l], w)
        hs, *rest_s = _sample_group(hs, p_sample[l], cache_kv[l], page_table, cache_win[l], state_ssm[l],
                                    state_conv[l], w)
        for j in range(4):
            outs[2 * j].append(rest_p[j])
            outs[2 * j + 1].append(rest_s[j])
    return (hp, hs) + tuple(jnp.stack(o) for o in outs)
```

```python
import functools
import math

import numpy as np
import jax
import jax.numpy as jnp
from jax import lax
from jax.experimental import pallas as pl
from jax.experimental.pallas import tpu as pltpu

F32 = jnp.float32
BF16 = jnp.bfloat16

D_MODEL = 1024
PAGE_SIZE = 128
N_HEADS = 8
HEAD_DIM = 64
KV_HEADS = 2
HPG = N_HEADS // KV_HEADS
ATT_W = N_HEADS * HEAD_DIM
L_CMP = 32
S_CMP = 16
L_SLC = 64
N_SEL = 16
WINDOW = 512
CMP_HID = 64
ROPE_THETA = 10000.0
SSM_HEADS = 8
SSM_HEAD_DIM = 64
D_INNER = SSM_HEADS * SSM_HEAD_DIM
SSM_GROUPS = 2
D_STATE = 64
CONV_W = 4
CONV_DIM = D_INNER + 2 * SSM_GROUPS * D_STATE
SSM_CHUNK = 128
N_EXPERT_GROUPS = 4
EXPERTS_PER_GROUP = 8
N_EXPERTS = N_EXPERT_GROUPS * EXPERTS_PER_GROUP
TOP_K = 2
D_EXPERT = 512
PLE_DIM = 256
EPS = 1e-6
N_GATES = 3 * N_HEADS
C_Q, C_KV, C_Z, C_XBC, C_SM = 512, 768, 512, 768, 128
D_IN_PAD = C_Q + C_KV + C_Z + C_XBC + C_SM

NEG = -1e30
VMEM_LIMIT = 48 * 1024 * 1024
MOE_ROWS = 256


def _cparams(sem):
    return pltpu.CompilerParams(dimension_semantics=sem, vmem_limit_bytes=VMEM_LIMIT)


def _dot(a, b):
    return jnp.dot(a, b, preferred_element_type=F32)


def _dot_nt(a, b):
    return lax.dot_general(a, b, (((1,), (1,)), ((), ())), preferred_element_type=F32)


def _dot_hi(a, b):
    return jnp.dot(a, b, preferred_element_type=F32, precision=lax.Precision.HIGHEST)


def _rms_inproj_kernel(x_ref, g_ref, w_ref, q_ref, kv_ref, z_ref, xbc_ref, sm_ref):
    x = x_ref[...]
    y = x * lax.rsqrt(jnp.mean(x * x, axis=-1, keepdims=True) + EPS) * g_ref[...]
    yb = y.astype(BF16)
    c0 = 0
    for ref, width in ((q_ref, C_Q), (kv_ref, C_KV), (z_ref, C_Z), (xbc_ref, C_XBC), (sm_ref, C_SM)):
        ref[...] = _dot(yb, w_ref[:, c0:c0 + width])
        c0 += width


def rms_inproj(x, gain, w_r, tm):
    m = x.shape[0]
    widths = (C_Q, C_KV, C_Z, C_XBC, C_SM)
    return pl.pallas_call(
        _rms_inproj_kernel,
        grid=(m // tm,),
        in_specs=[pl.BlockSpec((tm, D_MODEL), lambda i: (i, 0)),
                  pl.BlockSpec((1, D_MODEL), lambda i: (0, 0)),
                  pl.BlockSpec((D_MODEL, D_IN_PAD), lambda i: (0, 0))],
        out_specs=[pl.BlockSpec((tm, w), lambda i: (i, 0)) for w in widths],
        out_shape=[jax.ShapeDtypeStruct((m, w), F32) for w in widths],
        compiler_params=_cparams(("parallel",)),
        name="rms_inproj",
    )(x, gain.reshape(1, D_MODEL), w_r)


def _inproj_prompt_kernel(x_ref, g_ref, w_ref, cos_ref, sin_ref, gq_ref, gk_ref, z_ref, xbc_ref, smt_ref, kv_ref,
                          win_ref, qn_ref, qr_ref, ksa_ref, kw_ref, vst_ref, vwt_ref, *, tiles_per_seq):
    tm = x_ref.shape[0]
    x = x_ref[...]
    yb = (x * lax.rsqrt(jnp.mean(x * x, axis=-1, keepdims=True) + EPS) * g_ref[...]).astype(BF16)
    cos, sin = cos_ref[...], sin_ref[...]
    scale = HEAD_DIM ** -0.5
    half = HEAD_DIM // 2

    def proj(c0, width=128):
        return _dot(yb, w_ref[:, c0:c0 + width])

    def head_norm(x_t, gain):
        return x_t * lax.rsqrt(jnp.mean(x_t * x_t, axis=0, keepdims=True) + EPS) * gain

    def rotate(y_t):
        y1, y2 = y_t[0:half], y_t[half:HEAD_DIM]
        return jnp.concatenate([y1 * cos - y2 * sin, y2 * cos + y1 * sin], axis=0)

    def per_group(blk, fn):
        t = blk.T
        return jnp.concatenate([fn(t[g * HEAD_DIM:(g + 1) * HEAD_DIM]) for g in range(KV_HEADS)], axis=0).T

    for j in range(ATT_W // 128):
        q_t = proj(128 * j).T
        for h2 in range(128 // HEAD_DIM):
            head = (128 // HEAD_DIM) * j + h2
            g, hh = head // HPG, head % HPG
            qn = head_norm(q_t[h2 * HEAD_DIM:(h2 + 1) * HEAD_DIM], gq_ref[...])
            qn_ref[0, g, 0, :, hh * tm:(hh + 1) * tm] = (qn * scale).astype(BF16)
            qr_ref[0, g, 0, :, hh * tm:(hh + 1) * tm] = (rotate(qn) * scale).astype(BF16)

    c_kv = C_Q
    k_c = per_group(proj(c_kv), lambda t: head_norm(t, gk_ref[0]))
    v_c = proj(c_kv + 128)
    k_s = per_group(proj(c_kv + 256), lambda t: rotate(head_norm(t, gk_ref[1])))
    v_s = proj(c_kv + 384)
    k_w = per_group(proj(c_kv + 512), lambda t: rotate(head_norm(t, gk_ref[2])))
    v_w = proj(c_kv + 640)
    kv_ref[:, 0:128] = k_c
    kv_ref[:, 128:256] = v_c
    kv_ref[:, 256:384] = k_s
    kv_ref[:, 384:512] = v_s
    win_ref[:, 0:128] = k_w
    win_ref[:, 128:256] = v_w
    kw_ref[...] = k_w.astype(BF16)
    pos0 = (pl.program_id(0) % tiles_per_seq) * tm
    lane = lax.broadcasted_iota(jnp.int32, (tm, 128), 1)
    blk_of_row = (pos0 + lax.broadcasted_iota(jnp.int32, (tm, 128), 0)) // L_SLC
    onehot = (blk_of_row == (lane & (HEAD_DIM - 1))).astype(BF16)
    k_sb = k_s.astype(BF16)
    ksa_ref[0, 0] = jnp.where(lane < HEAD_DIM, k_sb, onehot)
    ksa_ref[0, 1] = jnp.where(lane >= HEAD_DIM, k_sb, onehot)
    vs_t = v_s.T.astype(BF16)
    vw_t = v_w.T.astype(BF16)
    for g in range(KV_HEADS):
        vst_ref[0, g, 0] = vs_t[g * HEAD_DIM:(g + 1) * HEAD_DIM]
        vwt_ref[0, g, 0] = vw_t[g * HEAD_DIM:(g + 1) * HEAD_DIM]
    z_ref[...] = proj(C_Q + C_KV, C_Z)
    xbc_ref[...] = proj(C_Q + C_KV + C_Z, C_XBC)
    smt_ref[0, 0] = proj(C_Q + C_KV + C_Z + C_XBC).T[0:32]


def rms_inproj_prompt(x, gain, w_r, q_gain, k_gain, bsz, t):
    tm = NSA_TQ
    assert NSA_TK == tm and NSA_TKW == tm and t // L_SLC <= HEAD_DIM
    m = bsz * t
    nt = t // tm
    inv = ROPE_THETA ** (-jnp.arange(HEAD_DIM // 2, dtype=F32) / (HEAD_DIM // 2))
    ang = inv[:, None] * jnp.arange(t, dtype=F32)[None, :]
    row = lambda width: pl.BlockSpec((tm, width), lambda i: (i, 0))
    qspec = pl.BlockSpec((1, KV_HEADS, 1, HEAD_DIM, HPG * tm), lambda i: (i // nt, 0, i % nt, 0, 0))
    vspec = pl.BlockSpec((1, KV_HEADS, 1, HEAD_DIM, tm), lambda i: (i // nt, 0, i % nt, 0, 0))
    tspec = pl.BlockSpec((HEAD_DIM // 2, tm), lambda i: (0, i % nt))
    qshape = jax.ShapeDtypeStruct((bsz, KV_HEADS, nt, HEAD_DIM, HPG * tm), BF16)
    vshape = jax.ShapeDtypeStruct((bsz, KV_HEADS, nt, HEAD_DIM, tm), BF16)
    return pl.pallas_call(
        functools.partial(_inproj_prompt_kernel, tiles_per_seq=nt),
        grid=(m // tm,),
        in_specs=[row(D_MODEL),
                  pl.BlockSpec((1, D_MODEL), lambda i: (0, 0)),
                  pl.BlockSpec((D_MODEL, D_IN_PAD), lambda i: (0, 0)),
                  tspec, tspec,
                  pl.BlockSpec((HEAD_DIM, 1), lambda i: (0, 0)),
                  pl.BlockSpec((3, HEAD_DIM, 1), lambda i: (0, 0, 0))],
        out_specs=[row(C_Z), row(C_XBC),
                   pl.BlockSpec((1, 1, 32, tm), lambda i: (i // nt, i % nt, 0, 0)),
                   row(512), row(256), qspec, qspec,
                   pl.BlockSpec((1, KV_HEADS, tm, 128), lambda i: (i // nt, 0, i % nt, 0)),
                   row(128), vspec, vspec],
        out_shape=[jax.ShapeDtypeStruct((m, C_Z), F32), jax.ShapeDtypeStruct((m, C_XBC), F32),
                   jax.ShapeDtypeStruct((bsz, nt, 32, tm), F32),
                   jax.ShapeDtypeStruct((m, 512), F32), jax.ShapeDtypeStruct((m, 256), F32), qshape, qshape,
                   jax.ShapeDtypeStruct((bsz, KV_HEADS, t, 128), BF16),
                   jax.ShapeDtypeStruct((m, 128), BF16), vshape, vshape],
        compiler_params=_cparams(("parallel",)),
        name="rms_inproj_prompt",
    )(x, gain.reshape(1, D_MODEL), w_r, jnp.cos(ang), jnp.sin(ang), q_gain.reshape(HEAD_DIM, 1),
      k_gain.reshape(3, HEAD_DIM, 1))


def _outproj_kernel(x_ref, a_ref, s_ref, w_ref, g_ref, wr_ref, br_ref, h_ref, f_ref, lg_ref):
    acc = _dot(a_ref[...], w_ref[0:ATT_W, :]) + _dot(s_ref[...], w_ref[ATT_W:ATT_W + D_INNER, :])
    h = x_ref[...] + acc
    h_ref[...] = h
    f = h * lax.rsqrt(jnp.mean(h * h, axis=-1, keepdims=True) + EPS) * g_ref[...]
    f_ref[...] = f.astype(BF16)
    lg_ref[...] = _dot_hi(f, wr_ref[...]) + br_ref[...]


def outproj(x, attn, ssd, w_out_b, ffn_gain, w_router, b_router, tm):
    m = x.shape[0]
    row = lambda width: pl.BlockSpec((tm, width), lambda i: (i, 0))
    fixed = lambda r, c: pl.BlockSpec((r, c), lambda i: (0, 0))
    return pl.pallas_call(
        _outproj_kernel,
        grid=(m // tm,),
        in_specs=[row(D_MODEL), row(ATT_W), row(D_INNER), fixed(ATT_W + D_INNER, D_MODEL),
                  fixed(1, D_MODEL), fixed(D_MODEL, 128), fixed(1, 128)],
        out_specs=[row(D_MODEL), row(D_MODEL), row(128)],
        out_shape=[jax.ShapeDtypeStruct((m, D_MODEL), F32), jax.ShapeDtypeStruct((m, D_MODEL), BF16),
                   jax.ShapeDtypeStruct((m, 128), F32)],
        compiler_params=_cparams(("parallel",)),
        name="outproj",
    )(x, attn, ssd, w_out_b, ffn_gain.reshape(1, D_MODEL), w_router, b_router)


def _ple_kernel(h_ref, y0_ref, y1_ref, wt_ref, p_ref, wg_ref, wp_ref, g_ref, o_ref):
    wt = wt_ref[...]
    h = h_ref[...] + (y0_ref[...] * wt[:, 0:1] + y1_ref[...] * wt[:, 1:2])
    gate = jax.nn.sigmoid(_dot(h.astype(BF16), wg_ref[...]))
    e = _dot(p_ref[...].astype(BF16), wp_ref[...])
    e = e * lax.rsqrt(jnp.mean(e * e, axis=-1, keepdims=True) + EPS) * g_ref[...]
    o_ref[...] = h + gate * e


def moe_combine_ple(h, y01, wts, p, wg_b, wp_b, gain, tm):
    m = h.shape[0]
    nt = m // tm
    return pl.pallas_call(
        _ple_kernel,
        grid=(nt,),
        in_specs=[pl.BlockSpec((tm, D_MODEL), lambda i: (i, 0)),
                  pl.BlockSpec((tm, D_MODEL), lambda i: (i, 0)),
                  pl.BlockSpec((tm, D_MODEL), lambda i: (i + nt, 0)),
                  pl.BlockSpec((tm, 128), lambda i: (i, 0)),
                  pl.BlockSpec((tm, PLE_DIM), lambda i: (i, 0)),
                  pl.BlockSpec((D_MODEL, D_MODEL), lambda i: (0, 0)),
                  pl.BlockSpec((PLE_DIM, D_MODEL), lambda i: (0, 0)),
                  pl.BlockSpec((1, D_MODEL), lambda i: (0, 0))],
        out_specs=pl.BlockSpec((tm, D_MODEL), lambda i: (i, 0)),
        out_shape=jax.ShapeDtypeStruct((m, D_MODEL), F32),
        compiler_params=_cparams(("parallel",)),
        name="moe_combine_ple",
    )(h, y01, y01, wts, p, wg_b, wp_b, gain.reshape(1, D_MODEL))


def _moe_kernel(be_ref, nb_ref, x_ref, wg_ref, wu_ref, wd_ref, y_ref, wg_sc, wu_sc, wd_sc):
    i = pl.program_id(0)

    @pl.when((i == 0) | (be_ref[i] != be_ref[jnp.maximum(i - 1, 0)]))
    def _():
        wg_sc[...] = wg_ref[0].astype(BF16)
        wu_sc[...] = wu_ref[0].astype(BF16)
        wd_sc[...] = wd_ref[0].astype(BF16)

    @pl.when(i < nb_ref[0])
    def _():
        x = x_ref[...]
        a = _dot(x, wg_sc[...])
        hb = (a * jax.nn.sigmoid(a)) * _dot(x, wu_sc[...])
        y_ref[...] = _dot(hb.astype(BF16), wd_sc[...])

    @pl.when(i >= nb_ref[0])
    def _():
        y_ref[...] = jnp.zeros(y_ref.shape, F32)


def moe_experts(xpad, blk_e, n_used, w_gate, w_up, w_down):
    n_blk = xpad.shape[0] // MOE_ROWS
    grid_spec = pltpu.PrefetchScalarGridSpec(
        num_scalar_prefetch=2,
        grid=(n_blk,),
        in_specs=[pl.BlockSpec((MOE_ROWS, D_MODEL), lambda i, be, nb: (i, 0)),
                  pl.BlockSpec((1, D_MODEL, D_EXPERT), lambda i, be, nb: (be[i], 0, 0)),
                  pl.BlockSpec((1, D_MODEL, D_EXPERT), lambda i, be, nb: (be[i], 0, 0)),
                  pl.BlockSpec((1, D_EXPERT, D_MODEL), lambda i, be, nb: (be[i], 0, 0))],
        out_specs=pl.BlockSpec((MOE_ROWS, D_MODEL), lambda i, be, nb: (i, 0)),
        scratch_shapes=[pltpu.VMEM((D_MODEL, D_EXPERT), BF16), pltpu.VMEM((D_MODEL, D_EXPERT), BF16),
                        pltpu.VMEM((D_EXPERT, D_MODEL), BF16)],
    )
    return pl.pallas_call(
        _moe_kernel,
        grid_spec=grid_spec,
        out_shape=jax.ShapeDtypeStruct((n_blk * MOE_ROWS, D_MODEL), F32),
        compiler_params=_cparams(("arbitrary",)),
        name="moe_experts",
    )(blk_e, n_used, xpad, w_gate, w_up, w_down)


def _compress_kernel(seg_ref, w1a_ref, w1b_ref, pe_ref, w1_ref, w2_ref, o_ref):
    seg = seg_ref[0, 0]
    n_seg = seg.shape[0]
    hid0 = _dot(pe_ref[0], w1_ref[0])[0:1]
    p0 = _dot(seg, w1a_ref[0])
    p1 = pltpu.roll(_dot(seg, w1b_ref[0]), n_seg - 1, axis=0)
    hid = hid0 + p0 + p1
    act = hid * jax.nn.sigmoid(hid)
    o_ref[0, 0] = _dot(act.astype(BF16), w2_ref[0]).astype(BF16)


def compress(segs, cmp_pe, cmp_w1, cmp_w2):
    _, r, n_seg, k = segs.shape
    w1 = cmp_w1.astype(BF16)
    pe = jnp.broadcast_to(cmp_pe.reshape(2, 1, L_CMP * HEAD_DIM), (2, 8, L_CMP * HEAD_DIM)).astype(BF16)
    return pl.pallas_call(
        _compress_kernel,
        grid=(2, r),
        in_specs=[pl.BlockSpec((1, 1, n_seg, k), lambda s, i: (s, i, 0, 0)),
                  pl.BlockSpec((1, k, CMP_HID), lambda s, i: (s, 0, 0)),
                  pl.BlockSpec((1, k, CMP_HID), lambda s, i: (s, 1, 0)),
                  pl.BlockSpec((1, 8, 2 * k), lambda s, i: (s, 0, 0)),
                  pl.BlockSpec((1, 2 * k, CMP_HID), lambda s, i: (s, 0, 0)),
                  pl.BlockSpec((1, CMP_HID, HEAD_DIM), lambda s, i: (s, 0, 0))],
        out_specs=pl.BlockSpec((1, 1, n_seg, HEAD_DIM), lambda s, i: (s, i, 0, 0)),
        out_shape=jax.ShapeDtypeStruct((2, r, n_seg, HEAD_DIM), BF16),
        compiler_params=_cparams(("parallel", "parallel")),
        name="compress",
    )(segs, w1, w1, pe, w1, cmp_w2.astype(BF16))


PAGES_PER_STEP = 16


def _page_copies(pt_ref, pool_ref, buf, sem, b, j, slot):
    return [pltpu.make_async_copy(pool_ref.at[pt_ref[b, j * PAGES_PER_STEP + p], pl.ds(0, 2)],
                                  buf.at[slot, p], sem.at[slot]) for p in range(PAGES_PER_STEP)]


def _paged_partials_kernel(pt_ref, pool_ref, wa_ref, wb_ref, p0_ref, p1_ref, buf, sem, x_sc):
    b = pl.program_id(0)
    j = pl.program_id(1)
    n_grp = pl.num_programs(1)
    step = b * n_grp + j
    slot = step % 2

    @pl.when(step == 0)
    def _():
        for c in _page_copies(pt_ref, pool_ref, buf, sem, b, j, slot):
            c.start()

    @pl.when(step + 1 < pl.num_programs(0) * n_grp)
    def _():
        wrap = j + 1 == n_grp
        for c in _page_copies(pt_ref, pool_ref, buf, sem, jnp.where(wrap, b + 1, b), jnp.where(wrap, 0, j + 1),
                              1 - slot):
            c.start()

    for c in _page_copies(pt_ref, pool_ref, buf, sem, b, j, slot):
        c.wait()

    n_seg = PAGES_PER_STEP * PAGE_SIZE // S_CMP
    for kv in range(2):
        for g in range(KV_HEADS):
            for p in range(PAGES_PER_STEP):
                x_sc[p * PAGE_SIZE:(p + 1) * PAGE_SIZE, :] = buf[slot, p, kv, g].T
            acc0 = jnp.zeros((n_seg, CMP_HID), F32)
            acc1 = jnp.zeros((n_seg, CMP_HID), F32)
            for s in range(S_CMP):
                xs = x_sc[pl.ds(s, n_seg, stride=S_CMP), :].astype(BF16)
                acc0 = acc0 + _dot(xs, wa_ref[kv, s])
                acc1 = acc1 + _dot(xs, wb_ref[kv, s])
            p0_ref[kv, 0, g] = acc0
            p1_ref[kv, 0, g] = acc1


def _compress_finish_kernel(p0_ref, p1_ref, pe_ref, w1_ref, w2_ref, o_ref):
    n_seg = p0_ref.shape[3]
    hid0 = _dot(pe_ref[0], w1_ref[0])[0:1]
    hid = hid0 + p0_ref[0, 0, 0] + pltpu.roll(p1_ref[0, 0, 0], n_seg - 1, axis=0)
    act = hid * jax.nn.sigmoid(hid)
    o_ref[0, 0] = _dot(act.astype(BF16), w2_ref[0]).astype(BF16)


def compress_paged(pool_t, page_table, cmp_pe, cmp_w1, cmp_w2):
    bsz, n_pages = page_table.shape
    assert n_pages % PAGES_PER_STEP == 0
    n_grp = n_pages // PAGES_PER_STEP
    seg_step = PAGES_PER_STEP * PAGE_SIZE // S_CMP
    n_seg = n_grp * seg_step
    w1 = cmp_w1.astype(BF16).reshape(2, L_CMP // S_CMP, S_CMP, HEAD_DIM, CMP_HID)
    part_shape = jax.ShapeDtypeStruct((2, bsz, KV_HEADS, n_seg, CMP_HID), F32)
    part_spec = pl.BlockSpec((2, 1, KV_HEADS, seg_step, CMP_HID), lambda b, j, pt: (0, b, 0, j, 0))
    wspec = pl.BlockSpec((2, S_CMP, HEAD_DIM, CMP_HID), lambda b, j, pt: (0, 0, 0, 0))
    p0, p1 = pl.pallas_call(
        _paged_partials_kernel,
        grid_spec=pltpu.PrefetchScalarGridSpec(
            num_scalar_prefetch=1,
            grid=(bsz, n_grp),
            in_specs=[pl.BlockSpec(memory_space=pl.ANY), wspec, wspec],
            out_specs=[part_spec, part_spec],
            scratch_shapes=[pltpu.VMEM((2, PAGES_PER_STEP, 2, KV_HEADS, HEAD_DIM, PAGE_SIZE), F32),
                            pltpu.SemaphoreType.DMA((2,)),
                            pltpu.VMEM((PAGES_PER_STEP * PAGE_SIZE, HEAD_DIM), F32)]),
        out_shape=[part_shape, part_shape],
        compiler_params=_cparams(("arbitrary", "arbitrary")),
        name="paged_partials",
    )(page_table, pool_t, w1[:, 0], w1[:, 1])
    pe = jnp.broadcast_to(cmp_pe.reshape(2, 1, L_CMP * HEAD_DIM), (2, 8, L_CMP * HEAD_DIM)).astype(BF16)
    pspec = pl.BlockSpec((1, 1, 1, n_seg, CMP_HID), lambda s, i: (s, i // KV_HEADS, i % KV_HEADS, 0, 0))
    return pl.pallas_call(
        _compress_finish_kernel,
        grid=(2, bsz * KV_HEADS),
        in_specs=[pspec, pspec,
                  pl.BlockSpec((1, 8, L_CMP * HEAD_DIM), lambda s, i: (s, 0, 0)),
                  pl.BlockSpec((1, L_CMP * HEAD_DIM, CMP_HID), lambda s, i: (s, 0, 0)),
                  pl.BlockSpec((1, CMP_HID, HEAD_DIM), lambda s, i: (s, 0, 0))],
        out_specs=pl.BlockSpec((1, 1, n_seg, HEAD_DIM), lambda s, i: (s, i, 0, 0)),
        out_shape=jax.ShapeDtypeStruct((2, bsz * KV_HEADS, n_seg, HEAD_DIM), BF16),
        compiler_params=_cparams(("parallel", "parallel")),
        name="compress_finish",
    )(p0, p1, pe, cmp_w1.astype(BF16), cmp_w2.astype(BF16))


FLASH_SPLIT = 4


def _flash_scores(k, q_ref, s_ref):
    cw = q_ref.shape[1] // FLASH_SPLIT
    for i in range(FLASH_SPLIT):
        c = slice(i * cw, (i + 1) * cw)
        s_ref[:, c] = _dot(k, q_ref[:, c])


def _flash_step(s_ref, v_t, m_sc, l_sc, acc_sc, keep=None):
    cw = s_ref.shape[1] // FLASH_SPLIT
    for i in range(FLASH_SPLIT):
        c = slice(i * cw, (i + 1) * cw)
        s_t = s_ref[:, c]
        if keep is not None:
            s_t = jnp.where(keep(i * cw, cw), s_t, NEG)
        m_prev = m_sc[:, c]
        m_new = jnp.maximum(m_prev, jnp.max(s_t, axis=0, keepdims=True))
        alpha = jnp.exp(m_prev - m_new)
        p = jnp.exp(s_t - m_new)
        l_sc[:, c] = alpha * l_sc[:, c] + jnp.sum(p, axis=0, keepdims=True)
        acc_sc[:, c] = alpha * acc_sc[:, c] + _dot(v_t, p.astype(BF16))
        m_sc[:, c] = m_new


def _flash_reset(m_sc, l_sc, acc_sc):
    m_sc[...] = jnp.full(m_sc.shape, NEG, F32)
    l_sc[...] = jnp.zeros(l_sc.shape, F32)
    acc_sc[...] = jnp.zeros(acc_sc.shape, F32)


def _nsa_prompt_kernel(qn_ref, qr_ref, kc_ref, vct_ref, ks_ref, vst_ref, kw_ref, vwt_ref, covert_ref, gate_ref,
                       o_ref, qa_sc, qw_sc, m_sc, l_sc, acc_sc, out_sc, score_sc, rank_sc, s_sc, sw_sc, *, tq, tk, tkw):
    qi = pl.program_id(2)
    pos0 = qi * tq
    w = HPG * tq

    def positions(rows, base):
        kpos = base + lax.broadcasted_iota(jnp.int32, (rows, w), 0)
        qpos = pos0 + (lax.broadcasted_iota(jnp.int32, (rows, w), 1) & (tq - 1))
        return kpos, qpos

    g = pl.program_id(1)

    def gate_row(branch):
        rows = [gate_ref[0, 0, pl.ds(3 * (g * HPG + hh) + branch, 1), :] for hh in range(HPG)]
        return jax.nn.sigmoid(jnp.concatenate(rows, axis=1))

    gate = [gate_row(branch) for branch in range(3)]

    kc = kc_ref[0, 0]
    n_c = kc.shape[0]
    cidx, qpos_c = positions(n_c, 0)
    valid = cidx * S_CMP + (L_CMP - 1) <= qpos_c
    s1 = jnp.where(valid, _dot(kc, qn_ref[0, 0, 0]), NEG)
    e1 = jnp.exp(s1 - jnp.max(s1, axis=0, keepdims=True)) * valid.astype(F32)
    p1b = (e1 * (1.0 / jnp.maximum(jnp.sum(e1, axis=0, keepdims=True), 1e-30))).astype(BF16)
    out_sc[...] = gate[0] * _dot(vct_ref[0, 0], p1b)

    p1_stack = jnp.concatenate([p1b[:, hh * tq:(hh + 1) * tq] for hh in range(HPG)], axis=0)
    imp = _dot(covert_ref[...], p1_stack)
    n_s = imp.shape[0]
    blk = lax.broadcasted_iota(jnp.int32, (n_s, tq), 0)
    cur = (pos0 + lax.broadcasted_iota(jnp.int32, (n_s, tq), 1)) // L_SLC
    vis = blk <= cur
    forced = vis & ((blk == 0) | (blk >= cur - 1))
    score_sc[...] = jnp.where(forced, 1e9, jnp.where(vis, imp, -1.0))
    rank_sc[...] = jnp.zeros(rank_sc.shape, F32)
    last_blk = (pos0 + tq - 1) // L_SLC
    n_oct = n_s // 8
    for oi in range(n_oct):
        @pl.when(oi * 8 <= last_blk)
        def _():
            rows = [slice(8 * oj, 8 * oj + 8) for oj in range(n_oct)]
            parts = [rank_sc[r, :] for r in rows]
            for i in range(8 * oi, 8 * oi + 8):
                c = score_sc[i:i + 1, :]
                for oj, r in enumerate(rows):
                    sj = score_sc[r, :]
                    if oj > oi:
                        beats = c >= sj
                    elif oj < oi:
                        beats = c > sj
                    else:
                        above = lax.broadcasted_iota(jnp.int32, (8, tq), 0) > i - 8 * oi
                        beats = (c > sj) | ((c == sj) & above)
                    parts[oj] = parts[oj] + beats.astype(F32)
            for r, part in zip(rows, parts):
                rank_sc[r, :] = part
    sel_bias = jnp.where(vis & (rank_sc[...] < N_SEL), 0.0, NEG).astype(BF16)
    own = pl.ds(pl.multiple_of(g * HEAD_DIM, HEAD_DIM), HEAD_DIM)
    other = pl.ds(pl.multiple_of((1 - g) * HEAD_DIM, HEAD_DIM), HEAD_DIM)
    qa_sc[other, :] = jnp.zeros((HEAD_DIM, w), BF16)
    qw_sc[other, :] = jnp.zeros((HEAD_DIM, w), BF16)
    qa_sc[own, :] = qr_ref[0, 0, 0]
    qw_sc[own, :] = qr_ref[0, 0, 0]
    qa_sc[pl.ds(pl.multiple_of((1 - g) * HEAD_DIM, HEAD_DIM), n_s), :] = jnp.concatenate([sel_bias] * HPG, axis=1)

    def sel_keys(kt):
        return ks_ref[0, 0, pl.ds(pl.multiple_of(kt * tk, tk), tk), :]

    def win_keys(kt):
        return kw_ref[pl.ds(pl.multiple_of(kt * tkw, tkw), tkw), :]

    def chunk_positions(rows, base, lane0, n_lanes):
        kpos = base + lax.broadcasted_iota(jnp.int32, (rows, n_lanes), 0)
        qpos = pos0 + ((lane0 + lax.broadcasted_iota(jnp.int32, (rows, n_lanes), 1)) & (tq - 1))
        return kpos, qpos

    def causal(rows, base):
        def keep(lane0, n_lanes):
            kpos, qpos = chunk_positions(rows, base, lane0, n_lanes)
            return kpos <= qpos
        return keep

    _flash_reset(m_sc, l_sc, acc_sc)
    n_full = pos0 // tk
    _flash_scores(sel_keys(0), qa_sc, s_sc.at[0])

    def sel_body(kt, carry):
        for parity in range(2):
            @pl.when(kt % 2 == parity)
            def _():
                _flash_scores(sel_keys(kt + 1), qa_sc, s_sc.at[1 - parity])
                _flash_step(s_sc.at[parity], vst_ref[0, 0, kt], m_sc, l_sc, acc_sc)
        return carry

    lax.fori_loop(0, n_full, sel_body, 0)

    n_inner = WINDOW // tkw - 1
    win_tiles = [qi - d for d in range(n_inner + 2)]
    for idx, kt in enumerate(win_tiles):
        _flash_scores(win_keys(jnp.maximum(kt, 0)), qw_sc, sw_sc.at[idx])

    _flash_step(s_sc.at[n_full % 2], vst_ref[0, 0, n_full], m_sc, l_sc, acc_sc, causal(tk, n_full * tk))
    out_sc[...] = out_sc[...] + (gate[1] * (1.0 / l_sc[...])) * acc_sc[...]

    _flash_reset(m_sc, l_sc, acc_sc)
    _flash_step(sw_sc.at[0], vwt_ref[0, 0, qi], m_sc, l_sc, acc_sc, causal(tkw, pos0))
    for idx in range(1, n_inner + 1):
        kt = win_tiles[idx]
        _flash_step(sw_sc.at[idx], vwt_ref[0, 0, jnp.maximum(kt, 0)], m_sc, l_sc, acc_sc,
                    lambda lane0, n_lanes, kt=kt: kt >= 0)
    kt_far = win_tiles[n_inner + 1]

    def far_keep(lane0, n_lanes):
        kpos, qpos = chunk_positions(tkw, kt_far * tkw, lane0, n_lanes)
        return (kpos > qpos - WINDOW) & (kt_far >= 0)

    _flash_step(sw_sc.at[n_inner + 1], vwt_ref[0, 0, jnp.maximum(kt_far, 0)], m_sc, l_sc, acc_sc, far_keep)

    o = out_sc[...] + (gate[2] * (1.0 / l_sc[...])) * acc_sc[...]
    o_ref[0, 0, 0] = o.astype(o_ref.dtype)


NSA_TQ, NSA_TK, NSA_TKW = 256, 256, 256


def nsa_prompt(qn_t, qr_t, kc, vc_t, ks_aug, vs_t, kw, vw_t, sm_t, bsz, t):
    tq, tk, tkw = NSA_TQ, NSA_TK, NSA_TKW
    assert tq == tkw and tk % tq == 0 and WINDOW % tkw == 0
    n_c = kc.shape[1]
    n_s = t // L_SLC
    assert n_s <= HEAD_DIM and KV_HEADS == 2
    w = HPG * tq
    c0 = np.arange(n_c)[None, :] * S_CMP
    s0 = np.arange(n_s)[:, None] * L_SLC
    cover_t = (np.maximum(np.minimum(c0 + L_CMP, s0 + L_SLC) - np.maximum(c0, s0), 0) / L_CMP).astype(np.float32)
    cover_t = np.tile(cover_t, (1, HPG))
    qspec = pl.BlockSpec((1, 1, 1, HEAD_DIM, w), lambda b, g, i: (b, g, i, 0, 0))
    kern = functools.partial(_nsa_prompt_kernel, tq=tq, tk=tk, tkw=tkw)
    return pl.pallas_call(
        kern,
        grid=(bsz, KV_HEADS, t // tq),
        in_specs=[qspec, qspec,
                  pl.BlockSpec((1, 1, n_c, HEAD_DIM), lambda b, g, i: (b, g, 0, 0)),
                  pl.BlockSpec((1, 1, HEAD_DIM, n_c), lambda b, g, i: (b, g, 0, 0)),
                  pl.BlockSpec((1, 1, t, 2 * HEAD_DIM), lambda b, g, i: (b, g, 0, 0)),
                  pl.BlockSpec((1, 1, t // tk, HEAD_DIM, tk), lambda b, g, i: (b, g, 0, 0, 0)),
                  pl.BlockSpec((t, KV_HEADS * HEAD_DIM), lambda b, g, i: (b, 0)),
                  pl.BlockSpec((1, 1, t // tkw, HEAD_DIM, tkw), lambda b, g, i: (b, g, 0, 0, 0)),
                  pl.BlockSpec((n_s, HPG * n_c), lambda b, g, i: (0, 0)),
                  pl.BlockSpec((1, 1, 32, tq), lambda b, g, i: (b, i, 0, 0))],
        out_specs=qspec,
        out_shape=jax.ShapeDtypeStruct((bsz, KV_HEADS, t // tq, HEAD_DIM, w), BF16),
        scratch_shapes=[pltpu.VMEM((2 * HEAD_DIM, w), BF16), pltpu.VMEM((2 * HEAD_DIM, w), BF16),
                        pltpu.VMEM((1, w), F32), pltpu.VMEM((1, w), F32),
                        pltpu.VMEM((HEAD_DIM, w), F32), pltpu.VMEM((HEAD_DIM, w), F32),
                        pltpu.VMEM((n_s, tq), F32), pltpu.VMEM((n_s, tq), F32),
                        pltpu.VMEM((2, tk, w), F32), pltpu.VMEM((WINDOW // tkw + 1, tkw, w), F32)],
        compiler_params=_cparams(("parallel", "parallel", "arbitrary")),
        name="nsa_prompt",
    )(qn_t, qr_t, kc.reshape(bsz, KV_HEADS, n_c, HEAD_DIM), vc_t.reshape(bsz, KV_HEADS, HEAD_DIM, n_c),
      ks_aug, vs_t, kw, vw_t, jnp.asarray(cover_t, BF16), sm_t)


def _sample_select_kernel(qn_ref, kc_ref, vc_ref, cover_ref, oc_ref, sb_ref, *, q_off, lq, n_cmp, n_slc):
    rows = HPG * lq
    for g in range(KV_HEADS):
        kc = kc_ref[0, g]
        n_c = kc.shape[0]
        cidx = lax.broadcasted_iota(jnp.int32, (rows, n_c), 1)
        qpos = q_off + (lax.broadcasted_iota(jnp.int32, (rows, n_c), 0) & (lq - 1))
        valid = (cidx * S_CMP + (L_CMP - 1) <= qpos) & (cidx < n_cmp)
        s1 = jnp.where(valid, _dot_nt(qn_ref[0, g], kc), NEG)
        e1 = jnp.exp(s1 - jnp.max(s1, axis=-1, keepdims=True)) * valid.astype(F32)
        p1b = (e1 * (1.0 / jnp.maximum(jnp.sum(e1, axis=-1, keepdims=True), 1e-30))).astype(BF16)
        oc_ref[0, g] = _dot(p1b, vc_ref[0, g])
        imp4 = _dot(p1b, cover_ref[...])
        imp = imp4[0:lq]
        for hh in range(1, HPG):
            imp = imp + imp4[hh * lq:(hh + 1) * lq]
        n_sp = imp.shape[1]
        blk = lax.broadcasted_iota(jnp.int32, (lq, n_sp), 1)
        cur = (q_off + lax.broadcasted_iota(jnp.int32, (lq, n_sp), 0)) // L_SLC
        vis = (blk <= cur) & (blk < n_slc)
        forced = vis & ((blk == 0) | (blk >= cur - 1))
        score = jnp.where(forced, 1e9, jnp.where(vis, imp, -1.0))
        rank = jnp.zeros((lq, n_sp), F32)
        for i in range(n_slc):
            c = score[:, i:i + 1]
            beats = (c > score) | ((c == score) & (blk > i))
            rank = rank + beats.astype(F32)
        sb_ref[0, g] = jnp.where(vis & (rank < min(N_SEL, n_slc)), 0.0, NEG)


def _sample_attend_kernel(pt_ref, pool_ref, qr_ref, sb_ref, exp_ref, oc_ref, gate_ref, kn_ref, vn_ref, kw_ref, vw_ref,
                          o_ref, buf, sem, m_sc, l_sc, acc_sc, *, q_off, lq, n_win):
    b = pl.program_id(0)
    j = pl.program_id(1)
    n_grp = pl.num_programs(1)
    step = b * n_grp + j
    slot = step % 2
    rows = HPG * lq

    def copies(bb, jj, sl):
        return [pltpu.make_async_copy(pool_ref.at[pt_ref[bb, jj * PAGES_PER_STEP + p], pl.ds(2, 2)],
                                      buf.at[sl, p], sem.at[sl]) for p in range(PAGES_PER_STEP)]

    @pl.when(step == 0)
    def _():
        for c in copies(b, j, slot):
            c.start()

    @pl.when(step + 1 < pl.num_programs(0) * n_grp)
    def _():
        wrap = j + 1 == n_grp
        for c in copies(jnp.where(wrap, b + 1, b), jnp.where(wrap, 0, j + 1), 1 - slot):
            c.start()

    for c in copies(b, j, slot):
        c.wait()

    @pl.when(j == 0)
    def _():
        m_sc[...] = jnp.full(m_sc.shape, NEG, F32)
        l_sc[...] = jnp.zeros(l_sc.shape, F32)
        acc_sc[...] = jnp.zeros(acc_sc.shape, F32)

    def update(g, s, v, v_transposed):
        m_prev = m_sc[g]
        m_new = jnp.maximum(m_prev, jnp.max(s, axis=-1, keepdims=True))
        alpha = jnp.exp(m_prev - m_new)
        p = jnp.exp(s - m_new)
        l_sc[g] = alpha * l_sc[g] + jnp.sum(p, axis=-1, keepdims=True)
        pv = _dot_nt(p.astype(BF16), v) if v_transposed else _dot(p.astype(BF16), v)
        acc_sc[g] = alpha * acc_sc[g] + pv
        m_sc[g] = m_new

    for g in range(KV_HEADS):
        k_t = jnp.concatenate([buf[slot, p, 0, g] for p in range(PAGES_PER_STEP)], axis=1).astype(BF16)
        v_t = jnp.concatenate([buf[slot, p, 1, g] for p in range(PAGES_PER_STEP)], axis=1).astype(BF16)
        sb = sb_ref[0, g, 0]
        bias = _dot(jnp.concatenate([sb] * HPG, axis=0).astype(BF16), exp_ref[...])
        update(g, _dot(qr_ref[0, g], k_t) + bias, v_t, True)

    @pl.when(j == n_grp - 1)
    def _():
        for g in range(KV_HEADS):
            qr = qr_ref[0, g]
            kn = kn_ref[0, g]
            kidx = lax.broadcasted_iota(jnp.int32, (rows, kn.shape[0]), 1)
            qidx = lax.broadcasted_iota(jnp.int32, (rows, kn.shape[0]), 0) & (lq - 1)
            update(g, jnp.where((kidx <= qidx) & (kidx < lq), _dot_nt(qr, kn), NEG), vn_ref[0, g], False)
            o_s = acc_sc[g] * (1.0 / l_sc[g])
            kw = kw_ref[0, g]
            widx = lax.broadcasted_iota(jnp.int32, (rows, kw.shape[0]), 1)
            kpos = q_off + lq - n_win + widx
            qpos = q_off + (lax.broadcasted_iota(jnp.int32, (rows, kw.shape[0]), 0) & (lq - 1))
            ok = (widx < n_win) & (kpos <= qpos) & (kpos > qpos - WINDOW) & (kpos >= 0)
            s3 = jnp.where(ok, _dot_nt(qr, kw), NEG)
            e3 = jnp.exp(s3 - jnp.max(s3, axis=-1, keepdims=True)) * ok.astype(F32)
            p3 = e3 * (1.0 / jnp.maximum(jnp.sum(e3, axis=-1, keepdims=True), 1e-30))
            o_w = _dot(p3.astype(BF16), vw_ref[0, g])
            gate = jax.nn.sigmoid(gate_ref[0, g])
            o_ref[0, g] = gate[:, 0:1] * oc_ref[0, g] + gate[:, 1:2] * o_s + gate[:, 2:3] * o_w


def nsa_sample(q, q_rot, kvc, kv_new, win_cat, pool_t, page_table, gate_logits, q_off):
    bsz, lq = q.shape[:2]
    assert lq & (lq - 1) == 0 and lq <= L_SLC and q_off % (PAGES_PER_STEP * PAGE_SIZE) == 0
    rows = HPG * lq
    scale = HEAD_DIM ** -0.5
    n_seg = kvc.shape[2]
    t_kv = q_off + lq
    n_cmp = (t_kv - L_CMP) // S_CMP + 1
    n_slc = -(-t_kv // L_SLC)
    n_sp = -(-n_slc // 128) * 128
    n_grp = page_table.shape[1] // PAGES_PER_STEP
    blk_step = PAGES_PER_STEP * PAGE_SIZE // L_SLC
    key_step = PAGES_PER_STEP * PAGE_SIZE

    def rows_major(a):
        return a.reshape(bsz, lq, KV_HEADS, HPG, -1).transpose(0, 2, 3, 1, 4).reshape(bsz, KV_HEADS, rows, -1)

    def keys_major(a, n_pad):
        a = a.astype(BF16).transpose(0, 2, 1, 3)
        return jnp.pad(a, ((0, 0), (0, 0), (0, n_pad - a.shape[2]), (0, 0)))

    qn_r = rows_major((q * scale).astype(BF16))
    qr_r = rows_major((q_rot * scale).astype(BF16))
    c0 = np.arange(n_seg)[:, None] * S_CMP
    s0 = np.arange(n_sp)[None, :] * L_SLC
    cover = np.maximum(np.minimum(c0 + L_CMP, s0 + L_SLC) - np.maximum(c0, s0), 0) / L_CMP
    cover = cover * (np.arange(n_seg)[:, None] < n_cmp) * (np.arange(n_sp)[None, :] < n_slc)
    kvc4 = kvc.reshape(2, bsz, KV_HEADS, n_seg, HEAD_DIM)
    spec_q = pl.BlockSpec((1, KV_HEADS, rows, HEAD_DIM), lambda b: (b, 0, 0, 0))
    o_c, sel_bias = pl.pallas_call(
        functools.partial(_sample_select_kernel, q_off=q_off, lq=lq, n_cmp=n_cmp, n_slc=n_slc),
        grid=(bsz,),
        in_specs=[spec_q,
                  pl.BlockSpec((1, KV_HEADS, n_seg, HEAD_DIM), lambda b: (b, 0, 0, 0)),
                  pl.BlockSpec((1, KV_HEADS, n_seg, HEAD_DIM), lambda b: (b, 0, 0, 0)),
                  pl.BlockSpec((n_seg, n_sp), lambda b: (0, 0))],
        out_specs=[spec_q, pl.BlockSpec((1, KV_HEADS, lq, n_sp), lambda b: (b, 0, 0, 0))],
        out_shape=[jax.ShapeDtypeStruct((bsz, KV_HEADS, rows, HEAD_DIM), F32),
                   jax.ShapeDtypeStruct((bsz, KV_HEADS, lq, n_sp), F32)],
        compiler_params=_cparams(("parallel",)),
        name="sample_select",
    )(qn_r, kvc4[0], kvc4[1], jnp.asarray(cover, BF16))

    sb_steps = sel_bias[..., :n_grp * blk_step].reshape(bsz, KV_HEADS, lq, n_grp, blk_step).transpose(0, 1, 3, 2, 4)
    expand = (np.arange(key_step)[None, :] // L_SLC == np.arange(blk_step)[:, None]).astype(np.float32)
    gates = jnp.pad(rows_major(gate_logits.reshape(bsz, lq, -1)), ((0, 0), (0, 0), (0, 0), (0, 128 - 3)))
    n_win = win_cat.shape[1]
    n_wp = -(-n_win // 128) * 128
    spec2 = lambda shape: pl.BlockSpec((1,) + shape, lambda b, j, pt: (b,) + (0,) * len(shape))
    attn = pl.pallas_call(
        functools.partial(_sample_attend_kernel, q_off=q_off, lq=lq, n_win=n_win),
        grid_spec=pltpu.PrefetchScalarGridSpec(
            num_scalar_prefetch=1,
            grid=(bsz, n_grp),
            in_specs=[pl.BlockSpec(memory_space=pl.ANY),
                      spec2((KV_HEADS, rows, HEAD_DIM)),
                      pl.BlockSpec((1, KV_HEADS, 1, lq, blk_step), lambda b, j, pt: (b, 0, j, 0, 0)),
                      pl.BlockSpec((blk_step, key_step), lambda b, j, pt: (0, 0)),
                      spec2((KV_HEADS, rows, HEAD_DIM)),
                      spec2((KV_HEADS, rows, 128)),
                      spec2((KV_HEADS, 128, HEAD_DIM)), spec2((KV_HEADS, 128, HEAD_DIM)),
                      spec2((KV_HEADS, n_wp, HEAD_DIM)), spec2((KV_HEADS, n_wp, HEAD_DIM))],
            out_specs=spec2((KV_HEADS, rows, HEAD_DIM)),
            scratch_shapes=[pltpu.VMEM((2, PAGES_PER_STEP, 2, KV_HEADS, HEAD_DIM, PAGE_SIZE), F32),
                            pltpu.SemaphoreType.DMA((2,)),
                            pltpu.VMEM((KV_HEADS, rows, 1), F32), pltpu.VMEM((KV_HEADS, rows, 1), F32),
                            pltpu.VMEM((KV_HEADS, rows, HEAD_DIM), F32)]),
        out_shape=jax.ShapeDtypeStruct((bsz, KV_HEADS, rows, HEAD_DIM), F32),
        compiler_params=_cparams(("arbitrary", "arbitrary")),
        name="sample_attend",
    )(page_table, pool_t, qr_r, sb_steps, jnp.asarray(expand, BF16), o_c, gates,
      keys_major(kv_new[:, :, 2], 128), keys_major(kv_new[:, :, 3], 128),
      keys_major(win_cat[:, :, 0], n_wp), keys_major(win_cat[:, :, 1], n_wp))
    attn = attn.reshape(bsz, KV_HEADS, HPG, lq, HEAD_DIM).transpose(0, 3, 1, 2, 4)
    return attn.astype(BF16).reshape(bsz * lq, ATT_W)


def _ssd_kernel(d_ref, x_ref, bt_ref, c_ref, acol_ref, arow_ref, dt_ref, s0_ref, y_ref, sout_ref, s_sc, *, cl):
    ci = pl.program_id(1)

    @pl.when(ci == 0)
    def _():
        s_sc[...] = s0_ref[0]

    li = lax.broadcasted_iota(jnp.int32, (cl, cl), 0)
    si = lax.broadcasted_iota(jnp.int32, (cl, cl), 1)
    causal = li >= si
    acum_col = _dot_hi(causal.astype(F32), acol_ref[0])
    acum_row = _dot_hi(arow_ref[0], (li <= si).astype(F32))
    dt = dt_ref[0]
    for g in range(SSM_GROUPS):
        cg = c_ref[0, g].astype(BF16)
        btg = bt_ref[0, g]
        cb = _dot(cg, btg.astype(BF16))
        for hh in range(SSM_HEADS // SSM_GROUPS):
            h = g * (SSM_HEADS // SSM_GROUPS) + hh
            ac = acum_col[:, h:h + 1]
            ar = acum_row[h:h + 1, :]
            decay = jnp.exp(jnp.where(causal, ac - ar, NEG))
            xs = x_ref[0, h]
            xd = (xs * dt[:, h:h + 1]).astype(BF16)
            st = s_sc[h]
            y = _dot((cb * decay).astype(BF16), xd) + _dot(cg, st.astype(BF16)) * jnp.exp(ac)
            y_ref[0, h] = y + d_ref[h] * xs
            a_last = ar[:, cl - 1:cl]
            snew = _dot((btg * jnp.exp(a_last - ar)).astype(BF16), xd)
            s_sc[h] = st * jnp.exp(a_last) + snew

    @pl.when(ci == pl.num_programs(1) - 1)
    def _():
        sout_ref[0] = s_sc[...]


def ssd_scan(ssm_d, x_hm, b_t, c, a_col, a_row, dt_col, s0_t, cl):
    bsz, nh, t, p = x_hm.shape
    n = D_STATE
    nc = t // cl
    kern = functools.partial(_ssd_kernel, cl=cl)
    return pl.pallas_call(
        kern,
        grid=(bsz, nc),
        in_specs=[pl.BlockSpec(memory_space=pltpu.SMEM),
                  pl.BlockSpec((1, nh, cl, p), lambda b, c_: (b, 0, c_, 0)),
                  pl.BlockSpec((1, SSM_GROUPS, n, cl), lambda b, c_: (b, 0, 0, c_)),
                  pl.BlockSpec((1, SSM_GROUPS, cl, n), lambda b, c_: (b, 0, c_, 0)),
                  pl.BlockSpec((1, cl, 128), lambda b, c_: (b, c_, 0)),
                  pl.BlockSpec((1, 8, cl), lambda b, c_: (b, 0, c_)),
                  pl.BlockSpec((1, cl, 128), lambda b, c_: (b, c_, 0)),
                  pl.BlockSpec((1, nh, n, p), lambda b, c_: (b, 0, 0, 0))],
        out_specs=[pl.BlockSpec((1, nh, cl, p), lambda b, c_: (b, 0, c_, 0)),
                   pl.BlockSpec((1, nh, n, p), lambda b, c_: (b, 0, 0, 0))],
        out_shape=[jax.ShapeDtypeStruct((bsz, nh, t, p), F32), jax.ShapeDtypeStruct((bsz, nh, n, p), F32)],
        scratch_shapes=[pltpu.VMEM((nh, n, p), F32)],
        compiler_params=_cparams(("parallel", "arbitrary")),
        name="ssd_scan",
    )(ssm_d, x_hm, b_t, c, a_col, a_row, dt_col, s0_t)


def _rmsnorm(x, g):
    return x * lax.rsqrt(jnp.mean(x * x, axis=-1, keepdims=True) + EPS) * g


def _rope(x, pos):
    half = HEAD_DIM // 2
    inv = ROPE_THETA ** (-jnp.arange(half, dtype=F32) / half)
    ang = pos.astype(F32)[:, None] * inv[None, :]
    cos, sin = jnp.cos(ang)[:, None, :], jnp.sin(ang)[:, None, :]
    x1, x2 = x[..., :half], x[..., half:]
    return jnp.concatenate([x1 * cos - x2 * sin, x2 * cos + x1 * sin], axis=-1)


def _ssd_inputs(xbc_all, dt_raw, w, t_pad):
    bsz = xbc_all.shape[0]
    length = xbc_all.shape[1] - (CONV_W - 1)
    conv = w['conv_b'] + xbc_all[:, 0:length] * w['conv_w'][0]
    for j in range(1, CONV_W):
        conv = conv + xbc_all[:, j:j + length] * w['conv_w'][j]
    act = jax.nn.silu(conv)
    dt = jax.nn.softplus(dt_raw + w['dt_bias'])
    a = -jnp.exp(w['a_log']) * dt
    pad = t_pad - length
    if pad:
        act = jnp.pad(act, ((0, 0), (0, pad), (0, 0)))
        dt = jnp.pad(dt, ((0, 0), (0, pad), (0, 0)))
        a = jnp.pad(a, ((0, 0), (0, pad), (0, 0)))
    xs = act[..., :D_INNER].reshape(bsz, t_pad, SSM_HEADS, SSM_HEAD_DIM).transpose(0, 2, 1, 3)
    b_in = act[..., D_INNER:D_INNER + SSM_GROUPS * D_STATE].reshape(bsz, t_pad, SSM_GROUPS, D_STATE)
    c_in = act[..., D_INNER + SSM_GROUPS * D_STATE:].reshape(bsz, t_pad, SSM_GROUPS, D_STATE)
    lane_pad = ((0, 0), (0, 0), (0, 128 - SSM_HEADS))
    return (xs, b_in.transpose(0, 2, 3, 1), c_in.transpose(0, 2, 1, 3), jnp.pad(a, lane_pad),
            a.transpose(0, 2, 1), jnp.pad(dt, lane_pad))


def _moe(f_b, logits, w):
    n_tok = f_b.shape[0]
    pg = jax.nn.softmax(logits[:, :N_EXPERT_GROUPS], axis=-1)
    g_sel = jnp.argmax(pg, axis=-1)
    g_w = jnp.max(pg, axis=-1)
    el = logits[:, N_EXPERT_GROUPS:N_EXPERT_GROUPS + N_EXPERTS].reshape(n_tok, N_EXPERT_GROUPS, EXPERTS_PER_GROUP)
    el = jnp.take_along_axis(el, g_sel[:, None, None], axis=1)[:, 0]
    top_p, top_i = lax.top_k(jax.nn.softmax(el, axis=-1), TOP_K)
    wts = g_w[:, None] * top_p / jnp.sum(top_p, axis=-1, keepdims=True)
    eid = (g_sel[:, None] * EXPERTS_PER_GROUP + top_i).reshape(-1).astype(jnp.int32)
    n_asg = eid.shape[0]
    onehot = (eid[:, None] == jnp.arange(N_EXPERTS, dtype=jnp.int32)[None, :]).astype(jnp.int32)
    within = jnp.take_along_axis(jnp.cumsum(onehot, axis=0), eid[:, None], axis=1)[:, 0] - 1
    sizes = jnp.sum(onehot, axis=0)
    padded = (sizes + MOE_ROWS - 1) // MOE_ROWS * MOE_ROWS
    pend = jnp.cumsum(padded)
    dest = (pend - padded)[eid] + within
    n_blk = -(-n_asg // MOE_ROWS) + N_EXPERTS
    src = jnp.zeros((n_blk * MOE_ROWS,), jnp.int32).at[dest].set(jnp.arange(n_asg, dtype=jnp.int32) // TOP_K)
    xpad = f_b[src]
    blk_start = jnp.arange(n_blk, dtype=jnp.int32) * MOE_ROWS
    blk_e = jnp.minimum(jnp.sum((pend[None, :] <= blk_start[:, None]).astype(jnp.int32), axis=1), N_EXPERTS - 1)
    n_used = (pend[-1] // MOE_ROWS).astype(jnp.int32).reshape(1)
    ypad = moe_experts(xpad, blk_e, n_used, w['w_gate'], w['w_up'], w['w_down'])
    y01 = ypad[dest.reshape(n_tok, TOP_K).T.reshape(-1)]
    return y01, jnp.pad(wts, ((0, 0), (0, 128 - TOP_K)))


def _token_tail(x2, mix_in, p2, w, tm):
    pad = 128 - N_EXPERT_GROUPS - N_EXPERTS
    w_router = jnp.concatenate([w['w_rg'], w['w_re'], jnp.zeros((D_MODEL, pad), F32)], axis=1)
    b_router = jnp.concatenate([w['b_rg'], w['b_re'], jnp.zeros((pad,), F32)]).reshape(1, 128)
    h, f_b, logits = outproj(x2, mix_in[0], mix_in[1], w['w_out_b'], w['ffn_norm'], w_router, b_router, tm)
    y01, wts = _moe(f_b, logits, w)
    return moe_combine_ple(h, y01, wts, p2, w['wpg_b'], w['wpp_b'], w['ple_norm'], tm)


def _ssd_finish(y_hm, z, w, length):
    bsz = y_hm.shape[0]
    y = y_hm[:, :, :length].transpose(0, 2, 1, 3).reshape(bsz, length, D_INNER)
    gated = y * jax.nn.silu(z)
    return _rmsnorm(gated, w['ssm_norm']).astype(BF16)


def _prompt_group(x, p, w):
    bsz, t, _ = x.shape
    m = bsz * t
    z, xbc, sm_t, kv, win, qn_t, qr_t, ks_aug, kw, vs_t, vw_t = rms_inproj_prompt(
        x.reshape(m, D_MODEL), w['attn_norm'], w['w_in_r'], w['q_norm'], w['k_norm'], bsz, t)
    nq = t // NSA_TQ
    kv_new = kv.reshape(bsz, t, 4, KV_HEADS, HEAD_DIM)
    win_new = win.reshape(bsz, t, 2, KV_HEADS, HEAD_DIM)

    n_seg = t // S_CMP
    segs = kv_new[:, :, 0:2].astype(BF16).transpose(2, 0, 3, 1, 4).reshape(2, bsz * KV_HEADS, n_seg, S_CMP * HEAD_DIM)
    kvc = compress(segs, w['cmp_pe'], w['cmp_w1'], w['cmp_w2'])
    attn = nsa_prompt(qn_t, qr_t, kvc[0], kvc[1].transpose(0, 2, 1), ks_aug, vs_t, kw, vw_t, sm_t, bsz, t)
    attn = attn.reshape(bsz, KV_HEADS, nq, HEAD_DIM, HPG, NSA_TQ).transpose(0, 2, 5, 1, 4, 3).reshape(m, ATT_W)

    xbc3 = xbc.reshape(bsz, t, CONV_DIM)
    xbc_all = jnp.concatenate([jnp.zeros((bsz, CONV_W - 1, CONV_DIM), F32), xbc3], axis=1)
    dt_raw = sm_t[:, :, N_GATES:N_GATES + SSM_HEADS].transpose(0, 1, 3, 2).reshape(bsz, t, SSM_HEADS)
    xs, b_t, c_in, a_col, a_row, dt_col = _ssd_inputs(xbc_all, dt_raw, w, t)
    s0 = jnp.zeros((bsz, SSM_HEADS, D_STATE, SSM_HEAD_DIM), F32)
    y_hm, s_t = ssd_scan(w['ssm_d'], xs, b_t, c_in, a_col, a_row, dt_col, s0, SSM_CHUNK)
    ssd = _ssd_finish(y_hm, z.reshape(bsz, t, D_INNER), w, t).reshape(m, D_INNER)

    y = _token_tail(x.reshape(m, D_MODEL), (attn, ssd), p.reshape(m, PLE_DIM), w, 512)
    keep = min(WINDOW, t)
    return (y.reshape(bsz, t, D_MODEL), kv_new, win_new[:, t - keep:], s_t.transpose(0, 1, 3, 2),
            xbc_all[:, xbc_all.shape[1] - (CONV_W - 1):])


def _sample_group(x, p, pool, page_table, cache_win, state_ssm, state_conv, w):
    bsz, lq, _ = x.shape
    m = bsz * lq
    q_off = page_table.shape[1] * PAGE_SIZE
    q, kvs, z, xbc, sm = rms_inproj(x.reshape(m, D_MODEL), w['attn_norm'], w['w_in_r'], m)
    pos = q_off + jnp.arange(lq)
    qn = _rmsnorm(q.reshape(bsz, lq, N_HEADS, HEAD_DIM), w['q_norm'])
    qr = _rope(qn, pos)
    kvs = kvs.reshape(bsz, lq, 6, KV_HEADS, HEAD_DIM)
    k_c = _rmsnorm(kvs[:, :, 0], w['k_norm'][0])
    k_s = _rope(_rmsnorm(kvs[:, :, 2], w['k_norm'][1]), pos)
    k_w = _rope(_rmsnorm(kvs[:, :, 4], w['k_norm'][2]), pos)
    kv_new = jnp.stack([k_c, kvs[:, :, 1], k_s, kvs[:, :, 3]], axis=2)
    win_new = jnp.stack([k_w, kvs[:, :, 5]], axis=2)
    win_cat = jnp.concatenate([cache_win, win_new], axis=1)

    pool_t = jnp.transpose(pool, (0, 2, 3, 4, 1))
    kvc = compress_paged(pool_t, page_table, w['cmp_pe'], w['cmp_w1'], w['cmp_w2'])
    attn = nsa_sample(qn, qr, kvc, kv_new, win_cat, pool_t, page_table, sm[:, :N_GATES], q_off)

    xbc_all = jnp.concatenate([state_conv, xbc.reshape(bsz, lq, CONV_DIM)], axis=1)
    dt_raw = sm[:, N_GATES:N_GATES + SSM_HEADS].reshape(bsz, lq, SSM_HEADS)
    xs, b_t, c_in, a_col, a_row, dt_col = _ssd_inputs(xbc_all, dt_raw, w, SSM_CHUNK)
    y_hm, s_t = ssd_scan(w['ssm_d'], xs, b_t, c_in, a_col, a_row, dt_col, state_ssm.transpose(0, 1, 3, 2), SSM_CHUNK)
    ssd = _ssd_finish(y_hm, z.reshape(bsz, lq, D_INNER), w, lq).reshape(m, D_INNER)

    y = _token_tail(x.reshape(m, D_MODEL), (attn, ssd), p.reshape(m, PLE_DIM), w, m)
    keep = cache_win.shape[1]
    return (y.reshape(bsz, lq, D_MODEL), kv_new, win_cat[:, win_cat.shape[1] - keep:], s_t.transpose(0, 1, 3, 2),
            xbc_all[:, xbc_all.shape[1] - (CONV_W - 1):])


def kernel(x_prompt, x_sample, cache_kv, cache_win, state_ssm, state_conv, page_table, p_prompt, p_sample,
           w_in, w_out, q_norm, k_norm, cmp_pe, cmp_w1, cmp_w2, conv_w, conv_b, dt_bias, a_log, ssm_d, ssm_norm,
           attn_norm, ffn_norm, w_rg, b_rg, w_re, b_re, w_gate, w_up, w_down, w_ple_proj, ple_norm, w_ple_gate):
    depth = w_in.shape[0]
    hp, hs = x_prompt, x_sample
    outs = [[] for _ in range(8)]
    cuts = np.cumsum((ATT_W, 6 * KV_HEADS * HEAD_DIM, N_GATES, D_INNER, CONV_DIM, SSM_HEADS))
    for l in range(depth):
        wi = w_in[l]
        w_in_r = jnp.concatenate(
            [wi[:, :cuts[1]], wi[:, cuts[2]:cuts[3]], wi[:, cuts[3]:cuts[4]], wi[:, cuts[1]:cuts[2]],
             wi[:, cuts[4]:cuts[5]], jnp.zeros((D_MODEL, C_SM - N_GATES - SSM_HEADS), F32)], axis=1).astype(BF16)
        w = dict(w_in_r=w_in_r, w_out_b=w_out[l].astype(BF16), q_norm=q_norm[l], k_norm=k_norm[l],
                 cmp_pe=cmp_pe[l], cmp_w1=cmp_w1[l], cmp_w2=cmp_w2[l], conv_w=conv_w[l], conv_b=conv_b[l],
                 dt_bias=dt_bias[l], a_log=a_log[l], ssm_d=ssm_d[l], ssm_norm=ssm_norm[l], attn_norm=attn_norm[l],
                 ffn_norm=ffn_norm[l], w_rg=w_rg[l], b_rg=b_rg[l], w_re=w_re[l], b_re=b_re[l],
                 w_gate=w_gate[l], w_up=w_up[l], w_down=w_down[l],
                 wpp_b=w_ple_proj[l].astype(BF16), ple_norm=ple_norm[l], wpg_b=w_ple_gate[l].astype(BF16))
        hp, *rest_p = _prompt_group(hp, p_prompt[l], w)
        hs, *rest_s = _sample_group(hs, p_sample[l], cache_kv[l], page_table, cache_win[l], state_ssm[l],
                                    state_conv[l], w)
        for j in range(4):
            outs[2 * j].append(rest_p[j])
            outs[2 * j + 1].append(rest_s[j])
    return (hp, hs) + tuple(jnp.stack(o) for o in outs)
```

```python
import functools
import math

import numpy as np
import jax
import jax.numpy as jnp
from jax import lax
from jax.experimental import pallas as pl
from jax.experimental.pallas import tpu as pltpu

F32 = jnp.float32
BF16 = jnp.bfloat16

D_MODEL = 1024
PAGE_SIZE = 128
N_HEADS = 8
HEAD_DIM = 64
KV_HEADS = 2
HPG = N_HEADS // KV_HEADS
ATT_W = N_HEADS * HEAD_DIM
L_CMP = 32
S_CMP = 16
L_SLC = 64
N_SEL = 16
WINDOW = 512
CMP_HID = 64
ROPE_THETA = 10000.0
SSM_HEADS = 8
SSM_HEAD_DIM = 64
D_INNER = SSM_HEADS * SSM_HEAD_DIM
SSM_GROUPS = 2
D_STATE = 64
CONV_W = 4
CONV_DIM = D_INNER + 2 * SSM_GROUPS * D_STATE
SSM_CHUNK = 128
N_EXPERT_GROUPS = 4
EXPERTS_PER_GROUP = 8
N_EXPERTS = N_EXPERT_GROUPS * EXPERTS_PER_GROUP
TOP_K = 2
D_EXPERT = 512
PLE_DIM = 256
EPS = 1e-6
N_GATES = 3 * N_HEADS
C_Q, C_KV, C_Z, C_XBC, C_SM = 512, 768, 512, 768, 128
D_IN_PAD = C_Q + C_KV + C_Z + C_XBC + C_SM

NEG = -1e30
VMEM_LIMIT = 48 * 1024 * 1024
MOE_ROWS = 256


def _cparams(sem):
    return pltpu.CompilerParams(dimension_semantics=sem, vmem_limit_bytes=VMEM_LIMIT)


def _dot(a, b):
    return jnp.dot(a, b, preferred_element_type=F32)


def _dot_nt(a, b):
    return lax.dot_general(a, b, (((1,), (1,)), ((), ())), preferred_element_type=F32)


def _dot_hi(a, b):
    return jnp.dot(a, b, preferred_element_type=F32, precision=lax.Precision.HIGHEST)


def _rms_inproj_kernel(x_ref, g_ref, w_ref, q_ref, kv_ref, z_ref, xbc_ref, sm_ref):
    x = x_ref[...]
    y = x * lax.rsqrt(jnp.mean(x * x, axis=-1, keepdims=True) + EPS) * g_ref[...]
    yb = y.astype(BF16)
    c0 = 0
    for ref, width in ((q_ref, C_Q), (kv_ref, C_KV), (z_ref, C_Z), (xbc_ref, C_XBC), (sm_ref, C_SM)):
        ref[...] = _dot(yb, w_ref[:, c0:c0 + width])
        c0 += width


def rms_inproj(x, gain, w_r, tm):
    m = x.shape[0]
    widths = (C_Q, C_KV, C_Z, C_XBC, C_SM)
    return pl.pallas_call(
        _rms_inproj_kernel,
        grid=(m // tm,),
        in_specs=[pl.BlockSpec((tm, D_MODEL), lambda i: (i, 0)),
                  pl.BlockSpec((1, D_MODEL), lambda i: (0, 0)),
                  pl.BlockSpec((D_MODEL, D_IN_PAD), lambda i: (0, 0))],
        out_specs=[pl.BlockSpec((tm, w), lambda i: (i, 0)) for w in widths],
        out_shape=[jax.ShapeDtypeStruct((m, w), F32) for w in widths],
        compiler_params=_cparams(("parallel",)),
        name="rms_inproj",
    )(x, gain.reshape(1, D_MODEL), w_r)


def _inproj_prompt_kernel(x_ref, g_ref, w_ref, cos_ref, sin_ref, gq_ref, gk_ref, z_ref, xbc_ref, sm_ref, smt_ref, kv_ref,
                          win_ref, qn_ref, qr_ref, ksa_ref, kw_ref, vst_ref, vwt_ref, *, tiles_per_seq):
    tm = x_ref.shape[0]
    x = x_ref[...]
    yb = (x * lax.rsqrt(jnp.mean(x * x, axis=-1, keepdims=True) + EPS) * g_ref[...]).astype(BF16)
    cos, sin = cos_ref[...], sin_ref[...]
    scale = HEAD_DIM ** -0.5
    half = HEAD_DIM // 2

    def proj(c0, width=128):
        return _dot(yb, w_ref[:, c0:c0 + width])

    def head_norm(x_t, gain):
        return x_t * lax.rsqrt(jnp.mean(x_t * x_t, axis=0, keepdims=True) + EPS) * gain

    def rotate(y_t):
        y1, y2 = y_t[0:half], y_t[half:HEAD_DIM]
        return jnp.concatenate([y1 * cos - y2 * sin, y2 * cos + y1 * sin], axis=0)

    def per_group(blk, fn):
        t = blk.T
        return jnp.concatenate([fn(t[g * HEAD_DIM:(g + 1) * HEAD_DIM]) for g in range(KV_HEADS)], axis=0).T

    for j in range(ATT_W // 128):
        q_t = proj(128 * j).T
        for h2 in range(128 // HEAD_DIM):
            head = (128 // HEAD_DIM) * j + h2
            g, hh = head // HPG, head % HPG
            qn = head_norm(q_t[h2 * HEAD_DIM:(h2 + 1) * HEAD_DIM], gq_ref[...])
            qn_ref[0, g, 0, :, hh * tm:(hh + 1) * tm] = (qn * scale).astype(BF16)
            qr_ref[0, g, 0, :, hh * tm:(hh + 1) * tm] = (rotate(qn) * scale).astype(BF16)

    c_kv = C_Q
    k_c = per_group(proj(c_kv), lambda t: head_norm(t, gk_ref[0]))
    v_c = proj(c_kv + 128)
    k_s = per_group(proj(c_kv + 256), lambda t: rotate(head_norm(t, gk_ref[1])))
    v_s = proj(c_kv + 384)
    k_w = per_group(proj(c_kv + 512), lambda t: rotate(head_norm(t, gk_ref[2])))
    v_w = proj(c_kv + 640)
    kv_ref[:, 0:128] = k_c
    kv_ref[:, 128:256] = v_c
    kv_ref[:, 256:384] = k_s
    kv_ref[:, 384:512] = v_s
    win_ref[:, 0:128] = k_w
    win_ref[:, 128:256] = v_w
    kw_ref[...] = k_w.astype(BF16)
    pos0 = (pl.program_id(0) % tiles_per_seq) * tm
    lane = lax.broadcasted_iota(jnp.int32, (tm, 128), 1)
    blk_of_row = (pos0 + lax.broadcasted_iota(jnp.int32, (tm, 128), 0)) // L_SLC
    onehot = (blk_of_row == (lane & (HEAD_DIM - 1))).astype(BF16)
    k_sb = k_s.astype(BF16)
    ksa_ref[0, 0] = jnp.where(lane < HEAD_DIM, k_sb, onehot)
    ksa_ref[0, 1] = jnp.where(lane >= HEAD_DIM, k_sb, onehot)
    vs_t = v_s.T.astype(BF16)
    vw_t = v_w.T.astype(BF16)
    for g in range(KV_HEADS):
        vst_ref[0, g, 0] = vs_t[g * HEAD_DIM:(g + 1) * HEAD_DIM]
        vwt_ref[0, g, 0] = vw_t[g * HEAD_DIM:(g + 1) * HEAD_DIM]
    z_ref[...] = proj(C_Q + C_KV, C_Z)
    xbc_ref[...] = proj(C_Q + C_KV + C_Z, C_XBC)
    sm = proj(C_Q + C_KV + C_Z + C_XBC)
    sm_ref[...] = sm
    smt_ref[0, 0] = sm.T[0:32]


def rms_inproj_prompt(x, gain, w_r, q_gain, k_gain, bsz, t):
    tm = NSA_TQ
    assert NSA_TK == tm and NSA_TKW == tm and t // L_SLC <= HEAD_DIM
    m = bsz * t
    nt = t // tm
    inv = ROPE_THETA ** (-jnp.arange(HEAD_DIM // 2, dtype=F32) / (HEAD_DIM // 2))
    ang = inv[:, None] * jnp.arange(t, dtype=F32)[None, :]
    row = lambda width: pl.BlockSpec((tm, width), lambda i: (i, 0))
    qspec = pl.BlockSpec((1, KV_HEADS, 1, HEAD_DIM, HPG * tm), lambda i: (i // nt, 0, i % nt, 0, 0))
    vspec = pl.BlockSpec((1, KV_HEADS, 1, HEAD_DIM, tm), lambda i: (i // nt, 0, i % nt, 0, 0))
    tspec = pl.BlockSpec((HEAD_DIM // 2, tm), lambda i: (0, i % nt))
    qshape = jax.ShapeDtypeStruct((bsz, KV_HEADS, nt, HEAD_DIM, HPG * tm), BF16)
    vshape = jax.ShapeDtypeStruct((bsz, KV_HEADS, nt, HEAD_DIM, tm), BF16)
    return pl.pallas_call(
        functools.partial(_inproj_prompt_kernel, tiles_per_seq=nt),
        grid=(m // tm,),
        in_specs=[row(D_MODEL),
                  pl.BlockSpec((1, D_MODEL), lambda i: (0, 0)),
                  pl.BlockSpec((D_MODEL, D_IN_PAD), lambda i: (0, 0)),
                  tspec, tspec,
                  pl.BlockSpec((HEAD_DIM, 1), lambda i: (0, 0)),
                  pl.BlockSpec((3, HEAD_DIM, 1), lambda i: (0, 0, 0))],
        out_specs=[row(C_Z), row(C_XBC), row(C_SM),
                   pl.BlockSpec((1, 1, 32, tm), lambda i: (i // nt, i % nt, 0, 0)),
                   row(512), row(256), qspec, qspec,
                   pl.BlockSpec((1, KV_HEADS, tm, 128), lambda i: (i // nt, 0, i % nt, 0)),
                   row(128), vspec, vspec],
        out_shape=[jax.ShapeDtypeStruct((m, C_Z), F32), jax.ShapeDtypeStruct((m, C_XBC), F32),
                   jax.ShapeDtypeStruct((m, C_SM), F32), jax.ShapeDtypeStruct((bsz, nt, 32, tm), F32),
                   jax.ShapeDtypeStruct((m, 512), F32), jax.ShapeDtypeStruct((m, 256), F32), qshape, qshape,
                   jax.ShapeDtypeStruct((bsz, KV_HEADS, t, 128), BF16),
                   jax.ShapeDtypeStruct((m, 128), BF16), vshape, vshape],
        compiler_params=_cparams(("parallel",)),
        name="rms_inproj_prompt",
    )(x, gain.reshape(1, D_MODEL), w_r, jnp.cos(ang), jnp.sin(ang), q_gain.reshape(HEAD_DIM, 1),
      k_gain.reshape(3, HEAD_DIM, 1))


def _outproj_kernel(x_ref, a_ref, s_ref, w_ref, g_ref, wr_ref, br_ref, h_ref, f_ref, lg_ref):
    acc = _dot(a_ref[...], w_ref[0:ATT_W, :]) + _dot(s_ref[...], w_ref[ATT_W:ATT_W + D_INNER, :])
    h = x_ref[...] + acc
    h_ref[...] = h
    f = h * lax.rsqrt(jnp.mean(h * h, axis=-1, keepdims=True) + EPS) * g_ref[...]
    f_ref[...] = f.astype(BF16)
    lg_ref[...] = _dot_hi(f, wr_ref[...]) + br_ref[...]


def outproj(x, attn, ssd, w_out_b, ffn_gain, w_router, b_router, tm):
    m = x.shape[0]
    row = lambda width: pl.BlockSpec((tm, width), lambda i: (i, 0))
    fixed = lambda r, c: pl.BlockSpec((r, c), lambda i: (0, 0))
    return pl.pallas_call(
        _outproj_kernel,
        grid=(m // tm,),
        in_specs=[row(D_MODEL), row(ATT_W), row(D_INNER), fixed(ATT_W + D_INNER, D_MODEL),
                  fixed(1, D_MODEL), fixed(D_MODEL, 128), fixed(1, 128)],
        out_specs=[row(D_MODEL), row(D_MODEL), row(128)],
        out_shape=[jax.ShapeDtypeStruct((m, D_MODEL), F32), jax.ShapeDtypeStruct((m, D_MODEL), BF16),
                   jax.ShapeDtypeStruct((m, 128), F32)],
        compiler_params=_cparams(("parallel",)),
        name="outproj",
    )(x, attn, ssd, w_out_b, ffn_gain.reshape(1, D_MODEL), w_router, b_router)


def _ple_kernel(h_ref, y0_ref, y1_ref, wt_ref, p_ref, wg_ref, wp_ref, g_ref, o_ref):
    wt = wt_ref[...]
    h = h_ref[...] + (y0_ref[...] * wt[:, 0:1] + y1_ref[...] * wt[:, 1:2])
    gate = jax.nn.sigmoid(_dot(h.astype(BF16), wg_ref[...]))
    e = _dot(p_ref[...].astype(BF16), wp_ref[...])
    e = e * lax.rsqrt(jnp.mean(e * e, axis=-1, keepdims=True) + EPS) * g_ref[...]
    o_ref[...] = h + gate * e


def moe_combine_ple(h, y01, wts, p, wg_b, wp_b, gain, tm):
    m = h.shape[0]
    nt = m // tm
    return pl.pallas_call(
        _ple_kernel,
        grid=(nt,),
        in_specs=[pl.BlockSpec((tm, D_MODEL), lambda i: (i, 0)),
                  pl.BlockSpec((tm, D_MODEL), lambda i: (i, 0)),
                  pl.BlockSpec((tm, D_MODEL), lambda i: (i + nt, 0)),
                  pl.BlockSpec((tm, 128), lambda i: (i, 0)),
                  pl.BlockSpec((tm, PLE_DIM), lambda i: (i, 0)),
                  pl.BlockSpec((D_MODEL, D_MODEL), lambda i: (0, 0)),
                  pl.BlockSpec((PLE_DIM, D_MODEL), lambda i: (0, 0)),
                  pl.BlockSpec((1, D_MODEL), lambda i: (0, 0))],
        out_specs=pl.BlockSpec((tm, D_MODEL), lambda i: (i, 0)),
        out_shape=jax.ShapeDtypeStruct((m, D_MODEL), F32),
        compiler_params=_cparams(("parallel",)),
        name="moe_combine_ple",
    )(h, y01, y01, wts, p, wg_b, wp_b, gain.reshape(1, D_MODEL))


def _moe_kernel(be_ref, nb_ref, x_ref, wg_ref, wu_ref, wd_ref, y_ref, wg_sc, wu_sc, wd_sc):
    i = pl.program_id(0)

    @pl.when((i == 0) | (be_ref[i] != be_ref[jnp.maximum(i - 1, 0)]))
    def _():
        wg_sc[...] = wg_ref[0].astype(BF16)
        wu_sc[...] = wu_ref[0].astype(BF16)
        wd_sc[...] = wd_ref[0].astype(BF16)

    @pl.when(i < nb_ref[0])
    def _():
        x = x_ref[...]
        a = _dot(x, wg_sc[...])
        hb = (a * jax.nn.sigmoid(a)) * _dot(x, wu_sc[...])
        y_ref[...] = _dot(hb.astype(BF16), wd_sc[...])

    @pl.when(i >= nb_ref[0])
    def _():
        y_ref[...] = jnp.zeros(y_ref.shape, F32)


def moe_experts(xpad, blk_e, n_used, w_gate, w_up, w_down):
    n_blk = xpad.shape[0] // MOE_ROWS
    grid_spec = pltpu.PrefetchScalarGridSpec(
        num_scalar_prefetch=2,
        grid=(n_blk,),
        in_specs=[pl.BlockSpec((MOE_ROWS, D_MODEL), lambda i, be, nb: (i, 0)),
                  pl.BlockSpec((1, D_MODEL, D_EXPERT), lambda i, be, nb: (be[i], 0, 0)),
                  pl.BlockSpec((1, D_MODEL, D_EXPERT), lambda i, be, nb: (be[i], 0, 0)),
                  pl.BlockSpec((1, D_EXPERT, D_MODEL), lambda i, be, nb: (be[i], 0, 0))],
        out_specs=pl.BlockSpec((MOE_ROWS, D_MODEL), lambda i, be, nb: (i, 0)),
        scratch_shapes=[pltpu.VMEM((D_MODEL, D_EXPERT), BF16), pltpu.VMEM((D_MODEL, D_EXPERT), BF16),
                        pltpu.VMEM((D_EXPERT, D_MODEL), BF16)],
    )
    return pl.pallas_call(
        _moe_kernel,
        grid_spec=grid_spec,
        out_shape=jax.ShapeDtypeStruct((n_blk * MOE_ROWS, D_MODEL), F32),
        compiler_params=_cparams(("arbitrary",)),
        name="moe_experts",
    )(blk_e, n_used, xpad, w_gate, w_up, w_down)


def _compress_kernel(seg_ref, w1a_ref, w1b_ref, pe_ref, w1_ref, w2_ref, o_ref):
    seg = seg_ref[0, 0]
    n_seg = seg.shape[0]
    hid0 = _dot(pe_ref[0], w1_ref[0])[0:1]
    p0 = _dot(seg, w1a_ref[0])
    p1 = pltpu.roll(_dot(seg, w1b_ref[0]), n_seg - 1, axis=0)
    hid = hid0 + p0 + p1
    act = hid * jax.nn.sigmoid(hid)
    o_ref[0, 0] = _dot(act.astype(BF16), w2_ref[0]).astype(BF16)


def compress(segs, cmp_pe, cmp_w1, cmp_w2):
    _, r, n_seg, k = segs.shape
    w1 = cmp_w1.astype(BF16)
    pe = jnp.broadcast_to(cmp_pe.reshape(2, 1, L_CMP * HEAD_DIM), (2, 8, L_CMP * HEAD_DIM)).astype(BF16)
    return pl.pallas_call(
        _compress_kernel,
        grid=(2, r),
        in_specs=[pl.BlockSpec((1, 1, n_seg, k), lambda s, i: (s, i, 0, 0)),
                  pl.BlockSpec((1, k, CMP_HID), lambda s, i: (s, 0, 0)),
                  pl.BlockSpec((1, k, CMP_HID), lambda s, i: (s, 1, 0)),
                  pl.BlockSpec((1, 8, 2 * k), lambda s, i: (s, 0, 0)),
                  pl.BlockSpec((1, 2 * k, CMP_HID), lambda s, i: (s, 0, 0)),
                  pl.BlockSpec((1, CMP_HID, HEAD_DIM), lambda s, i: (s, 0, 0))],
        out_specs=pl.BlockSpec((1, 1, n_seg, HEAD_DIM), lambda s, i: (s, i, 0, 0)),
        out_shape=jax.ShapeDtypeStruct((2, r, n_seg, HEAD_DIM), BF16),
        compiler_params=_cparams(("parallel", "parallel")),
        name="compress",
    )(segs, w1, w1, pe, w1, cmp_w2.astype(BF16))


PAGES_PER_STEP = 16


def _page_copies(pt_ref, pool_ref, buf, sem, b, j, slot):
    return [pltpu.make_async_copy(pool_ref.at[pt_ref[b, j * PAGES_PER_STEP + p], pl.ds(0, 2)],
                                  buf.at[slot, p], sem.at[slot]) for p in range(PAGES_PER_STEP)]


def _paged_partials_kernel(pt_ref, pool_ref, wa_ref, wb_ref, p0_ref, p1_ref, buf, sem, x_sc):
    b = pl.program_id(0)
    j = pl.program_id(1)
    n_grp = pl.num_programs(1)
    step = b * n_grp + j
    slot = step % 2

    @pl.when(step == 0)
    def _():
        for c in _page_copies(pt_ref, pool_ref, buf, sem, b, j, slot):
            c.start()

    @pl.when(step + 1 < pl.num_programs(0) * n_grp)
    def _():
        wrap = j + 1 == n_grp
        for c in _page_copies(pt_ref, pool_ref, buf, sem, jnp.where(wrap, b + 1, b), jnp.where(wrap, 0, j + 1),
                              1 - slot):
            c.start()

    for c in _page_copies(pt_ref, pool_ref, buf, sem, b, j, slot):
        c.wait()

    n_seg = PAGES_PER_STEP * PAGE_SIZE // S_CMP
    for kv in range(2):
        for g in range(KV_HEADS):
            for p in range(PAGES_PER_STEP):
                x_sc[p * PAGE_SIZE:(p + 1) * PAGE_SIZE, :] = buf[slot, p, kv, g].T
            acc0 = jnp.zeros((n_seg, CMP_HID), F32)
            acc1 = jnp.zeros((n_seg, CMP_HID), F32)
            for s in range(S_CMP):
                xs = x_sc[pl.ds(s, n_seg, stride=S_CMP), :].astype(BF16)
                acc0 = acc0 + _dot(xs, wa_ref[kv, s])
                acc1 = acc1 + _dot(xs, wb_ref[kv, s])
            p0_ref[kv, 0, g] = acc0
            p1_ref[kv, 0, g] = acc1


def _compress_finish_kernel(p0_ref, p1_ref, pe_ref, w1_ref, w2_ref, o_ref):
    n_seg = p0_ref.shape[3]
    hid0 = _dot(pe_ref[0], w1_ref[0])[0:1]
    hid = hid0 + p0_ref[0, 0, 0] + pltpu.roll(p1_ref[0, 0, 0], n_seg - 1, axis=0)
    act = hid * jax.nn.sigmoid(hid)
    o_ref[0, 0] = _dot(act.astype(BF16), w2_ref[0]).astype(BF16)


def compress_paged(pool_t, page_table, cmp_pe, cmp_w1, cmp_w2):
    bsz, n_pages = page_table.shape
    assert n_pages % PAGES_PER_STEP == 0
    n_grp = n_pages // PAGES_PER_STEP
    seg_step = PAGES_PER_STEP * PAGE_SIZE // S_CMP
    n_seg = n_grp * seg_step
    w1 = cmp_w1.astype(BF16).reshape(2, L_CMP // S_CMP, S_CMP, HEAD_DIM, CMP_HID)
    part_shape = jax.ShapeDtypeStruct((2, bsz, KV_HEADS, n_seg, CMP_HID), F32)
    part_spec = pl.BlockSpec((2, 1, KV_HEADS, seg_step, CMP_HID), lambda b, j, pt: (0, b, 0, j, 0))
    wspec = pl.BlockSpec((2, S_CMP, HEAD_DIM, CMP_HID), lambda b, j, pt: (0, 0, 0, 0))
    p0, p1 = pl.pallas_call(
        _paged_partials_kernel,
        grid_spec=pltpu.PrefetchScalarGridSpec(
            num_scalar_prefetch=1,
            grid=(bsz, n_grp),
            in_specs=[pl.BlockSpec(memory_space=pl.ANY), wspec, wspec],
            out_specs=[part_spec, part_spec],
            scratch_shapes=[pltpu.VMEM((2, PAGES_PER_STEP, 2, KV_HEADS, HEAD_DIM, PAGE_SIZE), F32),
                            pltpu.SemaphoreType.DMA((2,)),
                            pltpu.VMEM((PAGES_PER_STEP * PAGE_SIZE, HEAD_DIM), F32)]),
        out_shape=[part_shape, part_shape],
        compiler_params=_cparams(("arbitrary", "arbitrary")),
        name="paged_partials",
    )(page_table, pool_t, w1[:, 0], w1[:, 1])
    pe = jnp.broadcast_to(cmp_pe.reshape(2, 1, L_CMP * HEAD_DIM), (2, 8, L_CMP * HEAD_DIM)).astype(BF16)
    pspec = pl.BlockSpec((1, 1, 1, n_seg, CMP_HID), lambda s, i: (s, i // KV_HEADS, i % KV_HEADS, 0, 0))
    return pl.pallas_call(
        _compress_finish_kernel,
        grid=(2, bsz * KV_HEADS),
        in_specs=[pspec, pspec,
                  pl.BlockSpec((1, 8, L_CMP * HEAD_DIM), lambda s, i: (s, 0, 0)),
                  pl.BlockSpec((1, L_CMP * HEAD_DIM, CMP_HID), lambda s, i: (s, 0, 0)),
                  pl.BlockSpec((1, CMP_HID, HEAD_DIM), lambda s, i: (s, 0, 0))],
        out_specs=pl.BlockSpec((1, 1, n_seg, HEAD_DIM), lambda s, i: (s, i, 0, 0)),
        out_shape=jax.ShapeDtypeStruct((2, bsz * KV_HEADS, n_seg, HEAD_DIM), BF16),
        compiler_params=_cparams(("parallel", "parallel")),
        name="compress_finish",
    )(p0, p1, pe, cmp_w1.astype(BF16), cmp_w2.astype(BF16))


FLASH_SPLIT = 4


def _flash_scores(k, q_ref, s_ref):
    cw = q_ref.shape[1] // FLASH_SPLIT
    for i in range(FLASH_SPLIT):
        c = slice(i * cw, (i + 1) * cw)
        s_ref[:, c] = _dot(k, q_ref[:, c])


def _flash_step(s_ref, v_t, m_sc, l_sc, acc_sc, keep=None):
    cw = s_ref.shape[1] // FLASH_SPLIT
    for i in range(FLASH_SPLIT):
        c = slice(i * cw, (i + 1) * cw)
        s_t = s_ref[:, c]
        if keep is not None:
            s_t = jnp.where(keep(i * cw, cw), s_t, NEG)
        m_prev = m_sc[:, c]
        m_new = jnp.maximum(m_prev, jnp.max(s_t, axis=0, keepdims=True))
        alpha = jnp.exp(m_prev - m_new)
        p = jnp.exp(s_t - m_new)
        l_sc[:, c] = alpha * l_sc[:, c] + jnp.sum(p, axis=0, keepdims=True)
        acc_sc[:, c] = alpha * acc_sc[:, c] + _dot(v_t, p.astype(BF16))
        m_sc[:, c] = m_new


def _flash_reset(m_sc, l_sc, acc_sc):
    m_sc[...] = jnp.full(m_sc.shape, NEG, F32)
    l_sc[...] = jnp.zeros(l_sc.shape, F32)
    acc_sc[...] = jnp.zeros(acc_sc.shape, F32)


def _nsa_prompt_kernel(qn_ref, qr_ref, kc_ref, vct_ref, ks_ref, vst_ref, kw_ref, vwt_ref, covert_ref, gate_ref,
                       o_ref, qa_sc, qw_sc, m_sc, l_sc, acc_sc, out_sc, score_sc, rank_sc, s_sc, sw_sc, *, tq, tk, tkw):
    qi = pl.program_id(2)
    pos0 = qi * tq
    w = HPG * tq

    def positions(rows, base):
        kpos = base + lax.broadcasted_iota(jnp.int32, (rows, w), 0)
        qpos = pos0 + (lax.broadcasted_iota(jnp.int32, (rows, w), 1) & (tq - 1))
        return kpos, qpos

    g = pl.program_id(1)

    def gate_row(branch):
        rows = [gate_ref[0, 0, pl.ds(3 * (g * HPG + hh) + branch, 1), :] for hh in range(HPG)]
        return jax.nn.sigmoid(jnp.concatenate(rows, axis=1))

    gate = [gate_row(branch) for branch in range(3)]

    kc = kc_ref[0, 0]
    n_c = kc.shape[0]
    cidx, qpos_c = positions(n_c, 0)
    valid = cidx * S_CMP + (L_CMP - 1) <= qpos_c
    s1 = jnp.where(valid, _dot(kc, qn_ref[0, 0, 0]), NEG)
    e1 = jnp.exp(s1 - jnp.max(s1, axis=0, keepdims=True)) * valid.astype(F32)
    p1b = (e1 * (1.0 / jnp.maximum(jnp.sum(e1, axis=0, keepdims=True), 1e-30))).astype(BF16)
    out_sc[...] = gate[0] * _dot(vct_ref[0, 0], p1b)

    p1_stack = jnp.concatenate([p1b[:, hh * tq:(hh + 1) * tq] for hh in range(HPG)], axis=0)
    imp = _dot(covert_ref[...], p1_stack)
    n_s = imp.shape[0]
    blk = lax.broadcasted_iota(jnp.int32, (n_s, tq), 0)
    cur = (pos0 + lax.broadcasted_iota(jnp.int32, (n_s, tq), 1)) // L_SLC
    vis = blk <= cur
    forced = vis & ((blk == 0) | (blk >= cur - 1))
    score_sc[...] = jnp.where(forced, 1e9, jnp.where(vis, imp, -1.0))
    rank_sc[...] = jnp.zeros(rank_sc.shape, F32)
    last_blk = (pos0 + tq - 1) // L_SLC
    n_oct = n_s // 8
    for oi in range(n_oct):
        @pl.when(oi * 8 <= last_blk)
        def _():
            rows = [slice(8 * oj, 8 * oj + 8) for oj in range(n_oct)]
            parts = [rank_sc[r, :] for r in rows]
            for i in range(8 * oi, 8 * oi + 8):
                c = score_sc[i:i + 1, :]
                for oj, r in enumerate(rows):
                    sj = score_sc[r, :]
                    if oj > oi:
                        beats = c >= sj
                    elif oj < oi:
                        beats = c > sj
                    else:
                        above = lax.broadcasted_iota(jnp.int32, (8, tq), 0) > i - 8 * oi
                        beats = (c > sj) | ((c == sj) & above)
                    parts[oj] = parts[oj] + beats.astype(F32)
            for r, part in zip(rows, parts):
                rank_sc[r, :] = part
    sel_bias = jnp.where(vis & (rank_sc[...] < N_SEL), 0.0, NEG).astype(BF16)
    own = pl.ds(pl.multiple_of(g * HEAD_DIM, HEAD_DIM), HEAD_DIM)
    other = pl.ds(pl.multiple_of((1 - g) * HEAD_DIM, HEAD_DIM), HEAD_DIM)
    qa_sc[other, :] = jnp.zeros((HEAD_DIM, w), BF16)
    qw_sc[other, :] = jnp.zeros((HEAD_DIM, w), BF16)
    qa_sc[own, :] = qr_ref[0, 0, 0]
    qw_sc[own, :] = qr_ref[0, 0, 0]
    qa_sc[pl.ds(pl.multiple_of((1 - g) * HEAD_DIM, HEAD_DIM), n_s), :] = jnp.concatenate([sel_bias] * HPG, axis=1)

    def sel_keys(kt):
        return ks_ref[0, 0, pl.ds(pl.multiple_of(kt * tk, tk), tk), :]

    def win_keys(kt):
        return kw_ref[pl.ds(pl.multiple_of(kt * tkw, tkw), tkw), :]

    def chunk_positions(rows, base, lane0, n_lanes):
        kpos = base + lax.broadcasted_iota(jnp.int32, (rows, n_lanes), 0)
        qpos = pos0 + ((lane0 + lax.broadcasted_iota(jnp.int32, (rows, n_lanes), 1)) & (tq - 1))
        return kpos, qpos

    def causal(rows, base):
        def keep(lane0, n_lanes):
            kpos, qpos = chunk_positions(rows, base, lane0, n_lanes)
            return kpos <= qpos
        return keep

    _flash_reset(m_sc, l_sc, acc_sc)
    n_full = pos0 // tk
    _flash_scores(sel_keys(0), qa_sc, s_sc.at[0])

    def sel_body(kt, carry):
        for parity in range(2):
            @pl.when(kt % 2 == parity)
            def _():
                _flash_scores(sel_keys(kt + 1), qa_sc, s_sc.at[1 - parity])
                _flash_step(s_sc.at[parity], vst_ref[0, 0, kt], m_sc, l_sc, acc_sc)
        return carry

    lax.fori_loop(0, n_full, sel_body, 0)

    n_inner = WINDOW // tkw - 1
    win_tiles = [qi - d for d in range(n_inner + 2)]
    for idx, kt in enumerate(win_tiles):
        _flash_scores(win_keys(jnp.maximum(kt, 0)), qw_sc, sw_sc.at[idx])

    _flash_step(s_sc.at[n_full % 2], vst_ref[0, 0, n_full], m_sc, l_sc, acc_sc, causal(tk, n_full * tk))
    out_sc[...] = out_sc[...] + (gate[1] * (1.0 / l_sc[...])) * acc_sc[...]

    _flash_reset(m_sc, l_sc, acc_sc)
    _flash_step(sw_sc.at[0], vwt_ref[0, 0, qi], m_sc, l_sc, acc_sc, causal(tkw, pos0))
    for idx in range(1, n_inner + 1):
        kt = win_tiles[idx]
        _flash_step(sw_sc.at[idx], vwt_ref[0, 0, jnp.maximum(kt, 0)], m_sc, l_sc, acc_sc,
                    lambda lane0, n_lanes, kt=kt: kt >= 0)
    kt_far = win_tiles[n_inner + 1]

    def far_keep(lane0, n_lanes):
        kpos, qpos = chunk_positions(tkw, kt_far * tkw, lane0, n_lanes)
        return (kpos > qpos - WINDOW) & (kt_far >= 0)

    _flash_step(sw_sc.at[n_inner + 1], vwt_ref[0, 0, jnp.maximum(kt_far, 0)], m_sc, l_sc, acc_sc, far_keep)

    o = out_sc[...] + (gate[2] * (1.0 / l_sc[...])) * acc_sc[...]
    o_ref[0, 0, 0] = o.astype(o_ref.dtype)


NSA_TQ, NSA_TK, NSA_TKW = 256, 256, 256


def nsa_prompt(qn_t, qr_t, kc, vc_t, ks_aug, vs_t, kw, vw_t, sm_t, bsz, t):
    tq, tk, tkw = NSA_TQ, NSA_TK, NSA_TKW
    assert tq == tkw and tk % tq == 0 and WINDOW % tkw == 0
    n_c = kc.shape[1]
    n_s = t // L_SLC
    assert n_s <= HEAD_DIM and KV_HEADS == 2
    w = HPG * tq
    c0 = np.arange(n_c)[None, :] * S_CMP
    s0 = np.arange(n_s)[:, None] * L_SLC
    cover_t = (np.maximum(np.minimum(c0 + L_CMP, s0 + L_SLC) - np.maximum(c0, s0), 0) / L_CMP).astype(np.float32)
    cover_t = np.tile(cover_t, (1, HPG))
    qspec = pl.BlockSpec((1, 1, 1, HEAD_DIM, w), lambda b, g, i: (b, g, i, 0, 0))
    kern = functools.partial(_nsa_prompt_kernel, tq=tq, tk=tk, tkw=tkw)
    return pl.pallas_call(
        kern,
        grid=(bsz, KV_HEADS, t // tq),
        in_specs=[qspec, qspec,
                  pl.BlockSpec((1, 1, n_c, HEAD_DIM), lambda b, g, i: (b, g, 0, 0)),
                  pl.BlockSpec((1, 1, HEAD_DIM, n_c), lambda b, g, i: (b, g, 0, 0)),
                  pl.BlockSpec((1, 1, t, 2 * HEAD_DIM), lambda b, g, i: (b, g, 0, 0)),
                  pl.BlockSpec((1, 1, t // tk, HEAD_DIM, tk), lambda b, g, i: (b, g, 0, 0, 0)),
                  pl.BlockSpec((t, KV_HEADS * HEAD_DIM), lambda b, g, i: (b, 0)),
                  pl.BlockSpec((1, 1, t // tkw, HEAD_DIM, tkw), lambda b, g, i: (b, g, 0, 0, 0)),
                  pl.BlockSpec((n_s, HPG * n_c), lambda b, g, i: (0, 0)),
                  pl.BlockSpec((1, 1, 32, tq), lambda b, g, i: (b, i, 0, 0))],
        out_specs=qspec,
        out_shape=jax.ShapeDtypeStruct((bsz, KV_HEADS, t // tq, HEAD_DIM, w), BF16),
        scratch_shapes=[pltpu.VMEM((2 * HEAD_DIM, w), BF16), pltpu.VMEM((2 * HEAD_DIM, w), BF16),
                        pltpu.VMEM((1, w), F32), pltpu.VMEM((1, w), F32),
                        pltpu.VMEM((HEAD_DIM, w), F32), pltpu.VMEM((HEAD_DIM, w), F32),
                        pltpu.VMEM((n_s, tq), F32), pltpu.VMEM((n_s, tq), F32),
                        pltpu.VMEM((2, tk, w), F32), pltpu.VMEM((WINDOW // tkw + 1, tkw, w), F32)],
        compiler_params=_cparams(("parallel", "parallel", "arbitrary")),
        name="nsa_prompt",
    )(qn_t, qr_t, kc.reshape(bsz, KV_HEADS, n_c, HEAD_DIM), vc_t.reshape(bsz, KV_HEADS, HEAD_DIM, n_c),
      ks_aug, vs_t, kw, vw_t, jnp.asarray(cover_t, BF16), sm_t)


def _sample_select_kernel(qn_ref, kc_ref, vc_ref, cover_ref, oc_ref, sb_ref, *, q_off, lq, n_cmp, n_slc):
    rows = HPG * lq
    for g in range(KV_HEADS):
        kc = kc_ref[0, g]
        n_c = kc.shape[0]
        cidx = lax.broadcasted_iota(jnp.int32, (rows, n_c), 1)
        qpos = q_off + (lax.broadcasted_iota(jnp.int32, (rows, n_c), 0) & (lq - 1))
        valid = (cidx * S_CMP + (L_CMP - 1) <= qpos) & (cidx < n_cmp)
        s1 = jnp.where(valid, _dot_nt(qn_ref[0, g], kc), NEG)
        e1 = jnp.exp(s1 - jnp.max(s1, axis=-1, keepdims=True)) * valid.astype(F32)
        p1b = (e1 * (1.0 / jnp.maximum(jnp.sum(e1, axis=-1, keepdims=True), 1e-30))).astype(BF16)
        oc_ref[0, g] = _dot(p1b, vc_ref[0, g])
        imp4 = _dot(p1b, cover_ref[...])
        imp = imp4[0:lq]
        for hh in range(1, HPG):
            imp = imp + imp4[hh * lq:(hh + 1) * lq]
        n_sp = imp.shape[1]
        blk = lax.broadcasted_iota(jnp.int32, (lq, n_sp), 1)
        cur = (q_off + lax.broadcasted_iota(jnp.int32, (lq, n_sp), 0)) // L_SLC
        vis = (blk <= cur) & (blk < n_slc)
        forced = vis & ((blk == 0) | (blk >= cur - 1))
        score = jnp.where(forced, 1e9, jnp.where(vis, imp, -1.0))
        rank = jnp.zeros((lq, n_sp), F32)
        for i in range(n_slc):
            c = score[:, i:i + 1]
            beats = (c > score) | ((c == score) & (blk > i))
            rank = rank + beats.astype(F32)
        sb_ref[0, g] = jnp.where(vis & (rank < min(N_SEL, n_slc)), 0.0, NEG)


def _sample_attend_kernel(pt_ref, pool_ref, qr_ref, sb_ref, exp_ref, oc_ref, gate_ref, kn_ref, vn_ref, kw_ref, vw_ref,
                          o_ref, buf, sem, m_sc, l_sc, acc_sc, *, q_off, lq, n_win):
    b = pl.program_id(0)
    j = pl.program_id(1)
    n_grp = pl.num_programs(1)
    step = b * n_grp + j
    slot = step % 2
    rows = HPG * lq

    def copies(bb, jj, sl):
        return [pltpu.make_async_copy(pool_ref.at[pt_ref[bb, jj * PAGES_PER_STEP + p], pl.ds(2, 2)],
                                      buf.at[sl, p], sem.at[sl]) for p in range(PAGES_PER_STEP)]

    @pl.when(step == 0)
    def _():
        for c in copies(b, j, slot):
            c.start()

    @pl.when(step + 1 < pl.num_programs(0) * n_grp)
    def _():
        wrap = j + 1 == n_grp
        for c in copies(jnp.where(wrap, b + 1, b), jnp.where(wrap, 0, j + 1), 1 - slot):
            c.start()

    for c in copies(b, j, slot):
        c.wait()

    @pl.when(j == 0)
    def _():
        m_sc[...] = jnp.full(m_sc.shape, NEG, F32)
        l_sc[...] = jnp.zeros(l_sc.shape, F32)
        acc_sc[...] = jnp.zeros(acc_sc.shape, F32)

    def update(g, s, v, v_transposed):
        m_prev = m_sc[g]
        m_new = jnp.maximum(m_prev, jnp.max(s, axis=-1, keepdims=True))
        alpha = jnp.exp(m_prev - m_new)
        p = jnp.exp(s - m_new)
        l_sc[g] = alpha * l_sc[g] + jnp.sum(p, axis=-1, keepdims=True)
        pv = _dot_nt(p.astype(BF16), v) if v_transposed else _dot(p.astype(BF16), v)
        acc_sc[g] = alpha * acc_sc[g] + pv
        m_sc[g] = m_new

    for g in range(KV_HEADS):
        k_t = jnp.concatenate([buf[slot, p, 0, g] for p in range(PAGES_PER_STEP)], axis=1).astype(BF16)
        v_t = jnp.concatenate([buf[slot, p, 1, g] for p in range(PAGES_PER_STEP)], axis=1).astype(BF16)
        sb = sb_ref[0, g, 0]
        bias = _dot(jnp.concatenate([sb] * HPG, axis=0).astype(BF16), exp_ref[...])
        update(g, _dot(qr_ref[0, g], k_t) + bias, v_t, True)

    @pl.when(j == n_grp - 1)
    def _():
        for g in range(KV_HEADS):
            qr = qr_ref[0, g]
            kn = kn_ref[0, g]
            kidx = lax.broadcasted_iota(jnp.int32, (rows, kn.shape[0]), 1)
            qidx = lax.broadcasted_iota(jnp.int32, (rows, kn.shape[0]), 0) & (lq - 1)
            update(g, jnp.where((kidx <= qidx) & (kidx < lq), _dot_nt(qr, kn), NEG), vn_ref[0, g], False)
            o_s = acc_sc[g] * (1.0 / l_sc[g])
            kw = kw_ref[0, g]
            widx = lax.broadcasted_iota(jnp.int32, (rows, kw.shape[0]), 1)
            kpos = q_off + lq - n_win + widx
            qpos = q_off + (lax.broadcasted_iota(jnp.int32, (rows, kw.shape[0]), 0) & (lq - 1))
            ok = (widx < n_win) & (kpos <= qpos) & (kpos > qpos - WINDOW) & (kpos >= 0)
            s3 = jnp.where(ok, _dot_nt(qr, kw), NEG)
            e3 = jnp.exp(s3 - jnp.max(s3, axis=-1, keepdims=True)) * ok.astype(F32)
            p3 = e3 * (1.0 / jnp.maximum(jnp.sum(e3, axis=-1, keepdims=True), 1e-30))
            o_w = _dot(p3.astype(BF16), vw_ref[0, g])
            gate = jax.nn.sigmoid(gate_ref[0, g])
            o_ref[0, g] = gate[:, 0:1] * oc_ref[0, g] + gate[:, 1:2] * o_s + gate[:, 2:3] * o_w


def nsa_sample(q, q_rot, kvc, kv_new, win_cat, pool_t, page_table, gate_logits, q_off):
    bsz, lq = q.shape[:2]
    assert lq & (lq - 1) == 0 and lq <= L_SLC and q_off % (PAGES_PER_STEP * PAGE_SIZE) == 0
    rows = HPG * lq
    scale = HEAD_DIM ** -0.5
    n_seg = kvc.shape[2]
    t_kv = q_off + lq
    n_cmp = (t_kv - L_CMP) // S_CMP + 1
    n_slc = -(-t_kv // L_SLC)
    n_sp = -(-n_slc // 128) * 128
    n_grp = page_table.shape[1] // PAGES_PER_STEP
    blk_step = PAGES_PER_STEP * PAGE_SIZE // L_SLC
    key_step = PAGES_PER_STEP * PAGE_SIZE

    def rows_major(a):
        return a.reshape(bsz, lq, KV_HEADS, HPG, -1).transpose(0, 2, 3, 1, 4).reshape(bsz, KV_HEADS, rows, -1)

    def keys_major(a, n_pad):
        a = a.astype(BF16).transpose(0, 2, 1, 3)
        return jnp.pad(a, ((0, 0), (0, 0), (0, n_pad - a.shape[2]), (0, 0)))

    qn_r = rows_major((q * scale).astype(BF16))
    qr_r = rows_major((q_rot * scale).astype(BF16))
    c0 = np.arange(n_seg)[:, None] * S_CMP
    s0 = np.arange(n_sp)[None, :] * L_SLC
    cover = np.maximum(np.minimum(c0 + L_CMP, s0 + L_SLC) - np.maximum(c0, s0), 0) / L_CMP
    cover = cover * (np.arange(n_seg)[:, None] < n_cmp) * (np.arange(n_sp)[None, :] < n_slc)
    kvc4 = kvc.reshape(2, bsz, KV_HEADS, n_seg, HEAD_DIM)
    spec_q = pl.BlockSpec((1, KV_HEADS, rows, HEAD_DIM), lambda b: (b, 0, 0, 0))
    o_c, sel_bias = pl.pallas_call(
        functools.partial(_sample_select_kernel, q_off=q_off, lq=lq, n_cmp=n_cmp, n_slc=n_slc),
        grid=(bsz,),
        in_specs=[spec_q,
                  pl.BlockSpec((1, KV_HEADS, n_seg, HEAD_DIM), lambda b: (b, 0, 0, 0)),
                  pl.BlockSpec((1, KV_HEADS, n_seg, HEAD_DIM), lambda b: (b, 0, 0, 0)),
                  pl.BlockSpec((n_seg, n_sp), lambda b: (0, 0))],
        out_specs=[spec_q, pl.BlockSpec((1, KV_HEADS, lq, n_sp), lambda b: (b, 0, 0, 0))],
        out_shape=[jax.ShapeDtypeStruct((bsz, KV_HEADS, rows, HEAD_DIM), F32),
                   jax.ShapeDtypeStruct((bsz, KV_HEADS, lq, n_sp), F32)],
        compiler_params=_cparams(("parallel",)),
        name="sample_select",
    )(qn_r, kvc4[0], kvc4[1], jnp.asarray(cover, BF16))

    sb_steps = sel_bias[..., :n_grp * blk_step].reshape(bsz, KV_HEADS, lq, n_grp, blk_step).transpose(0, 1, 3, 2, 4)
    expand = (np.arange(key_step)[None, :] // L_SLC == np.arange(blk_step)[:, None]).astype(np.float32)
    gates = jnp.pad(rows_major(gate_logits.reshape(bsz, lq, -1)), ((0, 0), (0, 0), (0, 0), (0, 128 - 3)))
    n_win = win_cat.shape[1]
    n_wp = -(-n_win // 128) * 128
    spec2 = lambda shape: pl.BlockSpec((1,) + shape, lambda b, j, pt: (b,) + (0,) * len(shape))
    attn = pl.pallas_call(
        functools.partial(_sample_attend_kernel, q_off=q_off, lq=lq, n_win=n_win),
        grid_spec=pltpu.PrefetchScalarGridSpec(
            num_scalar_prefetch=1,
            grid=(bsz, n_grp),
            in_specs=[pl.BlockSpec(memory_space=pl.ANY),
                      spec2((KV_HEADS, rows, HEAD_DIM)),
                      pl.BlockSpec((1, KV_HEADS, 1, lq, blk_step), lambda b, j, pt: (b, 0, j, 0, 0)),
                      pl.BlockSpec((blk_step, key_step), lambda b, j, pt: (0, 0)),
                      spec2((KV_HEADS, rows, HEAD_DIM)),
                      spec2((KV_HEADS, rows, 128)),
                      spec2((KV_HEADS, 128, HEAD_DIM)), spec2((KV_HEADS, 128, HEAD_DIM)),
                      spec2((KV_HEADS, n_wp, HEAD_DIM)), spec2((KV_HEADS, n_wp, HEAD_DIM))],
            out_specs=spec2((KV_HEADS, rows, HEAD_DIM)),
            scratch_shapes=[pltpu.VMEM((2, PAGES_PER_STEP, 2, KV_HEADS, HEAD_DIM, PAGE_SIZE), F32),
                            pltpu.SemaphoreType.DMA((2,)),
                            pltpu.VMEM((KV_HEADS, rows, 1), F32), pltpu.VMEM((KV_HEADS, rows, 1), F32),
                            pltpu.VMEM((KV_HEADS, rows, HEAD_DIM), F32)]),
        out_shape=jax.ShapeDtypeStruct((bsz, KV_HEADS, rows, HEAD_DIM), F32),
        compiler_params=_cparams(("arbitrary", "arbitrary")),
        name="sample_attend",
    )(page_table, pool_t, qr_r, sb_steps, jnp.asarray(expand, BF16), o_c, gates,
      keys_major(kv_new[:, :, 2], 128), keys_major(kv_new[:, :, 3], 128),
      keys_major(win_cat[:, :, 0], n_wp), keys_major(win_cat[:, :, 1], n_wp))
    attn = attn.reshape(bsz, KV_HEADS, HPG, lq, HEAD_DIM).transpose(0, 3, 1, 2, 4)
    return attn.astype(BF16).reshape(bsz * lq, ATT_W)


def _ssd_kernel(d_ref, x_ref, bt_ref, c_ref, acol_ref, arow_ref, dt_ref, s0_ref, y_ref, sout_ref, s_sc, *, cl):
    ci = pl.program_id(1)

    @pl.when(ci == 0)
    def _():
        s_sc[...] = s0_ref[0]

    li = lax.broadcasted_iota(jnp.int32, (cl, cl), 0)
    si = lax.broadcasted_iota(jnp.int32, (cl, cl), 1)
    causal = li >= si
    acum_col = _dot_hi(causal.astype(F32), acol_ref[0])
    acum_row = _dot_hi(arow_ref[0], (li <= si).astype(F32))
    dt = dt_ref[0]
    for g in range(SSM_GROUPS):
        cg = c_ref[0, g].astype(BF16)
        btg = bt_ref[0, g]
        cb = _dot(cg, btg.astype(BF16))
        for hh in range(SSM_HEADS // SSM_GROUPS):
            h = g * (SSM_HEADS // SSM_GROUPS) + hh
            ac = acum_col[:, h:h + 1]
            ar = acum_row[h:h + 1, :]
            decay = jnp.exp(jnp.where(causal, ac - ar, NEG))
            xs = x_ref[0, h]
            xd = (xs * dt[:, h:h + 1]).astype(BF16)
            st = s_sc[h]
            y = _dot((cb * decay).astype(BF16), xd) + _dot(cg, st.astype(BF16)) * jnp.exp(ac)
            y_ref[0, h] = y + d_ref[h] * xs
            a_last = ar[:, cl - 1:cl]
            snew = _dot((btg * jnp.exp(a_last - ar)).astype(BF16), xd)
            s_sc[h] = st * jnp.exp(a_last) + snew

    @pl.when(ci == pl.num_programs(1) - 1)
    def _():
        sout_ref[0] = s_sc[...]


def ssd_scan(ssm_d, x_hm, b_t, c, a_col, a_row, dt_col, s0_t, cl):
    bsz, nh, t, p = x_hm.shape
    n = D_STATE
    nc = t // cl
    kern = functools.partial(_ssd_kernel, cl=cl)
    return pl.pallas_call(
        kern,
        grid=(bsz, nc),
        in_specs=[pl.BlockSpec(memory_space=pltpu.SMEM),
                  pl.BlockSpec((1, nh, cl, p), lambda b, c_: (b, 0, c_, 0)),
                  pl.BlockSpec((1, SSM_GROUPS, n, cl), lambda b, c_: (b, 0, 0, c_)),
                  pl.BlockSpec((1, SSM_GROUPS, cl, n), lambda b, c_: (b, 0, c_, 0)),
                  pl.BlockSpec((1, cl, 128), lambda b, c_: (b, c_, 0)),
                  pl.BlockSpec((1, 8, cl), lambda b, c_: (b, 0, c_)),
                  pl.BlockSpec((1, cl, 128), lambda b, c_: (b, c_, 0)),
                  pl.BlockSpec((1, nh, n, p), lambda b, c_: (b, 0, 0, 0))],
        out_specs=[pl.BlockSpec((1, nh, cl, p), lambda b, c_: (b, 0, c_, 0)),
                   pl.BlockSpec((1, nh, n, p), lambda b, c_: (b, 0, 0, 0))],
        out_shape=[jax.ShapeDtypeStruct((bsz, nh, t, p), F32), jax.ShapeDtypeStruct((bsz, nh, n, p), F32)],
        scratch_shapes=[pltpu.VMEM((nh, n, p), F32)],
        compiler_params=_cparams(("parallel", "arbitrary")),
        name="ssd_scan",
    )(ssm_d, x_hm, b_t, c, a_col, a_row, dt_col, s0_t)


def _ssd_prompt_kernel(d_ref, xbc_ref, sm_ref, z_ref, cw_ref, cb_ref, dtb_ref, ah_ref, gn_ref, o_ref, sout_ref,
                       s_sc, tail_sc, *, cl):
    ci = pl.program_id(1)

    @pl.when(ci == 0)
    def _():
        s_sc[...] = jnp.zeros(s_sc.shape, F32)
        tail_sc[...] = jnp.zeros(tail_sc.shape, F32)

    xb = xbc_ref[...]
    prev = tail_sc[...]
    row8 = lax.broadcasted_iota(jnp.int32, (8, CONV_DIM), 0)

    def shifted(j):
        body = pltpu.roll(xb, j, axis=0)
        top = jnp.where(row8 < j, pltpu.roll(prev, j, axis=0), body[0:8])
        return jnp.concatenate([top, body[8:]], axis=0)

    conv = cb_ref[...] + shifted(CONV_W - 1) * cw_ref[0:1]
    for j in range(1, CONV_W):
        conv = conv + (shifted(CONV_W - 1 - j) if j < CONV_W - 1 else xb) * cw_ref[j:j + 1]
    tail_sc[...] = xb[cl - 8:cl]
    act = conv * jax.nn.sigmoid(conv)
    n_bc = SSM_GROUPS * D_STATE
    bmat = act[:, D_INNER:D_INNER + n_bc]
    cmat = act[:, D_INNER + n_bc:D_INNER + 2 * n_bc]
    x_dt = sm_ref[...] + dtb_ref[...]
    dt = jnp.maximum(x_dt, 0.0) + jnp.log(1.0 + jnp.exp(-jnp.abs(x_dt)))
    a = ah_ref[...] * dt

    li = lax.broadcasted_iota(jnp.int32, (cl, cl), 0)
    si = lax.broadcasted_iota(jnp.int32, (cl, cl), 1)
    causal = li >= si
    acum_col = _dot_hi(causal.astype(F32), a)
    acum_row = _dot_hi(a.T[N_GATES:N_GATES + SSM_HEADS], (li <= si).astype(F32))
    bt = bmat.T
    bt_b = bt.astype(BF16)
    lane = lax.broadcasted_iota(jnp.int32, (cl, 128), 1)
    low = lane < SSM_HEAD_DIM
    low_s = lax.broadcasted_iota(jnp.int32, (128, 128), 1) < SSM_HEAD_DIM

    def col(arr, h):
        return arr[:, N_GATES + h:N_GATES + h + 1]

    per_group = SSM_HEADS // SSM_GROUPS
    cg_of, cb_of = {}, {}
    ys = []
    for j in range(SSM_HEADS // 2):
        ha, hb = 2 * j, 2 * j + 1
        g = ha // per_group
        if g not in cg_of:
            in_group = (lane >= g * D_STATE) & (lane < (g + 1) * D_STATE)
            cg_of[g] = jnp.where(in_group, cmat, 0.0).astype(BF16)
            cb_of[g] = _dot(cg_of[g], bt_b)
        cg, cb = cg_of[g], cb_of[g]
        xs_pair = act[:, 128 * j:128 * (j + 1)]
        xd = (xs_pair * jnp.where(low, col(dt, ha), col(dt, hb))).astype(BF16)
        st = s_sc[j]
        y_off = _dot(cg, st.astype(BF16)) * jnp.where(low, jnp.exp(col(acum_col, ha)), jnp.exp(col(acum_col, hb)))
        y_h, s_h, dec_h = [], [], []
        for h in (ha, hb):
            ar = acum_row[h:h + 1, :]
            decay = jnp.exp(jnp.where(causal, col(acum_col, h) - ar, NEG))
            y_h.append(_dot((cb * decay).astype(BF16), xd))
            a_last = ar[:, cl - 1:cl]
            s_h.append(_dot((bt * jnp.exp(a_last - ar)).astype(BF16), xd))
            dec_h.append(jnp.exp(a_last))
        skip = jnp.where(low[0:1], d_ref[ha], d_ref[hb])
        ys.append(jnp.where(low, y_h[0], y_h[1]) + y_off + skip * xs_pair)
        s_sc[j] = st * jnp.where(low_s, dec_h[0], dec_h[1]) + jnp.where(low_s, s_h[0], s_h[1])

    z = z_ref[...]
    gated = jnp.concatenate(ys, axis=1) * (z * jax.nn.sigmoid(z))
    out = gated * lax.rsqrt(jnp.mean(gated * gated, axis=-1, keepdims=True) + EPS) * gn_ref[...]
    o_ref[...] = out.astype(o_ref.dtype)

    @pl.when(ci == pl.num_programs(1) - 1)
    def _():
        sout_ref[0] = s_sc[...]


def ssd_prompt(xbc, sm, z, w, bsz, t):
    cl = SSM_CHUNK
    nc = t // cl
    lanes = lambda v: jnp.zeros((1, 128), F32).at[0, N_GATES:N_GATES + SSM_HEADS].set(v)
    row = lambda width: pl.BlockSpec((cl, width), lambda b, c: (b * nc + c, 0))
    fixed = lambda r, c_: pl.BlockSpec((r, c_), lambda b, c: (0, 0))
    n_pair = SSM_HEADS // 2
    out, s_pairs = pl.pallas_call(
        functools.partial(_ssd_prompt_kernel, cl=cl),
        grid=(bsz, nc),
        in_specs=[pl.BlockSpec(memory_space=pltpu.SMEM), row(CONV_DIM), row(128), row(D_INNER),
                  fixed(CONV_W, CONV_DIM), fixed(1, CONV_DIM), fixed(1, 128), fixed(1, 128), fixed(1, D_INNER)],
        out_specs=[row(D_INNER), pl.BlockSpec((1, n_pair, 128, 128), lambda b, c: (b, 0, 0, 0))],
        out_shape=[jax.ShapeDtypeStruct((bsz * t, D_INNER), BF16),
                   jax.ShapeDtypeStruct((bsz, n_pair, 128, 128), F32)],
        scratch_shapes=[pltpu.VMEM((n_pair, 128, 128), F32), pltpu.VMEM((8, CONV_DIM), F32)],
        compiler_params=_cparams(("parallel", "arbitrary")),
        name="ssd_prompt",
    )(w['ssm_d'], xbc, sm, z, w['conv_w'], w['conv_b'].reshape(1, CONV_DIM), lanes(w['dt_bias']),
      lanes(-jnp.exp(w['a_log'])), w['ssm_norm'].reshape(1, D_INNER))
    s6 = s_pairs.reshape(bsz, n_pair, SSM_GROUPS, D_STATE, 2, SSM_HEAD_DIM)
    per_group = SSM_HEADS // SSM_GROUPS
    heads = [s6[:, h // 2, h // per_group, :, h % 2, :] for h in range(SSM_HEADS)]
    return out, jnp.stack(heads, axis=1).transpose(0, 1, 3, 2)


def _rmsnorm(x, g):
    return x * lax.rsqrt(jnp.mean(x * x, axis=-1, keepdims=True) + EPS) * g


def _rope(x, pos):
    half = HEAD_DIM // 2
    inv = ROPE_THETA ** (-jnp.arange(half, dtype=F32) / half)
    ang = pos.astype(F32)[:, None] * inv[None, :]
    cos, sin = jnp.cos(ang)[:, None, :], jnp.sin(ang)[:, None, :]
    x1, x2 = x[..., :half], x[..., half:]
    return jnp.concatenate([x1 * cos - x2 * sin, x2 * cos + x1 * sin], axis=-1)


def _ssd_inputs(xbc_all, dt_raw, w, t_pad):
    bsz = xbc_all.shape[0]
    length = xbc_all.shape[1] - (CONV_W - 1)
    conv = w['conv_b'] + xbc_all[:, 0:length] * w['conv_w'][0]
    for j in range(1, CONV_W):
        conv = conv + xbc_all[:, j:j + length] * w['conv_w'][j]
    act = jax.nn.silu(conv)
    dt = jax.nn.softplus(dt_raw + w['dt_bias'])
    a = -jnp.exp(w['a_log']) * dt
    pad = t_pad - length
    if pad:
        act = jnp.pad(act, ((0, 0), (0, pad), (0, 0)))
        dt = jnp.pad(dt, ((0, 0), (0, pad), (0, 0)))
        a = jnp.pad(a, ((0, 0), (0, pad), (0, 0)))
    xs = act[..., :D_INNER].reshape(bsz, t_pad, SSM_HEADS, SSM_HEAD_DIM).transpose(0, 2, 1, 3)
    b_in = act[..., D_INNER:D_INNER + SSM_GROUPS * D_STATE].reshape(bsz, t_pad, SSM_GROUPS, D_STATE)
    c_in = act[..., D_INNER + SSM_GROUPS * D_STATE:].reshape(bsz, t_pad, SSM_GROUPS, D_STATE)
    lane_pad = ((0, 0), (0, 0), (0, 128 - SSM_HEADS))
    return (xs, b_in.transpose(0, 2, 3, 1), c_in.transpose(0, 2, 1, 3), jnp.pad(a, lane_pad),
            a.transpose(0, 2, 1), jnp.pad(dt, lane_pad))


def _moe(f_b, logits, w):
    n_tok = f_b.shape[0]
    pg = jax.nn.softmax(logits[:, :N_EXPERT_GROUPS], axis=-1)
    g_sel = jnp.argmax(pg, axis=-1)
    g_w = jnp.max(pg, axis=-1)
    el = logits[:, N_EXPERT_GROUPS:N_EXPERT_GROUPS + N_EXPERTS].reshape(n_tok, N_EXPERT_GROUPS, EXPERTS_PER_GROUP)
    el = jnp.take_along_axis(el, g_sel[:, None, None], axis=1)[:, 0]
    top_p, top_i = lax.top_k(jax.nn.softmax(el, axis=-1), TOP_K)
    wts = g_w[:, None] * top_p / jnp.sum(top_p, axis=-1, keepdims=True)
    eid = (g_sel[:, None] * EXPERTS_PER_GROUP + top_i).reshape(-1).astype(jnp.int32)
    n_asg = eid.shape[0]
    onehot = (eid[:, None] == jnp.arange(N_EXPERTS, dtype=jnp.int32)[None, :]).astype(jnp.int32)
    within = jnp.take_along_axis(jnp.cumsum(onehot, axis=0), eid[:, None], axis=1)[:, 0] - 1
    sizes = jnp.sum(onehot, axis=0)
    padded = (sizes + MOE_ROWS - 1) // MOE_ROWS * MOE_ROWS
    pend = jnp.cumsum(padded)
    dest = (pend - padded)[eid] + within
    n_blk = -(-n_asg // MOE_ROWS) + N_EXPERTS
    src = jnp.zeros((n_blk * MOE_ROWS,), jnp.int32).at[dest].set(jnp.arange(n_asg, dtype=jnp.int32) // TOP_K)
    xpad = f_b[src]
    blk_start = jnp.arange(n_blk, dtype=jnp.int32) * MOE_ROWS
    blk_e = jnp.minimum(jnp.sum((pend[None, :] <= blk_start[:, None]).astype(jnp.int32), axis=1), N_EXPERTS - 1)
    n_used = (pend[-1] // MOE_ROWS).astype(jnp.int32).reshape(1)
    ypad = moe_experts(xpad, blk_e, n_used, w['w_gate'], w['w_up'], w['w_down'])
    y01 = ypad[dest.reshape(n_tok, TOP_K).T.reshape(-1)]
    return y01, jnp.pad(wts, ((0, 0), (0, 128 - TOP_K)))


def _token_tail(x2, mix_in, p2, w, tm):
    pad = 128 - N_EXPERT_GROUPS - N_EXPERTS
    w_router = jnp.concatenate([w['w_rg'], w['w_re'], jnp.zeros((D_MODEL, pad), F32)], axis=1)
    b_router = jnp.concatenate([w['b_rg'], w['b_re'], jnp.zeros((pad,), F32)]).reshape(1, 128)
    h, f_b, logits = outproj(x2, mix_in[0], mix_in[1], w['w_out_b'], w['ffn_norm'], w_router, b_router, tm)
    y01, wts = _moe(f_b, logits, w)
    return moe_combine_ple(h, y01, wts, p2, w['wpg_b'], w['wpp_b'], w['ple_norm'], tm)


def _ssd_finish(y_hm, z, w, length):
    bsz = y_hm.shape[0]
    y = y_hm[:, :, :length].transpose(0, 2, 1, 3).reshape(bsz, length, D_INNER)
    gated = y * jax.nn.silu(z)
    return _rmsnorm(gated, w['ssm_norm']).astype(BF16)


def _prompt_group(x, p, w):
    bsz, t, _ = x.shape
    m = bsz * t
    z, xbc, sm, sm_t, kv, win, qn_t, qr_t, ks_aug, kw, vs_t, vw_t = rms_inproj_prompt(
        x.reshape(m, D_MODEL), w['attn_norm'], w['w_in_r'], w['q_norm'], w['k_norm'], bsz, t)
    nq = t // NSA_TQ
    kv_new = kv.reshape(bsz, t, 4, KV_HEADS, HEAD_DIM)
    win_new = win.reshape(bsz, t, 2, KV_HEADS, HEAD_DIM)

    n_seg = t // S_CMP
    segs = kv_new[:, :, 0:2].astype(BF16).transpose(2, 0, 3, 1, 4).reshape(2, bsz * KV_HEADS, n_seg, S_CMP * HEAD_DIM)
    kvc = compress(segs, w['cmp_pe'], w['cmp_w1'], w['cmp_w2'])
    attn = nsa_prompt(qn_t, qr_t, kvc[0], kvc[1].transpose(0, 2, 1), ks_aug, vs_t, kw, vw_t, sm_t, bsz, t)
    attn = attn.reshape(bsz, KV_HEADS, nq, HEAD_DIM, HPG, NSA_TQ).transpose(0, 2, 5, 1, 4, 3).reshape(m, ATT_W)

    ssd, ssm_new = ssd_prompt(xbc, sm, z, w, bsz, t)

    y = _token_tail(x.reshape(m, D_MODEL), (attn, ssd), p.reshape(m, PLE_DIM), w, 512)
    keep = min(WINDOW, t)
    conv_new = xbc.reshape(bsz, t, CONV_DIM)[:, t - (CONV_W - 1):]
    return y.reshape(bsz, t, D_MODEL), kv_new, win_new[:, t - keep:], ssm_new, conv_new


def _sample_group(x, p, pool, page_table, cache_win, state_ssm, state_conv, w):
    bsz, lq, _ = x.shape
    m = bsz * lq
    q_off = page_table.shape[1] * PAGE_SIZE
    q, kvs, z, xbc, sm = rms_inproj(x.reshape(m, D_MODEL), w['attn_norm'], w['w_in_r'], m)
    pos = q_off + jnp.arange(lq)
    qn = _rmsnorm(q.reshape(bsz, lq, N_HEADS, HEAD_DIM), w['q_norm'])
    qr = _rope(qn, pos)
    kvs = kvs.reshape(bsz, lq, 6, KV_HEADS, HEAD_DIM)
    k_c = _rmsnorm(kvs[:, :, 0], w['k_norm'][0])
    k_s = _rope(_rmsnorm(kvs[:, :, 2], w['k_norm'][1]), pos)
    k_w = _rope(_rmsnorm(kvs[:, :, 4], w['k_norm'][2]), pos)
    kv_new = jnp.stack([k_c, kvs[:, :, 1], k_s, kvs[:, :, 3]], axis=2)
    win_new = jnp.stack([k_w, kvs[:, :, 5]], axis=2)
    win_cat = jnp.concatenate([cache_win, win_new], axis=1)

    pool_t = jnp.transpose(pool, (0, 2, 3, 4, 1))
    kvc = compress_paged(pool_t, page_table, w['cmp_pe'], w['cmp_w1'], w['cmp_w2'])
    attn = nsa_sample(qn, qr, kvc, kv_new, win_cat, pool_t, page_table, sm[:, :N_GATES], q_off)

    xbc_all = jnp.concatenate([state_conv, xbc.reshape(bsz, lq, CONV_DIM)], axis=1)
    dt_raw = sm[:, N_GATES:N_GATES + SSM_HEADS].reshape(bsz, lq, SSM_HEADS)
    xs, b_t, c_in, a_col, a_row, dt_col = _ssd_inputs(xbc_all, dt_raw, w, SSM_CHUNK)
    y_hm, s_t = ssd_scan(w['ssm_d'], xs, b_t, c_in, a_col, a_row, dt_col, state_ssm.transpose(0, 1, 3, 2), SSM_CHUNK)
    ssd = _ssd_finish(y_hm, z.reshape(bsz, lq, D_INNER), w, lq).reshape(m, D_INNER)

    y = _token_tail(x.reshape(m, D_MODEL), (attn, ssd), p.reshape(m, PLE_DIM), w, m)
    keep = cache_win.shape[1]
    return (y.reshape(bsz, lq, D_MODEL), kv_new, win_cat[:, win_cat.shape[1] - keep:], s_t.transpose(0, 1, 3, 2),
            xbc_all[:, xbc_all.shape[1] - (CONV_W - 1):])


def kernel(x_prompt, x_sample, cache_kv, cache_win, state_ssm, state_conv, page_table, p_prompt, p_sample,
           w_in, w_out, q_norm, k_norm, cmp_pe, cmp_w1, cmp_w2, conv_w, conv_b, dt_bias, a_log, ssm_d, ssm_norm,
           attn_norm, ffn_norm, w_rg, b_rg, w_re, b_re, w_gate, w_up, w_down, w_ple_proj, ple_norm, w_ple_gate):
    depth = w_in.shape[0]
    hp, hs = x_prompt, x_sample
    outs = [[] for _ in range(8)]
    cuts = np.cumsum((ATT_W, 6 * KV_HEADS * HEAD_DIM, N_GATES, D_INNER, CONV_DIM, SSM_HEADS))
    for l in range(depth):
        wi = w_in[l]
        w_in_r = jnp.concatenate(
            [wi[:, :cuts[1]], wi[:, cuts[2]:cuts[3]], wi[:, cuts[3]:cuts[4]], wi[:, cuts[1]:cuts[2]],
             wi[:, cuts[4]:cuts[5]], jnp.zeros((D_MODEL, C_SM - N_GATES - SSM_HEADS), F32)], axis=1).astype(BF16)
        w = dict(w_in_r=w_in_r, w_out_b=w_out[l].astype(BF16), q_norm=q_norm[l], k_norm=k_norm[l],
                 cmp_pe=cmp_pe[l], cmp_w1=cmp_w1[l], cmp_w2=cmp_w2[l], conv_w=conv_w[l], conv_b=conv_b[l],
                 dt_bias=dt_bias[l], a_log=a_log[l], ssm_d=ssm_d[l], ssm_norm=ssm_norm[l], attn_norm=attn_norm[l],
                 ffn_norm=ffn_norm[l], w_rg=w_rg[l], b_rg=b_rg[l], w_re=w_re[l], b_re=b_re[l],
                 w_gate=w_gate[l], w_up=w_up[l], w_down=w_down[l],
                 wpp_b=w_ple_proj[l].astype(BF16), ple_norm=ple_norm[l], wpg_b=w_ple_gate[l].astype(BF16))
        hp, *rest_p = _prompt_group(hp, p_prompt[l], w)
        hs, *rest_s = _sample_group(hs, p_sample[l], cache_kv[l], page_table, cache_win[l], state_ssm[l],
                                    state_conv[l], w)
        for j in range(4):
            outs[2 * j].append(rest_p[j])
            outs[2 * j + 1].append(rest_s[j])
    return (hp, hs) + tuple(jnp.stack(o) for o in outs)
```

```python
import functools
import math

import numpy as np
import jax
import jax.numpy as jnp
from jax import lax
from jax.experimental import pallas as pl
from jax.experimental.pallas import tpu as pltpu

F32 = jnp.float32
BF16 = jnp.bfloat16

D_MODEL = 1024
PAGE_SIZE = 128
N_HEADS = 8
HEAD_DIM = 64
KV_HEADS = 2
HPG = N_HEADS // KV_HEADS
ATT_W = N_HEADS * HEAD_DIM
L_CMP = 32
S_CMP = 16
L_SLC = 64
N_SEL = 16
WINDOW = 512
CMP_HID = 64
ROPE_THETA = 10000.0
SSM_HEADS = 8
SSM_HEAD_DIM = 64
D_INNER = SSM_HEADS * SSM_HEAD_DIM
SSM_GROUPS = 2
D_STATE = 64
CONV_W = 4
CONV_DIM = D_INNER + 2 * SSM_GROUPS * D_STATE
SSM_CHUNK = 128
N_EXPERT_GROUPS = 4
EXPERTS_PER_GROUP = 8
N_EXPERTS = N_EXPERT_GROUPS * EXPERTS_PER_GROUP
TOP_K = 2
D_EXPERT = 512
PLE_DIM = 256
EPS = 1e-6
N_GATES = 3 * N_HEADS
C_Q, C_KV, C_Z, C_XBC, C_SM = 512, 768, 512, 768, 128
D_IN_PAD = C_Q + C_KV + C_Z + C_XBC + C_SM

NEG = -1e30
VMEM_LIMIT = 48 * 1024 * 1024
MOE_ROWS = 256


def _cparams(sem):
    return pltpu.CompilerParams(dimension_semantics=sem, vmem_limit_bytes=VMEM_LIMIT)


def _dot(a, b):
    return jnp.dot(a, b, preferred_element_type=F32)


def _dot_nt(a, b):
    return lax.dot_general(a, b, (((1,), (1,)), ((), ())), preferred_element_type=F32)


def _dot_hi(a, b):
    return jnp.dot(a, b, preferred_element_type=F32, precision=lax.Precision.HIGHEST)


def _rms_inproj_kernel(x_ref, g_ref, w_ref, q_ref, kv_ref, z_ref, xbc_ref, sm_ref):
    x = x_ref[...]
    y = x * lax.rsqrt(jnp.mean(x * x, axis=-1, keepdims=True) + EPS) * g_ref[...]
    yb = y.astype(BF16)
    c0 = 0
    for ref, width in ((q_ref, C_Q), (kv_ref, C_KV), (z_ref, C_Z), (xbc_ref, C_XBC), (sm_ref, C_SM)):
        ref[...] = _dot(yb, w_ref[:, c0:c0 + width])
        c0 += width


def rms_inproj(x, gain, w_r, tm):
    m = x.shape[0]
    widths = (C_Q, C_KV, C_Z, C_XBC, C_SM)
    return pl.pallas_call(
        _rms_inproj_kernel,
        grid=(m // tm,),
        in_specs=[pl.BlockSpec((tm, D_MODEL), lambda i: (i, 0)),
                  pl.BlockSpec((1, D_MODEL), lambda i: (0, 0)),
                  pl.BlockSpec((D_MODEL, D_IN_PAD), lambda i: (0, 0))],
        out_specs=[pl.BlockSpec((tm, w), lambda i: (i, 0)) for w in widths],
        out_shape=[jax.ShapeDtypeStruct((m, w), F32) for w in widths],
        compiler_params=_cparams(("parallel",)),
        name="rms_inproj",
    )(x, gain.reshape(1, D_MODEL), w_r)


def _inproj_prompt_kernel(x_ref, g_ref, w_ref, cos_ref, sin_ref, gq_ref, gk_ref, z_ref, xbc_ref, sm_ref, smt_ref, kv_ref,
                          win_ref, qn_ref, qr_ref, ksa_ref, kw_ref, vst_ref, vwt_ref, *, tiles_per_seq):
    tm = x_ref.shape[0]
    x = x_ref[...]
    yb = (x * lax.rsqrt(jnp.mean(x * x, axis=-1, keepdims=True) + EPS) * g_ref[...]).astype(BF16)
    cos, sin = cos_ref[...], sin_ref[...]
    scale = HEAD_DIM ** -0.5
    half = HEAD_DIM // 2

    def proj(c0, width=128):
        return _dot(yb, w_ref[:, c0:c0 + width])

    def head_norm(x_t, gain):
        return x_t * lax.rsqrt(jnp.mean(x_t * x_t, axis=0, keepdims=True) + EPS) * gain

    def rotate(y_t):
        y1, y2 = y_t[0:half], y_t[half:HEAD_DIM]
        return jnp.concatenate([y1 * cos - y2 * sin, y2 * cos + y1 * sin], axis=0)

    def per_group(blk, fn):
        t = blk.T
        return jnp.concatenate([fn(t[g * HEAD_DIM:(g + 1) * HEAD_DIM]) for g in range(KV_HEADS)], axis=0).T

    for j in range(ATT_W // 128):
        q_t = proj(128 * j).T
        for h2 in range(128 // HEAD_DIM):
            head = (128 // HEAD_DIM) * j + h2
            g, hh = head // HPG, head % HPG
            qn = head_norm(q_t[h2 * HEAD_DIM:(h2 + 1) * HEAD_DIM], gq_ref[...])
            qn_ref[0, g, 0, :, hh * tm:(hh + 1) * tm] = (qn * scale).astype(BF16)
            qr_ref[0, g, 0, :, hh * tm:(hh + 1) * tm] = (rotate(qn) * scale).astype(BF16)

    c_kv = C_Q
    k_c = per_group(proj(c_kv), lambda t: head_norm(t, gk_ref[0]))
    v_c = proj(c_kv + 128)
    k_s = per_group(proj(c_kv + 256), lambda t: rotate(head_norm(t, gk_ref[1])))
    v_s = proj(c_kv + 384)
    k_w = per_group(proj(c_kv + 512), lambda t: rotate(head_norm(t, gk_ref[2])))
    v_w = proj(c_kv + 640)
    kv_ref[:, 0:128] = k_c
    kv_ref[:, 128:256] = v_c
    kv_ref[:, 256:384] = k_s
    kv_ref[:, 384:512] = v_s
    win_ref[:, 0:128] = k_w
    win_ref[:, 128:256] = v_w
    kw_ref[...] = k_w.astype(BF16)
    pos0 = (pl.program_id(0) % tiles_per_seq) * tm
    lane = lax.broadcasted_iota(jnp.int32, (tm, 128), 1)
    blk_of_row = (pos0 + lax.broadcasted_iota(jnp.int32, (tm, 128), 0)) // L_SLC
    onehot = (blk_of_row == (lane & (HEAD_DIM - 1))).astype(BF16)
    k_sb = k_s.astype(BF16)
    ksa_ref[0, 0] = jnp.where(lane < HEAD_DIM, k_sb, onehot)
    ksa_ref[0, 1] = jnp.where(lane >= HEAD_DIM, k_sb, onehot)
    vs_t = v_s.T.astype(BF16)
    vw_t = v_w.T.astype(BF16)
    for g in range(KV_HEADS):
        vst_ref[0, g, 0] = vs_t[g * HEAD_DIM:(g + 1) * HEAD_DIM]
        vwt_ref[0, g, 0] = vw_t[g * HEAD_DIM:(g + 1) * HEAD_DIM]
    z_ref[...] = proj(C_Q + C_KV, C_Z)
    xbc_ref[...] = proj(C_Q + C_KV + C_Z, C_XBC)
    sm = proj(C_Q + C_KV + C_Z + C_XBC)
    sm_ref[...] = sm
    smt_ref[0, 0] = sm.T[0:32]


def rms_inproj_prompt(x, gain, w_r, q_gain, k_gain, bsz, t):
    tm = NSA_TQ
    assert NSA_TK == tm and NSA_TKW == tm and t // L_SLC <= HEAD_DIM
    m = bsz * t
    nt = t // tm
    inv = ROPE_THETA ** (-jnp.arange(HEAD_DIM // 2, dtype=F32) / (HEAD_DIM // 2))
    ang = inv[:, None] * jnp.arange(t, dtype=F32)[None, :]
    row = lambda width: pl.BlockSpec((tm, width), lambda i: (i, 0))
    qspec = pl.BlockSpec((1, KV_HEADS, 1, HEAD_DIM, HPG * tm), lambda i: (i // nt, 0, i % nt, 0, 0))
    vspec = pl.BlockSpec((1, KV_HEADS, 1, HEAD_DIM, tm), lambda i: (i // nt, 0, i % nt, 0, 0))
    tspec = pl.BlockSpec((HEAD_DIM // 2, tm), lambda i: (0, i % nt))
    qshape = jax.ShapeDtypeStruct((bsz, KV_HEADS, nt, HEAD_DIM, HPG * tm), BF16)
    vshape = jax.ShapeDtypeStruct((bsz, KV_HEADS, nt, HEAD_DIM, tm), BF16)
    return pl.pallas_call(
        functools.partial(_inproj_prompt_kernel, tiles_per_seq=nt),
        grid=(m // tm,),
        in_specs=[row(D_MODEL),
                  pl.BlockSpec((1, D_MODEL), lambda i: (0, 0)),
                  pl.BlockSpec((D_MODEL, D_IN_PAD), lambda i: (0, 0)),
                  tspec, tspec,
                  pl.BlockSpec((HEAD_DIM, 1), lambda i: (0, 0)),
                  pl.BlockSpec((3, HEAD_DIM, 1), lambda i: (0, 0, 0))],
        out_specs=[row(C_Z), row(C_XBC), row(C_SM),
                   pl.BlockSpec((1, 1, 32, tm), lambda i: (i // nt, i % nt, 0, 0)),
                   row(512), row(256), qspec, qspec,
                   pl.BlockSpec((1, KV_HEADS, tm, 128), lambda i: (i // nt, 0, i % nt, 0)),
                   row(128), vspec, vspec],
        out_shape=[jax.ShapeDtypeStruct((m, C_Z), F32), jax.ShapeDtypeStruct((m, C_XBC), F32),
                   jax.ShapeDtypeStruct((m, C_SM), F32), jax.ShapeDtypeStruct((bsz, nt, 32, tm), F32),
                   jax.ShapeDtypeStruct((m, 512), F32), jax.ShapeDtypeStruct((m, 256), F32), qshape, qshape,
                   jax.ShapeDtypeStruct((bsz, KV_HEADS, t, 128), BF16),
                   jax.ShapeDtypeStruct((m, 128), BF16), vshape, vshape],
        compiler_params=_cparams(("parallel",)),
        name="rms_inproj_prompt",
    )(x, gain.reshape(1, D_MODEL), w_r, jnp.cos(ang), jnp.sin(ang), q_gain.reshape(HEAD_DIM, 1),
      k_gain.reshape(3, HEAD_DIM, 1))


def _outproj_kernel(x_ref, a_ref, s_ref, w_ref, g_ref, wr_ref, br_ref, h_ref, f_ref, lg_ref):
    acc = _dot(a_ref[...], w_ref[0:ATT_W, :]) + _dot(s_ref[...], w_ref[ATT_W:ATT_W + D_INNER, :])
    h = x_ref[...] + acc
    h_ref[...] = h
    f = h * lax.rsqrt(jnp.mean(h * h, axis=-1, keepdims=True) + EPS) * g_ref[...]
    f_hi = f.astype(BF16)
    f_ref[...] = f_hi
    f_lo = (f - f_hi.astype(F32)).astype(BF16)
    lg_ref[...] = (_dot(f_hi, wr_ref[0]) + (_dot(f_hi, wr_ref[1]) + _dot(f_lo, wr_ref[0]))) + br_ref[...]


def outproj(x, attn, ssd, w_out_b, ffn_gain, w_router, b_router, tm):
    m = x.shape[0]
    row = lambda width: pl.BlockSpec((tm, width), lambda i: (i, 0))
    fixed = lambda r, c: pl.BlockSpec((r, c), lambda i: (0, 0))
    w_hi = w_router.astype(BF16)
    w_hi_rest = (w_router - w_hi.astype(F32)).astype(BF16)
    return pl.pallas_call(
        _outproj_kernel,
        grid=(m // tm,),
        in_specs=[row(D_MODEL), row(ATT_W), row(D_INNER), fixed(ATT_W + D_INNER, D_MODEL),
                  fixed(1, D_MODEL), pl.BlockSpec((2, D_MODEL, 128), lambda i: (0, 0, 0)), fixed(1, 128)],
        out_specs=[row(D_MODEL), row(D_MODEL), row(128)],
        out_shape=[jax.ShapeDtypeStruct((m, D_MODEL), F32), jax.ShapeDtypeStruct((m, D_MODEL), BF16),
                   jax.ShapeDtypeStruct((m, 128), F32)],
        compiler_params=_cparams(("parallel",)),
        name="outproj",
    )(x, attn, ssd, w_out_b, ffn_gain.reshape(1, D_MODEL), jnp.stack([w_hi, w_hi_rest]), b_router)


def _ple_kernel(h_ref, y0_ref, y1_ref, wt_ref, p_ref, wg_ref, wp_ref, g_ref, o_ref):
    wt = wt_ref[...]
    h = h_ref[...] + (y0_ref[...] * wt[:, 0:1] + y1_ref[...] * wt[:, 1:2])
    gate = jax.nn.sigmoid(_dot(h.astype(BF16), wg_ref[...]))
    e = _dot(p_ref[...].astype(BF16), wp_ref[...])
    e = e * lax.rsqrt(jnp.mean(e * e, axis=-1, keepdims=True) + EPS) * g_ref[...]
    o_ref[...] = h + gate * e


def moe_combine_ple(h, y01, wts, p, wg_b, wp_b, gain, tm):
    m = h.shape[0]
    nt = m // tm
    return pl.pallas_call(
        _ple_kernel,
        grid=(nt,),
        in_specs=[pl.BlockSpec((tm, D_MODEL), lambda i: (i, 0)),
                  pl.BlockSpec((tm, D_MODEL), lambda i: (i, 0)),
                  pl.BlockSpec((tm, D_MODEL), lambda i: (i + nt, 0)),
                  pl.BlockSpec((tm, 128), lambda i: (i, 0)),
                  pl.BlockSpec((tm, PLE_DIM), lambda i: (i, 0)),
                  pl.BlockSpec((D_MODEL, D_MODEL), lambda i: (0, 0)),
                  pl.BlockSpec((PLE_DIM, D_MODEL), lambda i: (0, 0)),
                  pl.BlockSpec((1, D_MODEL), lambda i: (0, 0))],
        out_specs=pl.BlockSpec((tm, D_MODEL), lambda i: (i, 0)),
        out_shape=jax.ShapeDtypeStruct((m, D_MODEL), F32),
        compiler_params=_cparams(("parallel",)),
        name="moe_combine_ple",
    )(h, y01, y01, wts, p, wg_b, wp_b, gain.reshape(1, D_MODEL))


def _moe_kernel(be_ref, nb_ref, x_ref, wg_ref, wu_ref, wd_ref, y_ref, wg_sc, wu_sc, wd_sc):
    i = pl.program_id(0)

    @pl.when((i == 0) | (be_ref[i] != be_ref[jnp.maximum(i - 1, 0)]))
    def _():
        wg_sc[...] = wg_ref[0].astype(BF16)
        wu_sc[...] = wu_ref[0].astype(BF16)
        wd_sc[...] = wd_ref[0].astype(BF16)

    @pl.when(i < nb_ref[0])
    def _():
        x = x_ref[...]
        a = _dot(x, wg_sc[...])
        hb = (a * jax.nn.sigmoid(a)) * _dot(x, wu_sc[...])
        y_ref[...] = _dot(hb.astype(BF16), wd_sc[...])

    @pl.when(i >= nb_ref[0])
    def _():
        y_ref[...] = jnp.zeros(y_ref.shape, F32)


def moe_experts(xpad, blk_e, n_used, w_gate, w_up, w_down):
    n_blk = xpad.shape[0] // MOE_ROWS
    grid_spec = pltpu.PrefetchScalarGridSpec(
        num_scalar_prefetch=2,
        grid=(n_blk,),
        in_specs=[pl.BlockSpec((MOE_ROWS, D_MODEL), lambda i, be, nb: (i, 0)),
                  pl.BlockSpec((1, D_MODEL, D_EXPERT), lambda i, be, nb: (be[i], 0, 0)),
                  pl.BlockSpec((1, D_MODEL, D_EXPERT), lambda i, be, nb: (be[i], 0, 0)),
                  pl.BlockSpec((1, D_EXPERT, D_MODEL), lambda i, be, nb: (be[i], 0, 0))],
        out_specs=pl.BlockSpec((MOE_ROWS, D_MODEL), lambda i, be, nb: (i, 0)),
        scratch_shapes=[pltpu.VMEM((D_MODEL, D_EXPERT), BF16), pltpu.VMEM((D_MODEL, D_EXPERT), BF16),
                        pltpu.VMEM((D_EXPERT, D_MODEL), BF16)],
    )
    return pl.pallas_call(
        _moe_kernel,
        grid_spec=grid_spec,
        out_shape=jax.ShapeDtypeStruct((n_blk * MOE_ROWS, D_MODEL), F32),
        compiler_params=_cparams(("arbitrary",)),
        name="moe_experts",
    )(blk_e, n_used, xpad, w_gate, w_up, w_down)


def _compress_kernel(seg_ref, w1a_ref, w1b_ref, pe_ref, w1_ref, w2_ref, o_ref):
    seg = seg_ref[0, 0]
    n_seg = seg.shape[0]
    hid0 = _dot(pe_ref[0], w1_ref[0])[0:1]
    p0 = _dot(seg, w1a_ref[0])
    p1 = pltpu.roll(_dot(seg, w1b_ref[0]), n_seg - 1, axis=0)
    hid = hid0 + p0 + p1
    act = hid * jax.nn.sigmoid(hid)
    o_ref[0, 0] = _dot(act.astype(BF16), w2_ref[0]).astype(BF16)


def compress(segs, cmp_pe, cmp_w1, cmp_w2):
    _, r, n_seg, k = segs.shape
    w1 = cmp_w1.astype(BF16)
    pe = jnp.broadcast_to(cmp_pe.reshape(2, 1, L_CMP * HEAD_DIM), (2, 8, L_CMP * HEAD_DIM)).astype(BF16)
    return pl.pallas_call(
        _compress_kernel,
        grid=(2, r),
        in_specs=[pl.BlockSpec((1, 1, n_seg, k), lambda s, i: (s, i, 0, 0)),
                  pl.BlockSpec((1, k, CMP_HID), lambda s, i: (s, 0, 0)),
                  pl.BlockSpec((1, k, CMP_HID), lambda s, i: (s, 1, 0)),
                  pl.BlockSpec((1, 8, 2 * k), lambda s, i: (s, 0, 0)),
                  pl.BlockSpec((1, 2 * k, CMP_HID), lambda s, i: (s, 0, 0)),
                  pl.BlockSpec((1, CMP_HID, HEAD_DIM), lambda s, i: (s, 0, 0))],
        out_specs=pl.BlockSpec((1, 1, n_seg, HEAD_DIM), lambda s, i: (s, i, 0, 0)),
        out_shape=jax.ShapeDtypeStruct((2, r, n_seg, HEAD_DIM), BF16),
        compiler_params=_cparams(("parallel", "parallel")),
        name="compress",
    )(segs, w1, w1, pe, w1, cmp_w2.astype(BF16))


PAGES_PER_STEP = 16


def _page_copies(pt_ref, pool_ref, buf, sem, b, j, slot):
    return [pltpu.make_async_copy(pool_ref.at[pt_ref[b, j * PAGES_PER_STEP + p], pl.ds(0, 2)],
                                  buf.at[slot, p], sem.at[slot]) for p in range(PAGES_PER_STEP)]


def _paged_partials_kernel(pt_ref, pool_ref, wa_ref, wb_ref, p0_ref, p1_ref, buf, sem, x_sc):
    b = pl.program_id(0)
    j = pl.program_id(1)
    n_grp = pl.num_programs(1)
    step = b * n_grp + j
    slot = step % 2

    @pl.when(step == 0)
    def _():
        for c in _page_copies(pt_ref, pool_ref, buf, sem, b, j, slot):
            c.start()

    @pl.when(step + 1 < pl.num_programs(0) * n_grp)
    def _():
        wrap = j + 1 == n_grp
        for c in _page_copies(pt_ref, pool_ref, buf, sem, jnp.where(wrap, b + 1, b), jnp.where(wrap, 0, j + 1),
                              1 - slot):
            c.start()

    for c in _page_copies(pt_ref, pool_ref, buf, sem, b, j, slot):
        c.wait()

    n_seg = PAGES_PER_STEP * PAGE_SIZE // S_CMP
    for kv in range(2):
        for p in range(PAGES_PER_STEP):
            x_sc[p * PAGE_SIZE:(p + 1) * PAGE_SIZE, :] = buf[slot, p, kv].reshape(KV_HEADS * HEAD_DIM, PAGE_SIZE).T
        acc0 = jnp.zeros((n_seg, KV_HEADS * CMP_HID), F32)
        acc1 = jnp.zeros((n_seg, KV_HEADS * CMP_HID), F32)
        for s in range(S_CMP):
            xs = x_sc[pl.ds(s, n_seg, stride=S_CMP), :].astype(BF16)
            acc0 = acc0 + _dot(xs, wa_ref[kv, s])
            acc1 = acc1 + _dot(xs, wb_ref[kv, s])
        p0_ref[kv, 0] = acc0
        p1_ref[kv, 0] = acc1


def _compress_finish_kernel(p0_ref, p1_ref, pe_ref, w1_ref, w2_ref, o_ref):
    n_seg = p0_ref.shape[2]
    hid0 = _dot(pe_ref[0], w1_ref[0])[0:1]
    hid = hid0 + p0_ref[0, 0] + pltpu.roll(p1_ref[0, 0], n_seg - 1, axis=0)
    act = hid * jax.nn.sigmoid(hid)
    o_ref[0, 0] = _dot(act.astype(BF16), w2_ref[0]).astype(BF16)


def _per_group(w):
    zero = jnp.zeros_like(w)
    return jnp.concatenate([jnp.concatenate([w, zero], axis=-1), jnp.concatenate([zero, w], axis=-1)], axis=-2)


def compress_paged(pool_t, page_table, cmp_pe, cmp_w1, cmp_w2):
    bsz, n_pages = page_table.shape
    assert n_pages % PAGES_PER_STEP == 0 and KV_HEADS == 2
    n_grp = n_pages // PAGES_PER_STEP
    seg_step = PAGES_PER_STEP * PAGE_SIZE // S_CMP
    n_seg = n_grp * seg_step
    gw = KV_HEADS * HEAD_DIM
    w1 = _per_group(cmp_w1.astype(BF16).reshape(2, L_CMP // S_CMP, S_CMP, HEAD_DIM, CMP_HID))
    part_shape = jax.ShapeDtypeStruct((2, bsz, n_seg, KV_HEADS * CMP_HID), F32)
    part_spec = pl.BlockSpec((2, 1, seg_step, KV_HEADS * CMP_HID), lambda b, j, pt: (0, b, j, 0))
    wspec = pl.BlockSpec((2, S_CMP, gw, KV_HEADS * CMP_HID), lambda b, j, pt: (0, 0, 0, 0))
    p0, p1 = pl.pallas_call(
        _paged_partials_kernel,
        grid_spec=pltpu.PrefetchScalarGridSpec(
            num_scalar_prefetch=1,
            grid=(bsz, n_grp),
            in_specs=[pl.BlockSpec(memory_space=pl.ANY), wspec, wspec],
            out_specs=[part_spec, part_spec],
            scratch_shapes=[pltpu.VMEM((2, PAGES_PER_STEP, 2, KV_HEADS, HEAD_DIM, PAGE_SIZE), F32),
                            pltpu.SemaphoreType.DMA((2,)),
                            pltpu.VMEM((PAGES_PER_STEP * PAGE_SIZE, gw), F32)]),
        out_shape=[part_shape, part_shape],
        compiler_params=_cparams(("arbitrary", "arbitrary")),
        name="paged_partials",
    )(page_table, pool_t, w1[:, 0], w1[:, 1])
    pe = jnp.broadcast_to(cmp_pe.reshape(2, 1, L_CMP * HEAD_DIM), (2, 8, L_CMP * HEAD_DIM)).astype(BF16)
    w1_both = jnp.concatenate([cmp_w1, cmp_w1], axis=-1).astype(BF16)
    pspec = pl.BlockSpec((1, 1, n_seg, KV_HEADS * CMP_HID), lambda s, i: (s, i, 0, 0))
    return pl.pallas_call(
        _compress_finish_kernel,
        grid=(2, bsz),
        in_specs=[pspec, pspec,
                  pl.BlockSpec((1, 8, L_CMP * HEAD_DIM), lambda s, i: (s, 0, 0)),
                  pl.BlockSpec((1, L_CMP * HEAD_DIM, KV_HEADS * CMP_HID), lambda s, i: (s, 0, 0)),
                  pl.BlockSpec((1, KV_HEADS * CMP_HID, gw), lambda s, i: (s, 0, 0))],
        out_specs=pl.BlockSpec((1, 1, n_seg, gw), lambda s, i: (s, i, 0, 0)),
        out_shape=jax.ShapeDtypeStruct((2, bsz, n_seg, gw), BF16),
        compiler_params=_cparams(("parallel", "parallel")),
        name="compress_finish",
    )(p0, p1, pe, w1_both, _per_group(cmp_w2.astype(BF16)))


FLASH_SPLIT = 4


def _flash_scores(k, q_ref, s_ref):
    cw = q_ref.shape[1] // FLASH_SPLIT
    for i in range(FLASH_SPLIT):
        c = slice(i * cw, (i + 1) * cw)
        s_ref[:, c] = _dot(k, q_ref[:, c])


def _flash_step(s_ref, v_t, m_sc, l_sc, acc_sc, keep=None):
    cw = s_ref.shape[1] // FLASH_SPLIT
    for i in range(FLASH_SPLIT):
        c = slice(i * cw, (i + 1) * cw)
        s_t = s_ref[:, c]
        if keep is not None:
            s_t = jnp.where(keep(i * cw, cw), s_t, NEG)
        m_prev = m_sc[:, c]
        m_new = jnp.maximum(m_prev, jnp.max(s_t, axis=0, keepdims=True))
        alpha = jnp.exp(m_prev - m_new)
        p = jnp.exp(s_t - m_new)
        l_sc[:, c] = alpha * l_sc[:, c] + jnp.sum(p, axis=0, keepdims=True)
        acc_sc[:, c] = alpha * acc_sc[:, c] + _dot(v_t, p.astype(BF16))
        m_sc[:, c] = m_new


def _flash_reset(m_sc, l_sc, acc_sc):
    m_sc[...] = jnp.full(m_sc.shape, NEG, F32)
    l_sc[...] = jnp.zeros(l_sc.shape, F32)
    acc_sc[...] = jnp.zeros(acc_sc.shape, F32)


def _nsa_prompt_kernel(qn_ref, qr_ref, kc_ref, vct_ref, ks_ref, vst_ref, kw_ref, vwt_ref, covert_ref, gate_ref,
                       o_ref, qa_sc, qw_sc, m_sc, l_sc, acc_sc, out_sc, score_sc, rank_sc, s_sc, sw_sc, *, tq, tk, tkw):
    qi = pl.program_id(2)
    pos0 = qi * tq
    w = HPG * tq

    def positions(rows, base):
        kpos = base + lax.broadcasted_iota(jnp.int32, (rows, w), 0)
        qpos = pos0 + (lax.broadcasted_iota(jnp.int32, (rows, w), 1) & (tq - 1))
        return kpos, qpos

    g = pl.program_id(1)

    def gate_row(branch):
        rows = [gate_ref[0, 0, pl.ds(3 * (g * HPG + hh) + branch, 1), :] for hh in range(HPG)]
        return jax.nn.sigmoid(jnp.concatenate(rows, axis=1))

    gate = [gate_row(branch) for branch in range(3)]

    kc = kc_ref[0, 0]
    n_c = kc.shape[0]
    cidx, qpos_c = positions(n_c, 0)
    valid = cidx * S_CMP + (L_CMP - 1) <= qpos_c
    s1 = jnp.where(valid, _dot(kc, qn_ref[0, 0, 0]), NEG)
    e1 = jnp.exp(s1 - jnp.max(s1, axis=0, keepdims=True)) * valid.astype(F32)
    p1b = (e1 * (1.0 / jnp.maximum(jnp.sum(e1, axis=0, keepdims=True), 1e-30))).astype(BF16)
    out_sc[...] = gate[0] * _dot(vct_ref[0, 0], p1b)

    p1_stack = jnp.concatenate([p1b[:, hh * tq:(hh + 1) * tq] for hh in range(HPG)], axis=0)
    imp = _dot(covert_ref[...], p1_stack)
    n_s = imp.shape[0]
    blk = lax.broadcasted_iota(jnp.int32, (n_s, tq), 0)
    cur = (pos0 + lax.broadcasted_iota(jnp.int32, (n_s, tq), 1)) // L_SLC
    vis = blk <= cur
    forced = vis & ((blk == 0) | (blk >= cur - 1))
    score_sc[...] = jnp.where(forced, 1e9, jnp.where(vis, imp, -1.0))
    rank_sc[...] = jnp.zeros(rank_sc.shape, F32)
    last_blk = (pos0 + tq - 1) // L_SLC
    n_oct = n_s // 8
    for oi in range(n_oct):
        @pl.when(oi * 8 <= last_blk)
        def _():
            rows = [slice(8 * oj, 8 * oj + 8) for oj in range(n_oct)]
            parts = [rank_sc[r, :] for r in rows]
            for i in range(8 * oi, 8 * oi + 8):
                c = score_sc[i:i + 1, :]
                for oj, r in enumerate(rows):
                    sj = score_sc[r, :]
                    if oj > oi:
                        beats = c >= sj
                    elif oj < oi:
                        beats = c > sj
                    else:
                        above = lax.broadcasted_iota(jnp.int32, (8, tq), 0) > i - 8 * oi
                        beats = (c > sj) | ((c == sj) & above)
                    parts[oj] = parts[oj] + beats.astype(F32)
            for r, part in zip(rows, parts):
                rank_sc[r, :] = part
    sel_bias = jnp.where(vis & (rank_sc[...] < N_SEL), 0.0, NEG).astype(BF16)
    own = pl.ds(pl.multiple_of(g * HEAD_DIM, HEAD_DIM), HEAD_DIM)
    other = pl.ds(pl.multiple_of((1 - g) * HEAD_DIM, HEAD_DIM), HEAD_DIM)
    qa_sc[other, :] = jnp.zeros((HEAD_DIM, w), BF16)
    qw_sc[other, :] = jnp.zeros((HEAD_DIM, w), BF16)
    qa_sc[own, :] = qr_ref[0, 0, 0]
    qw_sc[own, :] = qr_ref[0, 0, 0]
    qa_sc[pl.ds(pl.multiple_of((1 - g) * HEAD_DIM, HEAD_DIM), n_s), :] = jnp.concatenate([sel_bias] * HPG, axis=1)

    def sel_keys(kt):
        return ks_ref[0, 0, pl.ds(pl.multiple_of(kt * tk, tk), tk), :]

    def win_keys(kt):
        return kw_ref[pl.ds(pl.multiple_of(kt * tkw, tkw), tkw), :]

    def chunk_positions(rows, base, lane0, n_lanes):
        kpos = base + lax.broadcasted_iota(jnp.int32, (rows, n_lanes), 0)
        qpos = pos0 + ((lane0 + lax.broadcasted_iota(jnp.int32, (rows, n_lanes), 1)) & (tq - 1))
        return kpos, qpos

    def causal(rows, base):
        def keep(lane0, n_lanes):
            kpos, qpos = chunk_positions(rows, base, lane0, n_lanes)
            return kpos <= qpos
        return keep

    _flash_reset(m_sc, l_sc, acc_sc)
    n_full = pos0 // tk
    _flash_scores(sel_keys(0), qa_sc, s_sc.at[0])

    def sel_body(kt, carry):
        for parity in range(2):
            @pl.when(kt % 2 == parity)
            def _():
                _flash_scores(sel_keys(kt + 1), qa_sc, s_sc.at[1 - parity])
                _flash_step(s_sc.at[parity], vst_ref[0, 0, kt], m_sc, l_sc, acc_sc)
        return carry

    lax.fori_loop(0, n_full, sel_body, 0)

    n_inner = WINDOW // tkw - 1
    win_tiles = [qi - d for d in range(n_inner + 2)]
    for idx, kt in enumerate(win_tiles):
        _flash_scores(win_keys(jnp.maximum(kt, 0)), qw_sc, sw_sc.at[idx])

    _flash_step(s_sc.at[n_full % 2], vst_ref[0, 0, n_full], m_sc, l_sc, acc_sc, causal(tk, n_full * tk))
    out_sc[...] = out_sc[...] + (gate[1] * (1.0 / l_sc[...])) * acc_sc[...]

    _flash_reset(m_sc, l_sc, acc_sc)
    _flash_step(sw_sc.at[0], vwt_ref[0, 0, qi], m_sc, l_sc, acc_sc, causal(tkw, pos0))
    for idx in range(1, n_inner + 1):
        kt = win_tiles[idx]
        _flash_step(sw_sc.at[idx], vwt_ref[0, 0, jnp.maximum(kt, 0)], m_sc, l_sc, acc_sc,
                    lambda lane0, n_lanes, kt=kt: kt >= 0)
    kt_far = win_tiles[n_inner + 1]

    def far_keep(lane0, n_lanes):
        kpos, qpos = chunk_positions(tkw, kt_far * tkw, lane0, n_lanes)
        return (kpos > qpos - WINDOW) & (kt_far >= 0)

    _flash_step(sw_sc.at[n_inner + 1], vwt_ref[0, 0, jnp.maximum(kt_far, 0)], m_sc, l_sc, acc_sc, far_keep)

    o = out_sc[...] + (gate[2] * (1.0 / l_sc[...])) * acc_sc[...]
    for pair in range(HPG // 2):
        two = jnp.concatenate([o[:, (2 * pair + e) * tq:(2 * pair + e + 1) * tq] for e in range(2)], axis=0)
        o_ref[:, 128 * pair:128 * (pair + 1)] = two.T.astype(o_ref.dtype)


NSA_TQ, NSA_TK, NSA_TKW = 256, 256, 256


def nsa_prompt(qn_t, qr_t, kc, vc_t, ks_aug, vs_t, kw, vw_t, sm_t, bsz, t):
    tq, tk, tkw = NSA_TQ, NSA_TK, NSA_TKW
    assert tq == tkw and tk % tq == 0 and WINDOW % tkw == 0
    n_c = kc.shape[1]
    n_s = t // L_SLC
    assert n_s <= HEAD_DIM and KV_HEADS == 2
    w = HPG * tq
    c0 = np.arange(n_c)[None, :] * S_CMP
    s0 = np.arange(n_s)[:, None] * L_SLC
    cover_t = (np.maximum(np.minimum(c0 + L_CMP, s0 + L_SLC) - np.maximum(c0, s0), 0) / L_CMP).astype(np.float32)
    cover_t = np.tile(cover_t, (1, HPG))
    qspec = pl.BlockSpec((1, 1, 1, HEAD_DIM, w), lambda b, g, i: (b, g, i, 0, 0))
    kern = functools.partial(_nsa_prompt_kernel, tq=tq, tk=tk, tkw=tkw)
    return pl.pallas_call(
        kern,
        grid=(bsz, KV_HEADS, t // tq),
        in_specs=[qspec, qspec,
                  pl.BlockSpec((1, 1, n_c, HEAD_DIM), lambda b, g, i: (b, g, 0, 0)),
                  pl.BlockSpec((1, 1, HEAD_DIM, n_c), lambda b, g, i: (b, g, 0, 0)),
                  pl.BlockSpec((1, 1, t, 2 * HEAD_DIM), lambda b, g, i: (b, g, 0, 0)),
                  pl.BlockSpec((1, 1, t // tk, HEAD_DIM, tk), lambda b, g, i: (b, g, 0, 0, 0)),
                  pl.BlockSpec((t, KV_HEADS * HEAD_DIM), lambda b, g, i: (b, 0)),
                  pl.BlockSpec((1, 1, t // tkw, HEAD_DIM, tkw), lambda b, g, i: (b, g, 0, 0, 0)),
                  pl.BlockSpec((n_s, HPG * n_c), lambda b, g, i: (0, 0)),
                  pl.BlockSpec((1, 1, 32, tq), lambda b, g, i: (b, i, 0, 0))],
        out_specs=pl.BlockSpec((tq, HPG * HEAD_DIM), lambda b, g, i: (b * (t // tq) + i, g)),
        out_shape=jax.ShapeDtypeStruct((bsz * t, ATT_W), BF16),
        scratch_shapes=[pltpu.VMEM((2 * HEAD_DIM, w), BF16), pltpu.VMEM((2 * HEAD_DIM, w), BF16),
                        pltpu.VMEM((1, w), F32), pltpu.VMEM((1, w), F32),
                        pltpu.VMEM((HEAD_DIM, w), F32), pltpu.VMEM((HEAD_DIM, w), F32),
                        pltpu.VMEM((n_s, tq), F32), pltpu.VMEM((n_s, tq), F32),
                        pltpu.VMEM((2, tk, w), F32), pltpu.VMEM((WINDOW // tkw + 1, tkw, w), F32)],
        compiler_params=_cparams(("parallel", "parallel", "arbitrary")),
        name="nsa_prompt",
    )(qn_t, qr_t, kc.reshape(bsz, KV_HEADS, n_c, HEAD_DIM), vc_t.reshape(bsz, KV_HEADS, HEAD_DIM, n_c),
      ks_aug, vs_t, kw, vw_t, jnp.asarray(cover_t, BF16), sm_t)


def _sample_select_kernel(qn_ref, kc_ref, vc_ref, cover_ref, pick_ref, oc_ref, sb_ref, *, q_off, lq, n_cmp, n_slc):
    rows = HPG * lq
    for g in range(KV_HEADS):
        kc = kc_ref[0]
        n_c = kc.shape[0]
        vc_g = _dot(vc_ref[0], pick_ref[g]).astype(BF16)
        cidx = lax.broadcasted_iota(jnp.int32, (rows, n_c), 1)
        qpos = q_off + (lax.broadcasted_iota(jnp.int32, (rows, n_c), 0) & (lq - 1))
        valid = (cidx * S_CMP + (L_CMP - 1) <= qpos) & (cidx < n_cmp)
        s1 = jnp.where(valid, _dot_nt(qn_ref[0, g], kc), NEG)
        e1 = jnp.exp(s1 - jnp.max(s1, axis=-1, keepdims=True)) * valid.astype(F32)
        p1b = (e1 * (1.0 / jnp.maximum(jnp.sum(e1, axis=-1, keepdims=True), 1e-30))).astype(BF16)
        oc_ref[0, g] = _dot(p1b, vc_g)
        imp4 = _dot(p1b, cover_ref[...])
        imp = imp4[0:lq]
        for hh in range(1, HPG):
            imp = imp + imp4[hh * lq:(hh + 1) * lq]
        n_sp = imp.shape[1]
        blk = lax.broadcasted_iota(jnp.int32, (lq, n_sp), 1)
        cur = (q_off + lax.broadcasted_iota(jnp.int32, (lq, n_sp), 0)) // L_SLC
        vis = (blk <= cur) & (blk < n_slc)
        forced = vis & ((blk == 0) | (blk >= cur - 1))
        score = jnp.where(forced, 1e9, jnp.where(vis, imp, -1.0))
        rank = jnp.zeros((lq, n_sp), F32)
        for i in range(n_slc):
            c = score[:, i:i + 1]
            beats = (c > score) | ((c == score) & (blk > i))
            rank = rank + beats.astype(F32)
        sb_ref[0, g] = jnp.where(vis & (rank < min(N_SEL, n_slc)), 0.0, NEG)


def _sample_attend_kernel(pt_ref, pool_ref, qr_ref, sb_ref, exp_ref, oc_ref, gate_ref, kn_ref, vn_ref, kw_ref, vw_ref,
                          o_ref, buf, sem, m_sc, l_sc, acc_sc, *, q_off, lq, n_win):
    b = pl.program_id(0)
    j = pl.program_id(1)
    n_grp = pl.num_programs(1)
    step = b * n_grp + j
    slot = step % 2
    rows = HPG * lq

    def copies(bb, jj, sl):
        return [pltpu.make_async_copy(pool_ref.at[pt_ref[bb, jj * PAGES_PER_STEP + p], pl.ds(2, 2)],
                                      buf.at[sl, p], sem.at[sl]) for p in range(PAGES_PER_STEP)]

    @pl.when(step == 0)
    def _():
        for c in copies(b, j, slot):
            c.start()

    @pl.when(step + 1 < pl.num_programs(0) * n_grp)
    def _():
        wrap = j + 1 == n_grp
        for c in copies(jnp.where(wrap, b + 1, b), jnp.where(wrap, 0, j + 1), 1 - slot):
            c.start()

    for c in copies(b, j, slot):
        c.wait()

    @pl.when(j == 0)
    def _():
        m_sc[...] = jnp.full(m_sc.shape, NEG, F32)
        l_sc[...] = jnp.zeros(l_sc.shape, F32)
        acc_sc[...] = jnp.zeros(acc_sc.shape, F32)

    def update(g, s, v, v_transposed):
        m_prev = m_sc[g]
        m_new = jnp.maximum(m_prev, jnp.max(s, axis=-1, keepdims=True))
        alpha = jnp.exp(m_prev - m_new)
        p = jnp.exp(s - m_new)
        l_sc[g] = alpha * l_sc[g] + jnp.sum(p, axis=-1, keepdims=True)
        pv = _dot_nt(p.astype(BF16), v) if v_transposed else _dot(p.astype(BF16), v)
        acc_sc[g] = alpha * acc_sc[g] + pv
        m_sc[g] = m_new

    for g in range(KV_HEADS):
        k_t = jnp.concatenate([buf[slot, p, 0, g] for p in range(PAGES_PER_STEP)], axis=1).astype(BF16)
        v_t = jnp.concatenate([buf[slot, p, 1, g] for p in range(PAGES_PER_STEP)], axis=1).astype(BF16)
        sb = sb_ref[0, g, 0]
        bias = _dot(jnp.concatenate([sb] * HPG, axis=0).astype(BF16), exp_ref[...])
        update(g, _dot(qr_ref[0, g], k_t) + bias, v_t, True)

    @pl.when(j == n_grp - 1)
    def _():
        for g in range(KV_HEADS):
            qr = qr_ref[0, g]
            kn = kn_ref[0, g]
            kidx = lax.broadcasted_iota(jnp.int32, (rows, kn.shape[0]), 1)
            qidx = lax.broadcasted_iota(jnp.int32, (rows, kn.shape[0]), 0) & (lq - 1)
            update(g, jnp.where((kidx <= qidx) & (kidx < lq), _dot_nt(qr, kn), NEG), vn_ref[0, g], False)
            o_s = acc_sc[g] * (1.0 / l_sc[g])
            kw = kw_ref[0, g]
            widx = lax.broadcasted_iota(jnp.int32, (rows, kw.shape[0]), 1)
            kpos = q_off + lq - n_win + widx
            qpos = q_off + (lax.broadcasted_iota(jnp.int32, (rows, kw.shape[0]), 0) & (lq - 1))
            ok = (widx < n_win) & (kpos <= qpos) & (kpos > qpos - WINDOW) & (kpos >= 0)
            s3 = jnp.where(ok, _dot_nt(qr, kw), NEG)
            e3 = jnp.exp(s3 - jnp.max(s3, axis=-1, keepdims=True)) * ok.astype(F32)
            p3 = e3 * (1.0 / jnp.maximum(jnp.sum(e3, axis=-1, keepdims=True), 1e-30))
            o_w = _dot(p3.astype(BF16), vw_ref[0, g])
            gate = jax.nn.sigmoid(gate_ref[0, g])
            o_ref[0, g] = gate[:, 0:1] * oc_ref[0, g] + gate[:, 1:2] * o_s + gate[:, 2:3] * o_w


def nsa_sample(q, q_rot, kvc, kv_new, win_cat, pool_t, page_table, gate_logits, q_off):
    bsz, lq = q.shape[:2]
    assert lq & (lq - 1) == 0 and lq <= L_SLC and q_off % (PAGES_PER_STEP * PAGE_SIZE) == 0
    rows = HPG * lq
    scale = HEAD_DIM ** -0.5
    n_seg = kvc.shape[2]
    t_kv = q_off + lq
    n_cmp = (t_kv - L_CMP) // S_CMP + 1
    n_slc = -(-t_kv // L_SLC)
    n_sp = -(-n_slc // 128) * 128
    n_grp = page_table.shape[1] // PAGES_PER_STEP
    blk_step = PAGES_PER_STEP * PAGE_SIZE // L_SLC
    key_step = PAGES_PER_STEP * PAGE_SIZE

    def rows_major(a):
        return a.reshape(bsz, lq, KV_HEADS, HPG, -1).transpose(0, 2, 3, 1, 4).reshape(bsz, KV_HEADS, rows, -1)

    def keys_major(a, n_pad):
        a = a.astype(BF16).transpose(0, 2, 1, 3)
        return jnp.pad(a, ((0, 0), (0, 0), (0, n_pad - a.shape[2]), (0, 0)))

    qn_r = rows_major((q * scale).astype(BF16))
    qr_r = rows_major((q_rot * scale).astype(BF16))
    c0 = np.arange(n_seg)[:, None] * S_CMP
    s0 = np.arange(n_sp)[None, :] * L_SLC
    cover = np.maximum(np.minimum(c0 + L_CMP, s0 + L_SLC) - np.maximum(c0, s0), 0) / L_CMP
    cover = cover * (np.arange(n_seg)[:, None] < n_cmp) * (np.arange(n_sp)[None, :] < n_slc)
    gw = KV_HEADS * HEAD_DIM
    zeros_q = jnp.zeros_like(qn_r[:, 0])
    qn_wide = jnp.stack([jnp.concatenate([qn_r[:, 0], zeros_q], axis=-1),
                         jnp.concatenate([zeros_q, qn_r[:, 1]], axis=-1)], axis=1)
    pick = (np.arange(gw)[None, :, None] == np.arange(KV_HEADS)[:, None, None] * HEAD_DIM
            + np.arange(HEAD_DIM)[None, None, :]).astype(np.float32)
    spec_q = pl.BlockSpec((1, KV_HEADS, rows, HEAD_DIM), lambda b: (b, 0, 0, 0))
    o_c, sel_bias = pl.pallas_call(
        functools.partial(_sample_select_kernel, q_off=q_off, lq=lq, n_cmp=n_cmp, n_slc=n_slc),
        grid=(bsz,),
        in_specs=[pl.BlockSpec((1, KV_HEADS, rows, gw), lambda b: (b, 0, 0, 0)),
                  pl.BlockSpec((1, n_seg, gw), lambda b: (b, 0, 0)),
                  pl.BlockSpec((1, n_seg, gw), lambda b: (b, 0, 0)),
                  pl.BlockSpec((n_seg, n_sp), lambda b: (0, 0)),
                  pl.BlockSpec((KV_HEADS, gw, HEAD_DIM), lambda b: (0, 0, 0))],
        out_specs=[spec_q, pl.BlockSpec((1, KV_HEADS, lq, n_sp), lambda b: (b, 0, 0, 0))],
        out_shape=[jax.ShapeDtypeStruct((bsz, KV_HEADS, rows, HEAD_DIM), F32),
                   jax.ShapeDtypeStruct((bsz, KV_HEADS, lq, n_sp), F32)],
        compiler_params=_cparams(("parallel",)),
        name="sample_select",
    )(qn_wide, kvc[0], kvc[1], jnp.asarray(cover, BF16), jnp.asarray(pick, BF16))

    sb_steps = sel_bias[..., :n_grp * blk_step].reshape(bsz, KV_HEADS, lq, n_grp, blk_step).transpose(0, 1, 3, 2, 4)
    expand = (np.arange(key_step)[None, :] // L_SLC == np.arange(blk_step)[:, None]).astype(np.float32)
    gates = jnp.pad(rows_major(gate_logits.reshape(bsz, lq, -1)), ((0, 0), (0, 0), (0, 0), (0, 128 - 3)))
    n_win = win_cat.shape[1]
    n_wp = -(-n_win // 128) * 128
    spec2 = lambda shape: pl.BlockSpec((1,) + shape, lambda b, j, pt: (b,) + (0,) * len(shape))
    attn = pl.pallas_call(
        functools.partial(_sample_attend_kernel, q_off=q_off, lq=lq, n_win=n_win),
        grid_spec=pltpu.PrefetchScalarGridSpec(
            num_scalar_prefetch=1,
            grid=(bsz, n_grp),
            in_specs=[pl.BlockSpec(memory_space=pl.ANY),
                      spec2((KV_HEADS, rows, HEAD_DIM)),
                      pl.BlockSpec((1, KV_HEADS, 1, lq, blk_step), lambda b, j, pt: (b, 0, j, 0, 0)),
                      pl.BlockSpec((blk_step, key_step), lambda b, j, pt: (0, 0)),
                      spec2((KV_HEADS, rows, HEAD_DIM)),
                      spec2((KV_HEADS, rows, 128)),
                      spec2((KV_HEADS, 128, HEAD_DIM)), spec2((KV_HEADS, 128, HEAD_DIM)),
                      spec2((KV_HEADS, n_wp, HEAD_DIM)), spec2((KV_HEADS, n_wp, HEAD_DIM))],
            out_specs=spec2((KV_HEADS, rows, HEAD_DIM)),
            scratch_shapes=[pltpu.VMEM((2, PAGES_PER_STEP, 2, KV_HEADS, HEAD_DIM, PAGE_SIZE), F32),
                            pltpu.SemaphoreType.DMA((2,)),
                            pltpu.VMEM((KV_HEADS, rows, 1), F32), pltpu.VMEM((KV_HEADS, rows, 1), F32),
                            pltpu.VMEM((KV_HEADS, rows, HEAD_DIM), F32)]),
        out_shape=jax.ShapeDtypeStruct((bsz, KV_HEADS, rows, HEAD_DIM), F32),
        compiler_params=_cparams(("arbitrary", "arbitrary")),
        name="sample_attend",
    )(page_table, pool_t, qr_r, sb_steps, jnp.asarray(expand, BF16), o_c, gates,
      keys_major(kv_new[:, :, 2], 128), keys_major(kv_new[:, :, 3], 128),
      keys_major(win_cat[:, :, 0], n_wp), keys_major(win_cat[:, :, 1], n_wp))
    attn = attn.reshape(bsz, KV_HEADS, HPG, lq, HEAD_DIM).transpose(0, 3, 1, 2, 4)
    return attn.astype(BF16).reshape(bsz * lq, ATT_W)


def _ssd_kernel(d_ref, x_ref, bt_ref, c_ref, acol_ref, arow_ref, dt_ref, s0_ref, y_ref, sout_ref, s_sc, *, cl):
    ci = pl.program_id(1)

    @pl.when(ci == 0)
    def _():
        s_sc[...] = s0_ref[0]

    li = lax.broadcasted_iota(jnp.int32, (cl, cl), 0)
    si = lax.broadcasted_iota(jnp.int32, (cl, cl), 1)
    causal = li >= si
    acum_col = _dot_hi(causal.astype(F32), acol_ref[0])
    acum_row = _dot_hi(arow_ref[0], (li <= si).astype(F32))
    dt = dt_ref[0]
    for g in range(SSM_GROUPS):
        cg = c_ref[0, g].astype(BF16)
        btg = bt_ref[0, g]
        cb = _dot(cg, btg.astype(BF16))
        for hh in range(SSM_HEADS // SSM_GROUPS):
            h = g * (SSM_HEADS // SSM_GROUPS) + hh
            ac = acum_col[:, h:h + 1]
            ar = acum_row[h:h + 1, :]
            decay = jnp.exp(jnp.where(causal, ac - ar, NEG))
            xs = x_ref[0, h]
            xd = (xs * dt[:, h:h + 1]).astype(BF16)
            st = s_sc[h]
            y = _dot((cb * decay).astype(BF16), xd) + _dot(cg, st.astype(BF16)) * jnp.exp(ac)
            y_ref[0, h] = y + d_ref[h] * xs
            a_last = ar[:, cl - 1:cl]
            snew = _dot((btg * jnp.exp(a_last - ar)).astype(BF16), xd)
            s_sc[h] = st * jnp.exp(a_last) + snew

    @pl.when(ci == pl.num_programs(1) - 1)
    def _():
        sout_ref[0] = s_sc[...]


def ssd_scan(ssm_d, x_hm, b_t, c, a_col, a_row, dt_col, s0_t, cl):
    bsz, nh, t, p = x_hm.shape
    n = D_STATE
    nc = t // cl
    kern = functools.partial(_ssd_kernel, cl=cl)
    return pl.pallas_call(
        kern,
        grid=(bsz, nc),
        in_specs=[pl.BlockSpec(memory_space=pltpu.SMEM),
                  pl.BlockSpec((1, nh, cl, p), lambda b, c_: (b, 0, c_, 0)),
                  pl.BlockSpec((1, SSM_GROUPS, n, cl), lambda b, c_: (b, 0, 0, c_)),
                  pl.BlockSpec((1, SSM_GROUPS, cl, n), lambda b, c_: (b, 0, c_, 0)),
                  pl.BlockSpec((1, cl, 128), lambda b, c_: (b, c_, 0)),
                  pl.BlockSpec((1, 8, cl), lambda b, c_: (b, 0, c_)),
                  pl.BlockSpec((1, cl, 128), lambda b, c_: (b, c_, 0)),
                  pl.BlockSpec((1, nh, n, p), lambda b, c_: (b, 0, 0, 0))],
        out_specs=[pl.BlockSpec((1, nh, cl, p), lambda b, c_: (b, 0, c_, 0)),
                   pl.BlockSpec((1, nh, n, p), lambda b, c_: (b, 0, 0, 0))],
        out_shape=[jax.ShapeDtypeStruct((bsz, nh, t, p), F32), jax.ShapeDtypeStruct((bsz, nh, n, p), F32)],
        scratch_shapes=[pltpu.VMEM((nh, n, p), F32)],
        compiler_params=_cparams(("parallel", "arbitrary")),
        name="ssd_scan",
    )(ssm_d, x_hm, b_t, c, a_col, a_row, dt_col, s0_t)


def _ssd_prompt_kernel(d_ref, xbc_ref, sm_ref, z_ref, cw_ref, cb_ref, dtb_ref, ah_ref, gn_ref, o_ref, sout_ref,
                       s_sc, tail_sc, *, cl):
    ci = pl.program_id(1)

    @pl.when(ci == 0)
    def _():
        s_sc[...] = jnp.zeros(s_sc.shape, F32)
        tail_sc[...] = jnp.zeros(tail_sc.shape, F32)

    xb = xbc_ref[...]
    prev = tail_sc[...]
    row8 = lax.broadcasted_iota(jnp.int32, (8, CONV_DIM), 0)

    def shifted(j):
        body = pltpu.roll(xb, j, axis=0)
        top = jnp.where(row8 < j, pltpu.roll(prev, j, axis=0), body[0:8])
        return jnp.concatenate([top, body[8:]], axis=0)

    conv = cb_ref[...] + shifted(CONV_W - 1) * cw_ref[0:1]
    for j in range(1, CONV_W):
        conv = conv + (shifted(CONV_W - 1 - j) if j < CONV_W - 1 else xb) * cw_ref[j:j + 1]
    tail_sc[...] = xb[cl - 8:cl]
    act = conv * jax.nn.sigmoid(conv)
    n_bc = SSM_GROUPS * D_STATE
    bmat = act[:, D_INNER:D_INNER + n_bc]
    cmat = act[:, D_INNER + n_bc:D_INNER + 2 * n_bc]
    x_dt = sm_ref[...] + dtb_ref[...]
    dt = jnp.maximum(x_dt, 0.0) + jnp.log(1.0 + jnp.exp(-jnp.abs(x_dt)))
    a = ah_ref[...] * dt

    li = lax.broadcasted_iota(jnp.int32, (cl, cl), 0)
    si = lax.broadcasted_iota(jnp.int32, (cl, cl), 1)
    causal = li >= si
    acum_col = _dot_hi(causal.astype(F32), a)
    acum_row = _dot_hi(a.T[N_GATES:N_GATES + SSM_HEADS], (li <= si).astype(F32))
    bt = bmat.T
    bt_b = bt.astype(BF16)
    lane = lax.broadcasted_iota(jnp.int32, (cl, 128), 1)
    low = lane < SSM_HEAD_DIM
    low_s = lax.broadcasted_iota(jnp.int32, (128, 128), 1) < SSM_HEAD_DIM

    def col(arr, h):
        return arr[:, N_GATES + h:N_GATES + h + 1]

    per_group = SSM_HEADS // SSM_GROUPS
    cg_of, cb_of = {}, {}
    ys = []
    for j in range(SSM_HEADS // 2):
        ha, hb = 2 * j, 2 * j + 1
        g = ha // per_group
        if g not in cg_of:
            in_group = (lane >= g * D_STATE) & (lane < (g + 1) * D_STATE)
            cg_of[g] = jnp.where(in_group, cmat, 0.0).astype(BF16)
            cb_of[g] = _dot(cg_of[g], bt_b)
        cg, cb = cg_of[g], cb_of[g]
        xs_pair = act[:, 128 * j:128 * (j + 1)]
        xd = (xs_pair * jnp.where(low, col(dt, ha), col(dt, hb))).astype(BF16)
        st = s_sc[j]
        y_off = _dot(cg, st.astype(BF16)) * jnp.where(low, jnp.exp(col(acum_col, ha)), jnp.exp(col(acum_col, hb)))
        y_h, s_h, dec_h = [], [], []
        for h in (ha, hb):
            ar = acum_row[h:h + 1, :]
            decay = jnp.exp(jnp.where(causal, col(acum_col, h) - ar, NEG))
            y_h.append(_dot((cb * decay).astype(BF16), xd))
            a_last = ar[:, cl - 1:cl]
            s_h.append(_dot((bt * jnp.exp(a_last - ar)).astype(BF16), xd))
            dec_h.append(jnp.exp(a_last))
        skip = jnp.where(low[0:1], d_ref[ha], d_ref[hb])
        ys.append(jnp.where(low, y_h[0], y_h[1]) + y_off + skip * xs_pair)
        s_sc[j] = st * jnp.where(low_s, dec_h[0], dec_h[1]) + jnp.where(low_s, s_h[0], s_h[1])

    z = z_ref[...]
    gated = jnp.concatenate(ys, axis=1) * (z * jax.nn.sigmoid(z))
    out = gated * lax.rsqrt(jnp.mean(gated * gated, axis=-1, keepdims=True) + EPS) * gn_ref[...]
    o_ref[...] = out.astype(o_ref.dtype)

    @pl.when(ci == pl.num_programs(1) - 1)
    def _():
        sout_ref[0] = s_sc[...]


def ssd_prompt(xbc, sm, z, w, bsz, t):
    cl = SSM_CHUNK
    nc = t // cl
    lanes = lambda v: jnp.zeros((1, 128), F32).at[0, N_GATES:N_GATES + SSM_HEADS].set(v)
    row = lambda width: pl.BlockSpec((cl, width), lambda b, c: (b * nc + c, 0))
    fixed = lambda r, c_: pl.BlockSpec((r, c_), lambda b, c: (0, 0))
    n_pair = SSM_HEADS // 2
    out, s_pairs = pl.pallas_call(
        functools.partial(_ssd_prompt_kernel, cl=cl),
        grid=(bsz, nc),
        in_specs=[pl.BlockSpec(memory_space=pltpu.SMEM), row(CONV_DIM), row(128), row(D_INNER),
                  fixed(CONV_W, CONV_DIM), fixed(1, CONV_DIM), fixed(1, 128), fixed(1, 128), fixed(1, D_INNER)],
        out_specs=[row(D_INNER), pl.BlockSpec((1, n_pair, 128, 128), lambda b, c: (b, 0, 0, 0))],
        out_shape=[jax.ShapeDtypeStruct((bsz * t, D_INNER), BF16),
                   jax.ShapeDtypeStruct((bsz, n_pair, 128, 128), F32)],
        scratch_shapes=[pltpu.VMEM((n_pair, 128, 128), F32), pltpu.VMEM((8, CONV_DIM), F32)],
        compiler_params=_cparams(("parallel", "arbitrary")),
        name="ssd_prompt",
    )(w['ssm_d'], xbc, sm, z, w['conv_w'], w['conv_b'].reshape(1, CONV_DIM), lanes(w['dt_bias']),
      lanes(-jnp.exp(w['a_log'])), w['ssm_norm'].reshape(1, D_INNER))
    s6 = s_pairs.reshape(bsz, n_pair, SSM_GROUPS, D_STATE, 2, SSM_HEAD_DIM)
    per_group = SSM_HEADS // SSM_GROUPS
    heads = [s6[:, h // 2, h // per_group, :, h % 2, :] for h in range(SSM_HEADS)]
    return out, jnp.stack(heads, axis=1).transpose(0, 1, 3, 2)


def _rmsnorm(x, g):
    return x * lax.rsqrt(jnp.mean(x * x, axis=-1, keepdims=True) + EPS) * g


def _rope(x, pos):
    half = HEAD_DIM // 2
    inv = ROPE_THETA ** (-jnp.arange(half, dtype=F32) / half)
    ang = pos.astype(F32)[:, None] * inv[None, :]
    cos, sin = jnp.cos(ang)[:, None, :], jnp.sin(ang)[:, None, :]
    x1, x2 = x[..., :half], x[..., half:]
    return jnp.concatenate([x1 * cos - x2 * sin, x2 * cos + x1 * sin], axis=-1)


def _ssd_inputs(xbc_all, dt_raw, w, t_pad):
    bsz = xbc_all.shape[0]
    length = xbc_all.shape[1] - (CONV_W - 1)
    conv = w['conv_b'] + xbc_all[:, 0:length] * w['conv_w'][0]
    for j in range(1, CONV_W):
        conv = conv + xbc_all[:, j:j + length] * w['conv_w'][j]
    act = jax.nn.silu(conv)
    dt = jax.nn.softplus(dt_raw + w['dt_bias'])
    a = -jnp.exp(w['a_log']) * dt
    pad = t_pad - length
    if pad:
        act = jnp.pad(act, ((0, 0), (0, pad), (0, 0)))
        dt = jnp.pad(dt, ((0, 0), (0, pad), (0, 0)))
        a = jnp.pad(a, ((0, 0), (0, pad), (0, 0)))
    xs = act[..., :D_INNER].reshape(bsz, t_pad, SSM_HEADS, SSM_HEAD_DIM).transpose(0, 2, 1, 3)
    b_in = act[..., D_INNER:D_INNER + SSM_GROUPS * D_STATE].reshape(bsz, t_pad, SSM_GROUPS, D_STATE)
    c_in = act[..., D_INNER + SSM_GROUPS * D_STATE:].reshape(bsz, t_pad, SSM_GROUPS, D_STATE)
    lane_pad = ((0, 0), (0, 0), (0, 128 - SSM_HEADS))
    return (xs, b_in.transpose(0, 2, 3, 1), c_in.transpose(0, 2, 1, 3), jnp.pad(a, lane_pad),
            a.transpose(0, 2, 1), jnp.pad(dt, lane_pad))


def _moe(f_b, logits, w):
    n_tok = f_b.shape[0]
    pg = jax.nn.softmax(logits[:, :N_EXPERT_GROUPS], axis=-1)
    g_sel = jnp.argmax(pg, axis=-1)
    g_w = jnp.max(pg, axis=-1)
    el = logits[:, N_EXPERT_GROUPS:N_EXPERT_GROUPS + N_EXPERTS].reshape(n_tok, N_EXPERT_GROUPS, EXPERTS_PER_GROUP)
    el = jnp.take_along_axis(el, g_sel[:, None, None], axis=1)[:, 0]
    top_p, top_i = lax.top_k(jax.nn.softmax(el, axis=-1), TOP_K)
    wts = g_w[:, None] * top_p / jnp.sum(top_p, axis=-1, keepdims=True)
    eid = (g_sel[:, None] * EXPERTS_PER_GROUP + top_i).reshape(-1).astype(jnp.int32)
    n_asg = eid.shape[0]
    onehot = (eid[:, None] == jnp.arange(N_EXPERTS, dtype=jnp.int32)[None, :]).astype(jnp.int32)
    within = jnp.take_along_axis(jnp.cumsum(onehot, axis=0), eid[:, None], axis=1)[:, 0] - 1
    sizes = jnp.sum(onehot, axis=0)
    padded = (sizes + MOE_ROWS - 1) // MOE_ROWS * MOE_ROWS
    pend = jnp.cumsum(padded)
    dest = (pend - padded)[eid] + within
    n_blk = -(-n_asg // MOE_ROWS) + N_EXPERTS
    src = jnp.zeros((n_blk * MOE_ROWS,), jnp.int32).at[dest].set(jnp.arange(n_asg, dtype=jnp.int32) // TOP_K)
    xpad = f_b[src]
    blk_start = jnp.arange(n_blk, dtype=jnp.int32) * MOE_ROWS
    blk_e = jnp.minimum(jnp.sum((pend[None, :] <= blk_start[:, None]).astype(jnp.int32), axis=1), N_EXPERTS - 1)
    n_used = (pend[-1] // MOE_ROWS).astype(jnp.int32).reshape(1)
    ypad = moe_experts(xpad, blk_e, n_used, w['w_gate'], w['w_up'], w['w_down'])
    y01 = ypad[dest.reshape(n_tok, TOP_K).T.reshape(-1)]
    return y01, jnp.pad(wts, ((0, 0), (0, 128 - TOP_K)))


def _token_tail(x2, mix_in, p2, w, tm):
    pad = 128 - N_EXPERT_GROUPS - N_EXPERTS
    w_router = jnp.concatenate([w['w_rg'], w['w_re'], jnp.zeros((D_MODEL, pad), F32)], axis=1)
    b_router = jnp.concatenate([w['b_rg'], w['b_re'], jnp.zeros((pad,), F32)]).reshape(1, 128)
    h, f_b, logits = outproj(x2, mix_in[0], mix_in[1], w['w_out_b'], w['ffn_norm'], w_router, b_router, tm)
    y01, wts = _moe(f_b, logits, w)
    return moe_combine_ple(h, y01, wts, p2, w['wpg_b'], w['wpp_b'], w['ple_norm'], tm)


def _ssd_finish(y_hm, z, w, length):
    bsz = y_hm.shape[0]
    y = y_hm[:, :, :length].transpose(0, 2, 1, 3).reshape(bsz, length, D_INNER)
    gated = y * jax.nn.silu(z)
    return _rmsnorm(gated, w['ssm_norm']).astype(BF16)


def _prompt_group(x, p, w):
    bsz, t, _ = x.shape
    m = bsz * t
    z, xbc, sm, sm_t, kv, win, qn_t, qr_t, ks_aug, kw, vs_t, vw_t = rms_inproj_prompt(
        x.reshape(m, D_MODEL), w['attn_norm'], w['w_in_r'], w['q_norm'], w['k_norm'], bsz, t)
    nq = t // NSA_TQ
    kv_new = kv.reshape(bsz, t, 4, KV_HEADS, HEAD_DIM)
    win_new = win.reshape(bsz, t, 2, KV_HEADS, HEAD_DIM)

    n_seg = t // S_CMP
    segs = kv_new[:, :, 0:2].astype(BF16).transpose(2, 0, 3, 1, 4).reshape(2, bsz * KV_HEADS, n_seg, S_CMP * HEAD_DIM)
    kvc = compress(segs, w['cmp_pe'], w['cmp_w1'], w['cmp_w2'])
    attn = nsa_prompt(qn_t, qr_t, kvc[0], kvc[1].transpose(0, 2, 1), ks_aug, vs_t, kw, vw_t, sm_t, bsz, t)

    ssd, ssm_new = ssd_prompt(xbc, sm, z, w, bsz, t)

    y = _token_tail(x.reshape(m, D_MODEL), (attn, ssd), p.reshape(m, PLE_DIM), w, 512)
    keep = min(WINDOW, t)
    conv_new = xbc.reshape(bsz, t, CONV_DIM)[:, t - (CONV_W - 1):]
    return y.reshape(bsz, t, D_MODEL), kv_new, win_new[:, t - keep:], ssm_new, conv_new


def _sample_group(x, p, pool, page_table, cache_win, state_ssm, state_conv, w):
    bsz, lq, _ = x.shape
    m = bsz * lq
    q_off = page_table.shape[1] * PAGE_SIZE
    q, kvs, z, xbc, sm = rms_inproj(x.reshape(m, D_MODEL), w['attn_norm'], w['w_in_r'], m)
    pos = q_off + jnp.arange(lq)
    qn = _rmsnorm(q.reshape(bsz, lq, N_HEADS, HEAD_DIM), w['q_norm'])
    qr = _rope(qn, pos)
    kvs = kvs.reshape(bsz, lq, 6, KV_HEADS, HEAD_DIM)
    k_c = _rmsnorm(kvs[:, :, 0], w['k_norm'][0])
    k_s = _rope(_rmsnorm(kvs[:, :, 2], w['k_norm'][1]), pos)
    k_w = _rope(_rmsnorm(kvs[:, :, 4], w['k_norm'][2]), pos)
    kv_new = jnp.stack([k_c, kvs[:, :, 1], k_s, kvs[:, :, 3]], axis=2)
    win_new = jnp.stack([k_w, kvs[:, :, 5]], axis=2)
    win_cat = jnp.concatenate([cache_win, win_new], axis=1)

    pool_t = jnp.transpose(pool, (0, 2, 3, 4, 1))
    kvc = compress_paged(pool_t, page_table, w['cmp_pe'], w['cmp_w1'], w['cmp_w2'])
    attn = nsa_sample(qn, qr, kvc, kv_new, win_cat, pool_t, page_table, sm[:, :N_GATES], q_off)

    xbc_all = jnp.concatenate([state_conv, xbc.reshape(bsz, lq, CONV_DIM)], axis=1)
    dt_raw = sm[:, N_GATES:N_GATES + SSM_HEADS].reshape(bsz, lq, SSM_HEADS)
    xs, b_t, c_in, a_col, a_row, dt_col = _ssd_inputs(xbc_all, dt_raw, w, SSM_CHUNK)
    y_hm, s_t = ssd_scan(w['ssm_d'], xs, b_t, c_in, a_col, a_row, dt_col, state_ssm.transpose(0, 1, 3, 2), SSM_CHUNK)
    ssd = _ssd_finish(y_hm, z.reshape(bsz, lq, D_INNER), w, lq).reshape(m, D_INNER)

    y = _token_tail(x.reshape(m, D_MODEL), (attn, ssd), p.reshape(m, PLE_DIM), w, m)
    keep = cache_win.shape[1]
    return (y.reshape(bsz, lq, D_MODEL), kv_new, win_cat[:, win_cat.shape[1] - keep:], s_t.transpose(0, 1, 3, 2),
            xbc_all[:, xbc_all.shape[1] - (CONV_W - 1):])


def kernel(x_prompt, x_sample, cache_kv, cache_win, state_ssm, state_conv, page_table, p_prompt, p_sample,
           w_in, w_out, q_norm, k_norm, cmp_pe, cmp_w1, cmp_w2, conv_w, conv_b, dt_bias, a_log, ssm_d, ssm_norm,
           attn_norm, ffn_norm, w_rg, b_rg, w_re, b_re, w_gate, w_up, w_down, w_ple_proj, ple_norm, w_ple_gate):
    depth = w_in.shape[0]
    hp, hs = x_prompt, x_sample
    outs = [[] for _ in range(8)]
    cuts = np.cumsum((ATT_W, 6 * KV_HEADS * HEAD_DIM, N_GATES, D_INNER, CONV_DIM, SSM_HEADS))
    for l in range(depth):
        wi = w_in[l]
        w_in_r = jnp.concatenate(
            [wi[:, :cuts[1]], wi[:, cuts[2]:cuts[3]], wi[:, cuts[3]:cuts[4]], wi[:, cuts[1]:cuts[2]],
             wi[:, cuts[4]:cuts[5]], jnp.zeros((D_MODEL, C_SM - N_GATES - SSM_HEADS), F32)], axis=1).astype(BF16)
        w = dict(w_in_r=w_in_r, w_out_b=w_out[l].astype(BF16), q_norm=q_norm[l], k_norm=k_norm[l],
                 cmp_pe=cmp_pe[l], cmp_w1=cmp_w1[l], cmp_w2=cmp_w2[l], conv_w=conv_w[l], conv_b=conv_b[l],
                 dt_bias=dt_bias[l], a_log=a_log[l], ssm_d=ssm_d[l], ssm_norm=ssm_norm[l], attn_norm=attn_norm[l],
                 ffn_norm=ffn_norm[l], w_rg=w_rg[l], b_rg=b_rg[l], w_re=w_re[l], b_re=b_re[l],
                 w_gate=w_gate[l], w_up=w_up[l], w_down=w_down[l],
                 wpp_b=w_ple_proj[l].astype(BF16), ple_norm=ple_norm[l], wpg_b=w_ple_gate[l].astype(BF16))
        hp, *rest_p = _prompt_group(hp, p_prompt[l], w)
        hs, *rest_s = _sample_group(hs, p_sample[l], cache_kv[l], page_table, cache_win[l], state_ssm[l],
                                    state_conv[l], w)
        for j in range(4):
            outs[2 * j].append(rest_p[j])
            outs[2 * j + 1].append(rest_s[j])
    return (hp, hs) + tuple(jnp.stack(o) for o in outs)
```

```python
import functools
import math

import numpy as np
import jax
import jax.numpy as jnp
from jax import lax
from jax.experimental import pallas as pl
from jax.experimental.pallas import tpu as pltpu

F32 = jnp.float32
BF16 = jnp.bfloat16

D_MODEL = 1024
PAGE_SIZE = 128
N_HEADS = 8
HEAD_DIM = 64
KV_HEADS = 2
HPG = N_HEADS // KV_HEADS
ATT_W = N_HEADS * HEAD_DIM
L_CMP = 32
S_CMP = 16
L_SLC = 64
N_SEL = 16
WINDOW = 512
CMP_HID = 64
ROPE_THETA = 10000.0
SSM_HEADS = 8
SSM_HEAD_DIM = 64
D_INNER = SSM_HEADS * SSM_HEAD_DIM
SSM_GROUPS = 2
D_STATE = 64
CONV_W = 4
CONV_DIM = D_INNER + 2 * SSM_GROUPS * D_STATE
SSM_CHUNK = 128
N_EXPERT_GROUPS = 4
EXPERTS_PER_GROUP = 8
N_EXPERTS = N_EXPERT_GROUPS * EXPERTS_PER_GROUP
TOP_K = 2
D_EXPERT = 512
PLE_DIM = 256
EPS = 1e-6
N_GATES = 3 * N_HEADS
C_Q, C_KV, C_Z, C_XBC, C_SM = 512, 768, 512, 768, 128
D_IN_PAD = C_Q + C_KV + C_Z + C_XBC + C_SM

NEG = -1e30
VMEM_LIMIT = 48 * 1024 * 1024
MOE_ROWS = 256


def _cparams(sem):
    return pltpu.CompilerParams(dimension_semantics=sem, vmem_limit_bytes=VMEM_LIMIT)


def _dot(a, b):
    return jnp.dot(a, b, preferred_element_type=F32)


def _dot_nt(a, b):
    return lax.dot_general(a, b, (((1,), (1,)), ((), ())), preferred_element_type=F32)


def _dot_hi(a, b):
    return jnp.dot(a, b, preferred_element_type=F32, precision=lax.Precision.HIGHEST)


def _rms_inproj_kernel(x_ref, g_ref, w_ref, q_ref, kv_ref, z_ref, xbc_ref, sm_ref):
    x = x_ref[...]
    y = x * lax.rsqrt(jnp.mean(x * x, axis=-1, keepdims=True) + EPS) * g_ref[...]
    yb = y.astype(BF16)
    c0 = 0
    for ref, width in ((q_ref, C_Q), (kv_ref, C_KV), (z_ref, C_Z), (xbc_ref, C_XBC), (sm_ref, C_SM)):
        ref[...] = _dot(yb, w_ref[:, c0:c0 + width])
        c0 += width


def rms_inproj(x, gain, w_r, tm):
    m = x.shape[0]
    widths = (C_Q, C_KV, C_Z, C_XBC, C_SM)
    return pl.pallas_call(
        _rms_inproj_kernel,
        grid=(m // tm,),
        in_specs=[pl.BlockSpec((tm, D_MODEL), lambda i: (i, 0)),
                  pl.BlockSpec((1, D_MODEL), lambda i: (0, 0)),
                  pl.BlockSpec((D_MODEL, D_IN_PAD), lambda i: (0, 0))],
        out_specs=[pl.BlockSpec((tm, w), lambda i: (i, 0)) for w in widths],
        out_shape=[jax.ShapeDtypeStruct((m, w), F32) for w in widths],
        compiler_params=_cparams(("parallel",)),
        name="rms_inproj",
    )(x, gain.reshape(1, D_MODEL), w_r)


def _inproj_prompt_kernel(x_ref, g_ref, w_ref, cos_ref, sin_ref, gq_ref, gk_ref, z_ref, xbc_ref, sm_ref, smt_ref, kv_ref,
                          win_ref, qn_ref, qr_ref, ksa_ref, kw_ref, vst_ref, vwt_ref, *, tiles_per_seq):
    tm = x_ref.shape[0]
    x = x_ref[...]
    yb = (x * lax.rsqrt(jnp.mean(x * x, axis=-1, keepdims=True) + EPS) * g_ref[...]).astype(BF16)
    cos, sin = cos_ref[...], sin_ref[...]
    scale = HEAD_DIM ** -0.5
    half = HEAD_DIM // 2

    def proj(c0, width=128):
        return _dot(yb, w_ref[:, c0:c0 + width])

    def head_norm(x_t, gain):
        return x_t * lax.rsqrt(jnp.mean(x_t * x_t, axis=0, keepdims=True) + EPS) * gain

    def rotate(y_t):
        y1, y2 = y_t[0:half], y_t[half:HEAD_DIM]
        return jnp.concatenate([y1 * cos - y2 * sin, y2 * cos + y1 * sin], axis=0)

    def per_group(blk, fn):
        t = blk.T
        return jnp.concatenate([fn(t[g * HEAD_DIM:(g + 1) * HEAD_DIM]) for g in range(KV_HEADS)], axis=0).T

    for j in range(ATT_W // 128):
        q_t = proj(128 * j).T
        for h2 in range(128 // HEAD_DIM):
            head = (128 // HEAD_DIM) * j + h2
            g, hh = head // HPG, head % HPG
            qn = head_norm(q_t[h2 * HEAD_DIM:(h2 + 1) * HEAD_DIM], gq_ref[...])
            qn_ref[0, g, 0, :, hh * tm:(hh + 1) * tm] = (qn * scale).astype(BF16)
            qr_ref[0, g, 0, :, hh * tm:(hh + 1) * tm] = (rotate(qn) * scale).astype(BF16)

    c_kv = C_Q
    k_c = per_group(proj(c_kv), lambda t: head_norm(t, gk_ref[0]))
    v_c = proj(c_kv + 128)
    k_s = per_group(proj(c_kv + 256), lambda t: rotate(head_norm(t, gk_ref[1])))
    v_s = proj(c_kv + 384)
    k_w = per_group(proj(c_kv + 512), lambda t: rotate(head_norm(t, gk_ref[2])))
    v_w = proj(c_kv + 640)
    kv_ref[:, 0:128] = k_c
    kv_ref[:, 128:256] = v_c
    kv_ref[:, 256:384] = k_s
    kv_ref[:, 384:512] = v_s
    win_ref[:, 0:128] = k_w
    win_ref[:, 128:256] = v_w
    kw_ref[...] = k_w.astype(BF16)
    pos0 = (pl.program_id(0) % tiles_per_seq) * tm
    lane = lax.broadcasted_iota(jnp.int32, (tm, 128), 1)
    blk_of_row = (pos0 + lax.broadcasted_iota(jnp.int32, (tm, 128), 0)) // L_SLC
    onehot = (blk_of_row == (lane & (HEAD_DIM - 1))).astype(BF16)
    k_sb = k_s.astype(BF16)
    ksa_ref[0, 0] = jnp.where(lane < HEAD_DIM, k_sb, onehot)
    ksa_ref[0, 1] = jnp.where(lane >= HEAD_DIM, k_sb, onehot)
    vs_t = v_s.T.astype(BF16)
    vw_t = v_w.T.astype(BF16)
    for g in range(KV_HEADS):
        vst_ref[0, g, 0] = vs_t[g * HEAD_DIM:(g + 1) * HEAD_DIM]
        vwt_ref[0, g, 0] = vw_t[g * HEAD_DIM:(g + 1) * HEAD_DIM]
    z_ref[...] = proj(C_Q + C_KV, C_Z)
    xbc_ref[...] = proj(C_Q + C_KV + C_Z, C_XBC)
    sm = proj(C_Q + C_KV + C_Z + C_XBC)
    sm_ref[...] = sm
    smt_ref[0, 0] = sm.T[0:32]


def rms_inproj_prompt(x, gain, w_r, q_gain, k_gain, bsz, t):
    tm = NSA_TQ
    assert NSA_TK == tm and NSA_TKW == tm and t // L_SLC <= HEAD_DIM
    m = bsz * t
    nt = t // tm
    inv = ROPE_THETA ** (-jnp.arange(HEAD_DIM // 2, dtype=F32) / (HEAD_DIM // 2))
    ang = inv[:, None] * jnp.arange(t, dtype=F32)[None, :]
    row = lambda width: pl.BlockSpec((tm, width), lambda i: (i, 0))
    qspec = pl.BlockSpec((1, KV_HEADS, 1, HEAD_DIM, HPG * tm), lambda i: (i // nt, 0, i % nt, 0, 0))
    vspec = pl.BlockSpec((1, KV_HEADS, 1, HEAD_DIM, tm), lambda i: (i // nt, 0, i % nt, 0, 0))
    tspec = pl.BlockSpec((HEAD_DIM // 2, tm), lambda i: (0, i % nt))
    qshape = jax.ShapeDtypeStruct((bsz, KV_HEADS, nt, HEAD_DIM, HPG * tm), BF16)
    vshape = jax.ShapeDtypeStruct((bsz, KV_HEADS, nt, HEAD_DIM, tm), BF16)
    return pl.pallas_call(
        functools.partial(_inproj_prompt_kernel, tiles_per_seq=nt),
        grid=(m // tm,),
        in_specs=[row(D_MODEL),
                  pl.BlockSpec((1, D_MODEL), lambda i: (0, 0)),
                  pl.BlockSpec((D_MODEL, D_IN_PAD), lambda i: (0, 0)),
                  tspec, tspec,
                  pl.BlockSpec((HEAD_DIM, 1), lambda i: (0, 0)),
                  pl.BlockSpec((3, HEAD_DIM, 1), lambda i: (0, 0, 0))],
        out_specs=[row(C_Z), row(C_XBC), row(C_SM),
                   pl.BlockSpec((1, 1, 32, tm), lambda i: (i // nt, i % nt, 0, 0)),
                   row(512), row(256), qspec, qspec,
                   pl.BlockSpec((1, KV_HEADS, tm, 128), lambda i: (i // nt, 0, i % nt, 0)),
                   row(128), vspec, vspec],
        out_shape=[jax.ShapeDtypeStruct((m, C_Z), F32), jax.ShapeDtypeStruct((m, C_XBC), F32),
                   jax.ShapeDtypeStruct((m, C_SM), F32), jax.ShapeDtypeStruct((bsz, nt, 32, tm), F32),
                   jax.ShapeDtypeStruct((m, 512), F32), jax.ShapeDtypeStruct((m, 256), F32), qshape, qshape,
                   jax.ShapeDtypeStruct((bsz, KV_HEADS, t, 128), BF16),
                   jax.ShapeDtypeStruct((m, 128), BF16), vshape, vshape],
        compiler_params=_cparams(("parallel",)),
        name="rms_inproj_prompt",
    )(x, gain.reshape(1, D_MODEL), w_r, jnp.cos(ang), jnp.sin(ang), q_gain.reshape(HEAD_DIM, 1),
      k_gain.reshape(3, HEAD_DIM, 1))


def _outproj_kernel(x_ref, a_ref, s_ref, w_ref, g_ref, wr_ref, br_ref, h_ref, f_ref, lg_ref):
    acc = _dot(a_ref[...], w_ref[0:ATT_W, :]) + _dot(s_ref[...], w_ref[ATT_W:ATT_W + D_INNER, :])
    h = x_ref[...] + acc
    h_ref[...] = h
    f = h * lax.rsqrt(jnp.mean(h * h, axis=-1, keepdims=True) + EPS) * g_ref[...]
    f_hi = f.astype(BF16)
    f_ref[...] = f_hi
    f_lo = (f - f_hi.astype(F32)).astype(BF16)
    lg_ref[...] = (_dot(f_hi, wr_ref[0]) + (_dot(f_hi, wr_ref[1]) + _dot(f_lo, wr_ref[0]))) + br_ref[...]


def outproj(x, attn, ssd, w_out_b, ffn_gain, w_router, b_router, tm):
    m = x.shape[0]
    row = lambda width: pl.BlockSpec((tm, width), lambda i: (i, 0))
    fixed = lambda r, c: pl.BlockSpec((r, c), lambda i: (0, 0))
    w_hi = w_router.astype(BF16)
    w_hi_rest = (w_router - w_hi.astype(F32)).astype(BF16)
    return pl.pallas_call(
        _outproj_kernel,
        grid=(m // tm,),
        in_specs=[row(D_MODEL), row(ATT_W), row(D_INNER), fixed(ATT_W + D_INNER, D_MODEL),
                  fixed(1, D_MODEL), pl.BlockSpec((2, D_MODEL, 128), lambda i: (0, 0, 0)), fixed(1, 128)],
        out_specs=[row(D_MODEL), row(D_MODEL), row(128)],
        out_shape=[jax.ShapeDtypeStruct((m, D_MODEL), F32), jax.ShapeDtypeStruct((m, D_MODEL), BF16),
                   jax.ShapeDtypeStruct((m, 128), F32)],
        compiler_params=_cparams(("parallel",)),
        name="outproj",
    )(x, attn, ssd, w_out_b, ffn_gain.reshape(1, D_MODEL), jnp.stack([w_hi, w_hi_rest]), b_router)


def _ple_kernel(h_ref, y0_ref, y1_ref, wt_ref, p_ref, wg_ref, wp_ref, g_ref, o_ref):
    wt = wt_ref[...]
    h = h_ref[...] + (y0_ref[...] * wt[:, 0:1] + y1_ref[...] * wt[:, 1:2])
    gate = jax.nn.sigmoid(_dot(h.astype(BF16), wg_ref[...]))
    e = _dot(p_ref[...].astype(BF16), wp_ref[...])
    e = e * lax.rsqrt(jnp.mean(e * e, axis=-1, keepdims=True) + EPS) * g_ref[...]
    o_ref[...] = h + gate * e


def moe_combine_ple(h, y01, wts, p, wg_b, wp_b, gain, tm):
    m = h.shape[0]
    nt = m // tm
    return pl.pallas_call(
        _ple_kernel,
        grid=(nt,),
        in_specs=[pl.BlockSpec((tm, D_MODEL), lambda i: (i, 0)),
                  pl.BlockSpec((tm, D_MODEL), lambda i: (i, 0)),
                  pl.BlockSpec((tm, D_MODEL), lambda i: (i + nt, 0)),
                  pl.BlockSpec((tm, 128), lambda i: (i, 0)),
                  pl.BlockSpec((tm, PLE_DIM), lambda i: (i, 0)),
                  pl.BlockSpec((D_MODEL, D_MODEL), lambda i: (0, 0)),
                  pl.BlockSpec((PLE_DIM, D_MODEL), lambda i: (0, 0)),
                  pl.BlockSpec((1, D_MODEL), lambda i: (0, 0))],
        out_specs=pl.BlockSpec((tm, D_MODEL), lambda i: (i, 0)),
        out_shape=jax.ShapeDtypeStruct((m, D_MODEL), F32),
        compiler_params=_cparams(("parallel",)),
        name="moe_combine_ple",
    )(h, y01, y01, wts, p, wg_b, wp_b, gain.reshape(1, D_MODEL))


def _moe_kernel(be_ref, nb_ref, x_ref, wg_ref, wu_ref, wd_ref, y_ref, wg_sc, wu_sc, wd_sc):
    i = pl.program_id(0)

    @pl.when((i == 0) | (be_ref[i] != be_ref[jnp.maximum(i - 1, 0)]))
    def _():
        wg_sc[...] = wg_ref[0].astype(BF16)
        wu_sc[...] = wu_ref[0].astype(BF16)
        wd_sc[...] = wd_ref[0].astype(BF16)

    @pl.when(i < nb_ref[0])
    def _():
        x = x_ref[...]
        a = _dot(x, wg_sc[...])
        hb = (a * jax.nn.sigmoid(a)) * _dot(x, wu_sc[...])
        y_ref[...] = _dot(hb.astype(BF16), wd_sc[...])

    @pl.when(i >= nb_ref[0])
    def _():
        y_ref[...] = jnp.zeros(y_ref.shape, F32)


def moe_experts(xpad, blk_e, n_used, w_gate, w_up, w_down):
    n_blk = xpad.shape[0] // MOE_ROWS
    grid_spec = pltpu.PrefetchScalarGridSpec(
        num_scalar_prefetch=2,
        grid=(n_blk,),
        in_specs=[pl.BlockSpec((MOE_ROWS, D_MODEL), lambda i, be, nb: (i, 0)),
                  pl.BlockSpec((1, D_MODEL, D_EXPERT), lambda i, be, nb: (be[i], 0, 0)),
                  pl.BlockSpec((1, D_MODEL, D_EXPERT), lambda i, be, nb: (be[i], 0, 0)),
                  pl.BlockSpec((1, D_EXPERT, D_MODEL), lambda i, be, nb: (be[i], 0, 0))],
        out_specs=pl.BlockSpec((MOE_ROWS, D_MODEL), lambda i, be, nb: (i, 0)),
        scratch_shapes=[pltpu.VMEM((D_MODEL, D_EXPERT), BF16), pltpu.VMEM((D_MODEL, D_EXPERT), BF16),
                        pltpu.VMEM((D_EXPERT, D_MODEL), BF16)],
    )
    return pl.pallas_call(
        _moe_kernel,
        grid_spec=grid_spec,
        out_shape=jax.ShapeDtypeStruct((n_blk * MOE_ROWS, D_MODEL), F32),
        compiler_params=_cparams(("arbitrary",)),
        name="moe_experts",
    )(blk_e, n_used, xpad, w_gate, w_up, w_down)


def _compress_kernel(seg_ref, w1a_ref, w1b_ref, pe_ref, w1_ref, w2_ref, o_ref):
    seg = seg_ref[0, 0]
    n_seg = seg.shape[0]
    hid0 = _dot(pe_ref[0], w1_ref[0])[0:1]
    p0 = _dot(seg, w1a_ref[0])
    p1 = pltpu.roll(_dot(seg, w1b_ref[0]), n_seg - 1, axis=0)
    hid = hid0 + p0 + p1
    act = hid * jax.nn.sigmoid(hid)
    o_ref[0, 0] = _dot(act.astype(BF16), w2_ref[0]).astype(BF16)


def compress(segs, cmp_pe, cmp_w1, cmp_w2):
    _, r, n_seg, k = segs.shape
    w1 = cmp_w1.astype(BF16)
    pe = jnp.broadcast_to(cmp_pe.reshape(2, 1, L_CMP * HEAD_DIM), (2, 8, L_CMP * HEAD_DIM)).astype(BF16)
    return pl.pallas_call(
        _compress_kernel,
        grid=(2, r),
        in_specs=[pl.BlockSpec((1, 1, n_seg, k), lambda s, i: (s, i, 0, 0)),
                  pl.BlockSpec((1, k, CMP_HID), lambda s, i: (s, 0, 0)),
                  pl.BlockSpec((1, k, CMP_HID), lambda s, i: (s, 1, 0)),
                  pl.BlockSpec((1, 8, 2 * k), lambda s, i: (s, 0, 0)),
                  pl.BlockSpec((1, 2 * k, CMP_HID), lambda s, i: (s, 0, 0)),
                  pl.BlockSpec((1, CMP_HID, HEAD_DIM), lambda s, i: (s, 0, 0))],
        out_specs=pl.BlockSpec((1, 1, n_seg, HEAD_DIM), lambda s, i: (s, i, 0, 0)),
        out_shape=jax.ShapeDtypeStruct((2, r, n_seg, HEAD_DIM), BF16),
        compiler_params=_cparams(("parallel", "parallel")),
        name="compress",
    )(segs, w1, w1, pe, w1, cmp_w2.astype(BF16))


PAGES_PER_STEP = 16


def _page_copies(pt_ref, pool_ref, buf, sem, b, j, slot):
    return [pltpu.make_async_copy(pool_ref.at[pt_ref[b, j * PAGES_PER_STEP + p], pl.ds(0, 2)],
                                  buf.at[slot, p], sem.at[slot]) for p in range(PAGES_PER_STEP)]


def _paged_partials_kernel(pt_ref, pool_ref, wa_ref, wb_ref, p0_ref, p1_ref, buf, sem, x_sc):
    b = pl.program_id(0)
    j = pl.program_id(1)
    n_grp = pl.num_programs(1)
    step = b * n_grp + j
    slot = step % 2

    @pl.when(step == 0)
    def _():
        for c in _page_copies(pt_ref, pool_ref, buf, sem, b, j, slot):
            c.start()

    @pl.when(step + 1 < pl.num_programs(0) * n_grp)
    def _():
        wrap = j + 1 == n_grp
        for c in _page_copies(pt_ref, pool_ref, buf, sem, jnp.where(wrap, b + 1, b), jnp.where(wrap, 0, j + 1),
                              1 - slot):
            c.start()

    for c in _page_copies(pt_ref, pool_ref, buf, sem, b, j, slot):
        c.wait()

    n_seg = PAGES_PER_STEP * PAGE_SIZE // S_CMP
    for kv in range(2):
        for p in range(PAGES_PER_STEP):
            x_sc[p * PAGE_SIZE:(p + 1) * PAGE_SIZE, :] = buf[slot, p, kv].reshape(KV_HEADS * HEAD_DIM, PAGE_SIZE).T
        acc0 = jnp.zeros((n_seg, KV_HEADS * CMP_HID), F32)
        acc1 = jnp.zeros((n_seg, KV_HEADS * CMP_HID), F32)
        for s in range(S_CMP):
            xs = x_sc[pl.ds(s, n_seg, stride=S_CMP), :].astype(BF16)
            acc0 = acc0 + _dot(xs, wa_ref[kv, s])
            acc1 = acc1 + _dot(xs, wb_ref[kv, s])
        p0_ref[kv, 0] = acc0
        p1_ref[kv, 0] = acc1


def _compress_finish_kernel(p0_ref, p1_ref, pe_ref, w1_ref, w2_ref, o_ref):
    n_seg = p0_ref.shape[2]
    hid0 = _dot(pe_ref[0], w1_ref[0])[0:1]
    hid = hid0 + p0_ref[0, 0] + pltpu.roll(p1_ref[0, 0], n_seg - 1, axis=0)
    act = hid * jax.nn.sigmoid(hid)
    o_ref[0, 0] = _dot(act.astype(BF16), w2_ref[0]).astype(BF16)


def _per_group(w):
    zero = jnp.zeros_like(w)
    return jnp.concatenate([jnp.concatenate([w, zero], axis=-1), jnp.concatenate([zero, w], axis=-1)], axis=-2)


def compress_paged(pool_t, page_table, cmp_pe, cmp_w1, cmp_w2):
    bsz, n_pages = page_table.shape
    assert n_pages % PAGES_PER_STEP == 0 and KV_HEADS == 2
    n_grp = n_pages // PAGES_PER_STEP
    seg_step = PAGES_PER_STEP * PAGE_SIZE // S_CMP
    n_seg = n_grp * seg_step
    gw = KV_HEADS * HEAD_DIM
    w1 = _per_group(cmp_w1.astype(BF16).reshape(2, L_CMP // S_CMP, S_CMP, HEAD_DIM, CMP_HID))
    part_shape = jax.ShapeDtypeStruct((2, bsz, n_seg, KV_HEADS * CMP_HID), F32)
    part_spec = pl.BlockSpec((2, 1, seg_step, KV_HEADS * CMP_HID), lambda b, j, pt: (0, b, j, 0))
    wspec = pl.BlockSpec((2, S_CMP, gw, KV_HEADS * CMP_HID), lambda b, j, pt: (0, 0, 0, 0))
    p0, p1 = pl.pallas_call(
        _paged_partials_kernel,
        grid_spec=pltpu.PrefetchScalarGridSpec(
            num_scalar_prefetch=1,
            grid=(bsz, n_grp),
            in_specs=[pl.BlockSpec(memory_space=pl.ANY), wspec, wspec],
            out_specs=[part_spec, part_spec],
            scratch_shapes=[pltpu.VMEM((2, PAGES_PER_STEP, 2, KV_HEADS, HEAD_DIM, PAGE_SIZE), F32),
                            pltpu.SemaphoreType.DMA((2,)),
                            pltpu.VMEM((PAGES_PER_STEP * PAGE_SIZE, gw), F32)]),
        out_shape=[part_shape, part_shape],
        compiler_params=_cparams(("arbitrary", "arbitrary")),
        name="paged_partials",
    )(page_table, pool_t, w1[:, 0], w1[:, 1])
    pe = jnp.broadcast_to(cmp_pe.reshape(2, 1, L_CMP * HEAD_DIM), (2, 8, L_CMP * HEAD_DIM)).astype(BF16)
    w1_both = jnp.concatenate([cmp_w1, cmp_w1], axis=-1).astype(BF16)
    pspec = pl.BlockSpec((1, 1, n_seg, KV_HEADS * CMP_HID), lambda s, i: (s, i, 0, 0))
    return pl.pallas_call(
        _compress_finish_kernel,
        grid=(2, bsz),
        in_specs=[pspec, pspec,
                  pl.BlockSpec((1, 8, L_CMP * HEAD_DIM), lambda s, i: (s, 0, 0)),
                  pl.BlockSpec((1, L_CMP * HEAD_DIM, KV_HEADS * CMP_HID), lambda s, i: (s, 0, 0)),
                  pl.BlockSpec((1, KV_HEADS * CMP_HID, gw), lambda s, i: (s, 0, 0))],
        out_specs=pl.BlockSpec((1, 1, n_seg, gw), lambda s, i: (s, i, 0, 0)),
        out_shape=jax.ShapeDtypeStruct((2, bsz, n_seg, gw), BF16),
        compiler_params=_cparams(("parallel", "parallel")),
        name="compress_finish",
    )(p0, p1, pe, w1_both, _per_group(cmp_w2.astype(BF16)))


FLASH_SPLIT = 4


def _flash_scores(k, q_ref, s_ref):
    cw = q_ref.shape[1] // FLASH_SPLIT
    for i in range(FLASH_SPLIT):
        c = slice(i * cw, (i + 1) * cw)
        s_ref[:, c] = _dot(k, q_ref[:, c])


def _flash_step(s_ref, v_t, m_sc, l_sc, acc_sc, keep=None):
    cw = s_ref.shape[1] // FLASH_SPLIT
    for i in range(FLASH_SPLIT):
        c = slice(i * cw, (i + 1) * cw)
        s_t = s_ref[:, c]
        if keep is not None:
            s_t = jnp.where(keep(i * cw, cw), s_t, NEG)
        m_prev = m_sc[:, c]
        m_new = jnp.maximum(m_prev, jnp.max(s_t, axis=0, keepdims=True))
        alpha = jnp.exp(m_prev - m_new)
        p = jnp.exp(s_t - m_new)
        l_sc[:, c] = alpha * l_sc[:, c] + jnp.sum(p, axis=0, keepdims=True)
        acc_sc[:, c] = alpha * acc_sc[:, c] + _dot(v_t, p.astype(BF16))
        m_sc[:, c] = m_new


def _flash_reset(m_sc, l_sc, acc_sc):
    m_sc[...] = jnp.full(m_sc.shape, NEG, F32)
    l_sc[...] = jnp.zeros(l_sc.shape, F32)
    acc_sc[...] = jnp.zeros(acc_sc.shape, F32)


def _nsa_prompt_kernel(qn_ref, qr_ref, kc_ref, vct_ref, ks_ref, vst_ref, kw_ref, vwt_ref, covert_ref, gate_ref,
                       o_ref, qa_sc, qw_sc, m_sc, l_sc, acc_sc, out_sc, score_sc, rank_sc, s_sc, sw_sc, *, tq, tk, tkw):
    qi = pl.program_id(2)
    pos0 = qi * tq
    w = HPG * tq

    def positions(rows, base):
        kpos = base + lax.broadcasted_iota(jnp.int32, (rows, w), 0)
        qpos = pos0 + (lax.broadcasted_iota(jnp.int32, (rows, w), 1) & (tq - 1))
        return kpos, qpos

    g = pl.program_id(1)

    def gate_row(branch):
        rows = [gate_ref[0, 0, pl.ds(3 * (g * HPG + hh) + branch, 1), :] for hh in range(HPG)]
        return jax.nn.sigmoid(jnp.concatenate(rows, axis=1))

    gate = [gate_row(branch) for branch in range(3)]

    kc = kc_ref[0, 0]
    n_c = kc.shape[0]
    cidx, qpos_c = positions(n_c, 0)
    valid = cidx * S_CMP + (L_CMP - 1) <= qpos_c
    s1 = jnp.where(valid, _dot(kc, qn_ref[0, 0, 0]), NEG)
    e1 = jnp.exp(s1 - jnp.max(s1, axis=0, keepdims=True)) * valid.astype(F32)
    p1b = (e1 * (1.0 / jnp.maximum(jnp.sum(e1, axis=0, keepdims=True), 1e-30))).astype(BF16)
    out_sc[...] = gate[0] * _dot(vct_ref[0, 0], p1b)

    p1_stack = jnp.concatenate([p1b[:, hh * tq:(hh + 1) * tq] for hh in range(HPG)], axis=0)
    imp = _dot(covert_ref[...], p1_stack)
    n_s = imp.shape[0]
    blk = lax.broadcasted_iota(jnp.int32, (n_s, tq), 0)
    cur = (pos0 + lax.broadcasted_iota(jnp.int32, (n_s, tq), 1)) // L_SLC
    vis = blk <= cur
    forced = vis & ((blk == 0) | (blk >= cur - 1))
    score_sc[...] = jnp.where(forced, 1e9, jnp.where(vis, imp, -1.0))
    rank_sc[...] = jnp.zeros(rank_sc.shape, F32)
    last_blk = (pos0 + tq - 1) // L_SLC
    n_oct = n_s // 8
    for oi in range(n_oct):
        @pl.when(oi * 8 <= last_blk)
        def _():
            rows = [slice(8 * oj, 8 * oj + 8) for oj in range(n_oct)]
            parts = [rank_sc[r, :] for r in rows]
            for i in range(8 * oi, 8 * oi + 8):
                c = score_sc[i:i + 1, :]
                for oj, r in enumerate(rows):
                    sj = score_sc[r, :]
                    if oj > oi:
                        beats = c >= sj
                    elif oj < oi:
                        beats = c > sj
                    else:
                        above = lax.broadcasted_iota(jnp.int32, (8, tq), 0) > i - 8 * oi
                        beats = (c > sj) | ((c == sj) & above)
                    parts[oj] = parts[oj] + beats.astype(F32)
            for r, part in zip(rows, parts):
                rank_sc[r, :] = part
    sel_bias = jnp.where(vis & (rank_sc[...] < N_SEL), 0.0, NEG).astype(BF16)
    own = pl.ds(pl.multiple_of(g * HEAD_DIM, HEAD_DIM), HEAD_DIM)
    other = pl.ds(pl.multiple_of((1 - g) * HEAD_DIM, HEAD_DIM), HEAD_DIM)
    qa_sc[other, :] = jnp.zeros((HEAD_DIM, w), BF16)
    qw_sc[other, :] = jnp.zeros((HEAD_DIM, w), BF16)
    qa_sc[own, :] = qr_ref[0, 0, 0]
    qw_sc[own, :] = qr_ref[0, 0, 0]
    qa_sc[pl.ds(pl.multiple_of((1 - g) * HEAD_DIM, HEAD_DIM), n_s), :] = jnp.concatenate([sel_bias] * HPG, axis=1)

    def sel_keys(kt):
        return ks_ref[0, 0, pl.ds(pl.multiple_of(kt * tk, tk), tk), :]

    def win_keys(kt):
        return kw_ref[pl.ds(pl.multiple_of(kt * tkw, tkw), tkw), :]

    def chunk_positions(rows, base, lane0, n_lanes):
        kpos = base + lax.broadcasted_iota(jnp.int32, (rows, n_lanes), 0)
        qpos = pos0 + ((lane0 + lax.broadcasted_iota(jnp.int32, (rows, n_lanes), 1)) & (tq - 1))
        return kpos, qpos

    def causal(rows, base):
        def keep(lane0, n_lanes):
            kpos, qpos = chunk_positions(rows, base, lane0, n_lanes)
            return kpos <= qpos
        return keep

    _flash_reset(m_sc, l_sc, acc_sc)
    n_full = pos0 // tk
    _flash_scores(sel_keys(0), qa_sc, s_sc.at[0])

    def sel_body(kt, carry):
        for parity in range(2):
            @pl.when(kt % 2 == parity)
            def _():
                _flash_scores(sel_keys(kt + 1), qa_sc, s_sc.at[1 - parity])
                _flash_step(s_sc.at[parity], vst_ref[0, 0, kt], m_sc, l_sc, acc_sc)
        return carry

    lax.fori_loop(0, n_full, sel_body, 0)

    n_inner = WINDOW // tkw - 1
    win_tiles = [qi - d for d in range(n_inner + 2)]
    for idx, kt in enumerate(win_tiles):
        _flash_scores(win_keys(jnp.maximum(kt, 0)), qw_sc, sw_sc.at[idx])

    _flash_step(s_sc.at[n_full % 2], vst_ref[0, 0, n_full], m_sc, l_sc, acc_sc, causal(tk, n_full * tk))
    out_sc[...] = out_sc[...] + (gate[1] * (1.0 / l_sc[...])) * acc_sc[...]

    _flash_reset(m_sc, l_sc, acc_sc)
    _flash_step(sw_sc.at[0], vwt_ref[0, 0, qi], m_sc, l_sc, acc_sc, causal(tkw, pos0))
    for idx in range(1, n_inner + 1):
        kt = win_tiles[idx]
        _flash_step(sw_sc.at[idx], vwt_ref[0, 0, jnp.maximum(kt, 0)], m_sc, l_sc, acc_sc,
                    lambda lane0, n_lanes, kt=kt: kt >= 0)
    kt_far = win_tiles[n_inner + 1]

    def far_keep(lane0, n_lanes):
        kpos, qpos = chunk_positions(tkw, kt_far * tkw, lane0, n_lanes)
        return (kpos > qpos - WINDOW) & (kt_far >= 0)

    _flash_step(sw_sc.at[n_inner + 1], vwt_ref[0, 0, jnp.maximum(kt_far, 0)], m_sc, l_sc, acc_sc, far_keep)

    o = out_sc[...] + (gate[2] * (1.0 / l_sc[...])) * acc_sc[...]
    for pair in range(HPG // 2):
        two = jnp.concatenate([o[:, (2 * pair + e) * tq:(2 * pair + e + 1) * tq] for e in range(2)], axis=0)
        o_ref[:, 128 * pair:128 * (pair + 1)] = two.T.astype(o_ref.dtype)


NSA_TQ, NSA_TK, NSA_TKW = 256, 256, 256


def nsa_prompt(qn_t, qr_t, kc, vc_t, ks_aug, vs_t, kw, vw_t, sm_t, bsz, t):
    tq, tk, tkw = NSA_TQ, NSA_TK, NSA_TKW
    assert tq == tkw and tk % tq == 0 and WINDOW % tkw == 0
    n_c = kc.shape[1]
    n_s = t // L_SLC
    assert n_s <= HEAD_DIM and KV_HEADS == 2
    w = HPG * tq
    c0 = np.arange(n_c)[None, :] * S_CMP
    s0 = np.arange(n_s)[:, None] * L_SLC
    cover_t = (np.maximum(np.minimum(c0 + L_CMP, s0 + L_SLC) - np.maximum(c0, s0), 0) / L_CMP).astype(np.float32)
    cover_t = np.tile(cover_t, (1, HPG))
    qspec = pl.BlockSpec((1, 1, 1, HEAD_DIM, w), lambda b, g, i: (b, g, i, 0, 0))
    kern = functools.partial(_nsa_prompt_kernel, tq=tq, tk=tk, tkw=tkw)
    return pl.pallas_call(
        kern,
        grid=(bsz, KV_HEADS, t // tq),
        in_specs=[qspec, qspec,
                  pl.BlockSpec((1, 1, n_c, HEAD_DIM), lambda b, g, i: (b, g, 0, 0)),
                  pl.BlockSpec((1, 1, HEAD_DIM, n_c), lambda b, g, i: (b, g, 0, 0)),
                  pl.BlockSpec((1, 1, t, 2 * HEAD_DIM), lambda b, g, i: (b, g, 0, 0)),
                  pl.BlockSpec((1, 1, t // tk, HEAD_DIM, tk), lambda b, g, i: (b, g, 0, 0, 0)),
                  pl.BlockSpec((t, KV_HEADS * HEAD_DIM), lambda b, g, i: (b, 0)),
                  pl.BlockSpec((1, 1, t // tkw, HEAD_DIM, tkw), lambda b, g, i: (b, g, 0, 0, 0)),
                  pl.BlockSpec((n_s, HPG * n_c), lambda b, g, i: (0, 0)),
                  pl.BlockSpec((1, 1, 32, tq), lambda b, g, i: (b, i, 0, 0))],
        out_specs=pl.BlockSpec((tq, HPG * HEAD_DIM), lambda b, g, i: (b * (t // tq) + i, g)),
        out_shape=jax.ShapeDtypeStruct((bsz * t, ATT_W), BF16),
        scratch_shapes=[pltpu.VMEM((2 * HEAD_DIM, w), BF16), pltpu.VMEM((2 * HEAD_DIM, w), BF16),
                        pltpu.VMEM((1, w), F32), pltpu.VMEM((1, w), F32),
                        pltpu.VMEM((HEAD_DIM, w), F32), pltpu.VMEM((HEAD_DIM, w), F32),
                        pltpu.VMEM((n_s, tq), F32), pltpu.VMEM((n_s, tq), F32),
                        pltpu.VMEM((2, tk, w), F32), pltpu.VMEM((WINDOW // tkw + 1, tkw, w), F32)],
        compiler_params=_cparams(("parallel", "parallel", "arbitrary")),
        name="nsa_prompt",
    )(qn_t, qr_t, kc.reshape(bsz, KV_HEADS, n_c, HEAD_DIM), vc_t.reshape(bsz, KV_HEADS, HEAD_DIM, n_c),
      ks_aug, vs_t, kw, vw_t, jnp.asarray(cover_t, BF16), sm_t)


def _sample_select_kernel(qn_ref, kc_ref, vc_ref, cover_ref, pick_ref, oc_ref, sb_ref, *, q_off, lq, n_cmp, n_slc):
    rows = HPG * lq
    for g in range(KV_HEADS):
        kc = kc_ref[0]
        n_c = kc.shape[0]
        vc_g = _dot(vc_ref[0], pick_ref[g]).astype(BF16)
        cidx = lax.broadcasted_iota(jnp.int32, (rows, n_c), 1)
        qpos = q_off + (lax.broadcasted_iota(jnp.int32, (rows, n_c), 0) & (lq - 1))
        valid = (cidx * S_CMP + (L_CMP - 1) <= qpos) & (cidx < n_cmp)
        s1 = jnp.where(valid, _dot_nt(qn_ref[0, g], kc), NEG)
        e1 = jnp.exp(s1 - jnp.max(s1, axis=-1, keepdims=True)) * valid.astype(F32)
        p1b = (e1 * (1.0 / jnp.maximum(jnp.sum(e1, axis=-1, keepdims=True), 1e-30))).astype(BF16)
        oc_ref[0, g] = _dot(p1b, vc_g)
        imp4 = _dot(p1b, cover_ref[...])
        imp = imp4[0:lq]
        for hh in range(1, HPG):
            imp = imp + imp4[hh * lq:(hh + 1) * lq]
        n_sp = imp.shape[1]
        blk = lax.broadcasted_iota(jnp.int32, (lq, n_sp), 1)
        cur = (q_off + lax.broadcasted_iota(jnp.int32, (lq, n_sp), 0)) // L_SLC
        vis = (blk <= cur) & (blk < n_slc)
        forced = vis & ((blk == 0) | (blk >= cur - 1))
        score = jnp.where(forced, 1e9, jnp.where(vis, imp, -1.0))
        rank = jnp.zeros((lq, n_sp), F32)
        for i in range(n_slc):
            c = score[:, i:i + 1]
            beats = (c > score) | ((c == score) & (blk > i))
            rank = rank + beats.astype(F32)
        sb_ref[0, g] = jnp.where(vis & (rank < min(N_SEL, n_slc)), 0.0, NEG)


def _sample_attend_kernel(pt_ref, pool_ref, qr_ref, sb_ref, exp_ref, oc_ref, gate_ref, kn_ref, vn_ref, kw_ref, vw_ref,
                          o_ref, buf, sem, m_sc, l_sc, acc_sc, *, q_off, lq, n_win):
    b = pl.program_id(0)
    j = pl.program_id(1)
    n_grp = pl.num_programs(1)
    step = b * n_grp + j
    slot = step % 2
    rows = HPG * lq

    def copies(bb, jj, sl):
        return [pltpu.make_async_copy(pool_ref.at[pt_ref[bb, jj * PAGES_PER_STEP + p], pl.ds(2, 2)],
                                      buf.at[sl, p], sem.at[sl]) for p in range(PAGES_PER_STEP)]

    @pl.when(step == 0)
    def _():
        for c in copies(b, j, slot):
            c.start()

    @pl.when(step + 1 < pl.num_programs(0) * n_grp)
    def _():
        wrap = j + 1 == n_grp
        for c in copies(jnp.where(wrap, b + 1, b), jnp.where(wrap, 0, j + 1), 1 - slot):
            c.start()

    for c in copies(b, j, slot):
        c.wait()

    @pl.when(j == 0)
    def _():
        m_sc[...] = jnp.full(m_sc.shape, NEG, F32)
        l_sc[...] = jnp.zeros(l_sc.shape, F32)
        acc_sc[...] = jnp.zeros(acc_sc.shape, F32)

    def update(g, s, v, v_transposed):
        m_prev = m_sc[g]
        m_new = jnp.maximum(m_prev, jnp.max(s, axis=-1, keepdims=True))
        alpha = jnp.exp(m_prev - m_new)
        p = jnp.exp(s - m_new)
        l_sc[g] = alpha * l_sc[g] + jnp.sum(p, axis=-1, keepdims=True)
        pv = _dot_nt(p.astype(BF16), v) if v_transposed else _dot(p.astype(BF16), v)
        acc_sc[g] = alpha * acc_sc[g] + pv
        m_sc[g] = m_new

    for g in range(KV_HEADS):
        k_t = jnp.concatenate([buf[slot, p, 0, g] for p in range(PAGES_PER_STEP)], axis=1).astype(BF16)
        v_t = jnp.concatenate([buf[slot, p, 1, g] for p in range(PAGES_PER_STEP)], axis=1).astype(BF16)
        sb = sb_ref[0, g, 0]
        bias = _dot(jnp.concatenate([sb] * HPG, axis=0).astype(BF16), exp_ref[...])
        update(g, _dot(qr_ref[0, g], k_t) + bias, v_t, True)

    @pl.when(j == n_grp - 1)
    def _():
        for g in range(KV_HEADS):
            qr = qr_ref[0, g]
            kn = kn_ref[0, g]
            kidx = lax.broadcasted_iota(jnp.int32, (rows, kn.shape[0]), 1)
            qidx = lax.broadcasted_iota(jnp.int32, (rows, kn.shape[0]), 0) & (lq - 1)
            update(g, jnp.where((kidx <= qidx) & (kidx < lq), _dot_nt(qr, kn), NEG), vn_ref[0, g], False)
            o_s = acc_sc[g] * (1.0 / l_sc[g])
            kw = kw_ref[0, g]
            widx = lax.broadcasted_iota(jnp.int32, (rows, kw.shape[0]), 1)
            kpos = q_off + lq - n_win + widx
            qpos = q_off + (lax.broadcasted_iota(jnp.int32, (rows, kw.shape[0]), 0) & (lq - 1))
            ok = (widx < n_win) & (kpos <= qpos) & (kpos > qpos - WINDOW) & (kpos >= 0)
            s3 = jnp.where(ok, _dot_nt(qr, kw), NEG)
            e3 = jnp.exp(s3 - jnp.max(s3, axis=-1, keepdims=True)) * ok.astype(F32)
            p3 = e3 * (1.0 / jnp.maximum(jnp.sum(e3, axis=-1, keepdims=True), 1e-30))
            o_w = _dot(p3.astype(BF16), vw_ref[0, g])
            gate = jax.nn.sigmoid(gate_ref[0, g])
            o_ref[0, g] = gate[:, 0:1] * oc_ref[0, g] + gate[:, 1:2] * o_s + gate[:, 2:3] * o_w


def nsa_sample(q, q_rot, kvc, kv_new, win_cat, pool_t, page_table, gate_logits, q_off):
    bsz, lq = q.shape[:2]
    assert lq & (lq - 1) == 0 and lq <= L_SLC and q_off % (PAGES_PER_STEP * PAGE_SIZE) == 0
    rows = HPG * lq
    scale = HEAD_DIM ** -0.5
    n_seg = kvc.shape[2]
    t_kv = q_off + lq
    n_cmp = (t_kv - L_CMP) // S_CMP + 1
    n_slc = -(-t_kv // L_SLC)
    n_sp = -(-n_slc // 128) * 128
    n_grp = page_table.shape[1] // PAGES_PER_STEP
    blk_step = PAGES_PER_STEP * PAGE_SIZE // L_SLC
    key_step = PAGES_PER_STEP * PAGE_SIZE

    def rows_major(a):
        return a.reshape(bsz, lq, KV_HEADS, HPG, -1).transpose(0, 2, 3, 1, 4).reshape(bsz, KV_HEADS, rows, -1)

    def keys_major(a, n_pad):
        a = a.astype(BF16).transpose(0, 2, 1, 3)
        return jnp.pad(a, ((0, 0), (0, 0), (0, n_pad - a.shape[2]), (0, 0)))

    qn_r = rows_major((q * scale).astype(BF16))
    qr_r = rows_major((q_rot * scale).astype(BF16))
    c0 = np.arange(n_seg)[:, None] * S_CMP
    s0 = np.arange(n_sp)[None, :] * L_SLC
    cover = np.maximum(np.minimum(c0 + L_CMP, s0 + L_SLC) - np.maximum(c0, s0), 0) / L_CMP
    cover = cover * (np.arange(n_seg)[:, None] < n_cmp) * (np.arange(n_sp)[None, :] < n_slc)
    gw = KV_HEADS * HEAD_DIM
    zeros_q = jnp.zeros_like(qn_r[:, 0])
    qn_wide = jnp.stack([jnp.concatenate([qn_r[:, 0], zeros_q], axis=-1),
                         jnp.concatenate([zeros_q, qn_r[:, 1]], axis=-1)], axis=1)
    pick = (np.arange(gw)[None, :, None] == np.arange(KV_HEADS)[:, None, None] * HEAD_DIM
            + np.arange(HEAD_DIM)[None, None, :]).astype(np.float32)
    spec_q = pl.BlockSpec((1, KV_HEADS, rows, HEAD_DIM), lambda b: (b, 0, 0, 0))
    o_c, sel_bias = pl.pallas_call(
        functools.partial(_sample_select_kernel, q_off=q_off, lq=lq, n_cmp=n_cmp, n_slc=n_slc),
        grid=(bsz,),
        in_specs=[pl.BlockSpec((1, KV_HEADS, rows, gw), lambda b: (b, 0, 0, 0)),
                  pl.BlockSpec((1, n_seg, gw), lambda b: (b, 0, 0)),
                  pl.BlockSpec((1, n_seg, gw), lambda b: (b, 0, 0)),
                  pl.BlockSpec((n_seg, n_sp), lambda b: (0, 0)),
                  pl.BlockSpec((KV_HEADS, gw, HEAD_DIM), lambda b: (0, 0, 0))],
        out_specs=[spec_q, pl.BlockSpec((1, KV_HEADS, lq, n_sp), lambda b: (b, 0, 0, 0))],
        out_shape=[jax.ShapeDtypeStruct((bsz, KV_HEADS, rows, HEAD_DIM), F32),
                   jax.ShapeDtypeStruct((bsz, KV_HEADS, lq, n_sp), F32)],
        compiler_params=_cparams(("parallel",)),
        name="sample_select",
    )(qn_wide, kvc[0], kvc[1], jnp.asarray(cover, BF16), jnp.asarray(pick, BF16))

    sb_steps = sel_bias[..., :n_grp * blk_step].reshape(bsz, KV_HEADS, lq, n_grp, blk_step).transpose(0, 1, 3, 2, 4)
    expand = (np.arange(key_step)[None, :] // L_SLC == np.arange(blk_step)[:, None]).astype(np.float32)
    gates = jnp.pad(rows_major(gate_logits.reshape(bsz, lq, -1)), ((0, 0), (0, 0), (0, 0), (0, 128 - 3)))
    n_win = win_cat.shape[1]
    n_wp = -(-n_win // 128) * 128
    spec2 = lambda shape: pl.BlockSpec((1,) + shape, lambda b, j, pt: (b,) + (0,) * len(shape))
    attn = pl.pallas_call(
        functools.partial(_sample_attend_kernel, q_off=q_off, lq=lq, n_win=n_win),
        grid_spec=pltpu.PrefetchScalarGridSpec(
            num_scalar_prefetch=1,
            grid=(bsz, n_grp),
            in_specs=[pl.BlockSpec(memory_space=pl.ANY),
                      spec2((KV_HEADS, rows, HEAD_DIM)),
                      pl.BlockSpec((1, KV_HEADS, 1, lq, blk_step), lambda b, j, pt: (b, 0, j, 0, 0)),
                      pl.BlockSpec((blk_step, key_step), lambda b, j, pt: (0, 0)),
                      spec2((KV_HEADS, rows, HEAD_DIM)),
                      spec2((KV_HEADS, rows, 128)),
                      spec2((KV_HEADS, 128, HEAD_DIM)), spec2((KV_HEADS, 128, HEAD_DIM)),
                      spec2((KV_HEADS, n_wp, HEAD_DIM)), spec2((KV_HEADS, n_wp, HEAD_DIM))],
            out_specs=spec2((KV_HEADS, rows, HEAD_DIM)),
            scratch_shapes=[pltpu.VMEM((2, PAGES_PER_STEP, 2, KV_HEADS, HEAD_DIM, PAGE_SIZE), F32),
                            pltpu.SemaphoreType.DMA((2,)),
                            pltpu.VMEM((KV_HEADS, rows, 1), F32), pltpu.VMEM((KV_HEADS, rows, 1), F32),
                            pltpu.VMEM((KV_HEADS, rows, HEAD_DIM), F32)]),
        out_shape=jax.ShapeDtypeStruct((bsz, KV_HEADS, rows, HEAD_DIM), F32),
        compiler_params=_cparams(("arbitrary", "arbitrary")),
        name="sample_attend",
    )(page_table, pool_t, qr_r, sb_steps, jnp.asarray(expand, BF16), o_c, gates,
      keys_major(kv_new[:, :, 2], 128), keys_major(kv_new[:, :, 3], 128),
      keys_major(win_cat[:, :, 0], n_wp), keys_major(win_cat[:, :, 1], n_wp))
    attn = attn.reshape(bsz, KV_HEADS, HPG, lq, HEAD_DIM).transpose(0, 3, 1, 2, 4)
    return attn.astype(BF16).reshape(bsz * lq, ATT_W)


def _ssd_kernel(d_ref, x_ref, bt_ref, c_ref, acol_ref, arow_ref, dt_ref, s0_ref, y_ref, sout_ref, s_sc, *, cl):
    ci = pl.program_id(1)

    @pl.when(ci == 0)
    def _():
        s_sc[...] = s0_ref[0]

    li = lax.broadcasted_iota(jnp.int32, (cl, cl), 0)
    si = lax.broadcasted_iota(jnp.int32, (cl, cl), 1)
    causal = li >= si
    acum_col = _dot_hi(causal.astype(F32), acol_ref[0])
    acum_row = _dot_hi(arow_ref[0], (li <= si).astype(F32))
    dt = dt_ref[0]
    for g in range(SSM_GROUPS):
        cg = c_ref[0, g].astype(BF16)
        btg = bt_ref[0, g]
        cb = _dot(cg, btg.astype(BF16))
        for hh in range(SSM_HEADS // SSM_GROUPS):
            h = g * (SSM_HEADS // SSM_GROUPS) + hh
            ac = acum_col[:, h:h + 1]
            ar = acum_row[h:h + 1, :]
            decay = jnp.exp(jnp.where(causal, ac - ar, NEG))
            xs = x_ref[0, h]
            xd = (xs * dt[:, h:h + 1]).astype(BF16)
            st = s_sc[h]
            y = _dot((cb * decay).astype(BF16), xd) + _dot(cg, st.astype(BF16)) * jnp.exp(ac)
            y_ref[0, h] = y + d_ref[h] * xs
            a_last = ar[:, cl - 1:cl]
            snew = _dot((btg * jnp.exp(a_last - ar)).astype(BF16), xd)
            s_sc[h] = st * jnp.exp(a_last) + snew

    @pl.when(ci == pl.num_programs(1) - 1)
    def _():
        sout_ref[0] = s_sc[...]


def ssd_scan(ssm_d, x_hm, b_t, c, a_col, a_row, dt_col, s0_t, cl):
    bsz, nh, t, p = x_hm.shape
    n = D_STATE
    nc = t // cl
    kern = functools.partial(_ssd_kernel, cl=cl)
    return pl.pallas_call(
        kern,
        grid=(bsz, nc),
        in_specs=[pl.BlockSpec(memory_space=pltpu.SMEM),
                  pl.BlockSpec((1, nh, cl, p), lambda b, c_: (b, 0, c_, 0)),
                  pl.BlockSpec((1, SSM_GROUPS, n, cl), lambda b, c_: (b, 0, 0, c_)),
                  pl.BlockSpec((1, SSM_GROUPS, cl, n), lambda b, c_: (b, 0, c_, 0)),
                  pl.BlockSpec((1, cl, 128), lambda b, c_: (b, c_, 0)),
                  pl.BlockSpec((1, 8, cl), lambda b, c_: (b, 0, c_)),
                  pl.BlockSpec((1, cl, 128), lambda b, c_: (b, c_, 0)),
                  pl.BlockSpec((1, nh, n, p), lambda b, c_: (b, 0, 0, 0))],
        out_specs=[pl.BlockSpec((1, nh, cl, p), lambda b, c_: (b, 0, c_, 0)),
                   pl.BlockSpec((1, nh, n, p), lambda b, c_: (b, 0, 0, 0))],
        out_shape=[jax.ShapeDtypeStruct((bsz, nh, t, p), F32), jax.ShapeDtypeStruct((bsz, nh, n, p), F32)],
        scratch_shapes=[pltpu.VMEM((nh, n, p), F32)],
        compiler_params=_cparams(("parallel", "arbitrary")),
        name="ssd_scan",
    )(ssm_d, x_hm, b_t, c, a_col, a_row, dt_col, s0_t)


def _ssd_prompt_kernel(d_ref, xbc_ref, sm_ref, z_ref, cw_ref, cb_ref, dtb_ref, ah_ref, gn_ref, o_ref, sout_ref,
                       s_sc, tail_sc, *, cl):
    ci = pl.program_id(1)

    @pl.when(ci == 0)
    def _():
        s_sc[...] = jnp.zeros(s_sc.shape, F32)
        tail_sc[...] = jnp.zeros(tail_sc.shape, F32)

    xb = xbc_ref[...]
    prev = tail_sc[...]
    row8 = lax.broadcasted_iota(jnp.int32, (8, CONV_DIM), 0)

    def shifted(j):
        body = pltpu.roll(xb, j, axis=0)
        top = jnp.where(row8 < j, pltpu.roll(prev, j, axis=0), body[0:8])
        return jnp.concatenate([top, body[8:]], axis=0)

    conv = cb_ref[...] + shifted(CONV_W - 1) * cw_ref[0:1]
    for j in range(1, CONV_W):
        conv = conv + (shifted(CONV_W - 1 - j) if j < CONV_W - 1 else xb) * cw_ref[j:j + 1]
    tail_sc[...] = xb[cl - 8:cl]
    act = conv * jax.nn.sigmoid(conv)
    n_bc = SSM_GROUPS * D_STATE
    bmat = act[:, D_INNER:D_INNER + n_bc]
    cmat = act[:, D_INNER + n_bc:D_INNER + 2 * n_bc]
    x_dt = sm_ref[...] + dtb_ref[...]
    dt = jnp.maximum(x_dt, 0.0) + jnp.log(1.0 + jnp.exp(-jnp.abs(x_dt)))
    a = ah_ref[...] * dt

    li = lax.broadcasted_iota(jnp.int32, (cl, cl), 0)
    si = lax.broadcasted_iota(jnp.int32, (cl, cl), 1)
    causal = li >= si
    acum_col = _dot_hi(causal.astype(F32), a)
    acum_row = _dot_hi(a.T[N_GATES:N_GATES + SSM_HEADS], (li <= si).astype(F32))
    bt = bmat.T
    bt_b = bt.astype(BF16)
    lane = lax.broadcasted_iota(jnp.int32, (cl, 128), 1)
    low = lane < SSM_HEAD_DIM
    low_s = lax.broadcasted_iota(jnp.int32, (128, 128), 1) < SSM_HEAD_DIM

    def col(arr, h):
        return arr[:, N_GATES + h:N_GATES + h + 1]

    per_group = SSM_HEADS // SSM_GROUPS
    cg_of, cb_of = {}, {}
    ys = []
    for j in range(SSM_HEADS // 2):
        ha, hb = 2 * j, 2 * j + 1
        g = ha // per_group
        if g not in cg_of:
            in_group = (lane >= g * D_STATE) & (lane < (g + 1) * D_STATE)
            cg_of[g] = jnp.where(in_group, cmat, 0.0).astype(BF16)
            cb_of[g] = _dot(cg_of[g], bt_b)
        cg, cb = cg_of[g], cb_of[g]
        xs_pair = act[:, 128 * j:128 * (j + 1)]
        xd = (xs_pair * jnp.where(low, col(dt, ha), col(dt, hb))).astype(BF16)
        st = s_sc[j]
        y_off = _dot(cg, st.astype(BF16)) * jnp.where(low, jnp.exp(col(acum_col, ha)), jnp.exp(col(acum_col, hb)))
        y_h, s_h, dec_h = [], [], []
        for h in (ha, hb):
            ar = acum_row[h:h + 1, :]
            decay = jnp.exp(jnp.where(causal, col(acum_col, h) - ar, NEG))
            y_h.append(_dot((cb * decay).astype(BF16), xd))
            a_last = ar[:, cl - 1:cl]
            s_h.append(_dot((bt * jnp.exp(a_last - ar)).astype(BF16), xd))
            dec_h.append(jnp.exp(a_last))
        skip = jnp.where(low[0:1], d_ref[ha], d_ref[hb])
        ys.append(jnp.where(low, y_h[0], y_h[1]) + y_off + skip * xs_pair)
        s_sc[j] = st * jnp.where(low_s, dec_h[0], dec_h[1]) + jnp.where(low_s, s_h[0], s_h[1])

    z = z_ref[...]
    gated = jnp.concatenate(ys, axis=1) * (z * jax.nn.sigmoid(z))
    out = gated * lax.rsqrt(jnp.mean(gated * gated, axis=-1, keepdims=True) + EPS) * gn_ref[...]
    o_ref[...] = out.astype(o_ref.dtype)

    @pl.when(ci == pl.num_programs(1) - 1)
    def _():
        sout_ref[0] = s_sc[...]


def ssd_prompt(xbc, sm, z, w, bsz, t):
    cl = SSM_CHUNK
    nc = t // cl
    lanes = lambda v: jnp.zeros((1, 128), F32).at[0, N_GATES:N_GATES + SSM_HEADS].set(v)
    row = lambda width: pl.BlockSpec((cl, width), lambda b, c: (b * nc + c, 0))
    fixed = lambda r, c_: pl.BlockSpec((r, c_), lambda b, c: (0, 0))
    n_pair = SSM_HEADS // 2
    out, s_pairs = pl.pallas_call(
        functools.partial(_ssd_prompt_kernel, cl=cl),
        grid=(bsz, nc),
        in_specs=[pl.BlockSpec(memory_space=pltpu.SMEM), row(CONV_DIM), row(128), row(D_INNER),
                  fixed(CONV_W, CONV_DIM), fixed(1, CONV_DIM), fixed(1, 128), fixed(1, 128), fixed(1, D_INNER)],
        out_specs=[row(D_INNER), pl.BlockSpec((1, n_pair, 128, 128), lambda b, c: (b, 0, 0, 0))],
        out_shape=[jax.ShapeDtypeStruct((bsz * t, D_INNER), BF16),
                   jax.ShapeDtypeStruct((bsz, n_pair, 128, 128), F32)],
        scratch_shapes=[pltpu.VMEM((n_pair, 128, 128), F32), pltpu.VMEM((8, CONV_DIM), F32)],
        compiler_params=_cparams(("parallel", "arbitrary")),
        name="ssd_prompt",
    )(w['ssm_d'], xbc, sm, z, w['conv_w'], w['conv_b'].reshape(1, CONV_DIM), lanes(w['dt_bias']),
      lanes(-jnp.exp(w['a_log'])), w['ssm_norm'].reshape(1, D_INNER))
    s6 = s_pairs.reshape(bsz, n_pair, SSM_GROUPS, D_STATE, 2, SSM_HEAD_DIM)
    per_group = SSM_HEADS // SSM_GROUPS
    heads = [s6[:, h // 2, h // per_group, :, h % 2, :] for h in range(SSM_HEADS)]
    return out, jnp.stack(heads, axis=1).transpose(0, 1, 3, 2)


def _rmsnorm(x, g):
    return x * lax.rsqrt(jnp.mean(x * x, axis=-1, keepdims=True) + EPS) * g


def _rope(x, pos):
    half = HEAD_DIM // 2
    inv = ROPE_THETA ** (-jnp.arange(half, dtype=F32) / half)
    ang = pos.astype(F32)[:, None] * inv[None, :]
    cos, sin = jnp.cos(ang)[:, None, :], jnp.sin(ang)[:, None, :]
    x1, x2 = x[..., :half], x[..., half:]
    return jnp.concatenate([x1 * cos - x2 * sin, x2 * cos + x1 * sin], axis=-1)


def _ssd_inputs(xbc_all, dt_raw, w, t_pad):
    bsz = xbc_all.shape[0]
    length = xbc_all.shape[1] - (CONV_W - 1)
    conv = w['conv_b'] + xbc_all[:, 0:length] * w['conv_w'][0]
    for j in range(1, CONV_W):
        conv = conv + xbc_all[:, j:j + length] * w['conv_w'][j]
    act = jax.nn.silu(conv)
    dt = jax.nn.softplus(dt_raw + w['dt_bias'])
    a = -jnp.exp(w['a_log']) * dt
    pad = t_pad - length
    if pad:
        act = jnp.pad(act, ((0, 0), (0, pad), (0, 0)))
        dt = jnp.pad(dt, ((0, 0), (0, pad), (0, 0)))
        a = jnp.pad(a, ((0, 0), (0, pad), (0, 0)))
    xs = act[..., :D_INNER].reshape(bsz, t_pad, SSM_HEADS, SSM_HEAD_DIM).transpose(0, 2, 1, 3)
    b_in = act[..., D_INNER:D_INNER + SSM_GROUPS * D_STATE].reshape(bsz, t_pad, SSM_GROUPS, D_STATE)
    c_in = act[..., D_INNER + SSM_GROUPS * D_STATE:].reshape(bsz, t_pad, SSM_GROUPS, D_STATE)
    lane_pad = ((0, 0), (0, 0), (0, 128 - SSM_HEADS))
    return (xs, b_in.transpose(0, 2, 3, 1), c_in.transpose(0, 2, 1, 3), jnp.pad(a, lane_pad),
            a.transpose(0, 2, 1), jnp.pad(dt, lane_pad))


def _moe(f_b, logits, w):
    n_tok = f_b.shape[0]
    pg = jax.nn.softmax(logits[:, :N_EXPERT_GROUPS], axis=-1)
    g_sel = jnp.argmax(pg, axis=-1)
    g_w = jnp.max(pg, axis=-1)
    el = logits[:, N_EXPERT_GROUPS:N_EXPERT_GROUPS + N_EXPERTS].reshape(n_tok, N_EXPERT_GROUPS, EXPERTS_PER_GROUP)
    el = jnp.take_along_axis(el, g_sel[:, None, None], axis=1)[:, 0]
    top_p, top_i = lax.top_k(jax.nn.softmax(el, axis=-1), TOP_K)
    wts = g_w[:, None] * top_p / jnp.sum(top_p, axis=-1, keepdims=True)
    eid = (g_sel[:, None] * EXPERTS_PER_GROUP + top_i).reshape(-1).astype(jnp.int32)
    n_asg = eid.shape[0]
    onehot = (eid[:, None] == jnp.arange(N_EXPERTS, dtype=jnp.int32)[None, :]).astype(jnp.int32)
    within = jnp.take_along_axis(jnp.cumsum(onehot, axis=0), eid[:, None], axis=1)[:, 0] - 1
    sizes = jnp.sum(onehot, axis=0)
    padded = (sizes + MOE_ROWS - 1) // MOE_ROWS * MOE_ROWS
    pend = jnp.cumsum(padded)
    dest = (pend - padded)[eid] + within
    n_blk = -(-n_asg // MOE_ROWS) + N_EXPERTS
    src = jnp.zeros((n_blk * MOE_ROWS,), jnp.int32).at[dest].set(jnp.arange(n_asg, dtype=jnp.int32) // TOP_K)
    xpad = f_b[src]
    blk_start = jnp.arange(n_blk, dtype=jnp.int32) * MOE_ROWS
    blk_e = jnp.minimum(jnp.sum((pend[None, :] <= blk_start[:, None]).astype(jnp.int32), axis=1), N_EXPERTS - 1)
    n_used = (pend[-1] // MOE_ROWS).astype(jnp.int32).reshape(1)
    return xpad, blk_e, n_used, dest.reshape(n_tok, TOP_K).T.reshape(-1), jnp.pad(wts, ((0, 0), (0, 128 - TOP_K)))


def _token_dispatch(x2, mix_in, w, tm):
    pad = 128 - N_EXPERT_GROUPS - N_EXPERTS
    w_router = jnp.concatenate([w['w_rg'], w['w_re'], jnp.zeros((D_MODEL, pad), F32)], axis=1)
    b_router = jnp.concatenate([w['b_rg'], w['b_re'], jnp.zeros((pad,), F32)]).reshape(1, 128)
    h, f_b, logits = outproj(x2, mix_in[0], mix_in[1], w['w_out_b'], w['ffn_norm'], w_router, b_router, tm)
    return (h,) + _moe(f_b, logits, w)


def _token_finish(dispatched, p2, w, tm):
    h, xpad, blk_e, n_used, dest01, wts = dispatched
    ypad = moe_experts(xpad, blk_e, n_used, w['w_gate'], w['w_up'], w['w_down'])
    return moe_combine_ple(h, ypad[dest01], wts, p2, w['wpg_b'], w['wpp_b'], w['ple_norm'], tm)


def _ssd_finish(y_hm, z, w, length):
    bsz = y_hm.shape[0]
    y = y_hm[:, :, :length].transpose(0, 2, 1, 3).reshape(bsz, length, D_INNER)
    gated = y * jax.nn.silu(z)
    return _rmsnorm(gated, w['ssm_norm']).astype(BF16)


def _prompt_group(x, p, w):
    bsz, t, _ = x.shape
    m = bsz * t
    z, xbc, sm, sm_t, kv, win, qn_t, qr_t, ks_aug, kw, vs_t, vw_t = rms_inproj_prompt(
        x.reshape(m, D_MODEL), w['attn_norm'], w['w_in_r'], w['q_norm'], w['k_norm'], bsz, t)
    kv_new = kv.reshape(bsz, t, 4, KV_HEADS, HEAD_DIM)
    win_new = win.reshape(bsz, t, 2, KV_HEADS, HEAD_DIM)

    n_seg = t // S_CMP
    segs = kv_new[:, :, 0:2].astype(BF16).transpose(2, 0, 3, 1, 4).reshape(2, bsz * KV_HEADS, n_seg, S_CMP * HEAD_DIM)
    kvc = compress(segs, w['cmp_pe'], w['cmp_w1'], w['cmp_w2'])
    attn = nsa_prompt(qn_t, qr_t, kvc[0], kvc[1].transpose(0, 2, 1), ks_aug, vs_t, kw, vw_t, sm_t, bsz, t)

    ssd, ssm_new = ssd_prompt(xbc, sm, z, w, bsz, t)

    dispatched = _token_dispatch(x.reshape(m, D_MODEL), (attn, ssd), w, 512)
    keep = min(WINDOW, t)
    conv_new = xbc.reshape(bsz, t, CONV_DIM)[:, t - (CONV_W - 1):]

    def finish():
        return _token_finish(dispatched, p.reshape(m, PLE_DIM), w, 512).reshape(bsz, t, D_MODEL)

    return finish, kv_new, win_new[:, t - keep:], ssm_new, conv_new


def _sample_group(x, p, pool, page_table, cache_win, state_ssm, state_conv, w):
    bsz, lq, _ = x.shape
    m = bsz * lq
    q_off = page_table.shape[1] * PAGE_SIZE
    q, kvs, z, xbc, sm = rms_inproj(x.reshape(m, D_MODEL), w['attn_norm'], w['w_in_r'], m)
    pos = q_off + jnp.arange(lq)
    qn = _rmsnorm(q.reshape(bsz, lq, N_HEADS, HEAD_DIM), w['q_norm'])
    qr = _rope(qn, pos)
    kvs = kvs.reshape(bsz, lq, 6, KV_HEADS, HEAD_DIM)
    k_c = _rmsnorm(kvs[:, :, 0], w['k_norm'][0])
    k_s = _rope(_rmsnorm(kvs[:, :, 2], w['k_norm'][1]), pos)
    k_w = _rope(_rmsnorm(kvs[:, :, 4], w['k_norm'][2]), pos)
    kv_new = jnp.stack([k_c, kvs[:, :, 1], k_s, kvs[:, :, 3]], axis=2)
    win_new = jnp.stack([k_w, kvs[:, :, 5]], axis=2)
    win_cat = jnp.concatenate([cache_win, win_new], axis=1)

    pool_t = jnp.transpose(pool, (0, 2, 3, 4, 1))
    kvc = compress_paged(pool_t, page_table, w['cmp_pe'], w['cmp_w1'], w['cmp_w2'])
    attn = nsa_sample(qn, qr, kvc, kv_new, win_cat, pool_t, page_table, sm[:, :N_GATES], q_off)

    xbc_all = jnp.concatenate([state_conv, xbc.reshape(bsz, lq, CONV_DIM)], axis=1)
    dt_raw = sm[:, N_GATES:N_GATES + SSM_HEADS].reshape(bsz, lq, SSM_HEADS)
    xs, b_t, c_in, a_col, a_row, dt_col = _ssd_inputs(xbc_all, dt_raw, w, SSM_CHUNK)
    y_hm, s_t = ssd_scan(w['ssm_d'], xs, b_t, c_in, a_col, a_row, dt_col, state_ssm.transpose(0, 1, 3, 2), SSM_CHUNK)
    ssd = _ssd_finish(y_hm, z.reshape(bsz, lq, D_INNER), w, lq).reshape(m, D_INNER)

    y = _token_finish(_token_dispatch(x.reshape(m, D_MODEL), (attn, ssd), w, m), p.reshape(m, PLE_DIM), w, m)
    keep = cache_win.shape[1]
    return (y.reshape(bsz, lq, D_MODEL), kv_new, win_cat[:, win_cat.shape[1] - keep:], s_t.transpose(0, 1, 3, 2),
            xbc_all[:, xbc_all.shape[1] - (CONV_W - 1):])


def kernel(x_prompt, x_sample, cache_kv, cache_win, state_ssm, state_conv, page_table, p_prompt, p_sample,
           w_in, w_out, q_norm, k_norm, cmp_pe, cmp_w1, cmp_w2, conv_w, conv_b, dt_bias, a_log, ssm_d, ssm_norm,
           attn_norm, ffn_norm, w_rg, b_rg, w_re, b_re, w_gate, w_up, w_down, w_ple_proj, ple_norm, w_ple_gate):
    depth = w_in.shape[0]
    hp, hs = x_prompt, x_sample
    outs = [[] for _ in range(8)]
    cuts = np.cumsum((ATT_W, 6 * KV_HEADS * HEAD_DIM, N_GATES, D_INNER, CONV_DIM, SSM_HEADS))
    for l in range(depth):
        wi = w_in[l]
        w_in_r = jnp.concatenate(
            [wi[:, :cuts[1]], wi[:, cuts[2]:cuts[3]], wi[:, cuts[3]:cuts[4]], wi[:, cuts[1]:cuts[2]],
             wi[:, cuts[4]:cuts[5]], jnp.zeros((D_MODEL, C_SM - N_GATES - SSM_HEADS), F32)], axis=1).astype(BF16)
        w = dict(w_in_r=w_in_r, w_out_b=w_out[l].astype(BF16), q_norm=q_norm[l], k_norm=k_norm[l],
                 cmp_pe=cmp_pe[l], cmp_w1=cmp_w1[l], cmp_w2=cmp_w2[l], conv_w=conv_w[l], conv_b=conv_b[l],
                 dt_bias=dt_bias[l], a_log=a_log[l], ssm_d=ssm_d[l], ssm_norm=ssm_norm[l], attn_norm=attn_norm[l],
                 ffn_norm=ffn_norm[l], w_rg=w_rg[l], b_rg=b_rg[l], w_re=w_re[l], b_re=b_re[l],
                 w_gate=w_gate[l], w_up=w_up[l], w_down=w_down[l],
                 wpp_b=w_ple_proj[l].astype(BF16), ple_norm=ple_norm[l], wpg_b=w_ple_gate[l].astype(BF16))
        finish_prompt, *rest_p = _prompt_group(hp, p_prompt[l], w)
        hs, *rest_s = _sample_group(hs, p_sample[l], cache_kv[l], page_table, cache_win[l], state_ssm[l],
                                    state_conv[l], w)
        hp = finish_prompt()
        for j in range(4):
            outs[2 * j].append(rest_p[j])
            outs[2 * j + 1].append(rest_s[j])
    return (hp, hs) + tuple(jnp.stack(o) for o in outs)
```

```python
import functools
import math

import numpy as np
import jax
import jax.numpy as jnp
from jax import lax
from jax.experimental import pallas as pl
from jax.experimental.pallas import tpu as pltpu

F32 = jnp.float32
BF16 = jnp.bfloat16

D_MODEL = 1024
PAGE_SIZE = 128
N_HEADS = 8
HEAD_DIM = 64
KV_HEADS = 2
HPG = N_HEADS // KV_HEADS
ATT_W = N_HEADS * HEAD_DIM
L_CMP = 32
S_CMP = 16
L_SLC = 64
N_SEL = 16
WINDOW = 512
CMP_HID = 64
ROPE_THETA = 10000.0
SSM_HEADS = 8
SSM_HEAD_DIM = 64
D_INNER = SSM_HEADS * SSM_HEAD_DIM
SSM_GROUPS = 2
D_STATE = 64
CONV_W = 4
CONV_DIM = D_INNER + 2 * SSM_GROUPS * D_STATE
SSM_CHUNK = 128
N_EXPERT_GROUPS = 4
EXPERTS_PER_GROUP = 8
N_EXPERTS = N_EXPERT_GROUPS * EXPERTS_PER_GROUP
TOP_K = 2
D_EXPERT = 512
PLE_DIM = 256
EPS = 1e-6
N_GATES = 3 * N_HEADS
C_Q, C_KV, C_Z, C_XBC, C_SM = 512, 768, 512, 768, 128
D_IN_PAD = C_Q + C_KV + C_Z + C_XBC + C_SM

NEG = -1e30
VMEM_LIMIT = 48 * 1024 * 1024
MOE_ROWS = 256


def _cparams(sem):
    return pltpu.CompilerParams(dimension_semantics=sem, vmem_limit_bytes=VMEM_LIMIT)


def _dot(a, b):
    return jnp.dot(a, b, preferred_element_type=F32)


def _dot_nt(a, b):
    return lax.dot_general(a, b, (((1,), (1,)), ((), ())), preferred_element_type=F32)


def _dot_hi(a, b):
    return jnp.dot(a, b, preferred_element_type=F32, precision=lax.Precision.HIGHEST)


def _rms_inproj_kernel(x_ref, g_ref, w_ref, q_ref, kv_ref, z_ref, xbc_ref, sm_ref):
    x = x_ref[...]
    y = x * lax.rsqrt(jnp.mean(x * x, axis=-1, keepdims=True) + EPS) * g_ref[...]
    yb = y.astype(BF16)
    c0 = 0
    for ref, width in ((q_ref, C_Q), (kv_ref, C_KV), (z_ref, C_Z), (xbc_ref, C_XBC), (sm_ref, C_SM)):
        ref[...] = _dot(yb, w_ref[:, c0:c0 + width])
        c0 += width


def rms_inproj(x, gain, w_r, tm):
    m = x.shape[0]
    widths = (C_Q, C_KV, C_Z, C_XBC, C_SM)
    return pl.pallas_call(
        _rms_inproj_kernel,
        grid=(m // tm,),
        in_specs=[pl.BlockSpec((tm, D_MODEL), lambda i: (i, 0)),
                  pl.BlockSpec((1, D_MODEL), lambda i: (0, 0)),
                  pl.BlockSpec((D_MODEL, D_IN_PAD), lambda i: (0, 0))],
        out_specs=[pl.BlockSpec((tm, w), lambda i: (i, 0)) for w in widths],
        out_shape=[jax.ShapeDtypeStruct((m, w), F32) for w in widths],
        compiler_params=_cparams(("parallel",)),
        name="rms_inproj",
    )(x, gain.reshape(1, D_MODEL), w_r)


def _inproj_prompt_kernel(x_ref, g_ref, w_ref, cos_ref, sin_ref, gq_ref, gk_ref, z_ref, xbc_ref, sm_ref, smt_ref, kv_ref,
                          win_ref, qn_ref, qr_ref, ksa_ref, kw_ref, vst_ref, vwt_ref, *, tiles_per_seq):
    tm = x_ref.shape[0]
    x = x_ref[...]
    yb = (x * lax.rsqrt(jnp.mean(x * x, axis=-1, keepdims=True) + EPS) * g_ref[...]).astype(BF16)
    cos, sin = cos_ref[...], sin_ref[...]
    scale = HEAD_DIM ** -0.5
    half = HEAD_DIM // 2

    def proj(c0, width=128):
        return _dot(yb, w_ref[:, c0:c0 + width])

    def head_norm(x_t, gain):
        return x_t * lax.rsqrt(jnp.mean(x_t * x_t, axis=0, keepdims=True) + EPS) * gain

    def rotate(y_t):
        y1, y2 = y_t[0:half], y_t[half:HEAD_DIM]
        return jnp.concatenate([y1 * cos - y2 * sin, y2 * cos + y1 * sin], axis=0)

    def per_group(blk, fn):
        t = blk.T
        return jnp.concatenate([fn(t[g * HEAD_DIM:(g + 1) * HEAD_DIM]) for g in range(KV_HEADS)], axis=0).T

    for j in range(ATT_W // 128):
        q_t = proj(128 * j).T
        for h2 in range(128 // HEAD_DIM):
            head = (128 // HEAD_DIM) * j + h2
            g, hh = head // HPG, head % HPG
            qn = head_norm(q_t[h2 * HEAD_DIM:(h2 + 1) * HEAD_DIM], gq_ref[...])
            qn_ref[0, g, 0, :, hh * tm:(hh + 1) * tm] = (qn * scale).astype(BF16)
            qr_ref[0, g, 0, :, hh * tm:(hh + 1) * tm] = (rotate(qn) * scale).astype(BF16)

    c_kv = C_Q
    k_c = per_group(proj(c_kv), lambda t: head_norm(t, gk_ref[0]))
    v_c = proj(c_kv + 128)
    k_s = per_group(proj(c_kv + 256), lambda t: rotate(head_norm(t, gk_ref[1])))
    v_s = proj(c_kv + 384)
    k_w = per_group(proj(c_kv + 512), lambda t: rotate(head_norm(t, gk_ref[2])))
    v_w = proj(c_kv + 640)
    kv_ref[:, 0:128] = k_c
    kv_ref[:, 128:256] = v_c
    kv_ref[:, 256:384] = k_s
    kv_ref[:, 384:512] = v_s
    win_ref[:, 0:128] = k_w
    win_ref[:, 128:256] = v_w
    kw_ref[...] = k_w.astype(BF16)
    pos0 = (pl.program_id(0) % tiles_per_seq) * tm
    lane = lax.broadcasted_iota(jnp.int32, (tm, 128), 1)
    blk_of_row = (pos0 + lax.broadcasted_iota(jnp.int32, (tm, 128), 0)) // L_SLC
    onehot = (blk_of_row == (lane & (HEAD_DIM - 1))).astype(BF16)
    k_sb = k_s.astype(BF16)
    ksa_ref[0, 0] = jnp.where(lane < HEAD_DIM, k_sb, onehot)
    ksa_ref[0, 1] = jnp.where(lane >= HEAD_DIM, k_sb, onehot)
    vs_t = v_s.T.astype(BF16)
    vw_t = v_w.T.astype(BF16)
    for g in range(KV_HEADS):
        vst_ref[0, g, 0] = vs_t[g * HEAD_DIM:(g + 1) * HEAD_DIM]
        vwt_ref[0, g, 0] = vw_t[g * HEAD_DIM:(g + 1) * HEAD_DIM]
    z_ref[...] = proj(C_Q + C_KV, C_Z)
    xbc_ref[...] = proj(C_Q + C_KV + C_Z, C_XBC)
    sm = proj(C_Q + C_KV + C_Z + C_XBC)
    sm_ref[...] = sm
    smt_ref[0, 0] = sm.T[0:32]


def rms_inproj_prompt(x, gain, w_r, q_gain, k_gain, bsz, t):
    tm = NSA_TQ
    assert NSA_TK == tm and NSA_TKW == tm and t // L_SLC <= HEAD_DIM
    m = bsz * t
    nt = t // tm
    inv = ROPE_THETA ** (-jnp.arange(HEAD_DIM // 2, dtype=F32) / (HEAD_DIM // 2))
    ang = inv[:, None] * jnp.arange(t, dtype=F32)[None, :]
    row = lambda width: pl.BlockSpec((tm, width), lambda i: (i, 0))
    qspec = pl.BlockSpec((1, KV_HEADS, 1, HEAD_DIM, HPG * tm), lambda i: (i // nt, 0, i % nt, 0, 0))
    vspec = pl.BlockSpec((1, KV_HEADS, 1, HEAD_DIM, tm), lambda i: (i // nt, 0, i % nt, 0, 0))
    tspec = pl.BlockSpec((HEAD_DIM // 2, tm), lambda i: (0, i % nt))
    qshape = jax.ShapeDtypeStruct((bsz, KV_HEADS, nt, HEAD_DIM, HPG * tm), BF16)
    vshape = jax.ShapeDtypeStruct((bsz, KV_HEADS, nt, HEAD_DIM, tm), BF16)
    return pl.pallas_call(
        functools.partial(_inproj_prompt_kernel, tiles_per_seq=nt),
        grid=(m // tm,),
        in_specs=[row(D_MODEL),
                  pl.BlockSpec((1, D_MODEL), lambda i: (0, 0)),
                  pl.BlockSpec((D_MODEL, D_IN_PAD), lambda i: (0, 0)),
                  tspec, tspec,
                  pl.BlockSpec((HEAD_DIM, 1), lambda i: (0, 0)),
                  pl.BlockSpec((3, HEAD_DIM, 1), lambda i: (0, 0, 0))],
        out_specs=[row(C_Z), row(C_XBC), row(C_SM),
                   pl.BlockSpec((1, 1, 32, tm), lambda i: (i // nt, i % nt, 0, 0)),
                   row(512), row(256), qspec, qspec,
                   pl.BlockSpec((1, KV_HEADS, tm, 128), lambda i: (i // nt, 0, i % nt, 0)),
                   row(128), vspec, vspec],
        out_shape=[jax.ShapeDtypeStruct((m, C_Z), F32), jax.ShapeDtypeStruct((m, C_XBC), F32),
                   jax.ShapeDtypeStruct((m, C_SM), F32), jax.ShapeDtypeStruct((bsz, nt, 32, tm), F32),
                   jax.ShapeDtypeStruct((m, 512), F32), jax.ShapeDtypeStruct((m, 256), F32), qshape, qshape,
                   jax.ShapeDtypeStruct((bsz, KV_HEADS, t, 128), BF16),
                   jax.ShapeDtypeStruct((m, 128), BF16), vshape, vshape],
        compiler_params=_cparams(("parallel",)),
        name="rms_inproj_prompt",
    )(x, gain.reshape(1, D_MODEL), w_r, jnp.cos(ang), jnp.sin(ang), q_gain.reshape(HEAD_DIM, 1),
      k_gain.reshape(3, HEAD_DIM, 1))


def _outproj_kernel(x_ref, a_ref, s_ref, w_ref, g_ref, wr_ref, br_ref, h_ref, f_ref, lg_ref):
    acc = _dot(a_ref[...], w_ref[0:ATT_W, :]) + _dot(s_ref[...], w_ref[ATT_W:ATT_W + D_INNER, :])
    h = x_ref[...] + acc
    h_ref[...] = h
    f = h * lax.rsqrt(jnp.mean(h * h, axis=-1, keepdims=True) + EPS) * g_ref[...]
    f_hi = f.astype(BF16)
    f_ref[...] = f_hi
    f_lo = (f - f_hi.astype(F32)).astype(BF16)
    lg_ref[...] = (_dot(f_hi, wr_ref[0]) + (_dot(f_hi, wr_ref[1]) + _dot(f_lo, wr_ref[0]))) + br_ref[...]


def outproj(x, attn, ssd, w_out_b, ffn_gain, w_router, b_router, tm):
    m = x.shape[0]
    row = lambda width: pl.BlockSpec((tm, width), lambda i: (i, 0))
    fixed = lambda r, c: pl.BlockSpec((r, c), lambda i: (0, 0))
    w_hi = w_router.astype(BF16)
    w_hi_rest = (w_router - w_hi.astype(F32)).astype(BF16)
    return pl.pallas_call(
        _outproj_kernel,
        grid=(m // tm,),
        in_specs=[row(D_MODEL), row(ATT_W), row(D_INNER), fixed(ATT_W + D_INNER, D_MODEL),
                  fixed(1, D_MODEL), pl.BlockSpec((2, D_MODEL, 128), lambda i: (0, 0, 0)), fixed(1, 128)],
        out_specs=[row(D_MODEL), row(D_MODEL), row(128)],
        out_shape=[jax.ShapeDtypeStruct((m, D_MODEL), F32), jax.ShapeDtypeStruct((m, D_MODEL), BF16),
                   jax.ShapeDtypeStruct((m, 128), F32)],
        compiler_params=_cparams(("parallel",)),
        name="outproj",
    )(x, attn, ssd, w_out_b, ffn_gain.reshape(1, D_MODEL), jnp.stack([w_hi, w_hi_rest]), b_router)


def _ple_kernel(h_ref, y0_ref, y1_ref, wt_ref, p_ref, wg_ref, wp_ref, g_ref, o_ref):
    wt = wt_ref[...]
    h = h_ref[...] + (y0_ref[...] * wt[:, 0:1] + y1_ref[...] * wt[:, 1:2])
    gate = jax.nn.sigmoid(_dot(h.astype(BF16), wg_ref[...]))
    e = _dot(p_ref[...].astype(BF16), wp_ref[...])
    e = e * lax.rsqrt(jnp.mean(e * e, axis=-1, keepdims=True) + EPS) * g_ref[...]
    o_ref[...] = h + gate * e


def moe_combine_ple(h, y01, wts, p, wg_b, wp_b, gain, tm):
    m = h.shape[0]
    nt = m // tm
    return pl.pallas_call(
        _ple_kernel,
        grid=(nt,),
        in_specs=[pl.BlockSpec((tm, D_MODEL), lambda i: (i, 0)),
                  pl.BlockSpec((tm, D_MODEL), lambda i: (i, 0)),
                  pl.BlockSpec((tm, D_MODEL), lambda i: (i + nt, 0)),
                  pl.BlockSpec((tm, 128), lambda i: (i, 0)),
                  pl.BlockSpec((tm, PLE_DIM), lambda i: (i, 0)),
                  pl.BlockSpec((D_MODEL, D_MODEL), lambda i: (0, 0)),
                  pl.BlockSpec((PLE_DIM, D_MODEL), lambda i: (0, 0)),
                  pl.BlockSpec((1, D_MODEL), lambda i: (0, 0))],
        out_specs=pl.BlockSpec((tm, D_MODEL), lambda i: (i, 0)),
        out_shape=jax.ShapeDtypeStruct((m, D_MODEL), F32),
        compiler_params=_cparams(("parallel",)),
        name="moe_combine_ple",
    )(h, y01, y01, wts, p, wg_b, wp_b, gain.reshape(1, D_MODEL))


def _moe_kernel(be_ref, nb_ref, x_ref, wg_ref, wu_ref, wd_ref, y_ref, wg_sc, wu_sc, wd_sc):
    i = pl.program_id(0)

    @pl.when((i == 0) | (be_ref[i] != be_ref[jnp.maximum(i - 1, 0)]))
    def _():
        wg_sc[...] = wg_ref[0].astype(BF16)
        wu_sc[...] = wu_ref[0].astype(BF16)
        wd_sc[...] = wd_ref[0].astype(BF16)

    @pl.when(i < nb_ref[0])
    def _():
        x = x_ref[...]
        a = _dot(x, wg_sc[...])
        hb = (a * jax.nn.sigmoid(a)) * _dot(x, wu_sc[...])
        y_ref[...] = _dot(hb.astype(BF16), wd_sc[...])

    @pl.when(i >= nb_ref[0])
    def _():
        y_ref[...] = jnp.zeros(y_ref.shape, F32)


def moe_experts(xpad, blk_e, n_used, w_gate, w_up, w_down):
    n_blk = xpad.shape[0] // MOE_ROWS
    grid_spec = pltpu.PrefetchScalarGridSpec(
        num_scalar_prefetch=2,
        grid=(n_blk,),
        in_specs=[pl.BlockSpec((MOE_ROWS, D_MODEL), lambda i, be, nb: (i, 0)),
                  pl.BlockSpec((1, D_MODEL, D_EXPERT), lambda i, be, nb: (be[i], 0, 0)),
                  pl.BlockSpec((1, D_MODEL, D_EXPERT), lambda i, be, nb: (be[i], 0, 0)),
                  pl.BlockSpec((1, D_EXPERT, D_MODEL), lambda i, be, nb: (be[i], 0, 0))],
        out_specs=pl.BlockSpec((MOE_ROWS, D_MODEL), lambda i, be, nb: (i, 0)),
        scratch_shapes=[pltpu.VMEM((D_MODEL, D_EXPERT), BF16), pltpu.VMEM((D_MODEL, D_EXPERT), BF16),
                        pltpu.VMEM((D_EXPERT, D_MODEL), BF16)],
    )
    return pl.pallas_call(
        _moe_kernel,
        grid_spec=grid_spec,
        out_shape=jax.ShapeDtypeStruct((n_blk * MOE_ROWS, D_MODEL), F32),
        compiler_params=_cparams(("arbitrary",)),
        name="moe_experts",
    )(blk_e, n_used, xpad, w_gate, w_up, w_down)


def _compress_kernel(seg_ref, w1a_ref, w1b_ref, pe_ref, w1_ref, w2_ref, o_ref):
    seg = seg_ref[0, 0]
    n_seg = seg.shape[0]
    hid0 = _dot(pe_ref[0], w1_ref[0])[0:1]
    p0 = _dot(seg, w1a_ref[0])
    p1 = pltpu.roll(_dot(seg, w1b_ref[0]), n_seg - 1, axis=0)
    hid = hid0 + p0 + p1
    act = hid * jax.nn.sigmoid(hid)
    o_ref[0, 0] = _dot(act.astype(BF16), w2_ref[0]).astype(BF16)


def compress(segs, cmp_pe, cmp_w1, cmp_w2):
    _, r, n_seg, k = segs.shape
    w1 = cmp_w1.astype(BF16)
    pe = jnp.broadcast_to(cmp_pe.reshape(2, 1, L_CMP * HEAD_DIM), (2, 8, L_CMP * HEAD_DIM)).astype(BF16)
    return pl.pallas_call(
        _compress_kernel,
        grid=(2, r),
        in_specs=[pl.BlockSpec((1, 1, n_seg, k), lambda s, i: (s, i, 0, 0)),
                  pl.BlockSpec((1, k, CMP_HID), lambda s, i: (s, 0, 0)),
                  pl.BlockSpec((1, k, CMP_HID), lambda s, i: (s, 1, 0)),
                  pl.BlockSpec((1, 8, 2 * k), lambda s, i: (s, 0, 0)),
                  pl.BlockSpec((1, 2 * k, CMP_HID), lambda s, i: (s, 0, 0)),
                  pl.BlockSpec((1, CMP_HID, HEAD_DIM), lambda s, i: (s, 0, 0))],
        out_specs=pl.BlockSpec((1, 1, n_seg, HEAD_DIM), lambda s, i: (s, i, 0, 0)),
        out_shape=jax.ShapeDtypeStruct((2, r, n_seg, HEAD_DIM), BF16),
        compiler_params=_cparams(("parallel", "parallel")),
        name="compress",
    )(segs, w1, w1, pe, w1, cmp_w2.astype(BF16))


PAGES_PER_STEP = 16


def _page_copies(pt_ref, pool_ref, buf, sem, b, j, slot):
    return [pltpu.make_async_copy(pool_ref.at[pt_ref[b, j * PAGES_PER_STEP + p], pl.ds(0, 2)],
                                  buf.at[slot, p], sem.at[slot]) for p in range(PAGES_PER_STEP)]


def _paged_partials_kernel(pt_ref, pool_ref, wa_ref, wb_ref, p0_ref, p1_ref, buf, sem, x_sc):
    b = pl.program_id(0)
    j = pl.program_id(1)
    n_grp = pl.num_programs(1)
    step = b * n_grp + j
    slot = step % 2

    @pl.when(step == 0)
    def _():
        for c in _page_copies(pt_ref, pool_ref, buf, sem, b, j, slot):
            c.start()

    @pl.when(step + 1 < pl.num_programs(0) * n_grp)
    def _():
        wrap = j + 1 == n_grp
        for c in _page_copies(pt_ref, pool_ref, buf, sem, jnp.where(wrap, b + 1, b), jnp.where(wrap, 0, j + 1),
                              1 - slot):
            c.start()

    for c in _page_copies(pt_ref, pool_ref, buf, sem, b, j, slot):
        c.wait()

    n_seg = PAGES_PER_STEP * PAGE_SIZE // S_CMP
    for kv in range(2):
        for p in range(PAGES_PER_STEP):
            x_sc[p * PAGE_SIZE:(p + 1) * PAGE_SIZE, :] = buf[slot, p, kv].reshape(KV_HEADS * HEAD_DIM, PAGE_SIZE).T
        acc0 = jnp.zeros((n_seg, KV_HEADS * CMP_HID), F32)
        acc1 = jnp.zeros((n_seg, KV_HEADS * CMP_HID), F32)
        for s in range(S_CMP):
            xs = x_sc[pl.ds(s, n_seg, stride=S_CMP), :].astype(BF16)
            acc0 = acc0 + _dot(xs, wa_ref[kv, s])
            acc1 = acc1 + _dot(xs, wb_ref[kv, s])
        p0_ref[kv, 0] = acc0
        p1_ref[kv, 0] = acc1


def _compress_finish_kernel(p0_ref, p1_ref, pe_ref, w1_ref, w2_ref, o_ref):
    n_seg = p0_ref.shape[2]
    hid0 = _dot(pe_ref[0], w1_ref[0])[0:1]
    hid = hid0 + p0_ref[0, 0] + pltpu.roll(p1_ref[0, 0], n_seg - 1, axis=0)
    act = hid * jax.nn.sigmoid(hid)
    o_ref[0, 0] = _dot(act.astype(BF16), w2_ref[0]).astype(BF16)


def _per_group(w):
    zero = jnp.zeros_like(w)
    return jnp.concatenate([jnp.concatenate([w, zero], axis=-1), jnp.concatenate([zero, w], axis=-1)], axis=-2)


def compress_paged(pool_t, page_table, cmp_pe, cmp_w1, cmp_w2):
    bsz, n_pages = page_table.shape
    assert n_pages % PAGES_PER_STEP == 0 and KV_HEADS == 2
    n_grp = n_pages // PAGES_PER_STEP
    seg_step = PAGES_PER_STEP * PAGE_SIZE // S_CMP
    n_seg = n_grp * seg_step
    gw = KV_HEADS * HEAD_DIM
    w1 = _per_group(cmp_w1.astype(BF16).reshape(2, L_CMP // S_CMP, S_CMP, HEAD_DIM, CMP_HID))
    part_shape = jax.ShapeDtypeStruct((2, bsz, n_seg, KV_HEADS * CMP_HID), F32)
    part_spec = pl.BlockSpec((2, 1, seg_step, KV_HEADS * CMP_HID), lambda b, j, pt: (0, b, j, 0))
    wspec = pl.BlockSpec((2, S_CMP, gw, KV_HEADS * CMP_HID), lambda b, j, pt: (0, 0, 0, 0))
    p0, p1 = pl.pallas_call(
        _paged_partials_kernel,
        grid_spec=pltpu.PrefetchScalarGridSpec(
            num_scalar_prefetch=1,
            grid=(bsz, n_grp),
            in_specs=[pl.BlockSpec(memory_space=pl.ANY), wspec, wspec],
            out_specs=[part_spec, part_spec],
            scratch_shapes=[pltpu.VMEM((2, PAGES_PER_STEP, 2, KV_HEADS, HEAD_DIM, PAGE_SIZE), F32),
                            pltpu.SemaphoreType.DMA((2,)),
                            pltpu.VMEM((PAGES_PER_STEP * PAGE_SIZE, gw), F32)]),
        out_shape=[part_shape, part_shape],
        compiler_params=_cparams(("arbitrary", "arbitrary")),
        name="paged_partials",
    )(page_table, pool_t, w1[:, 0], w1[:, 1])
    pe = jnp.broadcast_to(cmp_pe.reshape(2, 1, L_CMP * HEAD_DIM), (2, 8, L_CMP * HEAD_DIM)).astype(BF16)
    w1_both = jnp.concatenate([cmp_w1, cmp_w1], axis=-1).astype(BF16)
    pspec = pl.BlockSpec((1, 1, n_seg, KV_HEADS * CMP_HID), lambda s, i: (s, i, 0, 0))
    return pl.pallas_call(
        _compress_finish_kernel,
        grid=(2, bsz),
        in_specs=[pspec, pspec,
                  pl.BlockSpec((1, 8, L_CMP * HEAD_DIM), lambda s, i: (s, 0, 0)),
                  pl.BlockSpec((1, L_CMP * HEAD_DIM, KV_HEADS * CMP_HID), lambda s, i: (s, 0, 0)),
                  pl.BlockSpec((1, KV_HEADS * CMP_HID, gw), lambda s, i: (s, 0, 0))],
        out_specs=pl.BlockSpec((1, 1, n_seg, gw), lambda s, i: (s, i, 0, 0)),
        out_shape=jax.ShapeDtypeStruct((2, bsz, n_seg, gw), BF16),
        compiler_params=_cparams(("parallel", "parallel")),
        name="compress_finish",
    )(p0, p1, pe, w1_both, _per_group(cmp_w2.astype(BF16)))


FLASH_SPLIT = 4


def _flash_scores(k, q_ref, s_ref):
    cw = q_ref.shape[1] // FLASH_SPLIT
    for i in range(FLASH_SPLIT):
        c = slice(i * cw, (i + 1) * cw)
        s_ref[:, c] = _dot(k, q_ref[:, c])


def _flash_step(s_ref, v_t, m_sc, l_sc, acc_sc, keep=None):
    cw = s_ref.shape[1] // FLASH_SPLIT
    for i in range(FLASH_SPLIT):
        c = slice(i * cw, (i + 1) * cw)
        s_t = s_ref[:, c]
        if keep is not None:
            s_t = jnp.where(keep(i * cw, cw), s_t, NEG)
        m_prev = m_sc[:, c]
        m_new = jnp.maximum(m_prev, jnp.max(s_t, axis=0, keepdims=True))
        alpha = jnp.exp(m_prev - m_new)
        p = jnp.exp(s_t - m_new)
        l_sc[:, c] = alpha * l_sc[:, c] + jnp.sum(p, axis=0, keepdims=True)
        acc_sc[:, c] = alpha * acc_sc[:, c] + _dot(v_t, p.astype(BF16))
        m_sc[:, c] = m_new


def _flash_reset(m_sc, l_sc, acc_sc):
    m_sc[...] = jnp.full(m_sc.shape, NEG, F32)
    l_sc[...] = jnp.zeros(l_sc.shape, F32)
    acc_sc[...] = jnp.zeros(acc_sc.shape, F32)


def _nsa_prompt_kernel(qn_ref, qr_ref, kc_ref, vct_ref, ks_ref, vst_ref, kw_ref, vwt_ref, covert_ref, gate_ref,
                       o_ref, qa_sc, qw_sc, m_sc, l_sc, acc_sc, out_sc, score_sc, rank_sc, s_sc, sw_sc, *, tq, tk, tkw):
    qi = pl.program_id(2)
    pos0 = qi * tq
    w = HPG * tq

    def positions(rows, base):
        kpos = base + lax.broadcasted_iota(jnp.int32, (rows, w), 0)
        qpos = pos0 + (lax.broadcasted_iota(jnp.int32, (rows, w), 1) & (tq - 1))
        return kpos, qpos

    g = pl.program_id(1)

    def gate_row(branch):
        rows = [gate_ref[0, 0, pl.ds(3 * (g * HPG + hh) + branch, 1), :] for hh in range(HPG)]
        return jax.nn.sigmoid(jnp.concatenate(rows, axis=1))

    gate = [gate_row(branch) for branch in range(3)]

    kc = kc_ref[0, 0]
    n_c = kc.shape[0]
    cidx, qpos_c = positions(n_c, 0)
    valid = cidx * S_CMP + (L_CMP - 1) <= qpos_c
    s1 = jnp.where(valid, _dot(kc, qn_ref[0, 0, 0]), NEG)
    e1 = jnp.exp(s1 - jnp.max(s1, axis=0, keepdims=True)) * valid.astype(F32)
    p1b = (e1 * (1.0 / jnp.maximum(jnp.sum(e1, axis=0, keepdims=True), 1e-30))).astype(BF16)
    out_sc[...] = gate[0] * _dot(vct_ref[0, 0], p1b)

    p1_stack = jnp.concatenate([p1b[:, hh * tq:(hh + 1) * tq] for hh in range(HPG)], axis=0)
    imp = _dot(covert_ref[...], p1_stack)
    n_s = imp.shape[0]
    blk = lax.broadcasted_iota(jnp.int32, (n_s, tq), 0)
    cur = (pos0 + lax.broadcasted_iota(jnp.int32, (n_s, tq), 1)) // L_SLC
    vis = blk <= cur
    forced = vis & ((blk == 0) | (blk >= cur - 1))
    score_sc[...] = jnp.where(forced, 1e9, jnp.where(vis, imp, -1.0))
    rank_sc[...] = jnp.zeros(rank_sc.shape, F32)
    last_blk = (pos0 + tq - 1) // L_SLC
    n_oct = n_s // 8
    for oi in range(n_oct):
        @pl.when(oi * 8 <= last_blk)
        def _():
            rows = [slice(8 * oj, 8 * oj + 8) for oj in range(n_oct)]
            parts = [rank_sc[r, :] for r in rows]
            for i in range(8 * oi, 8 * oi + 8):
                c = score_sc[i:i + 1, :]
                for oj, r in enumerate(rows):
                    sj = score_sc[r, :]
                    if oj > oi:
                        beats = c >= sj
                    elif oj < oi:
                        beats = c > sj
                    else:
                        above = lax.broadcasted_iota(jnp.int32, (8, tq), 0) > i - 8 * oi
                        beats = (c > sj) | ((c == sj) & above)
                    parts[oj] = parts[oj] + beats.astype(F32)
            for r, part in zip(rows, parts):
                rank_sc[r, :] = part
    sel_bias = jnp.where(vis & (rank_sc[...] < N_SEL), 0.0, NEG).astype(BF16)
    own = pl.ds(pl.multiple_of(g * HEAD_DIM, HEAD_DIM), HEAD_DIM)
    other = pl.ds(pl.multiple_of((1 - g) * HEAD_DIM, HEAD_DIM), HEAD_DIM)
    qa_sc[other, :] = jnp.zeros((HEAD_DIM, w), BF16)
    qw_sc[other, :] = jnp.zeros((HEAD_DIM, w), BF16)
    qa_sc[own, :] = qr_ref[0, 0, 0]
    qw_sc[own, :] = qr_ref[0, 0, 0]
    qa_sc[pl.ds(pl.multiple_of((1 - g) * HEAD_DIM, HEAD_DIM), n_s), :] = jnp.concatenate([sel_bias] * HPG, axis=1)

    def sel_keys(kt):
        return ks_ref[0, 0, pl.ds(pl.multiple_of(kt * tk, tk), tk), :]

    def win_keys(kt):
        return kw_ref[pl.ds(pl.multiple_of(kt * tkw, tkw), tkw), :]

    def chunk_positions(rows, base, lane0, n_lanes):
        kpos = base + lax.broadcasted_iota(jnp.int32, (rows, n_lanes), 0)
        qpos = pos0 + ((lane0 + lax.broadcasted_iota(jnp.int32, (rows, n_lanes), 1)) & (tq - 1))
        return kpos, qpos

    def causal(rows, base):
        def keep(lane0, n_lanes):
            kpos, qpos = chunk_positions(rows, base, lane0, n_lanes)
            return kpos <= qpos
        return keep

    _flash_reset(m_sc, l_sc, acc_sc)
    n_full = pos0 // tk
    _flash_scores(sel_keys(0), qa_sc, s_sc.at[0])

    def sel_body(kt, carry):
        for parity in range(2):
            @pl.when(kt % 2 == parity)
            def _():
                _flash_scores(sel_keys(kt + 1), qa_sc, s_sc.at[1 - parity])
                _flash_step(s_sc.at[parity], vst_ref[0, 0, kt], m_sc, l_sc, acc_sc)
        return carry

    lax.fori_loop(0, n_full, sel_body, 0)

    n_inner = WINDOW // tkw - 1
    win_tiles = [qi - d for d in range(n_inner + 2)]
    for idx, kt in enumerate(win_tiles):
        _flash_scores(win_keys(jnp.maximum(kt, 0)), qw_sc, sw_sc.at[idx])

    _flash_step(s_sc.at[n_full % 2], vst_ref[0, 0, n_full], m_sc, l_sc, acc_sc, causal(tk, n_full * tk))
    out_sc[...] = out_sc[...] + (gate[1] * (1.0 / l_sc[...])) * acc_sc[...]

    _flash_reset(m_sc, l_sc, acc_sc)
    _flash_step(sw_sc.at[0], vwt_ref[0, 0, qi], m_sc, l_sc, acc_sc, causal(tkw, pos0))
    for idx in range(1, n_inner + 1):
        kt = win_tiles[idx]
        _flash_step(sw_sc.at[idx], vwt_ref[0, 0, jnp.maximum(kt, 0)], m_sc, l_sc, acc_sc,
                    lambda lane0, n_lanes, kt=kt: kt >= 0)
    kt_far = win_tiles[n_inner + 1]

    def far_keep(lane0, n_lanes):
        kpos, qpos = chunk_positions(tkw, kt_far * tkw, lane0, n_lanes)
        return (kpos > qpos - WINDOW) & (kt_far >= 0)

    _flash_step(sw_sc.at[n_inner + 1], vwt_ref[0, 0, jnp.maximum(kt_far, 0)], m_sc, l_sc, acc_sc, far_keep)

    o = out_sc[...] + (gate[2] * (1.0 / l_sc[...])) * acc_sc[...]
    for pair in range(HPG // 2):
        two = jnp.concatenate([o[:, (2 * pair + e) * tq:(2 * pair + e + 1) * tq] for e in range(2)], axis=0)
        o_ref[:, 128 * pair:128 * (pair + 1)] = two.T.astype(o_ref.dtype)


NSA_TQ, NSA_TK, NSA_TKW = 256, 256, 256


def nsa_prompt(qn_t, qr_t, kc, vc_t, ks_aug, vs_t, kw, vw_t, sm_t, bsz, t):
    tq, tk, tkw = NSA_TQ, NSA_TK, NSA_TKW
    assert tq == tkw and tk % tq == 0 and WINDOW % tkw == 0
    n_c = kc.shape[1]
    n_s = t // L_SLC
    assert n_s <= HEAD_DIM and KV_HEADS == 2
    w = HPG * tq
    c0 = np.arange(n_c)[None, :] * S_CMP
    s0 = np.arange(n_s)[:, None] * L_SLC
    cover_t = (np.maximum(np.minimum(c0 + L_CMP, s0 + L_SLC) - np.maximum(c0, s0), 0) / L_CMP).astype(np.float32)
    cover_t = np.tile(cover_t, (1, HPG))
    qspec = pl.BlockSpec((1, 1, 1, HEAD_DIM, w), lambda b, g, i: (b, g, i, 0, 0))
    kern = functools.partial(_nsa_prompt_kernel, tq=tq, tk=tk, tkw=tkw)
    return pl.pallas_call(
        kern,
        grid=(bsz, KV_HEADS, t // tq),
        in_specs=[qspec, qspec,
                  pl.BlockSpec((1, 1, n_c, HEAD_DIM), lambda b, g, i: (b, g, 0, 0)),
                  pl.BlockSpec((1, 1, HEAD_DIM, n_c), lambda b, g, i: (b, g, 0, 0)),
                  pl.BlockSpec((1, 1, t, 2 * HEAD_DIM), lambda b, g, i: (b, g, 0, 0)),
                  pl.BlockSpec((1, 1, t // tk, HEAD_DIM, tk), lambda b, g, i: (b, g, 0, 0, 0)),
                  pl.BlockSpec((t, KV_HEADS * HEAD_DIM), lambda b, g, i: (b, 0)),
                  pl.BlockSpec((1, 1, t // tkw, HEAD_DIM, tkw), lambda b, g, i: (b, g, 0, 0, 0)),
                  pl.BlockSpec((n_s, HPG * n_c), lambda b, g, i: (0, 0)),
                  pl.BlockSpec((1, 1, 32, tq), lambda b, g, i: (b, i, 0, 0))],
        out_specs=pl.BlockSpec((tq, HPG * HEAD_DIM), lambda b, g, i: (b * (t // tq) + i, g)),
        out_shape=jax.ShapeDtypeStruct((bsz * t, ATT_W), BF16),
        scratch_shapes=[pltpu.VMEM((2 * HEAD_DIM, w), BF16), pltpu.VMEM((2 * HEAD_DIM, w), BF16),
                        pltpu.VMEM((1, w), F32), pltpu.VMEM((1, w), F32),
                        pltpu.VMEM((HEAD_DIM, w), F32), pltpu.VMEM((HEAD_DIM, w), F32),
                        pltpu.VMEM((n_s, tq), F32), pltpu.VMEM((n_s, tq), F32),
                        pltpu.VMEM((2, tk, w), F32), pltpu.VMEM((WINDOW // tkw + 1, tkw, w), F32)],
        compiler_params=_cparams(("parallel", "parallel", "arbitrary")),
        name="nsa_prompt",
    )(qn_t, qr_t, kc.reshape(bsz, KV_HEADS, n_c, HEAD_DIM), vc_t.reshape(bsz, KV_HEADS, HEAD_DIM, n_c),
      ks_aug, vs_t, kw, vw_t, jnp.asarray(cover_t, BF16), sm_t)


def _sample_select_kernel(qn_ref, kc_ref, vc_ref, cover_ref, pick_ref, oc_ref, sb_ref, *, q_off, lq, n_cmp, n_slc):
    rows = HPG * lq
    for g in range(KV_HEADS):
        kc = kc_ref[0]
        n_c = kc.shape[0]
        vc_g = _dot(vc_ref[0], pick_ref[g]).astype(BF16)
        cidx = lax.broadcasted_iota(jnp.int32, (rows, n_c), 1)
        qpos = q_off + (lax.broadcasted_iota(jnp.int32, (rows, n_c), 0) & (lq - 1))
        valid = (cidx * S_CMP + (L_CMP - 1) <= qpos) & (cidx < n_cmp)
        s1 = jnp.where(valid, _dot_nt(qn_ref[0, g], kc), NEG)
        e1 = jnp.exp(s1 - jnp.max(s1, axis=-1, keepdims=True)) * valid.astype(F32)
        p1b = (e1 * (1.0 / jnp.maximum(jnp.sum(e1, axis=-1, keepdims=True), 1e-30))).astype(BF16)
        oc_ref[0, g] = _dot(p1b, vc_g)
        imp4 = _dot(p1b, cover_ref[...])
        imp = imp4[0:lq]
        for hh in range(1, HPG):
            imp = imp + imp4[hh * lq:(hh + 1) * lq]
        n_sp = imp.shape[1]
        blk = lax.broadcasted_iota(jnp.int32, (lq, n_sp), 1)
        cur = (q_off + lax.broadcasted_iota(jnp.int32, (lq, n_sp), 0)) // L_SLC
        vis = (blk <= cur) & (blk < n_slc)
        forced = vis & ((blk == 0) | (blk >= cur - 1))
        score = jnp.where(forced, 1e9, jnp.where(vis, imp, -1.0))
        rank = jnp.zeros((lq, n_sp), F32)
        for i in range(n_slc):
            c = score[:, i:i + 1]
            beats = (c > score) | ((c == score) & (blk > i))
            rank = rank + beats.astype(F32)
        sb_ref[0, g] = jnp.where(vis & (rank < min(N_SEL, n_slc)), 0.0, NEG)


def _sample_attend_kernel(pt_ref, pool_ref, qr_ref, sb_ref, exp_ref, oc_ref, gate_ref, kn_ref, vn_ref, kw_ref, vw_ref,
                          o_ref, buf, sem, m_sc, l_sc, acc_sc, *, q_off, lq, n_win):
    b = pl.program_id(0)
    j = pl.program_id(1)
    n_grp = pl.num_programs(1)
    step = b * n_grp + j
    slot = step % 2
    rows = HPG * lq

    def copies(bb, jj, sl):
        return [pltpu.make_async_copy(pool_ref.at[pt_ref[bb, jj * PAGES_PER_STEP + p], pl.ds(2, 2)],
                                      buf.at[sl, p], sem.at[sl]) for p in range(PAGES_PER_STEP)]

    @pl.when(step == 0)
    def _():
        for c in copies(b, j, slot):
            c.start()

    @pl.when(step + 1 < pl.num_programs(0) * n_grp)
    def _():
        wrap = j + 1 == n_grp
        for c in copies(jnp.where(wrap, b + 1, b), jnp.where(wrap, 0, j + 1), 1 - slot):
            c.start()

    for c in copies(b, j, slot):
        c.wait()

    @pl.when(j == 0)
    def _():
        m_sc[...] = jnp.full(m_sc.shape, NEG, F32)
        l_sc[...] = jnp.zeros(l_sc.shape, F32)
        acc_sc[...] = jnp.zeros(acc_sc.shape, F32)

    def update(g, s, v, v_transposed):
        m_prev = m_sc[g]
        m_new = jnp.maximum(m_prev, jnp.max(s, axis=-1, keepdims=True))
        alpha = jnp.exp(m_prev - m_new)
        p = jnp.exp(s - m_new)
        l_sc[g] = alpha * l_sc[g] + jnp.sum(p, axis=-1, keepdims=True)
        pv = _dot_nt(p.astype(BF16), v) if v_transposed else _dot(p.astype(BF16), v)
        acc_sc[g] = alpha * acc_sc[g] + pv
        m_sc[g] = m_new

    scores, values = [], []
    for g in range(KV_HEADS):
        k_t = jnp.concatenate([buf[slot, p, 0, g] for p in range(PAGES_PER_STEP)], axis=1).astype(BF16)
        sb = sb_ref[0, g, 0]
        bias = _dot(jnp.concatenate([sb] * HPG, axis=0).astype(BF16), exp_ref[...])
        scores.append(_dot(qr_ref[0, g], k_t) + bias)
        values.append(jnp.concatenate([buf[slot, p, 1, g] for p in range(PAGES_PER_STEP)], axis=1).astype(BF16))
    for g in range(KV_HEADS):
        update(g, scores[g], values[g], True)

    @pl.when(j == n_grp - 1)
    def _():
        for g in range(KV_HEADS):
            qr = qr_ref[0, g]
            kn = kn_ref[0, g]
            kidx = lax.broadcasted_iota(jnp.int32, (rows, kn.shape[0]), 1)
            qidx = lax.broadcasted_iota(jnp.int32, (rows, kn.shape[0]), 0) & (lq - 1)
            update(g, jnp.where((kidx <= qidx) & (kidx < lq), _dot_nt(qr, kn), NEG), vn_ref[0, g], False)
            o_s = acc_sc[g] * (1.0 / l_sc[g])
            kw = kw_ref[0, g]
            widx = lax.broadcasted_iota(jnp.int32, (rows, kw.shape[0]), 1)
            kpos = q_off + lq - n_win + widx
            qpos = q_off + (lax.broadcasted_iota(jnp.int32, (rows, kw.shape[0]), 0) & (lq - 1))
            ok = (widx < n_win) & (kpos <= qpos) & (kpos > qpos - WINDOW) & (kpos >= 0)
            s3 = jnp.where(ok, _dot_nt(qr, kw), NEG)
            e3 = jnp.exp(s3 - jnp.max(s3, axis=-1, keepdims=True)) * ok.astype(F32)
            p3 = e3 * (1.0 / jnp.maximum(jnp.sum(e3, axis=-1, keepdims=True), 1e-30))
            o_w = _dot(p3.astype(BF16), vw_ref[0, g])
            gate = jax.nn.sigmoid(gate_ref[0, g])
            o_ref[0, g] = gate[:, 0:1] * oc_ref[0, g] + gate[:, 1:2] * o_s + gate[:, 2:3] * o_w


def nsa_sample(q, q_rot, kvc, kv_new, win_cat, pool_t, page_table, gate_logits, q_off):
    bsz, lq = q.shape[:2]
    assert lq & (lq - 1) == 0 and lq <= L_SLC and q_off % (PAGES_PER_STEP * PAGE_SIZE) == 0
    rows = HPG * lq
    scale = HEAD_DIM ** -0.5
    n_seg = kvc.shape[2]
    t_kv = q_off + lq
    n_cmp = (t_kv - L_CMP) // S_CMP + 1
    n_slc = -(-t_kv // L_SLC)
    n_sp = -(-n_slc // 128) * 128
    n_grp = page_table.shape[1] // PAGES_PER_STEP
    blk_step = PAGES_PER_STEP * PAGE_SIZE // L_SLC
    key_step = PAGES_PER_STEP * PAGE_SIZE

    def rows_major(a):
        return a.reshape(bsz, lq, KV_HEADS, HPG, -1).transpose(0, 2, 3, 1, 4).reshape(bsz, KV_HEADS, rows, -1)

    def keys_major(a, n_pad):
        a = a.astype(BF16).transpose(0, 2, 1, 3)
        return jnp.pad(a, ((0, 0), (0, 0), (0, n_pad - a.shape[2]), (0, 0)))

    qn_r = rows_major((q * scale).astype(BF16))
    qr_r = rows_major((q_rot * scale).astype(BF16))
    c0 = np.arange(n_seg)[:, None] * S_CMP
    s0 = np.arange(n_sp)[None, :] * L_SLC
    cover = np.maximum(np.minimum(c0 + L_CMP, s0 + L_SLC) - np.maximum(c0, s0), 0) / L_CMP
    cover = cover * (np.arange(n_seg)[:, None] < n_cmp) * (np.arange(n_sp)[None, :] < n_slc)
    gw = KV_HEADS * HEAD_DIM
    zeros_q = jnp.zeros_like(qn_r[:, 0])
    qn_wide = jnp.stack([jnp.concatenate([qn_r[:, 0], zeros_q], axis=-1),
                         jnp.concatenate([zeros_q, qn_r[:, 1]], axis=-1)], axis=1)
    pick = (np.arange(gw)[None, :, None] == np.arange(KV_HEADS)[:, None, None] * HEAD_DIM
            + np.arange(HEAD_DIM)[None, None, :]).astype(np.float32)
    spec_q = pl.BlockSpec((1, KV_HEADS, rows, HEAD_DIM), lambda b: (b, 0, 0, 0))
    o_c, sel_bias = pl.pallas_call(
        functools.partial(_sample_select_kernel, q_off=q_off, lq=lq, n_cmp=n_cmp, n_slc=n_slc),
        grid=(bsz,),
        in_specs=[pl.BlockSpec((1, KV_HEADS, rows, gw), lambda b: (b, 0, 0, 0)),
                  pl.BlockSpec((1, n_seg, gw), lambda b: (b, 0, 0)),
                  pl.BlockSpec((1, n_seg, gw), lambda b: (b, 0, 0)),
                  pl.BlockSpec((n_seg, n_sp), lambda b: (0, 0)),
                  pl.BlockSpec((KV_HEADS, gw, HEAD_DIM), lambda b: (0, 0, 0))],
        out_specs=[spec_q, pl.BlockSpec((1, KV_HEADS, lq, n_sp), lambda b: (b, 0, 0, 0))],
        out_shape=[jax.ShapeDtypeStruct((bsz, KV_HEADS, rows, HEAD_DIM), F32),
                   jax.ShapeDtypeStruct((bsz, KV_HEADS, lq, n_sp), F32)],
        compiler_params=_cparams(("parallel",)),
        name="sample_select",
    )(qn_wide, kvc[0], kvc[1], jnp.asarray(cover, BF16), jnp.asarray(pick, BF16))

    sb_steps = sel_bias[..., :n_grp * blk_step].reshape(bsz, KV_HEADS, lq, n_grp, blk_step).transpose(0, 1, 3, 2, 4)
    expand = (np.arange(key_step)[None, :] // L_SLC == np.arange(blk_step)[:, None]).astype(np.float32)
    gates = jnp.pad(rows_major(gate_logits.reshape(bsz, lq, -1)), ((0, 0), (0, 0), (0, 0), (0, 128 - 3)))
    n_win = win_cat.shape[1]
    n_wp = -(-n_win // 128) * 128
    spec2 = lambda shape: pl.BlockSpec((1,) + shape, lambda b, j, pt: (b,) + (0,) * len(shape))
    attn = pl.pallas_call(
        functools.partial(_sample_attend_kernel, q_off=q_off, lq=lq, n_win=n_win),
        grid_spec=pltpu.PrefetchScalarGridSpec(
            num_scalar_prefetch=1,
            grid=(bsz, n_grp),
            in_specs=[pl.BlockSpec(memory_space=pl.ANY),
                      spec2((KV_HEADS, rows, HEAD_DIM)),
                      pl.BlockSpec((1, KV_HEADS, 1, lq, blk_step), lambda b, j, pt: (b, 0, j, 0, 0)),
                      pl.BlockSpec((blk_step, key_step), lambda b, j, pt: (0, 0)),
                      spec2((KV_HEADS, rows, HEAD_DIM)),
                      spec2((KV_HEADS, rows, 128)),
                      spec2((KV_HEADS, 128, HEAD_DIM)), spec2((KV_HEADS, 128, HEAD_DIM)),
                      spec2((KV_HEADS, n_wp, HEAD_DIM)), spec2((KV_HEADS, n_wp, HEAD_DIM))],
            out_specs=spec2((KV_HEADS, rows, HEAD_DIM)),
            scratch_shapes=[pltpu.VMEM((2, PAGES_PER_STEP, 2, KV_HEADS, HEAD_DIM, PAGE_SIZE), F32),
                            pltpu.SemaphoreType.DMA((2,)),
                            pltpu.VMEM((KV_HEADS, rows, 1), F32), pltpu.VMEM((KV_HEADS, rows, 1), F32),
                            pltpu.VMEM((KV_HEADS, rows, HEAD_DIM), F32)]),
        out_shape=jax.ShapeDtypeStruct((bsz, KV_HEADS, rows, HEAD_DIM), F32),
        compiler_params=_cparams(("arbitrary", "arbitrary")),
        name="sample_attend",
    )(page_table, pool_t, qr_r, sb_steps, jnp.asarray(expand, BF16), o_c, gates,
      keys_major(kv_new[:, :, 2], 128), keys_major(kv_new[:, :, 3], 128),
      keys_major(win_cat[:, :, 0], n_wp), keys_major(win_cat[:, :, 1], n_wp))
    attn = attn.reshape(bsz, KV_HEADS, HPG, lq, HEAD_DIM).transpose(0, 3, 1, 2, 4)
    return attn.astype(BF16).reshape(bsz * lq, ATT_W)


def _ssd_kernel(d_ref, x_ref, bt_ref, c_ref, acol_ref, arow_ref, dt_ref, s0_ref, y_ref, sout_ref, s_sc, *, cl):
    ci = pl.program_id(1)

    @pl.when(ci == 0)
    def _():
        s_sc[...] = s0_ref[0]

    li = lax.broadcasted_iota(jnp.int32, (cl, cl), 0)
    si = lax.broadcasted_iota(jnp.int32, (cl, cl), 1)
    causal = li >= si
    acum_col = _dot_hi(causal.astype(F32), acol_ref[0])
    acum_row = _dot_hi(arow_ref[0], (li <= si).astype(F32))
    dt = dt_ref[0]
    for g in range(SSM_GROUPS):
        cg = c_ref[0, g].astype(BF16)
        btg = bt_ref[0, g]
        cb = _dot(cg, btg.astype(BF16))
        for hh in range(SSM_HEADS // SSM_GROUPS):
            h = g * (SSM_HEADS // SSM_GROUPS) + hh
            ac = acum_col[:, h:h + 1]
            ar = acum_row[h:h + 1, :]
            decay = jnp.exp(jnp.where(causal, ac - ar, NEG))
            xs = x_ref[0, h]
            xd = (xs * dt[:, h:h + 1]).astype(BF16)
            st = s_sc[h]
            y = _dot((cb * decay).astype(BF16), xd) + _dot(cg, st.astype(BF16)) * jnp.exp(ac)
            y_ref[0, h] = y + d_ref[h] * xs
            a_last = ar[:, cl - 1:cl]
            snew = _dot((btg * jnp.exp(a_last - ar)).astype(BF16), xd)
            s_sc[h] = st * jnp.exp(a_last) + snew

    @pl.when(ci == pl.num_programs(1) - 1)
    def _():
        sout_ref[0] = s_sc[...]


def ssd_scan(ssm_d, x_hm, b_t, c, a_col, a_row, dt_col, s0_t, cl):
    bsz, nh, t, p = x_hm.shape
    n = D_STATE
    nc = t // cl
    kern = functools.partial(_ssd_kernel, cl=cl)
    return pl.pallas_call(
        kern,
        grid=(bsz, nc),
        in_specs=[pl.BlockSpec(memory_space=pltpu.SMEM),
                  pl.BlockSpec((1, nh, cl, p), lambda b, c_: (b, 0, c_, 0)),
                  pl.BlockSpec((1, SSM_GROUPS, n, cl), lambda b, c_: (b, 0, 0, c_)),
                  pl.BlockSpec((1, SSM_GROUPS, cl, n), lambda b, c_: (b, 0, c_, 0)),
                  pl.BlockSpec((1, cl, 128), lambda b, c_: (b, c_, 0)),
                  pl.BlockSpec((1, 8, cl), lambda b, c_: (b, 0, c_)),
                  pl.BlockSpec((1, cl, 128), lambda b, c_: (b, c_, 0)),
                  pl.BlockSpec((1, nh, n, p), lambda b, c_: (b, 0, 0, 0))],
        out_specs=[pl.BlockSpec((1, nh, cl, p), lambda b, c_: (b, 0, c_, 0)),
                   pl.BlockSpec((1, nh, n, p), lambda b, c_: (b, 0, 0, 0))],
        out_shape=[jax.ShapeDtypeStruct((bsz, nh, t, p), F32), jax.ShapeDtypeStruct((bsz, nh, n, p), F32)],
        scratch_shapes=[pltpu.VMEM((nh, n, p), F32)],
        compiler_params=_cparams(("parallel", "arbitrary")),
        name="ssd_scan",
    )(ssm_d, x_hm, b_t, c, a_col, a_row, dt_col, s0_t)


def _ssd_prompt_kernel(d_ref, xbc_ref, sm_ref, z_ref, cw_ref, cb_ref, dtb_ref, ah_ref, gn_ref, o_ref, sout_ref,
                       s_sc, tail_sc, *, cl):
    ci = pl.program_id(1)

    @pl.when(ci == 0)
    def _():
        s_sc[...] = jnp.zeros(s_sc.shape, F32)
        tail_sc[...] = jnp.zeros(tail_sc.shape, F32)

    xb = xbc_ref[...]
    prev = tail_sc[...]
    row8 = lax.broadcasted_iota(jnp.int32, (8, CONV_DIM), 0)

    def shifted(j):
        body = pltpu.roll(xb, j, axis=0)
        top = jnp.where(row8 < j, pltpu.roll(prev, j, axis=0), body[0:8])
        return jnp.concatenate([top, body[8:]], axis=0)

    conv = cb_ref[...] + shifted(CONV_W - 1) * cw_ref[0:1]
    for j in range(1, CONV_W):
        conv = conv + (shifted(CONV_W - 1 - j) if j < CONV_W - 1 else xb) * cw_ref[j:j + 1]
    tail_sc[...] = xb[cl - 8:cl]
    act = conv * jax.nn.sigmoid(conv)
    n_bc = SSM_GROUPS * D_STATE
    bmat = act[:, D_INNER:D_INNER + n_bc]
    cmat = act[:, D_INNER + n_bc:D_INNER + 2 * n_bc]
    x_dt = sm_ref[...] + dtb_ref[...]
    dt = jnp.maximum(x_dt, 0.0) + jnp.log(1.0 + jnp.exp(-jnp.abs(x_dt)))
    a = ah_ref[...] * dt

    li = lax.broadcasted_iota(jnp.int32, (cl, cl), 0)
    si = lax.broadcasted_iota(jnp.int32, (cl, cl), 1)
    causal = li >= si
    acum_col = _dot_hi(causal.astype(F32), a)
    acum_row = _dot_hi(a.T[N_GATES:N_GATES + SSM_HEADS], (li <= si).astype(F32))
    bt = bmat.T
    bt_b = bt.astype(BF16)
    lane = lax.broadcasted_iota(jnp.int32, (cl, 128), 1)
    low = lane < SSM_HEAD_DIM
    low_s = lax.broadcasted_iota(jnp.int32, (128, 128), 1) < SSM_HEAD_DIM

    def col(arr, h):
        return arr[:, N_GATES + h:N_GATES + h + 1]

    per_group = SSM_HEADS // SSM_GROUPS
    cg_of, cb_of = {}, {}
    ys = []
    for j in range(SSM_HEADS // 2):
        ha, hb = 2 * j, 2 * j + 1
        g = ha // per_group
        if g not in cg_of:
            in_group = (lane >= g * D_STATE) & (lane < (g + 1) * D_STATE)
            cg_of[g] = jnp.where(in_group, cmat, 0.0).astype(BF16)
            cb_of[g] = _dot(cg_of[g], bt_b)
        cg, cb = cg_of[g], cb_of[g]
        xs_pair = act[:, 128 * j:128 * (j + 1)]
        xd = (xs_pair * jnp.where(low, col(dt, ha), col(dt, hb))).astype(BF16)
        st = s_sc[j]
        y_off = _dot(cg, st.astype(BF16)) * jnp.where(low, jnp.exp(col(acum_col, ha)), jnp.exp(col(acum_col, hb)))
        y_h, s_h, dec_h = [], [], []
        for h in (ha, hb):
            ar = acum_row[h:h + 1, :]
            decay = jnp.exp(jnp.where(causal, col(acum_col, h) - ar, NEG))
            y_h.append(_dot((cb * decay).astype(BF16), xd))
            a_last = ar[:, cl - 1:cl]
            s_h.append(_dot((bt * jnp.exp(a_last - ar)).astype(BF16), xd))
            dec_h.append(jnp.exp(a_last))
        skip = jnp.where(low[0:1], d_ref[ha], d_ref[hb])
        ys.append(jnp.where(low, y_h[0], y_h[1]) + y_off + skip * xs_pair)
        s_sc[j] = st * jnp.where(low_s, dec_h[0], dec_h[1]) + jnp.where(low_s, s_h[0], s_h[1])

    z = z_ref[...]
    gated = jnp.concatenate(ys, axis=1) * (z * jax.nn.sigmoid(z))
    out = gated * lax.rsqrt(jnp.mean(gated * gated, axis=-1, keepdims=True) + EPS) * gn_ref[...]
    o_ref[...] = out.astype(o_ref.dtype)

    @pl.when(ci == pl.num_programs(1) - 1)
    def _():
        sout_ref[0] = s_sc[...]


def ssd_prompt(xbc, sm, z, w, bsz, t):
    cl = SSM_CHUNK
    nc = t // cl
    lanes = lambda v: jnp.zeros((1, 128), F32).at[0, N_GATES:N_GATES + SSM_HEADS].set(v)
    row = lambda width: pl.BlockSpec((cl, width), lambda b, c: (b * nc + c, 0))
    fixed = lambda r, c_: pl.BlockSpec((r, c_), lambda b, c: (0, 0))
    n_pair = SSM_HEADS // 2
    out, s_pairs = pl.pallas_call(
        functools.partial(_ssd_prompt_kernel, cl=cl),
        grid=(bsz, nc),
        in_specs=[pl.BlockSpec(memory_space=pltpu.SMEM), row(CONV_DIM), row(128), row(D_INNER),
                  fixed(CONV_W, CONV_DIM), fixed(1, CONV_DIM), fixed(1, 128), fixed(1, 128), fixed(1, D_INNER)],
        out_specs=[row(D_INNER), pl.BlockSpec((1, n_pair, 128, 128), lambda b, c: (b, 0, 0, 0))],
        out_shape=[jax.ShapeDtypeStruct((bsz * t, D_INNER), BF16),
                   jax.ShapeDtypeStruct((bsz, n_pair, 128, 128), F32)],
        scratch_shapes=[pltpu.VMEM((n_pair, 128, 128), F32), pltpu.VMEM((8, CONV_DIM), F32)],
        compiler_params=_cparams(("parallel", "arbitrary")),
        name="ssd_prompt",
    )(w['ssm_d'], xbc, sm, z, w['conv_w'], w['conv_b'].reshape(1, CONV_DIM), lanes(w['dt_bias']),
      lanes(-jnp.exp(w['a_log'])), w['ssm_norm'].reshape(1, D_INNER))
    s6 = s_pairs.reshape(bsz, n_pair, SSM_GROUPS, D_STATE, 2, SSM_HEAD_DIM)
    per_group = SSM_HEADS // SSM_GROUPS
    heads = [s6[:, h // 2, h // per_group, :, h % 2, :] for h in range(SSM_HEADS)]
    return out, jnp.stack(heads, axis=1).transpose(0, 1, 3, 2)


def _rmsnorm(x, g):
    return x * lax.rsqrt(jnp.mean(x * x, axis=-1, keepdims=True) + EPS) * g


def _rope(x, pos):
    half = HEAD_DIM // 2
    inv = ROPE_THETA ** (-jnp.arange(half, dtype=F32) / half)
    ang = pos.astype(F32)[:, None] * inv[None, :]
    cos, sin = jnp.cos(ang)[:, None, :], jnp.sin(ang)[:, None, :]
    x1, x2 = x[..., :half], x[..., half:]
    return jnp.concatenate([x1 * cos - x2 * sin, x2 * cos + x1 * sin], axis=-1)


def _ssd_inputs(xbc_all, dt_raw, w, t_pad):
    bsz = xbc_all.shape[0]
    length = xbc_all.shape[1] - (CONV_W - 1)
    conv = w['conv_b'] + xbc_all[:, 0:length] * w['conv_w'][0]
    for j in range(1, CONV_W):
        conv = conv + xbc_all[:, j:j + length] * w['conv_w'][j]
    act = jax.nn.silu(conv)
    dt = jax.nn.softplus(dt_raw + w['dt_bias'])
    a = -jnp.exp(w['a_log']) * dt
    pad = t_pad - length
    if pad:
        act = jnp.pad(act, ((0, 0), (0, pad), (0, 0)))
        dt = jnp.pad(dt, ((0, 0), (0, pad), (0, 0)))
        a = jnp.pad(a, ((0, 0), (0, pad), (0, 0)))
    xs = act[..., :D_INNER].reshape(bsz, t_pad, SSM_HEADS, SSM_HEAD_DIM).transpose(0, 2, 1, 3)
    b_in = act[..., D_INNER:D_INNER + SSM_GROUPS * D_STATE].reshape(bsz, t_pad, SSM_GROUPS, D_STATE)
    c_in = act[..., D_INNER + SSM_GROUPS * D_STATE:].reshape(bsz, t_pad, SSM_GROUPS, D_STATE)
    lane_pad = ((0, 0), (0, 0), (0, 128 - SSM_HEADS))
    return (xs, b_in.transpose(0, 2, 3, 1), c_in.transpose(0, 2, 1, 3), jnp.pad(a, lane_pad),
            a.transpose(0, 2, 1), jnp.pad(dt, lane_pad))


def _moe(f_b, logits, w):
    n_tok = f_b.shape[0]
    pg = jax.nn.softmax(logits[:, :N_EXPERT_GROUPS], axis=-1)
    g_sel = jnp.argmax(pg, axis=-1)
    g_w = jnp.max(pg, axis=-1)
    el = logits[:, N_EXPERT_GROUPS:N_EXPERT_GROUPS + N_EXPERTS].reshape(n_tok, N_EXPERT_GROUPS, EXPERTS_PER_GROUP)
    el = jnp.take_along_axis(el, g_sel[:, None, None], axis=1)[:, 0]
    top_p, top_i = lax.top_k(jax.nn.softmax(el, axis=-1), TOP_K)
    wts = g_w[:, None] * top_p / jnp.sum(top_p, axis=-1, keepdims=True)
    eid = (g_sel[:, None] * EXPERTS_PER_GROUP + top_i).reshape(-1).astype(jnp.int32)
    n_asg = eid.shape[0]
    onehot = (eid[:, None] == jnp.arange(N_EXPERTS, dtype=jnp.int32)[None, :]).astype(jnp.int32)
    within = jnp.take_along_axis(jnp.cumsum(onehot, axis=0), eid[:, None], axis=1)[:, 0] - 1
    sizes = jnp.sum(onehot, axis=0)
    padded = (sizes + MOE_ROWS - 1) // MOE_ROWS * MOE_ROWS
    pend = jnp.cumsum(padded)
    dest = (pend - padded)[eid] + within
    n_blk = -(-n_asg // MOE_ROWS) + N_EXPERTS
    src = jnp.zeros((n_blk * MOE_ROWS,), jnp.int32).at[dest].set(jnp.arange(n_asg, dtype=jnp.int32) // TOP_K)
    xpad = f_b[src]
    blk_start = jnp.arange(n_blk, dtype=jnp.int32) * MOE_ROWS
    blk_e = jnp.minimum(jnp.sum((pend[None, :] <= blk_start[:, None]).astype(jnp.int32), axis=1), N_EXPERTS - 1)
    n_used = (pend[-1] // MOE_ROWS).astype(jnp.int32).reshape(1)
    return xpad, blk_e, n_used, dest.reshape(n_tok, TOP_K).T.reshape(-1), jnp.pad(wts, ((0, 0), (0, 128 - TOP_K)))


def _token_dispatch(x2, mix_in, w, tm):
    pad = 128 - N_EXPERT_GROUPS - N_EXPERTS
    w_router = jnp.concatenate([w['w_rg'], w['w_re'], jnp.zeros((D_MODEL, pad), F32)], axis=1)
    b_router = jnp.concatenate([w['b_rg'], w['b_re'], jnp.zeros((pad,), F32)]).reshape(1, 128)
    h, f_b, logits = outproj(x2, mix_in[0], mix_in[1], w['w_out_b'], w['ffn_norm'], w_router, b_router, tm)
    return (h,) + _moe(f_b, logits, w)


def _token_finish(dispatched, p2, w, tm):
    h, xpad, blk_e, n_used, dest01, wts = dispatched
    ypad = moe_experts(xpad, blk_e, n_used, w['w_gate'], w['w_up'], w['w_down'])
    return moe_combine_ple(h, ypad[dest01], wts, p2, w['wpg_b'], w['wpp_b'], w['ple_norm'], tm)


def _ssd_finish(y_hm, z, w, length):
    bsz = y_hm.shape[0]
    y = y_hm[:, :, :length].transpose(0, 2, 1, 3).reshape(bsz, length, D_INNER)
    gated = y * jax.nn.silu(z)
    return _rmsnorm(gated, w['ssm_norm']).astype(BF16)


def _prompt_group(x, p, w):
    bsz, t, _ = x.shape
    m = bsz * t
    z, xbc, sm, sm_t, kv, win, qn_t, qr_t, ks_aug, kw, vs_t, vw_t = rms_inproj_prompt(
        x.reshape(m, D_MODEL), w['attn_norm'], w['w_in_r'], w['q_norm'], w['k_norm'], bsz, t)
    kv_new = kv.reshape(bsz, t, 4, KV_HEADS, HEAD_DIM)
    win_new = win.reshape(bsz, t, 2, KV_HEADS, HEAD_DIM)

    n_seg = t // S_CMP
    segs = kv_new[:, :, 0:2].astype(BF16).transpose(2, 0, 3, 1, 4).reshape(2, bsz * KV_HEADS, n_seg, S_CMP * HEAD_DIM)
    kvc = compress(segs, w['cmp_pe'], w['cmp_w1'], w['cmp_w2'])
    attn = nsa_prompt(qn_t, qr_t, kvc[0], kvc[1].transpose(0, 2, 1), ks_aug, vs_t, kw, vw_t, sm_t, bsz, t)

    ssd, ssm_new = ssd_prompt(xbc, sm, z, w, bsz, t)

    dispatched = _token_dispatch(x.reshape(m, D_MODEL), (attn, ssd), w, 512)
    keep = min(WINDOW, t)
    conv_new = xbc.reshape(bsz, t, CONV_DIM)[:, t - (CONV_W - 1):]

    def finish():
        return _token_finish(dispatched, p.reshape(m, PLE_DIM), w, 512).reshape(bsz, t, D_MODEL)

    return finish, kv_new, win_new[:, t - keep:], ssm_new, conv_new


def _sample_group(x, p, pool, page_table, cache_win, state_ssm, state_conv, w):
    bsz, lq, _ = x.shape
    m = bsz * lq
    q_off = page_table.shape[1] * PAGE_SIZE
    q, kvs, z, xbc, sm = rms_inproj(x.reshape(m, D_MODEL), w['attn_norm'], w['w_in_r'], m)
    pos = q_off + jnp.arange(lq)
    qn = _rmsnorm(q.reshape(bsz, lq, N_HEADS, HEAD_DIM), w['q_norm'])
    qr = _rope(qn, pos)
    kvs = kvs.reshape(bsz, lq, 6, KV_HEADS, HEAD_DIM)
    k_c = _rmsnorm(kvs[:, :, 0], w['k_norm'][0])
    k_s = _rope(_rmsnorm(kvs[:, :, 2], w['k_norm'][1]), pos)
    k_w = _rope(_rmsnorm(kvs[:, :, 4], w['k_norm'][2]), pos)
    kv_new = jnp.stack([k_c, kvs[:, :, 1], k_s, kvs[:, :, 3]], axis=2)
    win_new = jnp.stack([k_w, kvs[:, :, 5]], axis=2)
    win_cat = jnp.concatenate([cache_win, win_new], axis=1)

    pool_t = jnp.transpose(pool, (0, 2, 3, 4, 1))
    kvc = compress_paged(pool_t, page_table, w['cmp_pe'], w['cmp_w1'], w['cmp_w2'])
    attn = nsa_sample(qn, qr, kvc, kv_new, win_cat, pool_t, page_table, sm[:, :N_GATES], q_off)

    xbc_all = jnp.concatenate([state_conv, xbc.reshape(bsz, lq, CONV_DIM)], axis=1)
    dt_raw = sm[:, N_GATES:N_GATES + SSM_HEADS].reshape(bsz, lq, SSM_HEADS)
    xs, b_t, c_in, a_col, a_row, dt_col = _ssd_inputs(xbc_all, dt_raw, w, SSM_CHUNK)
    y_hm, s_t = ssd_scan(w['ssm_d'], xs, b_t, c_in, a_col, a_row, dt_col, state_ssm.transpose(0, 1, 3, 2), SSM_CHUNK)
    ssd = _ssd_finish(y_hm, z.reshape(bsz, lq, D_INNER), w, lq).reshape(m, D_INNER)

    y = _token_finish(_token_dispatch(x.reshape(m, D_MODEL), (attn, ssd), w, m), p.reshape(m, PLE_DIM), w, m)
    keep = cache_win.shape[1]
    return (y.reshape(bsz, lq, D_MODEL), kv_new, win_cat[:, win_cat.shape[1] - keep:], s_t.transpose(0, 1, 3, 2),
            xbc_all[:, xbc_all.shape[1] - (CONV_W - 1):])


def kernel(x_prompt, x_sample, cache_kv, cache_win, state_ssm, state_conv, page_table, p_prompt, p_sample,
           w_in, w_out, q_norm, k_norm, cmp_pe, cmp_w1, cmp_w2, conv_w, conv_b, dt_bias, a_log, ssm_d, ssm_norm,
           attn_norm, ffn_norm, w_rg, b_rg, w_re, b_re, w_gate, w_up, w_down, w_ple_proj, ple_norm, w_ple_gate):
    depth = w_in.shape[0]
    hp, hs = x_prompt, x_sample
    outs = [[] for _ in range(8)]
    cuts = np.cumsum((ATT_W, 6 * KV_HEADS * HEAD_DIM, N_GATES, D_INNER, CONV_DIM, SSM_HEADS))
    for l in range(depth):
        wi = w_in[l]
        w_in_r = jnp.concatenate(
            [wi[:, :cuts[1]], wi[:, cuts[2]:cuts[3]], wi[:, cuts[3]:cuts[4]], wi[:, cuts[1]:cuts[2]],
             wi[:, cuts[4]:cuts[5]], jnp.zeros((D_MODEL, C_SM - N_GATES - SSM_HEADS), F32)], axis=1).astype(BF16)
        w = dict(w_in_r=w_in_r, w_out_b=w_out[l].astype(BF16), q_norm=q_norm[l], k_norm=k_norm[l],
                 cmp_pe=cmp_pe[l], cmp_w1=cmp_w1[l], cmp_w2=cmp_w2[l], conv_w=conv_w[l], conv_b=conv_b[l],
                 dt_bias=dt_bias[l], a_log=a_log[l], ssm_d=ssm_d[l], ssm_norm=ssm_norm[l], attn_norm=attn_norm[l],
                 ffn_norm=ffn_norm[l], w_rg=w_rg[l], b_rg=b_rg[l], w_re=w_re[l], b_re=b_re[l],
                 w_gate=w_gate[l], w_up=w_up[l], w_down=w_down[l],
                 wpp_b=w_ple_proj[l].astype(BF16), ple_norm=ple_norm[l], wpg_b=w_ple_gate[l].astype(BF16))
        finish_prompt, *rest_p = _prompt_group(hp, p_prompt[l], w)
        hs, *rest_s = _sample_group(hs, p_sample[l], cache_kv[l], page_table, cache_win[l], state_ssm[l],
                                    state_conv[l], w)
        hp = finish_prompt()
        for j in range(4):
            outs[2 * j].append(rest_p[j])
            outs[2 * j + 1].append(rest_s[j])
    return (hp, hs) + tuple(jnp.stack(o) for o in outs)
```

```python
import functools
import math

import numpy as np
import jax
import jax.numpy as jnp
from jax import lax
from jax.experimental import pallas as pl
from jax.experimental.pallas import tpu as pltpu

F32 = jnp.float32
BF16 = jnp.bfloat16

D_MODEL = 1024
PAGE_SIZE = 128
N_HEADS = 8
HEAD_DIM = 64
KV_HEADS = 2
HPG = N_HEADS // KV_HEADS
ATT_W = N_HEADS * HEAD_DIM
L_CMP = 32
S_CMP = 16
L_SLC = 64
N_SEL = 16
WINDOW = 512
CMP_HID = 64
ROPE_THETA = 10000.0
SSM_HEADS = 8
SSM_HEAD_DIM = 64
D_INNER = SSM_HEADS * SSM_HEAD_DIM
SSM_GROUPS = 2
D_STATE = 64
CONV_W = 4
CONV_DIM = D_INNER + 2 * SSM_GROUPS * D_STATE
SSM_CHUNK = 128
N_EXPERT_GROUPS = 4
EXPERTS_PER_GROUP = 8
N_EXPERTS = N_EXPERT_GROUPS * EXPERTS_PER_GROUP
TOP_K = 2
D_EXPERT = 512
PLE_DIM = 256
EPS = 1e-6
N_GATES = 3 * N_HEADS
C_Q, C_KV, C_Z, C_XBC, C_SM = 512, 768, 512, 768, 128
D_IN_PAD = C_Q + C_KV + C_Z + C_XBC + C_SM

NEG = -1e30
VMEM_LIMIT = 48 * 1024 * 1024
MOE_ROWS = 256


def _cparams(sem):
    return pltpu.CompilerParams(dimension_semantics=sem, vmem_limit_bytes=VMEM_LIMIT)


def _dot(a, b):
    return jnp.dot(a, b, preferred_element_type=F32)


def _dot_nt(a, b):
    return lax.dot_general(a, b, (((1,), (1,)), ((), ())), preferred_element_type=F32)


def _dot_hi(a, b):
    return jnp.dot(a, b, preferred_element_type=F32, precision=lax.Precision.HIGHEST)


def _rms_inproj_kernel(x_ref, g_ref, w_ref, q_ref, kv_ref, z_ref, xbc_ref, sm_ref):
    x = x_ref[...]
    y = x * lax.rsqrt(jnp.mean(x * x, axis=-1, keepdims=True) + EPS) * g_ref[...]
    yb = y.astype(BF16)
    c0 = 0
    for ref, width in ((q_ref, C_Q), (kv_ref, C_KV), (z_ref, C_Z), (xbc_ref, C_XBC), (sm_ref, C_SM)):
        ref[...] = _dot(yb, w_ref[:, c0:c0 + width])
        c0 += width


def rms_inproj(x, gain, w_r, tm):
    m = x.shape[0]
    widths = (C_Q, C_KV, C_Z, C_XBC, C_SM)
    return pl.pallas_call(
        _rms_inproj_kernel,
        grid=(m // tm,),
        in_specs=[pl.BlockSpec((tm, D_MODEL), lambda i: (i, 0)),
                  pl.BlockSpec((1, D_MODEL), lambda i: (0, 0)),
                  pl.BlockSpec((D_MODEL, D_IN_PAD), lambda i: (0, 0))],
        out_specs=[pl.BlockSpec((tm, w), lambda i: (i, 0)) for w in widths],
        out_shape=[jax.ShapeDtypeStruct((m, w), F32) for w in widths],
        compiler_params=_cparams(("parallel",)),
        name="rms_inproj",
    )(x, gain.reshape(1, D_MODEL), w_r)


def _inproj_prompt_kernel(x_ref, g_ref, w_ref, cos_ref, sin_ref, gq_ref, gk_ref, z_ref, xbc_ref, sm_ref, smt_ref, kv_ref,
                          win_ref, qn_ref, qr_ref, ksa_ref, kw_ref, vst_ref, vwt_ref, *, tiles_per_seq):
    tm = x_ref.shape[0]
    x = x_ref[...]
    yb = (x * lax.rsqrt(jnp.mean(x * x, axis=-1, keepdims=True) + EPS) * g_ref[...]).astype(BF16)
    cos, sin = cos_ref[...], sin_ref[...]
    scale = HEAD_DIM ** -0.5
    half = HEAD_DIM // 2

    def proj(c0, width=128):
        return _dot(yb, w_ref[:, c0:c0 + width])

    def head_norm(x_t, gain):
        return x_t * lax.rsqrt(jnp.mean(x_t * x_t, axis=0, keepdims=True) + EPS) * gain

    def rotate(y_t):
        y1, y2 = y_t[0:half], y_t[half:HEAD_DIM]
        return jnp.concatenate([y1 * cos - y2 * sin, y2 * cos + y1 * sin], axis=0)

    def per_group(blk, fn):
        t = blk.T
        return jnp.concatenate([fn(t[g * HEAD_DIM:(g + 1) * HEAD_DIM]) for g in range(KV_HEADS)], axis=0).T

    for j in range(ATT_W // 128):
        q_t = proj(128 * j).T
        for h2 in range(128 // HEAD_DIM):
            head = (128 // HEAD_DIM) * j + h2
            g, hh = head // HPG, head % HPG
            qn = head_norm(q_t[h2 * HEAD_DIM:(h2 + 1) * HEAD_DIM], gq_ref[...])
            qn_ref[0, g, 0, :, hh * tm:(hh + 1) * tm] = (qn * scale).astype(BF16)
            qr_ref[0, g, 0, :, hh * tm:(hh + 1) * tm] = (rotate(qn) * scale).astype(BF16)

    c_kv = C_Q
    k_c = per_group(proj(c_kv), lambda t: head_norm(t, gk_ref[0]))
    v_c = proj(c_kv + 128)
    k_s = per_group(proj(c_kv + 256), lambda t: rotate(head_norm(t, gk_ref[1])))
    v_s = proj(c_kv + 384)
    k_w = per_group(proj(c_kv + 512), lambda t: rotate(head_norm(t, gk_ref[2])))
    v_w = proj(c_kv + 640)
    kv_ref[:, 0:128] = k_c
    kv_ref[:, 128:256] = v_c
    kv_ref[:, 256:384] = k_s
    kv_ref[:, 384:512] = v_s
    win_ref[:, 0:128] = k_w
    win_ref[:, 128:256] = v_w
    kw_ref[...] = k_w.astype(BF16)
    pos0 = (pl.program_id(0) % tiles_per_seq) * tm
    lane = lax.broadcasted_iota(jnp.int32, (tm, 128), 1)
    blk_of_row = (pos0 + lax.broadcasted_iota(jnp.int32, (tm, 128), 0)) // L_SLC
    onehot = (blk_of_row == (lane & (HEAD_DIM - 1))).astype(BF16)
    k_sb = k_s.astype(BF16)
    ksa_ref[0, 0] = jnp.where(lane < HEAD_DIM, k_sb, onehot)
    ksa_ref[0, 1] = jnp.where(lane >= HEAD_DIM, k_sb, onehot)
    vs_t = v_s.T.astype(BF16)
    vw_t = v_w.T.astype(BF16)
    for g in range(KV_HEADS):
        vst_ref[0, g, 0] = vs_t[g * HEAD_DIM:(g + 1) * HEAD_DIM]
        vwt_ref[0, g, 0] = vw_t[g * HEAD_DIM:(g + 1) * HEAD_DIM]
    z_ref[...] = proj(C_Q + C_KV, C_Z)
    xbc_ref[...] = proj(C_Q + C_KV + C_Z, C_XBC)
    sm = proj(C_Q + C_KV + C_Z + C_XBC)
    sm_ref[...] = sm
    smt_ref[0, 0] = sm.T[0:32]


def rms_inproj_prompt(x, gain, w_r, q_gain, k_gain, bsz, t):
    tm = NSA_TQ
    assert NSA_TK == tm and NSA_TKW == tm and t // L_SLC <= HEAD_DIM
    m = bsz * t
    nt = t // tm
    inv = ROPE_THETA ** (-jnp.arange(HEAD_DIM // 2, dtype=F32) / (HEAD_DIM // 2))
    ang = inv[:, None] * jnp.arange(t, dtype=F32)[None, :]
    row = lambda width: pl.BlockSpec((tm, width), lambda i: (i, 0))
    qspec = pl.BlockSpec((1, KV_HEADS, 1, HEAD_DIM, HPG * tm), lambda i: (i // nt, 0, i % nt, 0, 0))
    vspec = pl.BlockSpec((1, KV_HEADS, 1, HEAD_DIM, tm), lambda i: (i // nt, 0, i % nt, 0, 0))
    tspec = pl.BlockSpec((HEAD_DIM // 2, tm), lambda i: (0, i % nt))
    qshape = jax.ShapeDtypeStruct((bsz, KV_HEADS, nt, HEAD_DIM, HPG * tm), BF16)
    vshape = jax.ShapeDtypeStruct((bsz, KV_HEADS, nt, HEAD_DIM, tm), BF16)
    return pl.pallas_call(
        functools.partial(_inproj_prompt_kernel, tiles_per_seq=nt),
        grid=(m // tm,),
        in_specs=[row(D_MODEL),
                  pl.BlockSpec((1, D_MODEL), lambda i: (0, 0)),
                  pl.BlockSpec((D_MODEL, D_IN_PAD), lambda i: (0, 0)),
                  tspec, tspec,
                  pl.BlockSpec((HEAD_DIM, 1), lambda i: (0, 0)),
                  pl.BlockSpec((3, HEAD_DIM, 1), lambda i: (0, 0, 0))],
        out_specs=[row(C_Z), row(C_XBC), row(C_SM),
                   pl.BlockSpec((1, 1, 32, tm), lambda i: (i // nt, i % nt, 0, 0)),
                   row(512), row(256), qspec, qspec,
                   pl.BlockSpec((1, KV_HEADS, tm, 128), lambda i: (i // nt, 0, i % nt, 0)),
                   row(128), vspec, vspec],
        out_shape=[jax.ShapeDtypeStruct((m, C_Z), F32), jax.ShapeDtypeStruct((m, C_XBC), F32),
                   jax.ShapeDtypeStruct((m, C_SM), F32), jax.ShapeDtypeStruct((bsz, nt, 32, tm), F32),
                   jax.ShapeDtypeStruct((m, 512), F32), jax.ShapeDtypeStruct((m, 256), F32), qshape, qshape,
                   jax.ShapeDtypeStruct((bsz, KV_HEADS, t, 128), BF16),
                   jax.ShapeDtypeStruct((m, 128), BF16), vshape, vshape],
        compiler_params=_cparams(("parallel",)),
        name="rms_inproj_prompt",
    )(x, gain.reshape(1, D_MODEL), w_r, jnp.cos(ang), jnp.sin(ang), q_gain.reshape(HEAD_DIM, 1),
      k_gain.reshape(3, HEAD_DIM, 1))


def _outproj_kernel(x_ref, a_ref, s_ref, w_ref, g_ref, wr_ref, br_ref, h_ref, f_ref, lg_ref):
    acc = _dot(a_ref[...], w_ref[0:ATT_W, :]) + _dot(s_ref[...], w_ref[ATT_W:ATT_W + D_INNER, :])
    h = x_ref[...] + acc
    h_ref[...] = h
    f = h * lax.rsqrt(jnp.mean(h * h, axis=-1, keepdims=True) + EPS) * g_ref[...]
    f_hi = f.astype(BF16)
    f_ref[...] = f_hi
    f_lo = (f - f_hi.astype(F32)).astype(BF16)
    lg_ref[...] = (_dot(f_hi, wr_ref[0]) + (_dot(f_hi, wr_ref[1]) + _dot(f_lo, wr_ref[0]))) + br_ref[...]


def outproj(x, attn, ssd, w_out_b, ffn_gain, w_router, b_router, tm):
    m = x.shape[0]
    row = lambda width: pl.BlockSpec((tm, width), lambda i: (i, 0))
    fixed = lambda r, c: pl.BlockSpec((r, c), lambda i: (0, 0))
    w_hi = w_router.astype(BF16)
    w_hi_rest = (w_router - w_hi.astype(F32)).astype(BF16)
    return pl.pallas_call(
        _outproj_kernel,
        grid=(m // tm,),
        in_specs=[row(D_MODEL), row(ATT_W), row(D_INNER), fixed(ATT_W + D_INNER, D_MODEL),
                  fixed(1, D_MODEL), pl.BlockSpec((2, D_MODEL, 128), lambda i: (0, 0, 0)), fixed(1, 128)],
        out_specs=[row(D_MODEL), row(D_MODEL), row(128)],
        out_shape=[jax.ShapeDtypeStruct((m, D_MODEL), F32), jax.ShapeDtypeStruct((m, D_MODEL), BF16),
                   jax.ShapeDtypeStruct((m, 128), F32)],
        compiler_params=_cparams(("parallel",)),
        name="outproj",
    )(x, attn, ssd, w_out_b, ffn_gain.reshape(1, D_MODEL), jnp.stack([w_hi, w_hi_rest]), b_router)


def _ple_kernel(h_ref, y0_ref, y1_ref, wt_ref, p_ref, wg_ref, wp_ref, g_ref, o_ref):
    wt = wt_ref[...]
    h = h_ref[...] + (y0_ref[...] * wt[:, 0:1] + y1_ref[...] * wt[:, 1:2])
    gate = jax.nn.sigmoid(_dot(h.astype(BF16), wg_ref[...]))
    e = _dot(p_ref[...].astype(BF16), wp_ref[...])
    e = e * lax.rsqrt(jnp.mean(e * e, axis=-1, keepdims=True) + EPS) * g_ref[...]
    o_ref[...] = h + gate * e


def moe_combine_ple(h, y01, wts, p, wg_b, wp_b, gain, tm):
    m = h.shape[0]
    nt = m // tm
    return pl.pallas_call(
        _ple_kernel,
        grid=(nt,),
        in_specs=[pl.BlockSpec((tm, D_MODEL), lambda i: (i, 0)),
                  pl.BlockSpec((tm, D_MODEL), lambda i: (i, 0)),
                  pl.BlockSpec((tm, D_MODEL), lambda i: (i + nt, 0)),
                  pl.BlockSpec((tm, 128), lambda i: (i, 0)),
                  pl.BlockSpec((tm, PLE_DIM), lambda i: (i, 0)),
                  pl.BlockSpec((D_MODEL, D_MODEL), lambda i: (0, 0)),
                  pl.BlockSpec((PLE_DIM, D_MODEL), lambda i: (0, 0)),
                  pl.BlockSpec((1, D_MODEL), lambda i: (0, 0))],
        out_specs=pl.BlockSpec((tm, D_MODEL), lambda i: (i, 0)),
        out_shape=jax.ShapeDtypeStruct((m, D_MODEL), F32),
        compiler_params=_cparams(("parallel",)),
        name="moe_combine_ple",
    )(h, y01, y01, wts, p, wg_b, wp_b, gain.reshape(1, D_MODEL))


def _moe_kernel(be_ref, nb_ref, x_ref, wg_ref, wu_ref, wd_ref, y_ref, wg_sc, wu_sc, wd_sc):
    i = pl.program_id(0)

    @pl.when((i == 0) | (be_ref[i] != be_ref[jnp.maximum(i - 1, 0)]))
    def _():
        wg_sc[...] = wg_ref[0].astype(BF16)
        wu_sc[...] = wu_ref[0].astype(BF16)
        wd_sc[...] = wd_ref[0].astype(BF16)

    @pl.when(i < nb_ref[0])
    def _():
        x = x_ref[...]
        a = _dot(x, wg_sc[...])
        hb = (a * jax.nn.sigmoid(a)) * _dot(x, wu_sc[...])
        y_ref[...] = _dot(hb.astype(BF16), wd_sc[...])

    @pl.when(i >= nb_ref[0])
    def _():
        y_ref[...] = jnp.zeros(y_ref.shape, F32)


def moe_experts(xpad, blk_e, n_used, w_gate, w_up, w_down):
    n_blk = xpad.shape[0] // MOE_ROWS
    grid_spec = pltpu.PrefetchScalarGridSpec(
        num_scalar_prefetch=2,
        grid=(n_blk,),
        in_specs=[pl.BlockSpec((MOE_ROWS, D_MODEL), lambda i, be, nb: (i, 0)),
                  pl.BlockSpec((1, D_MODEL, D_EXPERT), lambda i, be, nb: (be[i], 0, 0)),
                  pl.BlockSpec((1, D_MODEL, D_EXPERT), lambda i, be, nb: (be[i], 0, 0)),
                  pl.BlockSpec((1, D_EXPERT, D_MODEL), lambda i, be, nb: (be[i], 0, 0))],
        out_specs=pl.BlockSpec((MOE_ROWS, D_MODEL), lambda i, be, nb: (i, 0)),
        scratch_shapes=[pltpu.VMEM((D_MODEL, D_EXPERT), BF16), pltpu.VMEM((D_MODEL, D_EXPERT), BF16),
                        pltpu.VMEM((D_EXPERT, D_MODEL), BF16)],
    )
    return pl.pallas_call(
        _moe_kernel,
        grid_spec=grid_spec,
        out_shape=jax.ShapeDtypeStruct((n_blk * MOE_ROWS, D_MODEL), F32),
        compiler_params=_cparams(("arbitrary",)),
        name="moe_experts",
    )(blk_e, n_used, xpad, w_gate, w_up, w_down)


def _compress_kernel(seg_ref, w1a_ref, w1b_ref, pe_ref, w1_ref, w2_ref, o_ref):
    seg = seg_ref[0, 0]
    n_seg = seg.shape[0]
    hid0 = _dot(pe_ref[0], w1_ref[0])[0:1]
    p0 = _dot(seg, w1a_ref[0])
    p1 = pltpu.roll(_dot(seg, w1b_ref[0]), n_seg - 1, axis=0)
    hid = hid0 + p0 + p1
    act = hid * jax.nn.sigmoid(hid)
    o_ref[0, 0] = _dot(act.astype(BF16), w2_ref[0]).astype(BF16)


def compress(segs, cmp_pe, cmp_w1, cmp_w2):
    _, r, n_seg, k = segs.shape
    w1 = cmp_w1.astype(BF16)
    pe = jnp.broadcast_to(cmp_pe.reshape(2, 1, L_CMP * HEAD_DIM), (2, 8, L_CMP * HEAD_DIM)).astype(BF16)
    return pl.pallas_call(
        _compress_kernel,
        grid=(2, r),
        in_specs=[pl.BlockSpec((1, 1, n_seg, k), lambda s, i: (s, i, 0, 0)),
                  pl.BlockSpec((1, k, CMP_HID), lambda s, i: (s, 0, 0)),
                  pl.BlockSpec((1, k, CMP_HID), lambda s, i: (s, 1, 0)),
                  pl.BlockSpec((1, 8, 2 * k), lambda s, i: (s, 0, 0)),
                  pl.BlockSpec((1, 2 * k, CMP_HID), lambda s, i: (s, 0, 0)),
                  pl.BlockSpec((1, CMP_HID, HEAD_DIM), lambda s, i: (s, 0, 0))],
        out_specs=pl.BlockSpec((1, 1, n_seg, HEAD_DIM), lambda s, i: (s, i, 0, 0)),
        out_shape=jax.ShapeDtypeStruct((2, r, n_seg, HEAD_DIM), BF16),
        compiler_params=_cparams(("parallel", "parallel")),
        name="compress",
    )(segs, w1, w1, pe, w1, cmp_w2.astype(BF16))


PAGES_PER_STEP = 32


def _page_copies(pt_ref, pool_ref, buf, sem, b, j, slot):
    return [pltpu.make_async_copy(pool_ref.at[pt_ref[b, j * PAGES_PER_STEP + p], pl.ds(0, 2)],
                                  buf.at[slot, p], sem.at[slot]) for p in range(PAGES_PER_STEP)]


def _paged_partials_kernel(pt_ref, pool_ref, wa_ref, wb_ref, p0_ref, p1_ref, buf, sem, x_sc):
    b = pl.program_id(0)
    j = pl.program_id(1)
    n_grp = pl.num_programs(1)
    step = b * n_grp + j
    slot = step % 2

    @pl.when(step == 0)
    def _():
        for c in _page_copies(pt_ref, pool_ref, buf, sem, b, j, slot):
            c.start()

    @pl.when(step + 1 < pl.num_programs(0) * n_grp)
    def _():
        wrap = j + 1 == n_grp
        for c in _page_copies(pt_ref, pool_ref, buf, sem, jnp.where(wrap, b + 1, b), jnp.where(wrap, 0, j + 1),
                              1 - slot):
            c.start()

    for c in _page_copies(pt_ref, pool_ref, buf, sem, b, j, slot):
        c.wait()

    n_seg = PAGES_PER_STEP * PAGE_SIZE // S_CMP
    for kv in range(2):
        for p in range(PAGES_PER_STEP):
            x_sc[p * PAGE_SIZE:(p + 1) * PAGE_SIZE, :] = buf[slot, p, kv].reshape(KV_HEADS * HEAD_DIM, PAGE_SIZE).T
        acc0 = jnp.zeros((n_seg, KV_HEADS * CMP_HID), F32)
        acc1 = jnp.zeros((n_seg, KV_HEADS * CMP_HID), F32)
        for s in range(S_CMP):
            xs = x_sc[pl.ds(s, n_seg, stride=S_CMP), :].astype(BF16)
            acc0 = acc0 + _dot(xs, wa_ref[kv, s])
            acc1 = acc1 + _dot(xs, wb_ref[kv, s])
        p0_ref[kv, 0] = acc0
        p1_ref[kv, 0] = acc1


def _compress_finish_kernel(p0_ref, p1_ref, pe_ref, w1_ref, w2_ref, o_ref):
    n_seg = p0_ref.shape[2]
    hid0 = _dot(pe_ref[0], w1_ref[0])[0:1]
    hid = hid0 + p0_ref[0, 0] + pltpu.roll(p1_ref[0, 0], n_seg - 1, axis=0)
    act = hid * jax.nn.sigmoid(hid)
    o_ref[0, 0] = _dot(act.astype(BF16), w2_ref[0]).astype(BF16)


def _per_group(w):
    zero = jnp.zeros_like(w)
    return jnp.concatenate([jnp.concatenate([w, zero], axis=-1), jnp.concatenate([zero, w], axis=-1)], axis=-2)


def compress_paged(pool_t, page_table, cmp_pe, cmp_w1, cmp_w2):
    bsz, n_pages = page_table.shape
    assert n_pages % PAGES_PER_STEP == 0 and KV_HEADS == 2
    n_grp = n_pages // PAGES_PER_STEP
    seg_step = PAGES_PER_STEP * PAGE_SIZE // S_CMP
    n_seg = n_grp * seg_step
    gw = KV_HEADS * HEAD_DIM
    w1 = _per_group(cmp_w1.astype(BF16).reshape(2, L_CMP // S_CMP, S_CMP, HEAD_DIM, CMP_HID))
    part_shape = jax.ShapeDtypeStruct((2, bsz, n_seg, KV_HEADS * CMP_HID), F32)
    part_spec = pl.BlockSpec((2, 1, seg_step, KV_HEADS * CMP_HID), lambda b, j, pt: (0, b, j, 0))
    wspec = pl.BlockSpec((2, S_CMP, gw, KV_HEADS * CMP_HID), lambda b, j, pt: (0, 0, 0, 0))
    p0, p1 = pl.pallas_call(
        _paged_partials_kernel,
        grid_spec=pltpu.PrefetchScalarGridSpec(
            num_scalar_prefetch=1,
            grid=(bsz, n_grp),
            in_specs=[pl.BlockSpec(memory_space=pl.ANY), wspec, wspec],
            out_specs=[part_spec, part_spec],
            scratch_shapes=[pltpu.VMEM((2, PAGES_PER_STEP, 2, KV_HEADS, HEAD_DIM, PAGE_SIZE), F32),
                            pltpu.SemaphoreType.DMA((2,)),
                            pltpu.VMEM((PAGES_PER_STEP * PAGE_SIZE, gw), F32)]),
        out_shape=[part_shape, part_shape],
        compiler_params=_cparams(("arbitrary", "arbitrary")),
        name="paged_partials",
    )(page_table, pool_t, w1[:, 0], w1[:, 1])
    pe = jnp.broadcast_to(cmp_pe.reshape(2, 1, L_CMP * HEAD_DIM), (2, 8, L_CMP * HEAD_DIM)).astype(BF16)
    w1_both = jnp.concatenate([cmp_w1, cmp_w1], axis=-1).astype(BF16)
    pspec = pl.BlockSpec((1, 1, n_seg, KV_HEADS * CMP_HID), lambda s, i: (s, i, 0, 0))
    return pl.pallas_call(
        _compress_finish_kernel,
        grid=(2, bsz),
        in_specs=[pspec, pspec,
                  pl.BlockSpec((1, 8, L_CMP * HEAD_DIM), lambda s, i: (s, 0, 0)),
                  pl.BlockSpec((1, L_CMP * HEAD_DIM, KV_HEADS * CMP_HID), lambda s, i: (s, 0, 0)),
                  pl.BlockSpec((1, KV_HEADS * CMP_HID, gw), lambda s, i: (s, 0, 0))],
        out_specs=pl.BlockSpec((1, 1, n_seg, gw), lambda s, i: (s, i, 0, 0)),
        out_shape=jax.ShapeDtypeStruct((2, bsz, n_seg, gw), BF16),
        compiler_params=_cparams(("parallel", "parallel")),
        name="compress_finish",
    )(p0, p1, pe, w1_both, _per_group(cmp_w2.astype(BF16)))


FLASH_SPLIT = 4


def _flash_scores(k, q_ref, s_ref):
    cw = q_ref.shape[1] // FLASH_SPLIT
    for i in range(FLASH_SPLIT):
        c = slice(i * cw, (i + 1) * cw)
        s_ref[:, c] = _dot(k, q_ref[:, c])


def _flash_step(s_ref, v_t, m_sc, l_sc, acc_sc, keep=None):
    cw = s_ref.shape[1] // FLASH_SPLIT
    for i in range(FLASH_SPLIT):
        c = slice(i * cw, (i + 1) * cw)
        s_t = s_ref[:, c]
        if keep is not None:
            s_t = jnp.where(keep(i * cw, cw), s_t, NEG)
        m_prev = m_sc[:, c]
        m_new = jnp.maximum(m_prev, jnp.max(s_t, axis=0, keepdims=True))
        alpha = jnp.exp(m_prev - m_new)
        p = jnp.exp(s_t - m_new)
        l_sc[:, c] = alpha * l_sc[:, c] + jnp.sum(p, axis=0, keepdims=True)
        acc_sc[:, c] = alpha * acc_sc[:, c] + _dot(v_t, p.astype(BF16))
        m_sc[:, c] = m_new


def _flash_reset(m_sc, l_sc, acc_sc):
    m_sc[...] = jnp.full(m_sc.shape, NEG, F32)
    l_sc[...] = jnp.zeros(l_sc.shape, F32)
    acc_sc[...] = jnp.zeros(acc_sc.shape, F32)


def _nsa_prompt_kernel(qn_ref, qr_ref, kc_ref, vct_ref, ks_ref, vst_ref, kw_ref, vwt_ref, covert_ref, gate_ref,
                       o_ref, qa_sc, qw_sc, m_sc, l_sc, acc_sc, out_sc, score_sc, rank_sc, s_sc, sw_sc, *, tq, tk, tkw):
    qi = pl.program_id(2)
    pos0 = qi * tq
    w = HPG * tq

    def positions(rows, base):
        kpos = base + lax.broadcasted_iota(jnp.int32, (rows, w), 0)
        qpos = pos0 + (lax.broadcasted_iota(jnp.int32, (rows, w), 1) & (tq - 1))
        return kpos, qpos

    g = pl.program_id(1)

    def gate_row(branch):
        rows = [gate_ref[0, 0, pl.ds(3 * (g * HPG + hh) + branch, 1), :] for hh in range(HPG)]
        return jax.nn.sigmoid(jnp.concatenate(rows, axis=1))

    gate = [gate_row(branch) for branch in range(3)]

    kc = kc_ref[0, 0]
    n_c = kc.shape[0]
    cidx, qpos_c = positions(n_c, 0)
    valid = cidx * S_CMP + (L_CMP - 1) <= qpos_c
    s1 = jnp.where(valid, _dot(kc, qn_ref[0, 0, 0]), NEG)
    e1 = jnp.exp(s1 - jnp.max(s1, axis=0, keepdims=True)) * valid.astype(F32)
    p1b = (e1 * (1.0 / jnp.maximum(jnp.sum(e1, axis=0, keepdims=True), 1e-30))).astype(BF16)
    out_sc[...] = gate[0] * _dot(vct_ref[0, 0], p1b)

    p1_stack = jnp.concatenate([p1b[:, hh * tq:(hh + 1) * tq] for hh in range(HPG)], axis=0)
    imp = _dot(covert_ref[...], p1_stack)
    n_s = imp.shape[0]
    blk = lax.broadcasted_iota(jnp.int32, (n_s, tq), 0)
    cur = (pos0 + lax.broadcasted_iota(jnp.int32, (n_s, tq), 1)) // L_SLC
    vis = blk <= cur
    forced = vis & ((blk == 0) | (blk >= cur - 1))
    score_sc[...] = jnp.where(forced, 1e9, jnp.where(vis, imp, -1.0))
    rank_sc[...] = jnp.zeros(rank_sc.shape, F32)
    last_blk = (pos0 + tq - 1) // L_SLC
    n_oct = n_s // 8
    for oi in range(n_oct):
        @pl.when(oi * 8 <= last_blk)
        def _():
            rows = [slice(8 * oj, 8 * oj + 8) for oj in range(n_oct)]
            parts = [rank_sc[r, :] for r in rows]
            for i in range(8 * oi, 8 * oi + 8):
                c = score_sc[i:i + 1, :]
                for oj, r in enumerate(rows):
                    sj = score_sc[r, :]
                    if oj > oi:
                        beats = c >= sj
                    elif oj < oi:
                        beats = c > sj
                    else:
                        above = lax.broadcasted_iota(jnp.int32, (8, tq), 0) > i - 8 * oi
                        beats = (c > sj) | ((c == sj) & above)
                    parts[oj] = parts[oj] + beats.astype(F32)
            for r, part in zip(rows, parts):
                rank_sc[r, :] = part
    sel_bias = jnp.where(vis & (rank_sc[...] < N_SEL), 0.0, NEG).astype(BF16)
    own = pl.ds(pl.multiple_of(g * HEAD_DIM, HEAD_DIM), HEAD_DIM)
    other = pl.ds(pl.multiple_of((1 - g) * HEAD_DIM, HEAD_DIM), HEAD_DIM)
    qa_sc[other, :] = jnp.zeros((HEAD_DIM, w), BF16)
    qw_sc[other, :] = jnp.zeros((HEAD_DIM, w), BF16)
    qa_sc[own, :] = qr_ref[0, 0, 0]
    qw_sc[own, :] = qr_ref[0, 0, 0]
    qa_sc[pl.ds(pl.multiple_of((1 - g) * HEAD_DIM, HEAD_DIM), n_s), :] = jnp.concatenate([sel_bias] * HPG, axis=1)

    def sel_keys(kt):
        return ks_ref[0, 0, pl.ds(pl.multiple_of(kt * tk, tk), tk), :]

    def win_keys(kt):
        return kw_ref[pl.ds(pl.multiple_of(kt * tkw, tkw), tkw), :]

    def chunk_positions(rows, base, lane0, n_lanes):
        kpos = base + lax.broadcasted_iota(jnp.int32, (rows, n_lanes), 0)
        qpos = pos0 + ((lane0 + lax.broadcasted_iota(jnp.int32, (rows, n_lanes), 1)) & (tq - 1))
        return kpos, qpos

    def causal(rows, base):
        def keep(lane0, n_lanes):
            kpos, qpos = chunk_positions(rows, base, lane0, n_lanes)
            return kpos <= qpos
        return keep

    _flash_reset(m_sc, l_sc, acc_sc)
    n_full = pos0 // tk
    _flash_scores(sel_keys(0), qa_sc, s_sc.at[0])

    def sel_body(kt, carry):
        for parity in range(2):
            @pl.when(kt % 2 == parity)
            def _():
                _flash_scores(sel_keys(kt + 1), qa_sc, s_sc.at[1 - parity])
                _flash_step(s_sc.at[parity], vst_ref[0, 0, kt], m_sc, l_sc, acc_sc)
        return carry

    lax.fori_loop(0, n_full, sel_body, 0)

    n_inner = WINDOW // tkw - 1
    win_tiles = [qi - d for d in range(n_inner + 2)]
    for idx, kt in enumerate(win_tiles):
        _flash_scores(win_keys(jnp.maximum(kt, 0)), qw_sc, sw_sc.at[idx])

    _flash_step(s_sc.at[n_full % 2], vst_ref[0, 0, n_full], m_sc, l_sc, acc_sc, causal(tk, n_full * tk))
    out_sc[...] = out_sc[...] + (gate[1] * (1.0 / l_sc[...])) * acc_sc[...]

    _flash_reset(m_sc, l_sc, acc_sc)
    _flash_step(sw_sc.at[0], vwt_ref[0, 0, qi], m_sc, l_sc, acc_sc, causal(tkw, pos0))
    for idx in range(1, n_inner + 1):
        kt = win_tiles[idx]
        _flash_step(sw_sc.at[idx], vwt_ref[0, 0, jnp.maximum(kt, 0)], m_sc, l_sc, acc_sc,
                    lambda lane0, n_lanes, kt=kt: kt >= 0)
    kt_far = win_tiles[n_inner + 1]

    def far_keep(lane0, n_lanes):
        kpos, qpos = chunk_positions(tkw, kt_far * tkw, lane0, n_lanes)
        return (kpos > qpos - WINDOW) & (kt_far >= 0)

    _flash_step(sw_sc.at[n_inner + 1], vwt_ref[0, 0, jnp.maximum(kt_far, 0)], m_sc, l_sc, acc_sc, far_keep)

    o = out_sc[...] + (gate[2] * (1.0 / l_sc[...])) * acc_sc[...]
    for pair in range(HPG // 2):
        two = jnp.concatenate([o[:, (2 * pair + e) * tq:(2 * pair + e + 1) * tq] for e in range(2)], axis=0)
        o_ref[:, 128 * pair:128 * (pair + 1)] = two.T.astype(o_ref.dtype)


NSA_TQ, NSA_TK, NSA_TKW = 256, 256, 256


def nsa_prompt(qn_t, qr_t, kc, vc_t, ks_aug, vs_t, kw, vw_t, sm_t, bsz, t):
    tq, tk, tkw = NSA_TQ, NSA_TK, NSA_TKW
    assert tq == tkw and tk % tq == 0 and WINDOW % tkw == 0
    n_c = kc.shape[1]
    n_s = t // L_SLC
    assert n_s <= HEAD_DIM and KV_HEADS == 2
    w = HPG * tq
    c0 = np.arange(n_c)[None, :] * S_CMP
    s0 = np.arange(n_s)[:, None] * L_SLC
    cover_t = (np.maximum(np.minimum(c0 + L_CMP, s0 + L_SLC) - np.maximum(c0, s0), 0) / L_CMP).astype(np.float32)
    cover_t = np.tile(cover_t, (1, HPG))
    qspec = pl.BlockSpec((1, 1, 1, HEAD_DIM, w), lambda b, g, i: (b, g, i, 0, 0))
    kern = functools.partial(_nsa_prompt_kernel, tq=tq, tk=tk, tkw=tkw)
    return pl.pallas_call(
        kern,
        grid=(bsz, KV_HEADS, t // tq),
        in_specs=[qspec, qspec,
                  pl.BlockSpec((1, 1, n_c, HEAD_DIM), lambda b, g, i: (b, g, 0, 0)),
                  pl.BlockSpec((1, 1, HEAD_DIM, n_c), lambda b, g, i: (b, g, 0, 0)),
                  pl.BlockSpec((1, 1, t, 2 * HEAD_DIM), lambda b, g, i: (b, g, 0, 0)),
                  pl.BlockSpec((1, 1, t // tk, HEAD_DIM, tk), lambda b, g, i: (b, g, 0, 0, 0)),
                  pl.BlockSpec((t, KV_HEADS * HEAD_DIM), lambda b, g, i: (b, 0)),
                  pl.BlockSpec((1, 1, t // tkw, HEAD_DIM, tkw), lambda b, g, i: (b, g, 0, 0, 0)),
                  pl.BlockSpec((n_s, HPG * n_c), lambda b, g, i: (0, 0)),
                  pl.BlockSpec((1, 1, 32, tq), lambda b, g, i: (b, i, 0, 0))],
        out_specs=pl.BlockSpec((tq, HPG * HEAD_DIM), lambda b, g, i: (b * (t // tq) + i, g)),
        out_shape=jax.ShapeDtypeStruct((bsz * t, ATT_W), BF16),
        scratch_shapes=[pltpu.VMEM((2 * HEAD_DIM, w), BF16), pltpu.VMEM((2 * HEAD_DIM, w), BF16),
                        pltpu.VMEM((1, w), F32), pltpu.VMEM((1, w), F32),
                        pltpu.VMEM((HEAD_DIM, w), F32), pltpu.VMEM((HEAD_DIM, w), F32),
                        pltpu.VMEM((n_s, tq), F32), pltpu.VMEM((n_s, tq), F32),
                        pltpu.VMEM((2, tk, w), F32), pltpu.VMEM((WINDOW // tkw + 1, tkw, w), F32)],
        compiler_params=_cparams(("parallel", "parallel", "arbitrary")),
        name="nsa_prompt",
    )(qn_t, qr_t, kc.reshape(bsz, KV_HEADS, n_c, HEAD_DIM), vc_t.reshape(bsz, KV_HEADS, HEAD_DIM, n_c),
      ks_aug, vs_t, kw, vw_t, jnp.asarray(cover_t, BF16), sm_t)


def _sample_select_kernel(qn_ref, kc_ref, vc_ref, cover_ref, pick_ref, oc_ref, sb_ref, *, q_off, lq, n_cmp, n_slc):
    rows = HPG * lq
    for g in range(KV_HEADS):
        kc = kc_ref[0]
        n_c = kc.shape[0]
        vc_g = _dot(vc_ref[0], pick_ref[g]).astype(BF16)
        cidx = lax.broadcasted_iota(jnp.int32, (rows, n_c), 1)
        qpos = q_off + (lax.broadcasted_iota(jnp.int32, (rows, n_c), 0) & (lq - 1))
        valid = (cidx * S_CMP + (L_CMP - 1) <= qpos) & (cidx < n_cmp)
        s1 = jnp.where(valid, _dot_nt(qn_ref[0, g], kc), NEG)
        e1 = jnp.exp(s1 - jnp.max(s1, axis=-1, keepdims=True)) * valid.astype(F32)
        p1b = (e1 * (1.0 / jnp.maximum(jnp.sum(e1, axis=-1, keepdims=True), 1e-30))).astype(BF16)
        oc_ref[0, g] = _dot(p1b, vc_g)
        imp4 = _dot(p1b, cover_ref[...])
        imp = imp4[0:lq]
        for hh in range(1, HPG):
            imp = imp + imp4[hh * lq:(hh + 1) * lq]
        n_sp = imp.shape[1]
        blk = lax.broadcasted_iota(jnp.int32, (lq, n_sp), 1)
        cur = (q_off + lax.broadcasted_iota(jnp.int32, (lq, n_sp), 0)) // L_SLC
        vis = (blk <= cur) & (blk < n_slc)
        forced = vis & ((blk == 0) | (blk >= cur - 1))
        score = jnp.where(forced, 1e9, jnp.where(vis, imp, -1.0))
        rank = jnp.zeros((lq, n_sp), F32)
        for i in range(n_slc):
            c = score[:, i:i + 1]
            beats = (c > score) | ((c == score) & (blk > i))
            rank = rank + beats.astype(F32)
        sb_ref[0, g] = jnp.where(vis & (rank < min(N_SEL, n_slc)), 0.0, NEG)


def _sample_attend_kernel(pt_ref, pool_ref, qr_ref, sb_ref, exp_ref, oc_ref, gate_ref, kn_ref, vn_ref, kw_ref, vw_ref,
                          o_ref, buf, sem, m_sc, l_sc, acc_sc, *, q_off, lq, n_win):
    b = pl.program_id(0)
    j = pl.program_id(1)
    n_grp = pl.num_programs(1)
    step = b * n_grp + j
    slot = step % 2
    rows = HPG * lq

    def copies(bb, jj, sl):
        return [pltpu.make_async_copy(pool_ref.at[pt_ref[bb, jj * PAGES_PER_STEP + p], pl.ds(2, 2)],
                                      buf.at[sl, p], sem.at[sl]) for p in range(PAGES_PER_STEP)]

    @pl.when(step == 0)
    def _():
        for c in copies(b, j, slot):
            c.start()

    @pl.when(step + 1 < pl.num_programs(0) * n_grp)
    def _():
        wrap = j + 1 == n_grp
        for c in copies(jnp.where(wrap, b + 1, b), jnp.where(wrap, 0, j + 1), 1 - slot):
            c.start()

    for c in copies(b, j, slot):
        c.wait()

    @pl.when(j == 0)
    def _():
        m_sc[...] = jnp.full(m_sc.shape, NEG, F32)
        l_sc[...] = jnp.zeros(l_sc.shape, F32)
        acc_sc[...] = jnp.zeros(acc_sc.shape, F32)

    def update(g, s, v, v_transposed):
        m_prev = m_sc[g]
        m_new = jnp.maximum(m_prev, jnp.max(s, axis=-1, keepdims=True))
        alpha = jnp.exp(m_prev - m_new)
        p = jnp.exp(s - m_new)
        l_sc[g] = alpha * l_sc[g] + jnp.sum(p, axis=-1, keepdims=True)
        pv = _dot_nt(p.astype(BF16), v) if v_transposed else _dot(p.astype(BF16), v)
        acc_sc[g] = alpha * acc_sc[g] + pv
        m_sc[g] = m_new

    scores, values = [], []
    for g in range(KV_HEADS):
        k_t = jnp.concatenate([buf[slot, p, 0, g] for p in range(PAGES_PER_STEP)], axis=1).astype(BF16)
        sb = sb_ref[0, g, 0]
        bias = _dot(jnp.concatenate([sb] * HPG, axis=0).astype(BF16), exp_ref[...])
        scores.append(_dot(qr_ref[0, g], k_t) + bias)
        values.append(jnp.concatenate([buf[slot, p, 1, g] for p in range(PAGES_PER_STEP)], axis=1).astype(BF16))
    for g in range(KV_HEADS):
        update(g, scores[g], values[g], True)

    @pl.when(j == n_grp - 1)
    def _():
        for g in range(KV_HEADS):
            qr = qr_ref[0, g]
            kn = kn_ref[0, g]
            kidx = lax.broadcasted_iota(jnp.int32, (rows, kn.shape[0]), 1)
            qidx = lax.broadcasted_iota(jnp.int32, (rows, kn.shape[0]), 0) & (lq - 1)
            update(g, jnp.where((kidx <= qidx) & (kidx < lq), _dot_nt(qr, kn), NEG), vn_ref[0, g], False)
            o_s = acc_sc[g] * (1.0 / l_sc[g])
            kw = kw_ref[0, g]
            widx = lax.broadcasted_iota(jnp.int32, (rows, kw.shape[0]), 1)
            kpos = q_off + lq - n_win + widx
            qpos = q_off + (lax.broadcasted_iota(jnp.int32, (rows, kw.shape[0]), 0) & (lq - 1))
            ok = (widx < n_win) & (kpos <= qpos) & (kpos > qpos - WINDOW) & (kpos >= 0)
            s3 = jnp.where(ok, _dot_nt(qr, kw), NEG)
            e3 = jnp.exp(s3 - jnp.max(s3, axis=-1, keepdims=True)) * ok.astype(F32)
            p3 = e3 * (1.0 / jnp.maximum(jnp.sum(e3, axis=-1, keepdims=True), 1e-30))
            o_w = _dot(p3.astype(BF16), vw_ref[0, g])
            gate = jax.nn.sigmoid(gate_ref[0, g])
            o_ref[0, g] = gate[:, 0:1] * oc_ref[0, g] + gate[:, 1:2] * o_s + gate[:, 2:3] * o_w


def nsa_sample(q, q_rot, kvc, kv_new, win_cat, pool_t, page_table, gate_logits, q_off):
    bsz, lq = q.shape[:2]
    assert lq & (lq - 1) == 0 and lq <= L_SLC and q_off % (PAGES_PER_STEP * PAGE_SIZE) == 0
    rows = HPG * lq
    scale = HEAD_DIM ** -0.5
    n_seg = kvc.shape[2]
    t_kv = q_off + lq
    n_cmp = (t_kv - L_CMP) // S_CMP + 1
    n_slc = -(-t_kv // L_SLC)
    n_sp = -(-n_slc // 128) * 128
    n_grp = page_table.shape[1] // PAGES_PER_STEP
    blk_step = PAGES_PER_STEP * PAGE_SIZE // L_SLC
    key_step = PAGES_PER_STEP * PAGE_SIZE

    def rows_major(a):
        return a.reshape(bsz, lq, KV_HEADS, HPG, -1).transpose(0, 2, 3, 1, 4).reshape(bsz, KV_HEADS, rows, -1)

    def keys_major(a, n_pad):
        a = a.astype(BF16).transpose(0, 2, 1, 3)
        return jnp.pad(a, ((0, 0), (0, 0), (0, n_pad - a.shape[2]), (0, 0)))

    qn_r = rows_major((q * scale).astype(BF16))
    qr_r = rows_major((q_rot * scale).astype(BF16))
    c0 = np.arange(n_seg)[:, None] * S_CMP
    s0 = np.arange(n_sp)[None, :] * L_SLC
    cover = np.maximum(np.minimum(c0 + L_CMP, s0 + L_SLC) - np.maximum(c0, s0), 0) / L_CMP
    cover = cover * (np.arange(n_seg)[:, None] < n_cmp) * (np.arange(n_sp)[None, :] < n_slc)
    gw = KV_HEADS * HEAD_DIM
    zeros_q = jnp.zeros_like(qn_r[:, 0])
    qn_wide = jnp.stack([jnp.concatenate([qn_r[:, 0], zeros_q], axis=-1),
                         jnp.concatenate([zeros_q, qn_r[:, 1]], axis=-1)], axis=1)
    pick = (np.arange(gw)[None, :, None] == np.arange(KV_HEADS)[:, None, None] * HEAD_DIM
            + np.arange(HEAD_DIM)[None, None, :]).astype(np.float32)
    spec_q = pl.BlockSpec((1, KV_HEADS, rows, HEAD_DIM), lambda b: (b, 0, 0, 0))
    o_c, sel_bias = pl.pallas_call(
        functools.partial(_sample_select_kernel, q_off=q_off, lq=lq, n_cmp=n_cmp, n_slc=n_slc),
        grid=(bsz,),
        in_specs=[pl.BlockSpec((1, KV_HEADS, rows, gw), lambda b: (b, 0, 0, 0)),
                  pl.BlockSpec((1, n_seg, gw), lambda b: (b, 0, 0)),
                  pl.BlockSpec((1, n_seg, gw), lambda b: (b, 0, 0)),
                  pl.BlockSpec((n_seg, n_sp), lambda b: (0, 0)),
                  pl.BlockSpec((KV_HEADS, gw, HEAD_DIM), lambda b: (0, 0, 0))],
        out_specs=[spec_q, pl.BlockSpec((1, KV_HEADS, lq, n_sp), lambda b: (b, 0, 0, 0))],
        out_shape=[jax.ShapeDtypeStruct((bsz, KV_HEADS, rows, HEAD_DIM), F32),
                   jax.ShapeDtypeStruct((bsz, KV_HEADS, lq, n_sp), F32)],
        compiler_params=_cparams(("parallel",)),
        name="sample_select",
    )(qn_wide, kvc[0], kvc[1], jnp.asarray(cover, BF16), jnp.asarray(pick, BF16))

    sb_steps = sel_bias[..., :n_grp * blk_step].reshape(bsz, KV_HEADS, lq, n_grp, blk_step).transpose(0, 1, 3, 2, 4)
    expand = (np.arange(key_step)[None, :] // L_SLC == np.arange(blk_step)[:, None]).astype(np.float32)
    gates = jnp.pad(rows_major(gate_logits.reshape(bsz, lq, -1)), ((0, 0), (0, 0), (0, 0), (0, 128 - 3)))
    n_win = win_cat.shape[1]
    n_wp = -(-n_win // 128) * 128
    spec2 = lambda shape: pl.BlockSpec((1,) + shape, lambda b, j, pt: (b,) + (0,) * len(shape))
    attn = pl.pallas_call(
        functools.partial(_sample_attend_kernel, q_off=q_off, lq=lq, n_win=n_win),
        grid_spec=pltpu.PrefetchScalarGridSpec(
            num_scalar_prefetch=1,
            grid=(bsz, n_grp),
            in_specs=[pl.BlockSpec(memory_space=pl.ANY),
                      spec2((KV_HEADS, rows, HEAD_DIM)),
                      pl.BlockSpec((1, KV_HEADS, 1, lq, blk_step), lambda b, j, pt: (b, 0, j, 0, 0)),
                      pl.BlockSpec((blk_step, key_step), lambda b, j, pt: (0, 0)),
                      spec2((KV_HEADS, rows, HEAD_DIM)),
                      spec2((KV_HEADS, rows, 128)),
                      spec2((KV_HEADS, 128, HEAD_DIM)), spec2((KV_HEADS, 128, HEAD_DIM)),
                      spec2((KV_HEADS, n_wp, HEAD_DIM)), spec2((KV_HEADS, n_wp, HEAD_DIM))],
            out_specs=spec2((KV_HEADS, rows, HEAD_DIM)),
            scratch_shapes=[pltpu.VMEM((2, PAGES_PER_STEP, 2, KV_HEADS, HEAD_DIM, PAGE_SIZE), F32),
                            pltpu.SemaphoreType.DMA((2,)),
                            pltpu.VMEM((KV_HEADS, rows, 1), F32), pltpu.VMEM((KV_HEADS, rows, 1), F32),
                            pltpu.VMEM((KV_HEADS, rows, HEAD_DIM), F32)]),
        out_shape=jax.ShapeDtypeStruct((bsz, KV_HEADS, rows, HEAD_DIM), F32),
        compiler_params=_cparams(("arbitrary", "arbitrary")),
        name="sample_attend",
    )(page_table, pool_t, qr_r, sb_steps, jnp.asarray(expand, BF16), o_c, gates,
      keys_major(kv_new[:, :, 2], 128), keys_major(kv_new[:, :, 3], 128),
      keys_major(win_cat[:, :, 0], n_wp), keys_major(win_cat[:, :, 1], n_wp))
    attn = attn.reshape(bsz, KV_HEADS, HPG, lq, HEAD_DIM).transpose(0, 3, 1, 2, 4)
    return attn.astype(BF16).reshape(bsz * lq, ATT_W)


def _ssd_kernel(d_ref, x_ref, bt_ref, c_ref, acol_ref, arow_ref, dt_ref, s0_ref, y_ref, sout_ref, s_sc, *, cl):
    ci = pl.program_id(1)

    @pl.when(ci == 0)
    def _():
        s_sc[...] = s0_ref[0]

    li = lax.broadcasted_iota(jnp.int32, (cl, cl), 0)
    si = lax.broadcasted_iota(jnp.int32, (cl, cl), 1)
    causal = li >= si
    acum_col = _dot_hi(causal.astype(F32), acol_ref[0])
    acum_row = _dot_hi(arow_ref[0], (li <= si).astype(F32))
    dt = dt_ref[0]
    for g in range(SSM_GROUPS):
        cg = c_ref[0, g].astype(BF16)
        btg = bt_ref[0, g]
        cb = _dot(cg, btg.astype(BF16))
        for hh in range(SSM_HEADS // SSM_GROUPS):
            h = g * (SSM_HEADS // SSM_GROUPS) + hh
            ac = acum_col[:, h:h + 1]
            ar = acum_row[h:h + 1, :]
            decay = jnp.exp(jnp.where(causal, ac - ar, NEG))
            xs = x_ref[0, h]
            xd = (xs * dt[:, h:h + 1]).astype(BF16)
            st = s_sc[h]
            y = _dot((cb * decay).astype(BF16), xd) + _dot(cg, st.astype(BF16)) * jnp.exp(ac)
            y_ref[0, h] = y + d_ref[h] * xs
            a_last = ar[:, cl - 1:cl]
            snew = _dot((btg * jnp.exp(a_last - ar)).astype(BF16), xd)
            s_sc[h] = st * jnp.exp(a_last) + snew

    @pl.when(ci == pl.num_programs(1) - 1)
    def _():
        sout_ref[0] = s_sc[...]


def ssd_scan(ssm_d, x_hm, b_t, c, a_col, a_row, dt_col, s0_t, cl):
    bsz, nh, t, p = x_hm.shape
    n = D_STATE
    nc = t // cl
    kern = functools.partial(_ssd_kernel, cl=cl)
    return pl.pallas_call(
        kern,
        grid=(bsz, nc),
        in_specs=[pl.BlockSpec(memory_space=pltpu.SMEM),
                  pl.BlockSpec((1, nh, cl, p), lambda b, c_: (b, 0, c_, 0)),
                  pl.BlockSpec((1, SSM_GROUPS, n, cl), lambda b, c_: (b, 0, 0, c_)),
                  pl.BlockSpec((1, SSM_GROUPS, cl, n), lambda b, c_: (b, 0, c_, 0)),
                  pl.BlockSpec((1, cl, 128), lambda b, c_: (b, c_, 0)),
                  pl.BlockSpec((1, 8, cl), lambda b, c_: (b, 0, c_)),
                  pl.BlockSpec((1, cl, 128), lambda b, c_: (b, c_, 0)),
                  pl.BlockSpec((1, nh, n, p), lambda b, c_: (b, 0, 0, 0))],
        out_specs=[pl.BlockSpec((1, nh, cl, p), lambda b, c_: (b, 0, c_, 0)),
                   pl.BlockSpec((1, nh, n, p), lambda b, c_: (b, 0, 0, 0))],
        out_shape=[jax.ShapeDtypeStruct((bsz, nh, t, p), F32), jax.ShapeDtypeStruct((bsz, nh, n, p), F32)],
        scratch_shapes=[pltpu.VMEM((nh, n, p), F32)],
        compiler_params=_cparams(("parallel", "arbitrary")),
        name="ssd_scan",
    )(ssm_d, x_hm, b_t, c, a_col, a_row, dt_col, s0_t)


def _ssd_prompt_kernel(d_ref, xbc_ref, sm_ref, z_ref, cw_ref, cb_ref, dtb_ref, ah_ref, gn_ref, o_ref, sout_ref,
                       s_sc, tail_sc, *, cl):
    ci = pl.program_id(1)

    @pl.when(ci == 0)
    def _():
        s_sc[...] = jnp.zeros(s_sc.shape, F32)
        tail_sc[...] = jnp.zeros(tail_sc.shape, F32)

    xb = xbc_ref[...]
    prev = tail_sc[...]
    row8 = lax.broadcasted_iota(jnp.int32, (8, CONV_DIM), 0)

    def shifted(j):
        body = pltpu.roll(xb, j, axis=0)
        top = jnp.where(row8 < j, pltpu.roll(prev, j, axis=0), body[0:8])
        return jnp.concatenate([top, body[8:]], axis=0)

    conv = cb_ref[...] + shifted(CONV_W - 1) * cw_ref[0:1]
    for j in range(1, CONV_W):
        conv = conv + (shifted(CONV_W - 1 - j) if j < CONV_W - 1 else xb) * cw_ref[j:j + 1]
    tail_sc[...] = xb[cl - 8:cl]
    act = conv * jax.nn.sigmoid(conv)
    n_bc = SSM_GROUPS * D_STATE
    bmat = act[:, D_INNER:D_INNER + n_bc]
    cmat = act[:, D_INNER + n_bc:D_INNER + 2 * n_bc]
    x_dt = sm_ref[...] + dtb_ref[...]
    dt = jnp.maximum(x_dt, 0.0) + jnp.log(1.0 + jnp.exp(-jnp.abs(x_dt)))
    a = ah_ref[...] * dt

    li = lax.broadcasted_iota(jnp.int32, (cl, cl), 0)
    si = lax.broadcasted_iota(jnp.int32, (cl, cl), 1)
    causal = li >= si
    acum_col = _dot_hi(causal.astype(F32), a)
    acum_row = _dot_hi(a.T[N_GATES:N_GATES + SSM_HEADS], (li <= si).astype(F32))
    bt = bmat.T
    bt_b = bt.astype(BF16)
    lane = lax.broadcasted_iota(jnp.int32, (cl, 128), 1)
    low = lane < SSM_HEAD_DIM
    low_s = lax.broadcasted_iota(jnp.int32, (128, 128), 1) < SSM_HEAD_DIM

    def col(arr, h):
        return arr[:, N_GATES + h:N_GATES + h + 1]

    per_group = SSM_HEADS // SSM_GROUPS
    cg_of, cb_of = {}, {}
    ys = []
    for j in range(SSM_HEADS // 2):
        ha, hb = 2 * j, 2 * j + 1
        g = ha // per_group
        if g not in cg_of:
            in_group = (lane >= g * D_STATE) & (lane < (g + 1) * D_STATE)
            cg_of[g] = jnp.where(in_group, cmat, 0.0).astype(BF16)
            cb_of[g] = _dot(cg_of[g], bt_b)
        cg, cb = cg_of[g], cb_of[g]
        xs_pair = act[:, 128 * j:128 * (j + 1)]
        xd = (xs_pair * jnp.where(low, col(dt, ha), col(dt, hb))).astype(BF16)
        st = s_sc[j]
        y_off = _dot(cg, st.astype(BF16)) * jnp.where(low, jnp.exp(col(acum_col, ha)), jnp.exp(col(acum_col, hb)))
        y_h, s_h, dec_h = [], [], []
        for h in (ha, hb):
            ar = acum_row[h:h + 1, :]
            decay = jnp.exp(jnp.where(causal, col(acum_col, h) - ar, NEG))
            y_h.append(_dot((cb * decay).astype(BF16), xd))
            a_last = ar[:, cl - 1:cl]
            s_h.append(_dot((bt * jnp.exp(a_last - ar)).astype(BF16), xd))
            dec_h.append(jnp.exp(a_last))
        skip = jnp.where(low[0:1], d_ref[ha], d_ref[hb])
        ys.append(jnp.where(low, y_h[0], y_h[1]) + y_off + skip * xs_pair)
        s_sc[j] = st * jnp.where(low_s, dec_h[0], dec_h[1]) + jnp.where(low_s, s_h[0], s_h[1])

    z = z_ref[...]
    gated = jnp.concatenate(ys, axis=1) * (z * jax.nn.sigmoid(z))
    out = gated * lax.rsqrt(jnp.mean(gated * gated, axis=-1, keepdims=True) + EPS) * gn_ref[...]
    o_ref[...] = out.astype(o_ref.dtype)

    @pl.when(ci == pl.num_programs(1) - 1)
    def _():
        sout_ref[0] = s_sc[...]


def ssd_prompt(xbc, sm, z, w, bsz, t):
    cl = SSM_CHUNK
    nc = t // cl
    lanes = lambda v: jnp.zeros((1, 128), F32).at[0, N_GATES:N_GATES + SSM_HEADS].set(v)
    row = lambda width: pl.BlockSpec((cl, width), lambda b, c: (b * nc + c, 0))
    fixed = lambda r, c_: pl.BlockSpec((r, c_), lambda b, c: (0, 0))
    n_pair = SSM_HEADS // 2
    out, s_pairs = pl.pallas_call(
        functools.partial(_ssd_prompt_kernel, cl=cl),
        grid=(bsz, nc),
        in_specs=[pl.BlockSpec(memory_space=pltpu.SMEM), row(CONV_DIM), row(128), row(D_INNER),
                  fixed(CONV_W, CONV_DIM), fixed(1, CONV_DIM), fixed(1, 128), fixed(1, 128), fixed(1, D_INNER)],
        out_specs=[row(D_INNER), pl.BlockSpec((1, n_pair, 128, 128), lambda b, c: (b, 0, 0, 0))],
        out_shape=[jax.ShapeDtypeStruct((bsz * t, D_INNER), BF16),
                   jax.ShapeDtypeStruct((bsz, n_pair, 128, 128), F32)],
        scratch_shapes=[pltpu.VMEM((n_pair, 128, 128), F32), pltpu.VMEM((8, CONV_DIM), F32)],
        compiler_params=_cparams(("parallel", "arbitrary")),
        name="ssd_prompt",
    )(w['ssm_d'], xbc, sm, z, w['conv_w'], w['conv_b'].reshape(1, CONV_DIM), lanes(w['dt_bias']),
      lanes(-jnp.exp(w['a_log'])), w['ssm_norm'].reshape(1, D_INNER))
    s6 = s_pairs.reshape(bsz, n_pair, SSM_GROUPS, D_STATE, 2, SSM_HEAD_DIM)
    per_group = SSM_HEADS // SSM_GROUPS
    heads = [s6[:, h // 2, h // per_group, :, h % 2, :] for h in range(SSM_HEADS)]
    return out, jnp.stack(heads, axis=1).transpose(0, 1, 3, 2)


def _rmsnorm(x, g):
    return x * lax.rsqrt(jnp.mean(x * x, axis=-1, keepdims=True) + EPS) * g


def _rope(x, pos):
    half = HEAD_DIM // 2
    inv = ROPE_THETA ** (-jnp.arange(half, dtype=F32) / half)
    ang = pos.astype(F32)[:, None] * inv[None, :]
    cos, sin = jnp.cos(ang)[:, None, :], jnp.sin(ang)[:, None, :]
    x1, x2 = x[..., :half], x[..., half:]
    return jnp.concatenate([x1 * cos - x2 * sin, x2 * cos + x1 * sin], axis=-1)


def _ssd_inputs(xbc_all, dt_raw, w, t_pad):
    bsz = xbc_all.shape[0]
    length = xbc_all.shape[1] - (CONV_W - 1)
    conv = w['conv_b'] + xbc_all[:, 0:length] * w['conv_w'][0]
    for j in range(1, CONV_W):
        conv = conv + xbc_all[:, j:j + length] * w['conv_w'][j]
    act = jax.nn.silu(conv)
    dt = jax.nn.softplus(dt_raw + w['dt_bias'])
    a = -jnp.exp(w['a_log']) * dt
    pad = t_pad - length
    if pad:
        act = jnp.pad(act, ((0, 0), (0, pad), (0, 0)))
        dt = jnp.pad(dt, ((0, 0), (0, pad), (0, 0)))
        a = jnp.pad(a, ((0, 0), (0, pad), (0, 0)))
    xs = act[..., :D_INNER].reshape(bsz, t_pad, SSM_HEADS, SSM_HEAD_DIM).transpose(0, 2, 1, 3)
    b_in = act[..., D_INNER:D_INNER + SSM_GROUPS * D_STATE].reshape(bsz, t_pad, SSM_GROUPS, D_STATE)
    c_in = act[..., D_INNER + SSM_GROUPS * D_STATE:].reshape(bsz, t_pad, SSM_GROUPS, D_STATE)
    lane_pad = ((0, 0), (0, 0), (0, 128 - SSM_HEADS))
    return (xs, b_in.transpose(0, 2, 3, 1), c_in.transpose(0, 2, 1, 3), jnp.pad(a, lane_pad),
            a.transpose(0, 2, 1), jnp.pad(dt, lane_pad))


def _moe(f_b, logits, w):
    n_tok = f_b.shape[0]
    pg = jax.nn.softmax(logits[:, :N_EXPERT_GROUPS], axis=-1)
    g_sel = jnp.argmax(pg, axis=-1)
    g_w = jnp.max(pg, axis=-1)
    el = logits[:, N_EXPERT_GROUPS:N_EXPERT_GROUPS + N_EXPERTS].reshape(n_tok, N_EXPERT_GROUPS, EXPERTS_PER_GROUP)
    el = jnp.take_along_axis(el, g_sel[:, None, None], axis=1)[:, 0]
    top_p, top_i = lax.top_k(jax.nn.softmax(el, axis=-1), TOP_K)
    wts = g_w[:, None] * top_p / jnp.sum(top_p, axis=-1, keepdims=True)
    eid = (g_sel[:, None] * EXPERTS_PER_GROUP + top_i).reshape(-1).astype(jnp.int32)
    n_asg = eid.shape[0]
    onehot = (eid[:, None] == jnp.arange(N_EXPERTS, dtype=jnp.int32)[None, :]).astype(jnp.int32)
    within = jnp.take_along_axis(jnp.cumsum(onehot, axis=0), eid[:, None], axis=1)[:, 0] - 1
    sizes = jnp.sum(onehot, axis=0)
    padded = (sizes + MOE_ROWS - 1) // MOE_ROWS * MOE_ROWS
    pend = jnp.cumsum(padded)
    dest = (pend - padded)[eid] + within
    n_blk = -(-n_asg // MOE_ROWS) + N_EXPERTS
    src = jnp.zeros((n_blk * MOE_ROWS,), jnp.int32).at[dest].set(jnp.arange(n_asg, dtype=jnp.int32) // TOP_K)
    xpad = f_b[src]
    blk_start = jnp.arange(n_blk, dtype=jnp.int32) * MOE_ROWS
    blk_e = jnp.minimum(jnp.sum((pend[None, :] <= blk_start[:, None]).astype(jnp.int32), axis=1), N_EXPERTS - 1)
    n_used = (pend[-1] // MOE_ROWS).astype(jnp.int32).reshape(1)
    return xpad, blk_e, n_used, dest.reshape(n_tok, TOP_K).T.reshape(-1), jnp.pad(wts, ((0, 0), (0, 128 - TOP_K)))


def _token_dispatch(x2, mix_in, w, tm):
    pad = 128 - N_EXPERT_GROUPS - N_EXPERTS
    w_router = jnp.concatenate([w['w_rg'], w['w_re'], jnp.zeros((D_MODEL, pad), F32)], axis=1)
    b_router = jnp.concatenate([w['b_rg'], w['b_re'], jnp.zeros((pad,), F32)]).reshape(1, 128)
    h, f_b, logits = outproj(x2, mix_in[0], mix_in[1], w['w_out_b'], w['ffn_norm'], w_router, b_router, tm)
    return (h,) + _moe(f_b, logits, w)


def _token_finish(dispatched, p2, w, tm):
    h, xpad, blk_e, n_used, dest01, wts = dispatched
    ypad = moe_experts(xpad, blk_e, n_used, w['w_gate'], w['w_up'], w['w_down'])
    return moe_combine_ple(h, ypad[dest01], wts, p2, w['wpg_b'], w['wpp_b'], w['ple_norm'], tm)


def _ssd_finish(y_hm, z, w, length):
    bsz = y_hm.shape[0]
    y = y_hm[:, :, :length].transpose(0, 2, 1, 3).reshape(bsz, length, D_INNER)
    gated = y * jax.nn.silu(z)
    return _rmsnorm(gated, w['ssm_norm']).astype(BF16)


def _prompt_group(x, p, w):
    bsz, t, _ = x.shape
    m = bsz * t
    z, xbc, sm, sm_t, kv, win, qn_t, qr_t, ks_aug, kw, vs_t, vw_t = rms_inproj_prompt(
        x.reshape(m, D_MODEL), w['attn_norm'], w['w_in_r'], w['q_norm'], w['k_norm'], bsz, t)
    kv_new = kv.reshape(bsz, t, 4, KV_HEADS, HEAD_DIM)
    win_new = win.reshape(bsz, t, 2, KV_HEADS, HEAD_DIM)

    n_seg = t // S_CMP
    segs = kv_new[:, :, 0:2].astype(BF16).transpose(2, 0, 3, 1, 4).reshape(2, bsz * KV_HEADS, n_seg, S_CMP * HEAD_DIM)
    kvc = compress(segs, w['cmp_pe'], w['cmp_w1'], w['cmp_w2'])
    attn = nsa_prompt(qn_t, qr_t, kvc[0], kvc[1].transpose(0, 2, 1), ks_aug, vs_t, kw, vw_t, sm_t, bsz, t)

    ssd, ssm_new = ssd_prompt(xbc, sm, z, w, bsz, t)

    dispatched = _token_dispatch(x.reshape(m, D_MODEL), (attn, ssd), w, 512)
    keep = min(WINDOW, t)
    conv_new = xbc.reshape(bsz, t, CONV_DIM)[:, t - (CONV_W - 1):]

    def finish():
        return _token_finish(dispatched, p.reshape(m, PLE_DIM), w, 512).reshape(bsz, t, D_MODEL)

    return finish, kv_new, win_new[:, t - keep:], ssm_new, conv_new


def _sample_group(x, p, pool, page_table, cache_win, state_ssm, state_conv, w):
    bsz, lq, _ = x.shape
    m = bsz * lq
    q_off = page_table.shape[1] * PAGE_SIZE
    q, kvs, z, xbc, sm = rms_inproj(x.reshape(m, D_MODEL), w['attn_norm'], w['w_in_r'], m)
    pos = q_off + jnp.arange(lq)
    qn = _rmsnorm(q.reshape(bsz, lq, N_HEADS, HEAD_DIM), w['q_norm'])
    qr = _rope(qn, pos)
    kvs = kvs.reshape(bsz, lq, 6, KV_HEADS, HEAD_DIM)
    k_c = _rmsnorm(kvs[:, :, 0], w['k_norm'][0])
    k_s = _rope(_rmsnorm(kvs[:, :, 2], w['k_norm'][1]), pos)
    k_w = _rope(_rmsnorm(kvs[:, :, 4], w['k_norm'][2]), pos)
    kv_new = jnp.stack([k_c, kvs[:, :, 1], k_s, kvs[:, :, 3]], axis=2)
    win_new = jnp.stack([k_w, kvs[:, :, 5]], axis=2)
    win_cat = jnp.concatenate([cache_win, win_new], axis=1)

    pool_t = jnp.transpose(pool, (0, 2, 3, 4, 1))
    kvc = compress_paged(pool_t, page_table, w['cmp_pe'], w['cmp_w1'], w['cmp_w2'])
    attn = nsa_sample(qn, qr, kvc, kv_new, win_cat, pool_t, page_table, sm[:, :N_GATES], q_off)

    xbc_all = jnp.concatenate([state_conv, xbc.reshape(bsz, lq, CONV_DIM)], axis=1)
    dt_raw = sm[:, N_GATES:N_GATES + SSM_HEADS].reshape(bsz, lq, SSM_HEADS)
    xs, b_t, c_in, a_col, a_row, dt_col = _ssd_inputs(xbc_all, dt_raw, w, SSM_CHUNK)
    y_hm, s_t = ssd_scan(w['ssm_d'], xs, b_t, c_in, a_col, a_row, dt_col, state_ssm.transpose(0, 1, 3, 2), SSM_CHUNK)
    ssd = _ssd_finish(y_hm, z.reshape(bsz, lq, D_INNER), w, lq).reshape(m, D_INNER)

    y = _token_finish(_token_dispatch(x.reshape(m, D_MODEL), (attn, ssd), w, m), p.reshape(m, PLE_DIM), w, m)
    keep = cache_win.shape[1]
    return (y.reshape(bsz, lq, D_MODEL), kv_new, win_cat[:, win_cat.shape[1] - keep:], s_t.transpose(0, 1, 3, 2),
            xbc_all[:, xbc_all.shape[1] - (CONV_W - 1):])


def kernel(x_prompt, x_sample, cache_kv, cache_win, state_ssm, state_conv, page_table, p_prompt, p_sample,
           w_in, w_out, q_norm, k_norm, cmp_pe, cmp_w1, cmp_w2, conv_w, conv_b, dt_bias, a_log, ssm_d, ssm_norm,
           attn_norm, ffn_norm, w_rg, b_rg, w_re, b_re, w_gate, w_up, w_down, w_ple_proj, ple_norm, w_ple_gate):
    depth = w_in.shape[0]
    hp, hs = x_prompt, x_sample
    outs = [[] for _ in range(8)]
    cuts = np.cumsum((ATT_W, 6 * KV_HEADS * HEAD_DIM, N_GATES, D_INNER, CONV_DIM, SSM_HEADS))
    for l in range(depth):
        wi = w_in[l]
        w_in_r = jnp.concatenate(
            [wi[:, :cuts[1]], wi[:, cuts[2]:cuts[3]], wi[:, cuts[3]:cuts[4]], wi[:, cuts[1]:cuts[2]],
             wi[:, cuts[4]:cuts[5]], jnp.zeros((D_MODEL, C_SM - N_GATES - SSM_HEADS), F32)], axis=1).astype(BF16)
        w = dict(w_in_r=w_in_r, w_out_b=w_out[l].astype(BF16), q_norm=q_norm[l], k_norm=k_norm[l],
                 cmp_pe=cmp_pe[l], cmp_w1=cmp_w1[l], cmp_w2=cmp_w2[l], conv_w=conv_w[l], conv_b=conv_b[l],
                 dt_bias=dt_bias[l], a_log=a_log[l], ssm_d=ssm_d[l], ssm_norm=ssm_norm[l], attn_norm=attn_norm[l],
                 ffn_norm=ffn_norm[l], w_rg=w_rg[l], b_rg=b_rg[l], w_re=w_re[l], b_re=b_re[l],
                 w_gate=w_gate[l], w_up=w_up[l], w_down=w_down[l],
                 wpp_b=w_ple_proj[l].astype(BF16), ple_norm=ple_norm[l], wpg_b=w_ple_gate[l].astype(BF16))
        finish_prompt, *rest_p = _prompt_group(hp, p_prompt[l], w)
        hs, *rest_s = _sample_group(hs, p_sample[l], cache_kv[l], page_table, cache_win[l], state_ssm[l],
                                    state_conv[l], w)
        hp = finish_prompt()
        for j in range(4):
            outs[2 * j].append(rest_p[j])
            outs[2 * j + 1].append(rest_s[j])
    return (hp, hs) + tuple(jnp.stack(o) for o in outs)
```
